```python
import math
import jax, jax.numpy as jnp
from jax import lax
import numpy as np

D_MODEL = 1024
BATCH = 8
SEQ = 2048
DEPTH = 2
DEC_BATCH = 16
DEC_SEQ = 4096
PAST_LEN = 128

EPS = 1e-6
ROPE_THETA = 500000.0
HY_WIDTH = 1024
HY_SHORT = 3
HY_EMB = 33
HY_BANDS = (HY_EMB - 1) // 2
HY_FILT_HIDDEN = 64
HY_TARGET = 1e-2
HY_FAST_PCT = 0.3
HY_SLOW_PCT = 1.5
GDN_HEADS = 8
GDN_HEAD_DIM = 128
GDN_WIDTH = GDN_HEADS * GDN_HEAD_DIM
GDN_SHORT = 5
GDN_CHUNK = 64
DIL_PATTERNS = ((128, 1), (512, 4), (2048, 16))
N_DIL = len(DIL_PATTERNS)
DIL_HEADS = 4
DIL_HEAD_DIM = 128
DIL_WIDTH = DIL_HEADS * DIL_HEAD_DIM
DIL_BLOCK = 64
SWA_Q_HEADS = 16
SWA_KV_HEADS = 2
SWA_HEAD_DIM = 64
SWA_WIDTH = SWA_Q_HEADS * SWA_HEAD_DIM
SWA_HALF_WINDOW = 128
SWA_BLOCK = 128
MEM_TOKENS = 256
X_HEADS = 4
X_HEAD_DIM = 128
X_WIDTH = X_HEADS * X_HEAD_DIM

EVEN_IN_WIDTHS = (3 * HY_WIDTH, HY_WIDTH, 3 * GDN_WIDTH, GDN_WIDTH, 2 * GDN_HEADS, 2 * GDN_HEADS, X_WIDTH, X_WIDTH)
EVEN_IN = sum(EVEN_IN_WIDTHS)
EVEN_OUT = HY_WIDTH + GDN_WIDTH + X_WIDTH
ODD_IN_WIDTHS = (3 * N_DIL * DIL_WIDTH, DIL_WIDTH, SWA_WIDTH, 2 * SWA_KV_HEADS * SWA_HEAD_DIM, SWA_WIDTH, X_WIDTH, X_WIDTH)
ODD_IN = sum(ODD_IN_WIDTHS)
ODD_OUT = DIL_WIDTH + SWA_WIDTH + X_WIDTH

kernel_name = 'bidir_hybrid_encoder'


def rms_norm(x, g):
    xf = x.astype(jnp.float32)
    y = xf * lax.rsqrt(jnp.mean(xf * xf, axis=-1, keepdims=True) + EPS)
    return (y * g.astype(jnp.float32)).astype(x.dtype)


def l2_normalize(x):
    return x * lax.rsqrt(jnp.sum(x * x, axis=-1, keepdims=True) + EPS)


def split_cols(z, widths):
    cuts = [int(c) for c in np.cumsum(widths)[:-1]]
    return jnp.split(z, cuts, axis=-1)


def centred_depthwise_conv(x, w):
    k_width, seq = w.shape[0], x.shape[1]
    pad = k_width // 2
    xp = jnp.pad(x, ((0, 0), (pad, pad), (0, 0)))
    y = xp[:, 0:seq] * w[0]
    for j in range(1, k_width):
        y = y + xp[:, j:j + seq] * w[j]
    return y


def partial_rope(x):
    seq, dh = x.shape[1], x.shape[-1]
    rd = dh // 4
    half = rd // 2
    inv = ROPE_THETA ** (-jnp.arange(half, dtype=jnp.float32) / half)
    ang = jnp.arange(seq, dtype=jnp.float32)[:, None] * inv[None, :]
    shape = (1, seq) + (1,) * (x.ndim - 3) + (half,)
    cos = jnp.cos(ang).reshape(shape)
    sin = jnp.sin(ang).reshape(shape)
    xf = x.astype(jnp.float32)
    x1, x2, rest = xf[..., :half], xf[..., half:rd], xf[..., rd:]
    return jnp.concatenate([x1 * cos - x2 * sin, x2 * cos + x1 * sin, rest], axis=-1).astype(x.dtype)


def hyena_filters(seq, w1, b1, w2, b2, w3, freq):
    t = jnp.linspace(0.0, 1.0, seq, dtype=jnp.float32)[:, None]
    w = (2.0 * math.pi / seq) * jnp.arange(seq, dtype=jnp.float32)[:, None]
    f = jnp.linspace(1e-4, HY_BANDS - 1, HY_BANDS, dtype=jnp.float32)[None, :]
    emb = jnp.concatenate([t, jnp.cos(f * w), -jnp.sin(f * w)], axis=-1)
    freq = freq.astype(jnp.float32)
    hid = jnp.sin(freq * (emb @ w1.astype(jnp.float32) + b1.astype(jnp.float32)))
    hid = jnp.sin(freq * (hid @ w2.astype(jnp.float32) + b2.astype(jnp.float32)))
    filt = hid @ w3.astype(jnp.float32)
    deltas = jnp.abs(jnp.linspace(math.log(HY_TARGET) / HY_SLOW_PCT, math.log(HY_TARGET) / HY_FAST_PCT, HY_WIDTH, dtype=jnp.float32))
    decay = jnp.exp(-t * jnp.tile(deltas, 2)[None, :])
    filt = filt * decay
    return filt[:, :HY_WIDTH], filt[:, HY_WIDTH:]


def bidir_long_conv(u, h_fwd, h_bwd, skip):
    seq, ch = u.shape[1], u.shape[2]
    n_fft = 2 * seq
    filt = jnp.concatenate([h_fwd, jnp.zeros((1, ch), jnp.float32), h_bwd[1:][::-1]], axis=0)
    filt_f = jnp.fft.rfft(filt, n=n_fft, axis=0)
    u_f = jnp.fft.rfft(u.astype(jnp.float32), n=n_fft, axis=1)
    y = jnp.fft.irfft(u_f * filt_f[None], n=n_fft, axis=1)[:, :seq]
    return (y + u.astype(jnp.float32) * skip.astype(jnp.float32)).astype(u.dtype)


def gated_delta_chunked(q, k, v, g, beta):
    bsz, heads, seq, dk = k.shape
    dv = v.shape[-1]
    c = GDN_CHUNK
    n = seq // c
    q = q.reshape(bsz, heads, n, c, dk)
    k = k.reshape(bsz, heads, n, c, dk)
    v = v.reshape(bsz, heads, n, c, dv)
    g = jnp.cumsum(g.reshape(bsz, heads, n, c), axis=-1)
    beta = beta.reshape(bsz, heads, n, c)
    kb = k * beta[..., None]
    vb = v * beta[..., None]
    lower = jnp.tril(jnp.ones((c, c), dtype=bool))
    strict = jnp.tril(jnp.ones((c, c), dtype=bool), -1)
    decay = jnp.exp(jnp.where(lower, g[..., :, None] - g[..., None, :], -jnp.inf))
    a = jnp.where(strict, jnp.einsum('bhnid,bhnjd->bhnij', kb, k) * decay, 0.0)
    eye = jnp.eye(c, dtype=jnp.float32)
    t_inv = lax.linalg.triangular_solve(a + eye, jnp.broadcast_to(eye, a.shape), left_side=True, lower=True, unit_diagonal=True)
    u = jnp.einsum('bhnij,bhnjd->bhnid', t_inv, vb)
    w = jnp.einsum('bhnij,bhnjd->bhnid', t_inv, kb * jnp.exp(g)[..., None])
    qk = jnp.where(lower, jnp.einsum('bhnid,bhnjd->bhnij', q, k) * decay, 0.0)
    qg = q * jnp.exp(g)[..., None]
    kd = k * jnp.exp(g[..., -1:] - g)[..., None]
    g_last = jnp.exp(g[..., -1])

    def step(state, xs):
        qk_i, qg_i, kd_i, u_i, w_i, gl_i = xs
        v_new = u_i - jnp.einsum('bhik,bhkv->bhiv', w_i, state)
        o_i = jnp.einsum('bhik,bhkv->bhiv', qg_i, state) + jnp.einsum('bhij,bhjv->bhiv', qk_i, v_new)
        state = state * gl_i[..., None, None] + jnp.einsum('bhik,bhiv->bhkv', kd_i, v_new)
        return state, o_i

    xs = tuple(jnp.moveaxis(z, 2, 0) for z in (qk, qg, kd, u, w, g_last))
    s0 = jnp.zeros((bsz, heads, dk, dv), jnp.float32)
    _, o = lax.scan(step, s0, xs)
    return jnp.moveaxis(o, 0, 2).reshape(bsz, heads, seq, dv)


def banded_attention(q, k, v, half_window, block, sink=None):
    nbat, seq, hk, grp, dh = q.shape
    nb = -(-seq // block)
    lp = nb * block
    width = block + 2 * half_window
    qb = jnp.pad(q, ((0, 0), (0, lp - seq), (0, 0), (0, 0), (0, 0))).reshape(nbat, nb, block, hk, grp, dh)
    pad_kv = ((0, 0), (half_window, lp - seq + half_window), (0, 0), (0, 0))
    kp = jnp.pad(k, pad_kv)
    vp = jnp.pad(v, pad_kv)
    idx = jnp.arange(nb)[:, None] * block + jnp.arange(width)[None, :]
    kb = kp[:, idx]
    vb = vp[:, idx]
    q_pos = jnp.arange(lp).reshape(nb, block)
    k_pos = (idx - half_window)[:, None, :]
    valid = (jnp.abs(q_pos[:, :, None] - k_pos) <= half_window) & (k_pos >= 0) & (k_pos < seq)
    s = jnp.einsum('nbqhgd,nbkhd->nbhgqk', qb, kb, preferred_element_type=jnp.float32) * (dh ** -0.5)
    s = jnp.where(valid[None, :, None, None], s, -jnp.inf)
    m = jnp.max(s, axis=-1)
    if sink is not None:
        sk = sink.astype(jnp.float32)[None, None, :, :, None]
        m = jnp.maximum(m, sk)
    p = jnp.exp(s - m[..., None])
    den = jnp.sum(p, axis=-1)
    if sink is not None:
        den = den + jnp.exp(sk - m)
    o = jnp.einsum('nbhgqk,nbkhd->nbqhgd', p, vb.astype(jnp.float32)) / jnp.moveaxis(den, -1, 2)[..., None]
    o = o.reshape(nbat, lp, hk, grp, dh)[:, :seq].astype(q.dtype)
    lse = jnp.moveaxis(m + jnp.log(den), -1, 2).reshape(nbat, lp, hk, grp)[:, :seq]
    return o, lse


def dilated_attention(q, k, v, dilation, half_span):
    bsz, seq, heads, dh = q.shape
    d = dilation
    ls = seq // d

    def to_res(t):
        return jnp.swapaxes(t.reshape(bsz, ls, d, heads, dh), 1, 2).reshape(bsz * d, ls, heads, dh)

    o, lse = banded_attention(to_res(q)[:, :, :, None], to_res(k), to_res(v), half_span, DIL_BLOCK)
    o = jnp.swapaxes(o[:, :, :, 0].reshape(bsz, d, ls, heads, dh), 1, 2).reshape(bsz, seq, heads, dh)
    lse = jnp.swapaxes(lse[..., 0].reshape(bsz, d, ls, heads), 1, 2).reshape(bsz, seq, heads)
    return o, lse


def memory_cross_attention(z_q, mem, mem_g, w_mem_kv):
    bsz, seq = z_q.shape[0], z_q.shape[1]
    n_mem = mem.shape[1]
    q = z_q.reshape(bsz, seq, X_HEADS, X_HEAD_DIM)
    kv = (rms_norm(mem, mem_g) @ w_mem_kv).reshape(bsz, n_mem, 2, X_HEADS, X_HEAD_DIM)
    s = jnp.einsum('blhd,bmhd->bhlm', q, kv[:, :, 0], preferred_element_type=jnp.float32) * (X_HEAD_DIM ** -0.5)
    p = jax.nn.softmax(s, axis=-1)
    o = jnp.einsum('bhlm,bmhd->blhd', p, kv[:, :, 1].astype(jnp.float32))
    return o.reshape(bsz, seq, X_WIDTH).astype(z_q.dtype)


def even_layer_mixer(h, mem, w_in, w_out, hy_conv_w, hy_conv_b, hy_filt_w1, hy_filt_b1, hy_filt_w2, hy_filt_b2,
                     hy_filt_w3, hy_freq, hy_skip, gdn_conv_w, gdn_A_log, gdn_dt_bias, gdn_norm_g, mem_g, w_mem_kv):
    bsz, seq, _ = h.shape
    z = h @ w_in
    z_hy, g_hy, z_qkv, g_gdn, z_beta, z_a, z_xq, g_x = split_cols(z, EVEN_IN_WIDTHS)
    uc = centred_depthwise_conv(z_hy, hy_conv_w) + hy_conv_b
    x0, x1, v_hy = jnp.split(uc, 3, axis=-1)
    h_fwd, h_bwd = hyena_filters(seq, hy_filt_w1, hy_filt_b1, hy_filt_w2, hy_filt_b2, hy_filt_w3, hy_freq)
    y_a = x0 * bidir_long_conv(v_hy * x1, h_fwd, h_bwd, hy_skip)
    y_a = y_a * jax.nn.silu(g_hy)
    qkv = jax.nn.silu(centred_depthwise_conv(z_qkv, gdn_conv_w)).astype(jnp.float32)
    qkv = qkv.reshape(bsz, seq, 3, GDN_HEADS, GDN_HEAD_DIM)
    q = l2_normalize(qkv[:, :, 0]) * (GDN_HEAD_DIM ** -0.5)
    k = l2_normalize(qkv[:, :, 1])
    v = qkv[:, :, 2]
    beta = jax.nn.sigmoid(z_beta.astype(jnp.float32)).reshape(bsz, seq, 2, GDN_HEADS)
    g = -jnp.exp(gdn_A_log.astype(jnp.float32)) * jax.nn.softplus(
        z_a.astype(jnp.float32).reshape(bsz, seq, 2, GDN_HEADS) + gdn_dt_bias.astype(jnp.float32))
    qh, kh, vh = jnp.moveaxis(q, 1, 2), jnp.moveaxis(k, 1, 2), jnp.moveaxis(v, 1, 2)
    gh, bh = jnp.moveaxis(g, 1, 3), jnp.moveaxis(beta, 1, 3)
    o_f = gated_delta_chunked(qh, kh, vh, gh[:, 0], bh[:, 0])
    o_b = jnp.flip(gated_delta_chunked(jnp.flip(qh, 2), jnp.flip(kh, 2), jnp.flip(vh, 2),
                                       jnp.flip(gh[:, 1], 2), jnp.flip(bh[:, 1], 2)), 2)
    o = jnp.moveaxis(o_f + o_b, 1, 2)
    y_b = rms_norm(o, gdn_norm_g).reshape(bsz, seq, GDN_WIDTH).astype(h.dtype) * jax.nn.silu(g_gdn)
    y_x = memory_cross_attention(z_xq, mem, mem_g, w_mem_kv) * jax.nn.silu(g_x)
    return jnp.concatenate([y_a, y_b, y_x], axis=-1) @ w_out


def odd_layer_mixer(h, mem, w_in, w_out, swa_sink, mem_g, w_mem_kv):
    bsz, seq, _ = h.shape
    z = h @ w_in
    z_cqkv, g_c, z_dq, z_dkv, g_d, z_xq, g_x = split_cols(z, ODD_IN_WIDTHS)
    cqkv = z_cqkv.reshape(bsz, seq, 3, N_DIL, DIL_HEADS, DIL_HEAD_DIM)
    cq = partial_rope(cqkv[:, :, 0])
    ck = partial_rope(cqkv[:, :, 1])
    cv = cqkv[:, :, 2]
    outs, lses = [], []
    for gi, (window, dilation) in enumerate(DIL_PATTERNS):
        o_g, lse_g = dilated_attention(cq[:, :, gi], ck[:, :, gi], cv[:, :, gi], dilation, window // (2 * dilation))
        outs.append(o_g)
        lses.append(lse_g)
    wts = jax.nn.softmax(jnp.stack(lses, axis=0), axis=0)
    y_c = jnp.sum(wts[..., None] * jnp.stack(outs, axis=0).astype(jnp.float32), axis=0)
    y_c = y_c.reshape(bsz, seq, DIL_WIDTH).astype(h.dtype) * jax.nn.silu(g_c)
    dq = partial_rope(z_dq.reshape(bsz, seq, SWA_Q_HEADS, SWA_HEAD_DIM))
    dq = dq.reshape(bsz, seq, SWA_KV_HEADS, SWA_Q_HEADS // SWA_KV_HEADS, SWA_HEAD_DIM)
    dkv = z_dkv.reshape(bsz, seq, 2, SWA_KV_HEADS, SWA_HEAD_DIM)
    dk = partial_rope(dkv[:, :, 0])
    dv = dkv[:, :, 1]
    y_d, _ = banded_attention(dq, dk, dv, SWA_HALF_WINDOW, SWA_BLOCK,
                              sink=swa_sink.reshape(SWA_KV_HEADS, SWA_Q_HEADS // SWA_KV_HEADS))
    y_d = y_d.reshape(bsz, seq, SWA_WIDTH) * jax.nn.silu(g_d)
    y_x = memory_cross_attention(z_xq, mem, mem_g, w_mem_kv) * jax.nn.silu(g_x)
    return jnp.concatenate([y_c, y_d, y_x], axis=-1) @ w_out


def encoder_trunk(x, mem, even_params, odd_params):
    for layer in range(DEPTH):
        i = layer // 2
        if layer % 2 == 0:
            pre_g, post_g, *mix = [p[i] for p in even_params]
            out = even_layer_mixer(rms_norm(x, pre_g), mem, *mix)
        else:
            pre_g, post_g, *mix = [p[i] for p in odd_params]
            out = odd_layer_mixer(rms_norm(x, pre_g), mem, *mix)
        x = x + rms_norm(out, post_g)
    return x


def setup_inputs(seed: int = 0) -> dict:
    key = jax.random.key(seed)
    keys = list(jax.random.split(key, 32))
    nk = keys.pop
    n_even = (DEPTH + 1) // 2
    n_odd = DEPTH // 2
    d = D_MODEL

    def normal(shape, scale=1.0):
        return scale * jax.random.normal(nk(), shape, jnp.float32)

    def gain(shape):
        return 1.0 + normal(shape, 0.02)

    dt = jnp.exp(jax.random.uniform(nk(), (n_even, 2, GDN_HEADS), jnp.float32, math.log(1e-3), math.log(1e-1)))
    a_log = jnp.log(jax.random.uniform(nk(), (n_even, 2, GDN_HEADS), jnp.float32, 1.0, 16.0))
    return {
        'x_prompt': normal((BATCH, SEQ, d)),
        'x_sample': normal((DEC_BATCH, DEC_SEQ, d)),
        'mem_prompt': normal((BATCH, MEM_TOKENS, d)),
        'mem_sample': normal((DEC_BATCH, MEM_TOKENS, d)),
        'e_pre_g': gain((n_even, d)),
        'e_post_g': gain((n_even, d)),
        'e_w_in': normal((n_even, d, EVEN_IN), d ** -0.5),
        'e_w_out': normal((n_even, EVEN_OUT, d), EVEN_OUT ** -0.5),
        'hy_conv_w': normal((n_even, HY_SHORT, 3 * HY_WIDTH), HY_SHORT ** -0.5),
        'hy_conv_b': normal((n_even, 3 * HY_WIDTH), 0.02),
        'hy_filt_w1': normal((n_even, HY_EMB, HY_FILT_HIDDEN), HY_EMB ** -0.5),
        'hy_filt_b1': normal((n_even, HY_FILT_HIDDEN), 0.1),
        'hy_filt_w2': normal((n_even, HY_FILT_HIDDEN, HY_FILT_HIDDEN), HY_FILT_HIDDEN ** -0.5),
        'hy_filt_b2': normal((n_even, HY_FILT_HIDDEN), 0.1),
        'hy_filt_w3': normal((n_even, HY_FILT_HIDDEN, 2 * HY_WIDTH), HY_FILT_HIDDEN ** -0.5),
        'hy_freq': gain((n_even, HY_FILT_HIDDEN)),
        'hy_skip': normal((n_even, HY_WIDTH)),
        'gdn_conv_w': normal((n_even, GDN_SHORT, 3 * GDN_WIDTH), GDN_SHORT ** -0.5),
        'gdn_A_log': a_log,
        'gdn_dt_bias': dt + jnp.log(-jnp.expm1(-dt)),
        'gdn_norm_g': gain((n_even, GDN_HEAD_DIM)),
        'e_mem_g': gain((n_even, d)),
        'e_w_mem_kv': normal((n_even, d, 2 * X_WIDTH), d ** -0.5),
        'o_pre_g': gain((n_odd, d)),
        'o_post_g': gain((n_odd, d)),
        'o_w_in': normal((n_odd, d, ODD_IN), d ** -0.5),
        'o_w_out': normal((n_odd, ODD_OUT, d), ODD_OUT ** -0.5),
        'swa_sink': normal((n_odd, SWA_Q_HEADS), 0.5),
        'o_mem_g': gain((n_odd, d)),
        'o_w_mem_kv': normal((n_odd, d, 2 * X_WIDTH), d ** -0.5),
    }


def reference(x_prompt, x_sample, mem_prompt, mem_sample, e_pre_g, e_post_g, e_w_in, e_w_out, hy_conv_w, hy_conv_b,
              hy_filt_w1, hy_filt_b1, hy_filt_w2, hy_filt_b2, hy_filt_w3, hy_freq, hy_skip, gdn_conv_w, gdn_A_log,
              gdn_dt_bias, gdn_norm_g, e_mem_g, e_w_mem_kv, o_pre_g, o_post_g, o_w_in, o_w_out, swa_sink, o_mem_g,
              o_w_mem_kv):
    even_params = (e_pre_g, e_post_g, e_w_in, e_w_out, hy_conv_w, hy_conv_b, hy_filt_w1, hy_filt_b1, hy_filt_w2,
                   hy_filt_b2, hy_filt_w3, hy_freq, hy_skip, gdn_conv_w, gdn_A_log, gdn_dt_bias, gdn_norm_g,
                   e_mem_g, e_w_mem_kv)
    odd_params = (o_pre_g, o_post_g, o_w_in, o_w_out, swa_sink, o_mem_g, o_w_mem_kv)
    y_prompt = encoder_trunk(x_prompt, mem_prompt, even_params, odd_params)
    y_sample = encoder_trunk(x_sample, mem_sample, even_params, odd_params)
    return (y_prompt, y_sample)
```

```python
import functools
import math

import jax
import jax.numpy as jnp
import numpy as np
from jax import lax
from jax.experimental import pallas as pl
from jax.experimental.pallas import tpu as pltpu

D_MODEL = 1024
EPS = 1e-6
ROPE_THETA = 500000.0
HY_WIDTH = 1024
HY_EMB = 33
HY_BANDS = (HY_EMB - 1) // 2
HY_FILT_HIDDEN = 64
HY_TARGET = 1e-2
HY_FAST_PCT = 0.3
HY_SLOW_PCT = 1.5
GDN_HEADS = 8
GDN_HEAD_DIM = 128
GDN_WIDTH = GDN_HEADS * GDN_HEAD_DIM
GDN_CHUNK = 64
X_HEADS = 4
X_HEAD_DIM = 128
X_WIDTH = X_HEADS * X_HEAD_DIM

LANES = 128
FFT_N2 = 128
VMEM_LIMIT_BYTES = 48 * 1024 * 1024

BF16 = jnp.bfloat16
F32 = jnp.float32


def _cparams(*sem):
    return pltpu.CompilerParams(dimension_semantics=sem, vmem_limit_bytes=VMEM_LIMIT_BYTES)


def _dot(a, b):
    return jnp.dot(a, b, preferred_element_type=F32)


def _silu(x):
    return x * jax.nn.sigmoid(x)


def _norm_matmul_kernel(x_ref, g_ref, w_ref, o_ref, xn_ref):
    @pl.when(pl.program_id(1) == 0)
    def _():
        x = x_ref[...]
        ms = jnp.mean(x * x, axis=-1, keepdims=True)
        xn_ref[...] = (x * lax.rsqrt(ms + EPS) * g_ref[...]).astype(BF16)

    o_ref[...] = _dot(xn_ref[...], w_ref[...]).astype(o_ref.dtype)


def _norm_matmul(x2d, g, w, tn, out_dtype=F32, tm=512, name="norm_matmul"):
    t, d = x2d.shape
    n = w.shape[1]
    assert t % tm == 0 and n % tn == 0
    return pl.pallas_call(
        _norm_matmul_kernel,
        grid=(t // tm, n // tn),
        in_specs=[
            pl.BlockSpec((tm, d), lambda i, j: (i, 0)),
            pl.BlockSpec((1, d), lambda i, j: (0, 0)),
            pl.BlockSpec((d, tn), lambda i, j: (0, j)),
        ],
        out_specs=pl.BlockSpec((tm, tn), lambda i, j: (i, j)),
        out_shape=jax.ShapeDtypeStruct((t, n), out_dtype),
        scratch_shapes=[pltpu.VMEM((tm, d), BF16)],
        compiler_params=_cparams("parallel", "arbitrary"),
        name=name,
    )(x2d, g.reshape(1, d).astype(F32), w.astype(BF16))


def _hy_prep_kernel(zc_ref, zp_ref, zn_ref, g_ref, cw_ref, cb_ref, u_ref, e_ref, buf_ref):
    t = pl.program_id(1)
    nt = pl.num_programs(1)
    tl = zc_ref.shape[1]
    c = HY_WIDTH
    buf_ref[0:8, :] = jnp.where(t > 0, zp_ref[0], 0.0)
    buf_ref[8:8 + tl, :] = zc_ref[0]
    buf_ref[8 + tl:16 + tl, :] = jnp.where(t < nt - 1, zn_ref[0], 0.0)
    cw = 256
    for c0 in range(0, c, cw):
        parts = []
        for p in range(3):
            lo = p * c + c0
            acc = cb_ref[:, lo:lo + cw]
            for j in range(3):
                acc = acc + buf_ref[pl.ds(7 + j, tl), lo:lo + cw] * cw_ref[j:j + 1, lo:lo + cw]
            parts.append(acc)
        x0, x1, v = parts
        u_ref[0, :, c0:c0 + cw] = (v * x1).astype(u_ref.dtype)
        e_ref[0, :, c0:c0 + cw] = (x0 * _silu(g_ref[0, :, c0:c0 + cw])).astype(e_ref.dtype)


def _hy_prep(z, conv_w, conv_b, tl=256):
    b, l, _ = z.shape
    c = HY_WIDTH
    assert l % tl == 0
    nb8 = l // 8
    r = tl // 8
    return pl.pallas_call(
        _hy_prep_kernel,
        grid=(b, l // tl),
        in_specs=[
            pl.BlockSpec((1, tl, 3 * c), lambda i, t: (i, t, 1)),
            pl.BlockSpec((1, 8, 3 * c), lambda i, t: (i, jnp.maximum(t * r - 1, 0), 1)),
            pl.BlockSpec((1, 8, 3 * c), lambda i, t: (i, jnp.minimum((t + 1) * r, nb8 - 1), 1)),
            pl.BlockSpec((1, tl, c), lambda i, t: (i, t, 6)),
            pl.BlockSpec((3, 3 * c), lambda i, t: (0, 0)),
            pl.BlockSpec((1, 3 * c), lambda i, t: (0, 0)),
        ],
        out_specs=[
            pl.BlockSpec((1, tl, c), lambda i, t: (i, t, 0)),
            pl.BlockSpec((1, tl, c), lambda i, t: (i, t, 0)),
        ],
        out_shape=[jax.ShapeDtypeStruct((b, l, c), BF16), jax.ShapeDtypeStruct((b, l, c), F32)],
        scratch_shapes=[pltpu.VMEM((tl + 16, 3 * c), F32)],
        compiler_params=_cparams("parallel", "arbitrary"),
        name="hy_prep",
    )(z, z, z, z, conv_w.astype(F32), conv_b.reshape(1, 3 * c).astype(F32))


def _hy_filter_tables(l):
    t = jnp.linspace(0.0, 1.0, l, dtype=F32)[:, None]
    w = (2.0 * math.pi / l) * jnp.arange(l, dtype=F32)[:, None]
    f = jnp.linspace(1e-4, HY_BANDS - 1, HY_BANDS, dtype=F32)[None, :]
    emb = jnp.concatenate([t, jnp.cos(f * w), -jnp.sin(f * w)], axis=-1)
    emb = jnp.pad(emb, ((0, 0), (0, LANES - HY_EMB)))
    deltas = jnp.abs(jnp.linspace(math.log(HY_TARGET) / HY_SLOW_PCT, math.log(HY_TARGET) / HY_FAST_PCT, HY_WIDTH, dtype=F32))
    decay = jnp.exp(-t * jnp.tile(deltas, 2)[None, :])
    return emb, decay


def _hy_filter_kernel(emb_ref, dec_ref, w1_ref, b1_ref, w2_ref, b2_ref, w3_ref, fr_ref, sk_ref, o_ref):
    c = HY_WIDTH
    hp = lax.Precision.HIGHEST
    fr = fr_ref[...]
    hid = jnp.sin(fr * (jnp.dot(emb_ref[...], w1_ref[...], precision=hp, preferred_element_type=F32) + b1_ref[...]))
    hid = jnp.sin(fr * (jnp.dot(hid, w2_ref[...], precision=hp, preferred_element_type=F32) + b2_ref[...]))
    tl = emb_ref.shape[0]
    row = lax.broadcasted_iota(jnp.int32, (tl, 1), 0) + pl.program_id(0) * tl
    first = row == 0
    cw = 512
    for c0 in range(0, 2 * c, cw):
        filt = jnp.dot(hid, w3_ref[:, c0:c0 + cw], precision=hp, preferred_element_type=F32) * dec_ref[:, c0:c0 + cw]
        if c0 < c:
            filt = jnp.where(first, filt + sk_ref[:, c0:c0 + cw], filt)
            o_ref[0, :, c0:c0 + cw] = filt.astype(o_ref.dtype)
        else:
            filt = jnp.where(first, 0.0, filt)
            o_ref[1, :, c0 - c:c0 - c + cw] = filt.astype(o_ref.dtype)


def _hy_filters(l, w1, b1, w2, b2, w3, freq, skip, tl=256):
    c = HY_WIDTH
    hdim = HY_FILT_HIDDEN
    emb, decay = _hy_filter_tables(l)
    w1p = jnp.pad(w1.astype(F32), ((0, LANES - HY_EMB), (0, 0)))
    return pl.pallas_call(
        _hy_filter_kernel,
        grid=(l // tl,),
        in_specs=[
            pl.BlockSpec((tl, LANES), lambda t: (t, 0)),
            pl.BlockSpec((tl, 2 * c), lambda t: (t, 0)),
            pl.BlockSpec((LANES, hdim), lambda t: (0, 0)),
            pl.BlockSpec((1, hdim), lambda t: (0, 0)),
            pl.BlockSpec((hdim, hdim), lambda t: (0, 0)),
            pl.BlockSpec((1, hdim), lambda t: (0, 0)),
            pl.BlockSpec((hdim, 2 * c), lambda t: (0, 0)),
            pl.BlockSpec((1, hdim), lambda t: (0, 0)),
            pl.BlockSpec((1, c), lambda t: (0, 0)),
        ],
        out_specs=pl.BlockSpec((2, tl, c), lambda t: (0, t, 0)),
        out_shape=jax.ShapeDtypeStruct((2, l, c), BF16),
        compiler_params=_cparams("arbitrary"),
        name="hy_filter",
    )(emb, decay, w1p, b1.reshape(1, hdim).astype(F32), w2.astype(F32), b2.reshape(1, hdim).astype(F32),
      w3.astype(F32), freq.reshape(1, hdim).astype(F32), skip.reshape(1, c).astype(F32))


def _fft_consts(l):
    n = 2 * l
    n2 = FFT_N2
    n1 = n // n2
    n1h = n1 // 2
    w0 = 2.0 * math.pi / n
    k1 = jnp.arange(n1, dtype=jnp.int32)
    m1 = jnp.arange(n1h, dtype=jnp.int32)
    m2 = jnp.arange(n2, dtype=jnp.int32)
    ea = (n2 * m1[None, None, :] * k1[None, :, None] + m2[:, None, None] * k1[None, :, None]) % n
    ang = ea.astype(F32) * w0
    fa = jnp.concatenate([jnp.cos(ang), -jnp.sin(ang)], axis=1).astype(BF16)
    eb = (n1 * m2[:, None] * m2[None, :]) % n
    angb = eb.astype(F32) * w0
    fr, fi = jnp.cos(angb), -jnp.sin(angb)
    fb = jnp.concatenate([jnp.concatenate([fr, -fi], axis=1), jnp.concatenate([fi, fr], axis=1)], axis=0).astype(BF16)
    eg = (n1 * m2[None, :, None] * m2[None, None, :] + m2[None, :, None] * k1[:, None, None]) % n
    angg = eg.astype(F32) * w0
    gr, gi = jnp.cos(angg), jnp.sin(angg)
    gb = jnp.concatenate([jnp.concatenate([gr, -gi], axis=2), jnp.concatenate([gi, gr], axis=2)], axis=1).astype(BF16)
    ec = (n2 * m1[:, None] * k1[None, :]) % n
    angc = ec.astype(F32) * w0
    fc = (jnp.concatenate([jnp.cos(angc), -jnp.sin(angc)], axis=1) * (1.0 / n)).astype(BF16)
    return fa, fb, gb, fc


def _fft_a_kernel(u_ref, f_ref, o_ref):
    tn2 = f_ref.shape[0]
    n1 = f_ref.shape[1] // 2
    c = u_ref.shape[2] // tn2
    for j in range(tn2):
        a = _dot(f_ref[j], u_ref[0, :, j * c:(j + 1) * c])
        o_ref[0, 0, :, j * c:(j + 1) * c] = a[:n1].astype(o_ref.dtype)
        o_ref[0, 1, :, j * c:(j + 1) * c] = a[n1:].astype(o_ref.dtype)


def _fft_a(u, fa, tn2=8):
    b, l, c = u.shape
    n2, n1x2, n1h = fa.shape
    n1 = n1x2 // 2
    uv = u.reshape(b, n1h, n2 * c)
    return pl.pallas_call(
        _fft_a_kernel,
        grid=(b, n2 // tn2),
        in_specs=[
            pl.BlockSpec((1, n1h, tn2 * c), lambda i, j: (i, 0, j)),
            pl.BlockSpec((tn2, n1x2, n1h), lambda i, j: (j, 0, 0)),
        ],
        out_specs=pl.BlockSpec((1, 2, n1, tn2 * c), lambda i, j: (i, 0, 0, j)),
        out_shape=jax.ShapeDtypeStruct((b, 2, n1, n2 * c), BF16),
        compiler_params=_cparams("parallel", "arbitrary"),
        name="fft_a",
    )(uv, fa)


def _fft_bf_kernel(a_ref, w_ref, o_ref):
    n2 = a_ref.shape[2]
    ct = a_ref.shape[3]
    cw = 256
    for c0 in range(0, ct, cw):
        xf = _dot(w_ref[...], a_ref[0, :, :, c0:c0 + cw].reshape(2 * n2, cw))
        xb = _dot(w_ref[...], a_ref[1, :, :, c0:c0 + cw].reshape(2 * n2, cw))
        o_ref[0, :, c0:c0 + cw] = xf[:n2] + xb[:n2]
        o_ref[1, :, c0:c0 + cw] = xf[n2:] - xb[n2:]


def _fft_bf(a, fb):
    _, _, n, c = a.shape
    n2 = FFT_N2
    return pl.pallas_call(
        _fft_bf_kernel,
        grid=(n // n2,),
        in_specs=[
            pl.BlockSpec((2, 2, n2, c), lambda k: (0, 0, k, 0)),
            pl.BlockSpec((2 * n2, 2 * n2), lambda k: (0, 0)),
        ],
        out_specs=pl.BlockSpec((2, n2, c), lambda k: (0, k, 0)),
        out_shape=jax.ShapeDtypeStruct((2, n, c), F32),
        compiler_params=_cparams("arbitrary"),
        name="fft_bf",
    )(a, fb)


def _fft_b_kernel(a_ref, h_ref, w_ref, g_ref, o_ref):
    n2 = a_ref.shape[2]
    ct = a_ref.shape[3]
    cw = 256
    for c0 in range(0, ct, cw):
        x = _dot(w_ref[...], a_ref[0, :, :, c0:c0 + cw].reshape(2 * n2, cw))
        xr, xi = x[:n2], x[n2:]
        hr, hi = h_ref[0, :, c0:c0 + cw], h_ref[1, :, c0:c0 + cw]
        y = jnp.concatenate([xr * hr - xi * hi, xr * hi + xi * hr], axis=0).astype(BF16)
        z = _dot(g_ref[0], y)
        o_ref[0, 0, :, c0:c0 + cw] = z[:n2].astype(o_ref.dtype)
        o_ref[0, 1, :, c0:c0 + cw] = z[n2:].astype(o_ref.dtype)


def _fft_b(a, h, fb, gb):
    b, _, n, c = a.shape
    n2 = FFT_N2
    n1 = n // n2
    return pl.pallas_call(
        _fft_b_kernel,
        grid=(n1, b),
        in_specs=[
            pl.BlockSpec((1, 2, n2, c), lambda k, i: (i, 0, k, 0)),
            pl.BlockSpec((2, n2, c), lambda k, i: (0, k, 0)),
            pl.BlockSpec((2 * n2, 2 * n2), lambda k, i: (0, 0)),
            pl.BlockSpec((1, 2 * n2, 2 * n2), lambda k, i: (k, 0, 0)),
        ],
        out_specs=pl.BlockSpec((1, 2, n2, c), lambda k, i: (i, 0, k, 0)),
        out_shape=jax.ShapeDtypeStruct((b, 2, n, c), BF16),
        compiler_params=_cparams("arbitrary", "arbitrary"),
        name="fft_b",
    )(a, h, fb, gb)


def _fft_c_kernel(z_ref, e_ref, f_ref, o_ref):
    n1 = z_ref.shape[2]
    w = z_ref.shape[3]
    cw = 512
    for c0 in range(0, w, cw):
        y = _dot(f_ref[...], z_ref[0, :, :, c0:c0 + cw].reshape(2 * n1, cw))
        o_ref[0, :, c0:c0 + cw] = (y * e_ref[0, :, c0:c0 + cw]).astype(o_ref.dtype)


def _fft_c(z, e, fc, tn2=8):
    b, l, c = e.shape
    n1h, n1x2 = fc.shape
    n1 = n1x2 // 2
    n2 = FFT_N2
    ev = e.reshape(b, n1h, n2 * c)
    out = pl.pallas_call(
        _fft_c_kernel,
        grid=(b, n2 // tn2),
        in_specs=[
            pl.BlockSpec((1, 2, n1, tn2 * c), lambda i, j: (i, 0, 0, j)),
            pl.BlockSpec((1, n1h, tn2 * c), lambda i, j: (i, 0, j)),
            pl.BlockSpec((n1h, n1x2), lambda i, j: (0, 0)),
        ],
        out_specs=pl.BlockSpec((1, n1h, tn2 * c), lambda i, j: (i, 0, j)),
        out_shape=jax.ShapeDtypeStruct((b, n1h, n2 * c), BF16),
        compiler_params=_cparams("parallel", "arbitrary"),
        name="fft_c",
    )(z, ev, fc)
    return out.reshape(b, l, c)


def _hyena_branch(z, l, conv_w, conv_b, w1, b1, w2, b2, w3, freq, skip):
    b = z.shape[0]
    c = HY_WIDTH
    n = 2 * l
    n1 = n // FFT_N2
    fa, fb, gb, fc = _fft_consts(l)
    filt = _hy_filters(l, w1, b1, w2, b2, w3, freq, skip)
    fa_spec = _fft_a(filt, fa).reshape(2, 2, n, c)
    h = _fft_bf(fa_spec, fb)
    u, e = _hy_prep(z, conv_w, conv_b)
    a = _fft_a(u, fa).reshape(b, 2, n, c)
    zz = _fft_b(a, h, fb, gb).reshape(b, 2, n1, FFT_N2 * c)
    return _fft_c(zz, e, fc)


def _gdn_prep_kernel(zc_ref, zp_ref, zn_ref, cw_ref, q_ref, k_ref, v_ref, buf_ref):
    t = pl.program_id(1)
    nt = pl.num_programs(1)
    tl = zc_ref.shape[1]
    dh = GDN_HEAD_DIM
    buf_ref[0:8, :] = jnp.where(t > 0, zp_ref[0], 0.0)
    buf_ref[8:8 + tl, :] = zc_ref[0]
    buf_ref[8 + tl:16 + tl, :] = jnp.where(t < nt - 1, zn_ref[0], 0.0)
    outs = (q_ref, k_ref, v_ref)
    for p in range(3):
        for h in range(GDN_HEADS):
            lo = p * GDN_WIDTH + h * dh
            acc = buf_ref[pl.ds(6, tl), lo:lo + dh] * cw_ref[0:1, lo:lo + dh]
            for j in range(1, 5):
                acc = acc + buf_ref[pl.ds(6 + j, tl), lo:lo + dh] * cw_ref[j:j + 1, lo:lo + dh]
            a = _silu(acc)
            if p < 2:
                a = a * lax.rsqrt(jnp.sum(a * a, axis=-1, keepdims=True) + EPS)
            if p == 0:
                a = a * (dh ** -0.5)
            outs[p][0, :, h * dh:(h + 1) * dh] = a.astype(BF16)


def _gdn_prep(z, conv_w, tl=256):
    b, l, _ = z.shape
    w = 3 * GDN_WIDTH
    nb8 = l // 8
    r = tl // 8
    o_spec = pl.BlockSpec((1, tl, GDN_WIDTH), lambda i, t: (i, t, 0))
    o_shape = jax.ShapeDtypeStruct((b, l, GDN_WIDTH), BF16)
    return pl.pallas_call(
        _gdn_prep_kernel,
        grid=(b, l // tl),
        in_specs=[
            pl.BlockSpec((1, tl, w), lambda i, t: (i, t, 0)),
            pl.BlockSpec((1, 8, w), lambda i, t: (i, jnp.maximum(t * r - 1, 0), 0)),
            pl.BlockSpec((1, 8, w), lambda i, t: (i, jnp.minimum((t + 1) * r, nb8 - 1), 0)),
            pl.BlockSpec((5, w), lambda i, t: (0, 0)),
        ],
        out_specs=[o_spec, o_spec, o_spec],
        out_shape=[o_shape, o_shape, o_shape],
        scratch_shapes=[pltpu.VMEM((tl + 16, w), F32)],
        compiler_params=_cparams("parallel", "arbitrary"),
        name="gdn_prep",
    )(z, z, z, conv_w.astype(F32))


def _dot_nt(a, b):
    return lax.dot_general(a, b, (((1,), (1,)), ((), ())), preferred_element_type=F32)


def _dot_tn(a, b):
    return lax.dot_general(a, b, (((0,), (0,)), ((), ())), preferred_element_type=F32)


def _split3(x):
    a = x.astype(BF16)
    r = x - a.astype(F32)
    b = r.astype(BF16)
    c = (r - b.astype(F32)).astype(BF16)
    return a, b, c


def _gdn_scan_kernel(*refs, direction, final):
    if final:
        q_ref, k_ref, v_ref, zs_ref, al_ref, dt_ref, of_ref, gg_ref, ng_ref, o_ref, s_ref = refs
    else:
        q_ref, k_ref, v_ref, zs_ref, al_ref, dt_ref, o_ref, s_ref = refs
    c = GDN_CHUNK
    dh = GDN_HEAD_DIM

    @pl.when(pl.program_id(1) == 0)
    def _():
        s_ref[...] = jnp.zeros_like(s_ref)

    zs = zs_ref[0]
    beta_all = jax.nn.sigmoid(zs)
    gl = -jnp.exp(al_ref[...]) * jax.nn.softplus(zs + dt_ref[...])
    row = lax.broadcasted_iota(jnp.int32, (c, c), 0)
    col = lax.broadcasted_iota(jnp.int32, (c, c), 1)
    if direction == 0:
        incl, strict, last = col <= row, col < row, c - 1
    else:
        incl, strict, last = col >= row, col > row, 0
    tri = incl.astype(BF16)
    g1, g2, g3 = _split3(gl)
    gcum = _dot(tri, g1) + _dot(tri, g2) + _dot(tri, g3)
    gcum_t = jnp.concatenate([gcum, jnp.zeros_like(gcum)], axis=0).T
    eye = (row == col).astype(F32)

    for h in range(GDN_HEADS):
        bl = direction * GDN_HEADS + h
        ln = 2 * GDN_HEADS + bl
        beta = beta_all[:, bl:bl + 1]
        gc = gcum[:, ln:ln + 1]
        gct = gcum_t[ln:ln + 1, 0:c]
        glast = gcum_t[ln:ln + 1, last:last + 1]
        hs = slice(h * dh, (h + 1) * dh)
        qh, kh, vh = q_ref[0, :, hs], k_ref[0, :, hs], v_ref[0, :, hs]
        kk = _dot_nt(kh, kh)
        qk = _dot_nt(qh, kh)
        dmat = jnp.exp(jnp.where(incl, gc - gct, -jnp.inf))
        a = jnp.where(strict, (beta * kk) * dmat, 0.0)
        tinv = eye - a
        ab = a.astype(BF16)
        p = _dot(ab, ab)
        for j in range(5):
            pb = p.astype(BF16)
            tinv = tinv + _dot(tinv.astype(BF16), pb)
            if j < 4:
                p = _dot(pb, pb)
        qf, kf, vf = qh.astype(F32), kh.astype(F32), vh.astype(F32)
        eg = jnp.exp(gc)
        rhs = jnp.concatenate([vf * beta, kf * (beta * eg)], axis=1).astype(BF16)
        uw = _dot(tinv.astype(BF16), rhs)
        s = s_ref[h]
        lhs = jnp.concatenate([uw[:, dh:], qf * eg], axis=0).astype(BF16)
        ws = _dot(lhs, s.astype(BF16))
        vnew = uw[:, :dh] - ws[:c]
        vnb = vnew.astype(BF16)
        o = ws[c:] + _dot((qk * dmat).astype(BF16), vnb)
        kd = (kf * jnp.exp(glast - gc)).astype(BF16)
        s_ref[h] = s * jnp.exp(glast) + _dot_tn(kd, vnb)
        if final:
            tot = of_ref[0, :, hs] + o
            y = tot * lax.rsqrt(jnp.mean(tot * tot, axis=-1, keepdims=True) + EPS) * ng_ref[...]
            o_ref[0, :, hs] = (y * _silu(gg_ref[0, :, hs])).astype(o_ref.dtype)
        else:
            o_ref[0, :, hs] = o.astype(o_ref.dtype)


def _gdn_scan(q, k, v, zs, a_log, dt_bias, direction, o_fwd=None, z=None, gate_blk=None, norm_g=None):
    b, l, w = q.shape
    c = GDN_CHUNK
    n = l // c
    final = o_fwd is not None
    if direction == 0:
        cmap = lambda i, t: (i, t, 0)
    else:
        cmap = lambda i, t: (i, n - 1 - t, 0)
    blk = pl.BlockSpec((1, c, w), cmap)
    vec = pl.BlockSpec((1, LANES), lambda i, t: (0, 0))
    pad = lambda x: jnp.pad(x.reshape(1, -1).astype(F32), ((0, 0), (2 * GDN_HEADS, LANES - 4 * GDN_HEADS)))
    in_specs = [blk, blk, blk, pl.BlockSpec((1, c, LANES), cmap), vec, vec]
    args = [q, k, v, zs, pad(a_log), pad(dt_bias)]
    if final:
        if direction == 0:
            gmap = lambda i, t: (i, t, gate_blk)
        else:
            gmap = lambda i, t: (i, n - 1 - t, gate_blk)
        in_specs += [blk, pl.BlockSpec((1, c, w), gmap), pl.BlockSpec((1, GDN_HEAD_DIM), lambda i, t: (0, 0))]
        args += [o_fwd, z, norm_g.reshape(1, GDN_HEAD_DIM).astype(F32)]
    return pl.pallas_call(
        functools.partial(_gdn_scan_kernel, direction=direction, final=final),
        grid=(b, n),
        in_specs=in_specs,
        out_specs=blk,
        out_shape=jax.ShapeDtypeStruct((b, l, w), BF16 if final else F32),
        scratch_shapes=[pltpu.VMEM((GDN_HEADS, GDN_HEAD_DIM, GDN_HEAD_DIM), F32)],
        compiler_params=_cparams("parallel", "arbitrary"),
        name="gdn_scan_bwd" if direction else "gdn_scan_fwd",
    )(*args)


def _gdn_branch(z, zs, conv_w, a_log, dt_bias, norm_g, gate_blk):
    q, k, v = _gdn_prep(z, conv_w)
    o_f = _gdn_scan(q, k, v, zs, a_log, dt_bias, 0)
    return _gdn_scan(q, k, v, zs, a_log, dt_bias, 1, o_fwd=o_f, z=z, gate_blk=gate_blk, norm_g=norm_g)


def _xattn_kernel(q_ref, g_ref, kv_ref, o_ref):
    dh = X_HEAD_DIM
    for h in range(X_HEADS):
        hs = slice(h * dh, (h + 1) * dh)
        qh = q_ref[0, :, hs].astype(BF16)
        kh = kv_ref[0, :, hs]
        vh = kv_ref[0, :, X_WIDTH + h * dh:X_WIDTH + (h + 1) * dh]
        s = _dot_nt(qh, kh) * (dh ** -0.5)
        p = jnp.exp(s - jnp.max(s, axis=-1, keepdims=True))
        den = jnp.sum(p, axis=-1, keepdims=True)
        o = _dot(p.astype(BF16), vh) / den
        o_ref[0, :, hs] = (o * _silu(g_ref[0, :, hs])).astype(o_ref.dtype)


def _xattn(z, kv, q_blk, g_blk, tq=512):
    b, l, _ = z.shape
    m = kv.shape[1]
    w = X_WIDTH
    return pl.pallas_call(
        _xattn_kernel,
        grid=(b, l // tq),
        in_specs=[
            pl.BlockSpec((1, tq, w), lambda i, t: (i, t, q_blk)),
            pl.BlockSpec((1, tq, w), lambda i, t: (i, t, g_blk)),
            pl.BlockSpec((1, m, 2 * w), lambda i, t: (i, 0, 0)),
        ],
        out_specs=pl.BlockSpec((1, tq, w), lambda i, t: (i, t, 0)),
        out_shape=jax.ShapeDtypeStruct((b, l, w), BF16),
        compiler_params=_cparams("parallel", "arbitrary"),
        name="xattn",
    )(z, z, kv)


def _out_proj_kernel(*refs, widths):
    nparts = len(widths)
    y_refs = refs[:nparts]
    w_ref, g_ref, x_ref, o_ref = refs[nparts:]
    d = o_ref.shape[1]
    cw = 256
    ssq = jnp.zeros((o_ref.shape[0], 1), F32)
    for c0 in range(0, d, cw):
        acc = None
        off = 0
        for y_ref, wd in zip(y_refs, widths):
            part = _dot(y_ref[...], w_ref[off:off + wd, c0:c0 + cw])
            acc = part if acc is None else acc + part
            off += wd
        ssq = ssq + jnp.sum(acc * acc, axis=-1, keepdims=True)
        o_ref[:, c0:c0 + cw] = acc
    r = lax.rsqrt(ssq * (1.0 / d) + EPS)
    for c0 in range(0, d, cw):
        o_ref[:, c0:c0 + cw] = x_ref[:, c0:c0 + cw] + o_ref[:, c0:c0 + cw] * r * g_ref[:, c0:c0 + cw]


def _out_proj(parts, w_out, post_g, x2d, tm=512):
    t, d = x2d.shape
    widths = tuple(int(p.shape[1]) for p in parts)
    kdim = sum(widths)
    return pl.pallas_call(
        functools.partial(_out_proj_kernel, widths=widths),
        grid=(t // tm,),
        in_specs=[pl.BlockSpec((tm, wd), lambda i: (i, 0)) for wd in widths] + [
            pl.BlockSpec((kdim, d), lambda i: (0, 0)),
            pl.BlockSpec((1, d), lambda i: (0, 0)),
            pl.BlockSpec((tm, d), lambda i: (i, 0)),
        ],
        out_specs=pl.BlockSpec((tm, d), lambda i: (i, 0)),
        out_shape=jax.ShapeDtypeStruct((t, d), F32),
        compiler_params=_cparams("parallel"),
        name="out_proj",
    )(*parts, w_out.astype(BF16), post_g.reshape(1, d).astype(F32), x2d)


def _mem_kv(mem, mem_g, w_mem_kv):
    b, m, d = mem.shape
    kv = _norm_matmul(mem.reshape(b * m, d), mem_g, w_mem_kv, 1024, out_dtype=BF16, name="mem_kv")
    return kv.reshape(b, m, 2 * X_WIDTH)


E_QKV, E_HY, E_GHY, E_GGDN, E_XQ, E_GX, E_BIG = 0, 3072, 6144, 7168, 8192, 8704, 9216


def _even_layer(x, mem, pre_g, post_g, w_in, w_out, hy_conv_w, hy_conv_b, hy_w1, hy_b1, hy_w2, hy_b2, hy_w3, hy_freq,
                hy_skip, gdn_conv_w, gdn_a_log, gdn_dt_bias, gdn_norm_g, mem_g, w_mem_kv):
    b, l, d = x.shape
    x2d = x.reshape(b * l, d)
    w_big = jnp.concatenate([w_in[:, 4096:7168], w_in[:, 0:3072], w_in[:, 3072:4096], w_in[:, 7168:8192],
                             w_in[:, 8224:8736], w_in[:, 8736:9248]], axis=1)
    w_small = jnp.pad(w_in[:, 8192:8224], ((0, 0), (0, LANES - 4 * GDN_HEADS)))
    z = _norm_matmul(x2d, pre_g, w_big, 1024, name="even_in").reshape(b, l, E_BIG)
    zs = _norm_matmul(x2d, pre_g, w_small, LANES, name="even_in_gates").reshape(b, l, LANES)
    y_a = _hyena_branch(z, l, hy_conv_w, hy_conv_b, hy_w1, hy_b1, hy_w2, hy_b2, hy_w3, hy_freq, hy_skip)
    y_b = _gdn_branch(z, zs, gdn_conv_w, gdn_a_log, gdn_dt_bias, gdn_norm_g, E_GGDN // GDN_WIDTH)
    kv = _mem_kv(mem, mem_g, w_mem_kv)
    y_x = _xattn(z, kv, E_XQ // X_WIDTH, E_GX // X_WIDTH)
    t = b * l
    out = _out_proj([y_a.reshape(t, -1), y_b.reshape(t, -1), y_x.reshape(t, -1)], w_out, post_g, x2d)
    return out.reshape(b, l, d)


DIL_PATTERNS = ((128, 1), (512, 4), (2048, 16))
N_DIL = len(DIL_PATTERNS)
DIL_HEADS = 4
DIL_HEAD_DIM = 128
DIL_WIDTH = DIL_HEADS * DIL_HEAD_DIM
SWA_Q_HEADS = 16
SWA_KV_HEADS = 2
SWA_HEAD_DIM = 64
SWA_WIDTH = SWA_Q_HEADS * SWA_HEAD_DIM
SWA_HALF_WINDOW = 128
O_CQKV, O_GC, O_DQ, O_GD, O_XQ, O_GX, O_DKV, O_ALL = 0, 4608, 5120, 6144, 7168, 7680, 8192, 8448


def _rope_tables(l, dh):
    half = dh // 8
    inv = ROPE_THETA ** (-jnp.arange(half, dtype=F32) / half)
    ang = jnp.arange(l, dtype=F32)[:, None] * inv[None, :]
    cos, sin = jnp.cos(ang), jnp.sin(ang)
    one = jnp.ones((l, dh - 2 * half), F32)
    zero_h = jnp.zeros((l, half), F32)
    zero_r = jnp.zeros((l, dh - 2 * half), F32)
    c = jnp.concatenate([cos, cos, one], axis=1)
    sa = jnp.concatenate([-sin, zero_h, zero_r], axis=1)
    sb = jnp.concatenate([zero_h, sin, zero_r], axis=1)
    rep = LANES // dh
    return tuple(jnp.tile(t, (1, rep)) for t in (c, sa, sb))


def _odd_prep_kernel(c_ref, dq_ref, dkv_ref, c1_ref, a1_ref, b1_ref, c2_ref, a2_ref, b2_ref,
                     cq_ref, ck_ref, cv_ref, dqe_ref, dk_ref, dv_ref):
    tl = c_ref.shape[1]
    lane = lax.broadcasted_iota(jnp.int32, (tl, LANES), 1)
    low = lane < SWA_HEAD_DIM
    c1, a1, b1 = c1_ref[...], a1_ref[...], b1_ref[...]
    c2, a2, b2 = c2_ref[...], a2_ref[...], b2_ref[...]
    h1 = DIL_HEAD_DIM // 8
    h2 = SWA_HEAD_DIM // 8

    def rope1(x):
        return x * c1 + pltpu.roll(x, LANES - h1, 1) * a1 + pltpu.roll(x, h1, 1) * b1

    def rope2(x):
        return x * c2 + pltpu.roll(x, LANES - h2, 1) * a2 + pltpu.roll(x, h2, 1) * b2

    nblk = N_DIL * DIL_HEADS
    for j in range(nblk):
        ls = slice(j * LANES, (j + 1) * LANES)
        cq_ref[0, :, ls] = rope1(c_ref[0, :, ls]).astype(BF16)
        ck_ref[0, :, ls] = rope1(c_ref[0, :, nblk * LANES + j * LANES:nblk * LANES + (j + 1) * LANES]).astype(BF16)
        cv_ref[0, :, ls] = c_ref[0, :, 2 * nblk * LANES + j * LANES:2 * nblk * LANES + (j + 1) * LANES].astype(BF16)
    for j in range(SWA_Q_HEADS // 2):
        xr = rope2(dq_ref[0, :, j * LANES:(j + 1) * LANES])
        dqe_ref[0, :, (2 * j) * LANES:(2 * j + 1) * LANES] = jnp.where(low, xr, 0.0).astype(BF16)
        dqe_ref[0, :, (2 * j + 1) * LANES:(2 * j + 2) * LANES] = jnp.where(low, pltpu.roll(xr, SWA_HEAD_DIM, 1), 0.0).astype(BF16)
    kr = rope2(dkv_ref[0, :, 0:LANES])
    vv = dkv_ref[0, :, LANES:2 * LANES]
    kr_sw = pltpu.roll(kr, SWA_HEAD_DIM, 1)
    vv_sw = pltpu.roll(vv, SWA_HEAD_DIM, 1)
    dk_ref[0, :, 0:LANES] = jnp.where(low, kr, 0.0).astype(BF16)
    dk_ref[0, :, LANES:2 * LANES] = jnp.where(low, kr_sw, 0.0).astype(BF16)
    dv_ref[0, :, 0:LANES] = jnp.where(low, vv, vv_sw).astype(BF16)
    dv_ref[0, :, LANES:2 * LANES] = jnp.where(low, vv_sw, vv).astype(BF16)


def _odd_prep(z, tl=256):
    b, l, _ = z.shape
    wc = 3 * N_DIL * DIL_WIDTH
    t1 = _rope_tables(l, DIL_HEAD_DIM)
    t2 = _rope_tables(l, SWA_HEAD_DIM)
    tab = pl.BlockSpec((tl, LANES), lambda i, t: (t, 0))

    def spec(w):
        return pl.BlockSpec((1, tl, w), lambda i, t: (i, t, 0))

    def shape(w):
        return jax.ShapeDtypeStruct((b, l, w), BF16)

    wd = N_DIL * DIL_WIDTH
    return pl.pallas_call(
        _odd_prep_kernel,
        grid=(b, l // tl),
        in_specs=[
            pl.BlockSpec((1, tl, wc), lambda i, t: (i, t, 0)),
            pl.BlockSpec((1, tl, SWA_WIDTH), lambda i, t: (i, t, O_DQ // SWA_WIDTH)),
            pl.BlockSpec((1, tl, 2 * LANES), lambda i, t: (i, t, O_DKV // (2 * LANES))),
            tab, tab, tab, tab, tab, tab,
        ],
        out_specs=[spec(wd), spec(wd), spec(wd), spec(2 * SWA_WIDTH), spec(2 * LANES), spec(2 * LANES)],
        out_shape=[shape(wd), shape(wd), shape(wd), shape(2 * SWA_WIDTH), shape(2 * LANES), shape(2 * LANES)],
        compiler_params=_cparams("parallel", "arbitrary"),
        name="odd_prep",
    )(z, z, z, *t1, *t2)


def _band_mask(tq, half, ls):
    p0 = pl.program_id(1) * tq
    i = lax.broadcasted_iota(jnp.int32, (tq, tq + 2 * half), 0)
    j = lax.broadcasted_iota(jnp.int32, (tq, tq + 2 * half), 1)
    kpos = p0 - half + j
    return (j >= i) & (j - i <= 2 * half) & (kpos >= 0) & (kpos < ls)


def _band_attn_kernel(q_ref, kp_ref, kc_ref, kn_ref, vp_ref, vc_ref, vn_ref, o_ref, lse_ref, *, half, ls):
    tq = q_ref.shape[1]
    dh = DIL_HEAD_DIM
    valid = _band_mask(tq, half, ls)
    lane = lax.broadcasted_iota(jnp.int32, (tq, LANES), 1)
    lse_all = jnp.zeros((tq, LANES), F32)
    for h in range(DIL_HEADS):
        hs = slice(h * dh, (h + 1) * dh)
        kall = jnp.concatenate([kp_ref[0, :, hs], kc_ref[0, :, hs], kn_ref[0, :, hs]], axis=0)
        vall = jnp.concatenate([vp_ref[0, :, hs], vc_ref[0, :, hs], vn_ref[0, :, hs]], axis=0)
        s = jnp.where(valid, _dot_nt(q_ref[0, :, hs], kall) * (dh ** -0.5), -jnp.inf)
        m = jnp.max(s, axis=-1, keepdims=True)
        p = jnp.exp(s - m)
        den = jnp.sum(p, axis=-1, keepdims=True)
        o_ref[0, :, hs] = (_dot(p.astype(BF16), vall) / den).astype(o_ref.dtype)
        lse_all = jnp.where(lane == h, m + jnp.log(den), lse_all)
    lse_ref[0] = lse_all


def _band_specs(tq, half, ls, w, col):
    r = tq // half
    nb = ls // half
    cur = pl.BlockSpec((1, tq, w), lambda i, t: (i, t, col))
    prev = pl.BlockSpec((1, half, w), lambda i, t: (i, jnp.maximum(t * r - 1, 0), col))
    nxt = pl.BlockSpec((1, half, w), lambda i, t: (i, jnp.minimum((t + 1) * r, nb - 1), col))
    return cur, prev, nxt


def _band_attn(q, k, v, half):
    n, ls, w = q.shape
    tq = min(256, ls)
    cur, prev, nxt = _band_specs(tq, half, ls, w, 0)
    return pl.pallas_call(
        functools.partial(_band_attn_kernel, half=half, ls=ls),
        grid=(n, ls // tq),
        in_specs=[cur, prev, cur, nxt, prev, cur, nxt],
        out_specs=[cur, pl.BlockSpec((1, tq, LANES), lambda i, t: (i, t, 0))],
        out_shape=[jax.ShapeDtypeStruct((n, ls, w), F32), jax.ShapeDtypeStruct((n, ls, LANES), F32)],
        compiler_params=_cparams("parallel", "arbitrary"),
        name="band_attn",
    )(q, k, k, k, v, v, v)


def _dil_merge_kernel(o0_ref, o1_ref, o2_ref, l0_ref, l1_ref, l2_ref, g_ref, y_ref):
    dh = DIL_HEAD_DIM
    o_refs = (o0_ref, o1_ref, o2_ref)
    for h in range(DIL_HEADS):
        hs = slice(h * dh, (h + 1) * dh)
        ls = [r[0, :, h:h + 1] for r in (l0_ref, l1_ref, l2_ref)]
        m = jnp.maximum(jnp.maximum(ls[0], ls[1]), ls[2])
        ws = [jnp.exp(x - m) for x in ls]
        den = ws[0] + ws[1] + ws[2]
        y = (ws[0] / den) * o_refs[0][0, :, hs] + (ws[1] / den) * o_refs[1][0, :, hs] + (ws[2] / den) * o_refs[2][0, :, hs]
        y_ref[0, :, hs] = (y * _silu(g_ref[0, :, hs])).astype(y_ref.dtype)


def _dil_merge(outs, lses, z, tl=512):
    b, l, w = outs[0].shape
    o_spec = pl.BlockSpec((1, tl, w), lambda i, t: (i, t, 0))
    l_spec = pl.BlockSpec((1, tl, LANES), lambda i, t: (i, t, 0))
    return pl.pallas_call(
        _dil_merge_kernel,
        grid=(b, l // tl),
        in_specs=[o_spec, o_spec, o_spec, l_spec, l_spec, l_spec, pl.BlockSpec((1, tl, w), lambda i, t: (i, t, O_GC // DIL_WIDTH))],
        out_specs=o_spec,
        out_shape=jax.ShapeDtypeStruct((b, l, w), BF16),
        compiler_params=_cparams("parallel", "arbitrary"),
        name="dil_merge",
    )(*outs, *lses, z)


def _dilated_branch(cq, ck, cv, z):
    b, l, _ = cq.shape
    w = DIL_WIDTH
    outs, lses = [], []
    for gi, (window, d) in enumerate(DIL_PATTERNS):
        half = window // (2 * d)
        ls = l // d

        def to_res(x):
            x = x[:, :, gi * w:(gi + 1) * w]
            if d == 1:
                return x
            return jnp.swapaxes(x.reshape(b, ls, d, w), 1, 2).reshape(b * d, ls, w)

        def from_res(x):
            if d == 1:
                return x
            return jnp.swapaxes(x.reshape(b, d, ls, x.shape[-1]), 1, 2).reshape(b, l, x.shape[-1])

        o, lse = _band_attn(to_res(cq), to_res(ck), to_res(cv), half)
        outs.append(from_res(o))
        lses.append(from_res(lse))
    return _dil_merge(outs, lses, z)


def _swa_kernel(q_ref, kp_ref, kc_ref, kn_ref, vp_ref, vc_ref, vn_ref, g_ref, sink_ref, o_ref, *, half, ls):
    tq = q_ref.shape[1]
    dh = SWA_HEAD_DIM
    valid = _band_mask(tq, half, ls)
    low = lax.broadcasted_iota(jnp.int32, (tq, LANES), 1) < dh
    grp = SWA_Q_HEADS // SWA_KV_HEADS
    for g in range(SWA_KV_HEADS):
        gs = slice(g * LANES, (g + 1) * LANES)
        kall = jnp.concatenate([kp_ref[0, :, gs], kc_ref[0, :, gs], kn_ref[0, :, gs]], axis=0)
        vall = jnp.concatenate([vp_ref[0, :, gs], vc_ref[0, :, gs], vn_ref[0, :, gs]], axis=0)
        for jp in range(grp // 2):
            pair = []
            for hf in range(2):
                h = g * grp + 2 * jp + hf
                s = jnp.where(valid, _dot_nt(q_ref[0, :, h * LANES:(h + 1) * LANES], kall) * (dh ** -0.5), -jnp.inf)
                sk = sink_ref[:, h:h + 1]
                m = jnp.maximum(jnp.max(s, axis=-1, keepdims=True), sk)
                p = jnp.exp(s - m)
                den = jnp.sum(p, axis=-1, keepdims=True) + jnp.exp(sk - m)
                pair.append(_dot(p.astype(BF16), vall) / den)
            blk = (g * grp) // 2 + jp
            bs = slice(blk * LANES, (blk + 1) * LANES)
            o_ref[0, :, bs] = (jnp.where(low, pair[0], pair[1]) * _silu(g_ref[0, :, bs])).astype(o_ref.dtype)


def _swa_branch(dqe, dk, dv, z, sink, tq=256):
    b, l, _ = dqe.shape
    half = SWA_HALF_WINDOW
    _, prev, nxt = _band_specs(tq, half, l, 2 * LANES, 0)
    cur = pl.BlockSpec((1, tq, 2 * LANES), lambda i, t: (i, t, 0))
    sink_p = jnp.pad(sink.reshape(1, SWA_Q_HEADS).astype(F32), ((0, 0), (0, LANES - SWA_Q_HEADS)))
    return pl.pallas_call(
        functools.partial(_swa_kernel, half=half, ls=l),
        grid=(b, l // tq),
        in_specs=[
            pl.BlockSpec((1, tq, 2 * SWA_WIDTH), lambda i, t: (i, t, 0)),
            prev, cur, nxt, prev, cur, nxt,
            pl.BlockSpec((1, tq, SWA_WIDTH), lambda i, t: (i, t, O_GD // SWA_WIDTH)),
            pl.BlockSpec((1, LANES), lambda i, t: (0, 0)),
        ],
        out_specs=pl.BlockSpec((1, tq, SWA_WIDTH), lambda i, t: (i, t, 0)),
        out_shape=jax.ShapeDtypeStruct((b, l, SWA_WIDTH), BF16),
        compiler_params=_cparams("parallel", "arbitrary"),
        name="swa",
    )(dqe, dk, dk, dk, dv, dv, dv, z, sink_p)


def _odd_layer(x, mem, pre_g, post_g, w_in, w_out, swa_sink, mem_g, w_mem_kv):
    b, l, d = x.shape
    x2d = x.reshape(b * l, d)
    w_re = jnp.concatenate([w_in[:, 0:6144], w_in[:, 6400:8448], w_in[:, 6144:6400]], axis=1)
    z = _norm_matmul(x2d, pre_g, w_re, 1408, name="odd_in").reshape(b, l, O_ALL)
    cq, ck, cv, dqe, dk, dv = _odd_prep(z)
    y_c = _dilated_branch(cq, ck, cv, z)
    y_d = _swa_branch(dqe, dk, dv, z, swa_sink)
    kv = _mem_kv(mem, mem_g, w_mem_kv)
    y_x = _xattn(z, kv, O_XQ // X_WIDTH, O_GX // X_WIDTH)
    t = b * l
    out = _out_proj([y_c.reshape(t, -1), y_d.reshape(t, -1), y_x.reshape(t, -1)], w_out, post_g, x2d)
    return out.reshape(b, l, d)


def _trunk(x, mem, even_params, odd_params):
    x = _even_layer(x, mem, *[p[0] for p in even_params])
    return _odd_layer(x, mem, *[p[0] for p in odd_params])


def kernel(x_prompt, x_sample, mem_prompt, mem_sample, e_pre_g, e_post_g, e_w_in, e_w_out, hy_conv_w, hy_conv_b,
           hy_filt_w1, hy_filt_b1, hy_filt_w2, hy_filt_b2, hy_filt_w3, hy_freq, hy_skip, gdn_conv_w, gdn_A_log,
           gdn_dt_bias, gdn_norm_g, e_mem_g, e_w_mem_kv, o_pre_g, o_post_g, o_w_in, o_w_out, swa_sink, o_mem_g,
           o_w_mem_kv):
    even_params = (e_pre_g, e_post_g, e_w_in, e_w_out, hy_conv_w, hy_conv_b, hy_filt_w1, hy_filt_b1, hy_filt_w2,
                   hy_filt_b2, hy_filt_w3, hy_freq, hy_skip, gdn_conv_w, gdn_A_log, gdn_dt_bias, gdn_norm_g,
                   e_mem_g, e_w_mem_kv)
    odd_params = (o_pre_g, o_post_g, o_w_in, o_w_out, swa_sink, o_mem_g, o_w_mem_kv)
    y_prompt = _trunk(x_prompt, mem_prompt, even_params, odd_params)
    y_sample = _trunk(x_sample, mem_sample, even_params, odd_params)
    return (y_prompt, y_sample)
```

```python
import functools
import math

import jax
import jax.numpy as jnp
import numpy as np
from jax import lax
from jax.experimental import pallas as pl
from jax.experimental.pallas import tpu as pltpu

D_MODEL = 1024
EPS = 1e-6
ROPE_THETA = 500000.0
HY_WIDTH = 1024
HY_EMB = 33
HY_BANDS = (HY_EMB - 1) // 2
HY_FILT_HIDDEN = 64
HY_TARGET = 1e-2
HY_FAST_PCT = 0.3
HY_SLOW_PCT = 1.5
GDN_HEADS = 8
GDN_HEAD_DIM = 128
GDN_WIDTH = GDN_HEADS * GDN_HEAD_DIM
GDN_CHUNK = 64
X_HEADS = 4
X_HEAD_DIM = 128
X_WIDTH = X_HEADS * X_HEAD_DIM

LANES = 128
FFT_N2 = 128
VMEM_LIMIT_BYTES = 48 * 1024 * 1024

BF16 = jnp.bfloat16
F32 = jnp.float32


def _cparams(*sem):
    return pltpu.CompilerParams(dimension_semantics=sem, vmem_limit_bytes=VMEM_LIMIT_BYTES)


def _dot(a, b):
    return jnp.dot(a, b, preferred_element_type=F32)


def _silu(x):
    return x * jax.nn.sigmoid(x)


def _norm_matmul_kernel(x_ref, g_ref, w_ref, o_ref, xn_ref):
    @pl.when(pl.program_id(1) == 0)
    def _():
        x = x_ref[...]
        ms = jnp.mean(x * x, axis=-1, keepdims=True)
        xn_ref[...] = (x * lax.rsqrt(ms + EPS) * g_ref[...]).astype(BF16)

    o_ref[...] = _dot(xn_ref[...], w_ref[...]).astype(o_ref.dtype)


def _norm_matmul(x2d, g, w, tn, out_dtype=F32, tm=512, name="norm_matmul"):
    t, d = x2d.shape
    n = w.shape[1]
    assert t % tm == 0 and n % tn == 0
    return pl.pallas_call(
        _norm_matmul_kernel,
        grid=(t // tm, n // tn),
        in_specs=[
            pl.BlockSpec((tm, d), lambda i, j: (i, 0)),
            pl.BlockSpec((1, d), lambda i, j: (0, 0)),
            pl.BlockSpec((d, tn), lambda i, j: (0, j)),
        ],
        out_specs=pl.BlockSpec((tm, tn), lambda i, j: (i, j)),
        out_shape=jax.ShapeDtypeStruct((t, n), out_dtype),
        scratch_shapes=[pltpu.VMEM((tm, d), BF16)],
        compiler_params=_cparams("parallel", "arbitrary"),
        name=name,
    )(x2d, g.reshape(1, d).astype(F32), w.astype(BF16))


HALO_ROWS = 16


def _fill_halo_buf(buf_ref, zc_ref, zp_ref, zn_ref, t, nt):
    tl = zc_ref.shape[1]
    hr = HALO_ROWS
    buf_ref[0:hr, :] = jnp.where(t > 0, zp_ref[0].astype(F32), 0.0)
    buf_ref[hr:hr + tl, :] = zc_ref[0].astype(F32)
    buf_ref[hr + tl:2 * hr + tl, :] = jnp.where(t < nt - 1, zn_ref[0].astype(F32), 0.0)


def _halo_specs(tl, w, l, col):
    r = tl // HALO_ROWS
    nb = l // HALO_ROWS
    cur = pl.BlockSpec((1, tl, w), lambda i, t: (i, t, col))
    prev = pl.BlockSpec((1, HALO_ROWS, w), lambda i, t: (i, jnp.maximum(t * r - 1, 0), col))
    nxt = pl.BlockSpec((1, HALO_ROWS, w), lambda i, t: (i, jnp.minimum((t + 1) * r, nb - 1), col))
    return cur, prev, nxt


def _hy_prep_kernel(zc_ref, zp_ref, zn_ref, g_ref, cw_ref, cb_ref, u_ref, e_ref, buf_ref):
    t = pl.program_id(1)
    nt = pl.num_programs(1)
    tl = zc_ref.shape[1]
    c = HY_WIDTH
    hr = HALO_ROWS
    _fill_halo_buf(buf_ref, zc_ref, zp_ref, zn_ref, t, nt)
    cw = 256
    for c0 in range(0, c, cw):
        parts = []
        for p in range(3):
            lo = p * c + c0
            acc = cb_ref[:, lo:lo + cw]
            for j in range(3):
                acc = acc + buf_ref[pl.ds(hr - 1 + j, tl), lo:lo + cw] * cw_ref[j:j + 1, lo:lo + cw]
            parts.append(acc)
        x0, x1, v = parts
        u_ref[0, :, c0:c0 + cw] = (v * x1).astype(u_ref.dtype)
        e_ref[0, :, c0:c0 + cw] = (x0 * _silu(g_ref[0, :, c0:c0 + cw].astype(F32))).astype(e_ref.dtype)


def _hy_prep(z, conv_w, conv_b, tl=256):
    b, l, _ = z.shape
    c = HY_WIDTH
    assert l % tl == 0
    cur, prev, nxt = _halo_specs(tl, 3 * c, l, E_HY // (3 * c))
    return pl.pallas_call(
        _hy_prep_kernel,
        grid=(b, l // tl),
        in_specs=[
            cur, prev, nxt,
            pl.BlockSpec((1, tl, c), lambda i, t: (i, t, E_GHY // c)),
            pl.BlockSpec((3, 3 * c), lambda i, t: (0, 0)),
            pl.BlockSpec((1, 3 * c), lambda i, t: (0, 0)),
        ],
        out_specs=[
            pl.BlockSpec((1, tl, c), lambda i, t: (i, t, 0)),
            pl.BlockSpec((1, tl, c), lambda i, t: (i, t, 0)),
        ],
        out_shape=[jax.ShapeDtypeStruct((b, l, c), BF16), jax.ShapeDtypeStruct((b, l, c), F32)],
        scratch_shapes=[pltpu.VMEM((tl + 2 * HALO_ROWS, 3 * c), F32)],
        compiler_params=_cparams("parallel", "arbitrary"),
        name="hy_prep",
    )(z, z, z, z, conv_w.astype(F32), conv_b.reshape(1, 3 * c).astype(F32))


def _hy_filter_tables(l):
    t = jnp.linspace(0.0, 1.0, l, dtype=F32)[:, None]
    w = (2.0 * math.pi / l) * jnp.arange(l, dtype=F32)[:, None]
    f = jnp.linspace(1e-4, HY_BANDS - 1, HY_BANDS, dtype=F32)[None, :]
    emb = jnp.concatenate([t, jnp.cos(f * w), -jnp.sin(f * w)], axis=-1)
    emb = jnp.pad(emb, ((0, 0), (0, LANES - HY_EMB)))
    deltas = jnp.abs(jnp.linspace(math.log(HY_TARGET) / HY_SLOW_PCT, math.log(HY_TARGET) / HY_FAST_PCT, HY_WIDTH, dtype=F32))
    decay = jnp.exp(-t * jnp.tile(deltas, 2)[None, :])
    return emb, decay


def _hy_filter_kernel(emb_ref, dec_ref, w1_ref, b1_ref, w2_ref, b2_ref, w3_ref, fr_ref, sk_ref, o_ref):
    c = HY_WIDTH
    hp = lax.Precision.HIGHEST
    fr = fr_ref[...]
    hid = jnp.sin(fr * (jnp.dot(emb_ref[...], w1_ref[...], precision=hp, preferred_element_type=F32) + b1_ref[...]))
    hid = jnp.sin(fr * (jnp.dot(hid, w2_ref[...], precision=hp, preferred_element_type=F32) + b2_ref[...]))
    tl = emb_ref.shape[0]
    row = lax.broadcasted_iota(jnp.int32, (tl, 1), 0) + pl.program_id(0) * tl
    first = row == 0
    cw = 512
    for c0 in range(0, 2 * c, cw):
        filt = jnp.dot(hid, w3_ref[:, c0:c0 + cw], precision=hp, preferred_element_type=F32) * dec_ref[:, c0:c0 + cw]
        if c0 < c:
            filt = jnp.where(first, filt + sk_ref[:, c0:c0 + cw], filt)
            o_ref[0, :, c0:c0 + cw] = filt.astype(o_ref.dtype)
        else:
            filt = jnp.where(first, 0.0, filt)
            o_ref[1, :, c0 - c:c0 - c + cw] = filt.astype(o_ref.dtype)


def _hy_filters(l, w1, b1, w2, b2, w3, freq, skip, tl=256):
    c = HY_WIDTH
    hdim = HY_FILT_HIDDEN
    emb, decay = _hy_filter_tables(l)
    w1p = jnp.pad(w1.astype(F32), ((0, LANES - HY_EMB), (0, 0)))
    return pl.pallas_call(
        _hy_filter_kernel,
        grid=(l // tl,),
        in_specs=[
            pl.BlockSpec((tl, LANES), lambda t: (t, 0)),
            pl.BlockSpec((tl, 2 * c), lambda t: (t, 0)),
            pl.BlockSpec((LANES, hdim), lambda t: (0, 0)),
            pl.BlockSpec((1, hdim), lambda t: (0, 0)),
            pl.BlockSpec((hdim, hdim), lambda t: (0, 0)),
            pl.BlockSpec((1, hdim), lambda t: (0, 0)),
            pl.BlockSpec((hdim, 2 * c), lambda t: (0, 0)),
            pl.BlockSpec((1, hdim), lambda t: (0, 0)),
            pl.BlockSpec((1, c), lambda t: (0, 0)),
        ],
        out_specs=pl.BlockSpec((2, tl, c), lambda t: (0, t, 0)),
        out_shape=jax.ShapeDtypeStruct((2, l, c), BF16),
        compiler_params=_cparams("arbitrary"),
        name="hy_filter",
    )(emb, decay, w1p, b1.reshape(1, hdim).astype(F32), w2.astype(F32), b2.reshape(1, hdim).astype(F32),
      w3.astype(F32), freq.reshape(1, hdim).astype(F32), skip.reshape(1, c).astype(F32))


def _fft_consts(l):
    n = 2 * l
    n2 = FFT_N2
    n1 = n // n2
    n1h = n1 // 2
    w0 = 2.0 * math.pi / n
    k1 = jnp.arange(n1, dtype=jnp.int32)
    m1 = jnp.arange(n1h, dtype=jnp.int32)
    m2 = jnp.arange(n2, dtype=jnp.int32)
    ea = (n2 * m1[None, None, :] * k1[None, :, None] + m2[:, None, None] * k1[None, :, None]) % n
    ang = ea.astype(F32) * w0
    fa = jnp.concatenate([jnp.cos(ang), -jnp.sin(ang)], axis=1).astype(BF16)
    eb = (n1 * m2[:, None] * m2[None, :]) % n
    angb = eb.astype(F32) * w0
    fr, fi = jnp.cos(angb), -jnp.sin(angb)
    fb = jnp.concatenate([jnp.concatenate([fr, -fi], axis=1), jnp.concatenate([fi, fr], axis=1)], axis=0).astype(BF16)
    eg = (n1 * m2[None, :, None] * m2[None, None, :] + m2[None, :, None] * k1[:, None, None]) % n
    angg = eg.astype(F32) * w0
    gr, gi = jnp.cos(angg), jnp.sin(angg)
    gb = jnp.concatenate([jnp.concatenate([gr, -gi], axis=2), jnp.concatenate([gi, gr], axis=2)], axis=1).astype(BF16)
    ec = (n2 * m1[:, None] * k1[None, :]) % n
    angc = ec.astype(F32) * w0
    fc = (jnp.concatenate([jnp.cos(angc), -jnp.sin(angc)], axis=1) * (1.0 / n)).astype(BF16)
    return fa, fb, gb, fc


def _fft_a_kernel(u_ref, f_ref, o_ref):
    tn2 = f_ref.shape[0]
    n1 = f_ref.shape[1] // 2
    c = u_ref.shape[2] // tn2
    for j in range(tn2):
        a = _dot(f_ref[j], u_ref[0, :, j * c:(j + 1) * c])
        o_ref[0, 0, :, j * c:(j + 1) * c] = a[:n1].astype(o_ref.dtype)
        o_ref[0, 1, :, j * c:(j + 1) * c] = a[n1:].astype(o_ref.dtype)


def _fft_a(u, fa, tn2=8):
    b, l, c = u.shape
    n2, n1x2, n1h = fa.shape
    n1 = n1x2 // 2
    uv = u.reshape(b, n1h, n2 * c)
    return pl.pallas_call(
        _fft_a_kernel,
        grid=(b, n2 // tn2),
        in_specs=[
            pl.BlockSpec((1, n1h, tn2 * c), lambda i, j: (i, 0, j)),
            pl.BlockSpec((tn2, n1x2, n1h), lambda i, j: (j, 0, 0)),
        ],
        out_specs=pl.BlockSpec((1, 2, n1, tn2 * c), lambda i, j: (i, 0, 0, j)),
        out_shape=jax.ShapeDtypeStruct((b, 2, n1, n2 * c), BF16),
        compiler_params=_cparams("parallel", "arbitrary"),
        name="fft_a",
    )(uv, fa)


def _fft_bf_kernel(a_ref, w_ref, o_ref):
    n2 = a_ref.shape[2]
    ct = a_ref.shape[3]
    cw = 256
    for c0 in range(0, ct, cw):
        xf = _dot(w_ref[...], a_ref[0, :, :, c0:c0 + cw].reshape(2 * n2, cw))
        xb = _dot(w_ref[...], a_ref[1, :, :, c0:c0 + cw].reshape(2 * n2, cw))
        o_ref[0, :, c0:c0 + cw] = xf[:n2] + xb[:n2]
        o_ref[1, :, c0:c0 + cw] = xf[n2:] - xb[n2:]


def _fft_bf(a, fb):
    _, _, n, c = a.shape
    n2 = FFT_N2
    return pl.pallas_call(
        _fft_bf_kernel,
        grid=(n // n2,),
        in_specs=[
            pl.BlockSpec((2, 2, n2, c), lambda k: (0, 0, k, 0)),
            pl.BlockSpec((2 * n2, 2 * n2), lambda k: (0, 0)),
        ],
        out_specs=pl.BlockSpec((2, n2, c), lambda k: (0, k, 0)),
        out_shape=jax.ShapeDtypeStruct((2, n, c), F32),
        compiler_params=_cparams("arbitrary"),
        name="fft_bf",
    )(a, fb)


def _fft_b_kernel(a_ref, h_ref, w_ref, g_ref, o_ref):
    n2 = a_ref.shape[2]
    ct = a_ref.shape[3]
    cw = 256
    for c0 in range(0, ct, cw):
        x = _dot(w_ref[...], a_ref[0, :, :, c0:c0 + cw].reshape(2 * n2, cw))
        xr, xi = x[:n2], x[n2:]
        hr, hi = h_ref[0, :, c0:c0 + cw], h_ref[1, :, c0:c0 + cw]
        y = jnp.concatenate([xr * hr - xi * hi, xr * hi + xi * hr], axis=0).astype(BF16)
        z = _dot(g_ref[0], y)
        o_ref[0, 0, :, c0:c0 + cw] = z[:n2].astype(o_ref.dtype)
        o_ref[0, 1, :, c0:c0 + cw] = z[n2:].astype(o_ref.dtype)


def _fft_b(a, h, fb, gb):
    b, _, n, c = a.shape
    n2 = FFT_N2
    n1 = n // n2
    return pl.pallas_call(
        _fft_b_kernel,
        grid=(n1, b),
        in_specs=[
            pl.BlockSpec((1, 2, n2, c), lambda k, i: (i, 0, k, 0)),
            pl.BlockSpec((2, n2, c), lambda k, i: (0, k, 0)),
            pl.BlockSpec((2 * n2, 2 * n2), lambda k, i: (0, 0)),
            pl.BlockSpec((1, 2 * n2, 2 * n2), lambda k, i: (k, 0, 0)),
        ],
        out_specs=pl.BlockSpec((1, 2, n2, c), lambda k, i: (i, 0, k, 0)),
        out_shape=jax.ShapeDtypeStruct((b, 2, n, c), BF16),
        compiler_params=_cparams("arbitrary", "arbitrary"),
        name="fft_b",
    )(a, h, fb, gb)


def _fft_c_kernel(z_ref, e_ref, f_ref, o_ref):
    n1 = z_ref.shape[2]
    w = z_ref.shape[3]
    cw = 512
    for c0 in range(0, w, cw):
        y = _dot(f_ref[...], z_ref[0, :, :, c0:c0 + cw].reshape(2 * n1, cw))
        o_ref[0, :, c0:c0 + cw] = (y * e_ref[0, :, c0:c0 + cw]).astype(o_ref.dtype)


def _fft_c(z, e, fc, tn2=8):
    b, l, c = e.shape
    n1h, n1x2 = fc.shape
    n1 = n1x2 // 2
    n2 = FFT_N2
    ev = e.reshape(b, n1h, n2 * c)
    out = pl.pallas_call(
        _fft_c_kernel,
        grid=(b, n2 // tn2),
        in_specs=[
            pl.BlockSpec((1, 2, n1, tn2 * c), lambda i, j: (i, 0, 0, j)),
            pl.BlockSpec((1, n1h, tn2 * c), lambda i, j: (i, 0, j)),
            pl.BlockSpec((n1h, n1x2), lambda i, j: (0, 0)),
        ],
        out_specs=pl.BlockSpec((1, n1h, tn2 * c), lambda i, j: (i, 0, j)),
        out_shape=jax.ShapeDtypeStruct((b, n1h, n2 * c), BF16),
        compiler_params=_cparams("parallel", "arbitrary"),
        name="fft_c",
    )(z, ev, fc)
    return out.reshape(b, l, c)


def _hyena_branch(z, l, conv_w, conv_b, w1, b1, w2, b2, w3, freq, skip):
    b = z.shape[0]
    c = HY_WIDTH
    n = 2 * l
    n1 = n // FFT_N2
    fa, fb, gb, fc = _fft_consts(l)
    filt = _hy_filters(l, w1, b1, w2, b2, w3, freq, skip)
    fa_spec = _fft_a(filt, fa).reshape(2, 2, n, c)
    h = _fft_bf(fa_spec, fb)
    u, e = _hy_prep(z, conv_w, conv_b)
    a = _fft_a(u, fa).reshape(b, 2, n, c)
    zz = _fft_b(a, h, fb, gb).reshape(b, 2, n1, FFT_N2 * c)
    return _fft_c(zz, e, fc)


def _gdn_prep_kernel(zc_ref, zp_ref, zn_ref, cw_ref, q_ref, k_ref, v_ref, buf_ref):
    t = pl.program_id(1)
    nt = pl.num_programs(1)
    tl = zc_ref.shape[1]
    dh = GDN_HEAD_DIM
    hr = HALO_ROWS
    _fill_halo_buf(buf_ref, zc_ref, zp_ref, zn_ref, t, nt)
    outs = (q_ref, k_ref, v_ref)
    for p in range(3):
        for h in range(GDN_HEADS):
            lo = p * GDN_WIDTH + h * dh
            acc = buf_ref[pl.ds(hr - 2, tl), lo:lo + dh] * cw_ref[0:1, lo:lo + dh]
            for j in range(1, 5):
                acc = acc + buf_ref[pl.ds(hr - 2 + j, tl), lo:lo + dh] * cw_ref[j:j + 1, lo:lo + dh]
            a = _silu(acc)
            if p < 2:
                a = a * lax.rsqrt(jnp.sum(a * a, axis=-1, keepdims=True) + EPS)
            if p == 0:
                a = a * (dh ** -0.5)
            outs[p][0, :, h * dh:(h + 1) * dh] = a.astype(BF16)


def _gdn_prep(z, conv_w, tl=256):
    b, l, _ = z.shape
    w = 3 * GDN_WIDTH
    cur, prev, nxt = _halo_specs(tl, w, l, E_QKV // w)
    o_spec = pl.BlockSpec((1, tl, GDN_WIDTH), lambda i, t: (i, t, 0))
    o_shape = jax.ShapeDtypeStruct((b, l, GDN_WIDTH), BF16)
    return pl.pallas_call(
        _gdn_prep_kernel,
        grid=(b, l // tl),
        in_specs=[cur, prev, nxt, pl.BlockSpec((5, w), lambda i, t: (0, 0))],
        out_specs=[o_spec, o_spec, o_spec],
        out_shape=[o_shape, o_shape, o_shape],
        scratch_shapes=[pltpu.VMEM((tl + 2 * HALO_ROWS, w), F32)],
        compiler_params=_cparams("parallel", "arbitrary"),
        name="gdn_prep",
    )(z, z, z, conv_w.astype(F32))


def _dot_nt(a, b):
    return lax.dot_general(a, b, (((1,), (1,)), ((), ())), preferred_element_type=F32)


def _dot_tn(a, b):
    return lax.dot_general(a, b, (((0,), (0,)), ((), ())), preferred_element_type=F32)


def _split3(x):
    a = x.astype(BF16)
    r = x - a.astype(F32)
    b = r.astype(BF16)
    c = (r - b.astype(F32)).astype(BF16)
    return a, b, c


def _gdn_scan_kernel(*refs, direction, final):
    if final:
        q_ref, k_ref, v_ref, zs_ref, al_ref, dt_ref, of_ref, gg_ref, ng_ref, o_ref, s_ref = refs
    else:
        q_ref, k_ref, v_ref, zs_ref, al_ref, dt_ref, o_ref, s_ref = refs
    c = GDN_CHUNK
    dh = GDN_HEAD_DIM

    @pl.when(pl.program_id(1) == 0)
    def _():
        s_ref[...] = jnp.zeros_like(s_ref)

    zs = zs_ref[0]
    beta_all = jax.nn.sigmoid(zs)
    gl = -jnp.exp(al_ref[...]) * jax.nn.softplus(zs + dt_ref[...])
    row = lax.broadcasted_iota(jnp.int32, (c, c), 0)
    col = lax.broadcasted_iota(jnp.int32, (c, c), 1)
    if direction == 0:
        incl, strict, last = col <= row, col < row, c - 1
    else:
        incl, strict, last = col >= row, col > row, 0
    tri = incl.astype(BF16)
    g1, g2, g3 = _split3(gl)
    gcum = _dot(tri, g1) + _dot(tri, g2) + _dot(tri, g3)
    gcum_t = jnp.concatenate([gcum, jnp.zeros_like(gcum)], axis=0).T
    eye = (row == col).astype(F32)

    heads = range(GDN_HEADS)
    hsl = [slice(h * dh, (h + 1) * dh) for h in heads]
    lns = [2 * GDN_HEADS + direction * GDN_HEADS + h for h in heads]
    beta = [beta_all[:, direction * GDN_HEADS + h:direction * GDN_HEADS + h + 1] for h in heads]
    gc = [gcum[:, ln:ln + 1] for ln in lns]
    glast = [gcum_t[ln:ln + 1, last:last + 1] for ln in lns]
    qh = [q_ref[0, :, s] for s in hsl]
    kh = [k_ref[0, :, s] for s in hsl]
    vh = [v_ref[0, :, s] for s in hsl]
    kk = [_dot_nt(kh[h], kh[h]) for h in heads]
    qk = [_dot_nt(qh[h], kh[h]) for h in heads]
    dmat = [jnp.exp(jnp.where(incl, gc[h] - gcum_t[lns[h]:lns[h] + 1, 0:c], -jnp.inf)) for h in heads]
    a = [jnp.where(strict, (beta[h] * kk[h]) * dmat[h], 0.0) for h in heads]
    tinv = [eye - a[h] for h in heads]
    ab = [a[h].astype(BF16) for h in heads]
    p = [_dot(ab[h], ab[h]) for h in heads]
    for j in range(5):
        pb = [p[h].astype(BF16) for h in heads]
        tinv = [tinv[h] + _dot(tinv[h].astype(BF16), pb[h]) for h in heads]
        if j < 4:
            p = [_dot(pb[h], pb[h]) for h in heads]
    eg = [jnp.exp(gc[h]) for h in heads]
    kf = [kh[h].astype(F32) for h in heads]
    rhs = [jnp.concatenate([vh[h].astype(F32) * beta[h], kf[h] * (beta[h] * eg[h])], axis=1).astype(BF16) for h in heads]
    uw = [_dot(tinv[h].astype(BF16), rhs[h]) for h in heads]
    s_old = [s_ref[h] for h in heads]
    lhs = [jnp.concatenate([uw[h][:, dh:], qh[h].astype(F32) * eg[h]], axis=0).astype(BF16) for h in heads]
    ws = [_dot(lhs[h], s_old[h].astype(BF16)) for h in heads]
    vnb = [(uw[h][:, :dh] - ws[h][:c]).astype(BF16) for h in heads]
    o = [ws[h][c:] + _dot((qk[h] * dmat[h]).astype(BF16), vnb[h]) for h in heads]
    kd = [(kf[h] * jnp.exp(glast[h] - gc[h])).astype(BF16) for h in heads]
    for h in heads:
        s_ref[h] = s_old[h] * jnp.exp(glast[h]) + _dot_tn(kd[h], vnb[h])
    for h in heads:
        if final:
            tot = of_ref[0, :, hsl[h]] + o[h]
            y = tot * lax.rsqrt(jnp.mean(tot * tot, axis=-1, keepdims=True) + EPS) * ng_ref[...]
            o_ref[0, :, hsl[h]] = (y * _silu(gg_ref[0, :, hsl[h]].astype(F32))).astype(o_ref.dtype)
        else:
            o_ref[0, :, hsl[h]] = o[h].astype(o_ref.dtype)


def _gdn_scan(q, k, v, zs, a_log, dt_bias, direction, o_fwd=None, z=None, gate_blk=None, norm_g=None):
    b, l, w = q.shape
    c = GDN_CHUNK
    n = l // c
    final = o_fwd is not None
    if direction == 0:
        cmap = lambda i, t: (i, t, 0)
    else:
        cmap = lambda i, t: (i, n - 1 - t, 0)
    blk = pl.BlockSpec((1, c, w), cmap)
    vec = pl.BlockSpec((1, LANES), lambda i, t: (0, 0))
    pad = lambda x: jnp.pad(x.reshape(1, -1).astype(F32), ((0, 0), (2 * GDN_HEADS, LANES - 4 * GDN_HEADS)))
    in_specs = [blk, blk, blk, pl.BlockSpec((1, c, LANES), cmap), vec, vec]
    args = [q, k, v, zs, pad(a_log), pad(dt_bias)]
    if final:
        if direction == 0:
            gmap = lambda i, t: (i, t, gate_blk)
        else:
            gmap = lambda i, t: (i, n - 1 - t, gate_blk)
        in_specs += [blk, pl.BlockSpec((1, c, w), gmap), pl.BlockSpec((1, GDN_HEAD_DIM), lambda i, t: (0, 0))]
        args += [o_fwd, z, norm_g.reshape(1, GDN_HEAD_DIM).astype(F32)]
    return pl.pallas_call(
        functools.partial(_gdn_scan_kernel, direction=direction, final=final),
        grid=(b, n),
        in_specs=in_specs,
        out_specs=blk,
        out_shape=jax.ShapeDtypeStruct((b, l, w), BF16 if final else F32),
        scratch_shapes=[pltpu.VMEM((GDN_HEADS, GDN_HEAD_DIM, GDN_HEAD_DIM), F32)],
        compiler_params=_cparams("parallel", "arbitrary"),
        name="gdn_scan_bwd" if direction else "gdn_scan_fwd",
    )(*args)


def _gdn_branch(z, zs, conv_w, a_log, dt_bias, norm_g, gate_blk):
    q, k, v = _gdn_prep(z, conv_w)
    o_f = _gdn_scan(q, k, v, zs, a_log, dt_bias, 0)
    return _gdn_scan(q, k, v, zs, a_log, dt_bias, 1, o_fwd=o_f, z=z, gate_blk=gate_blk, norm_g=norm_g)


def _xattn_kernel(q_ref, g_ref, kv_ref, o_ref):
    dh = X_HEAD_DIM
    for h in range(X_HEADS):
        hs = slice(h * dh, (h + 1) * dh)
        qh = q_ref[0, :, hs].astype(BF16)
        kh = kv_ref[0, :, hs]
        vh = kv_ref[0, :, X_WIDTH + h * dh:X_WIDTH + (h + 1) * dh]
        s = _dot_nt(qh, kh) * (dh ** -0.5)
        p = jnp.exp(s - jnp.max(s, axis=-1, keepdims=True))
        den = jnp.sum(p, axis=-1, keepdims=True)
        o = _dot(p.astype(BF16), vh) / den
        o_ref[0, :, hs] = (o * _silu(g_ref[0, :, hs].astype(F32))).astype(o_ref.dtype)


def _xattn(z, kv, q_blk, g_blk, tq=512):
    b, l, _ = z.shape
    m = kv.shape[1]
    w = X_WIDTH
    return pl.pallas_call(
        _xattn_kernel,
        grid=(b, l // tq),
        in_specs=[
            pl.BlockSpec((1, tq, w), lambda i, t: (i, t, q_blk)),
            pl.BlockSpec((1, tq, w), lambda i, t: (i, t, g_blk)),
            pl.BlockSpec((1, m, 2 * w), lambda i, t: (i, 0, 0)),
        ],
        out_specs=pl.BlockSpec((1, tq, w), lambda i, t: (i, t, 0)),
        out_shape=jax.ShapeDtypeStruct((b, l, w), BF16),
        compiler_params=_cparams("parallel", "arbitrary"),
        name="xattn",
    )(z, z, kv)


def _out_proj_kernel(*refs, widths):
    nparts = len(widths)
    y_refs = refs[:nparts]
    w_ref, g_ref, x_ref, o_ref = refs[nparts:]
    d = o_ref.shape[1]
    cw = 256
    ssq = jnp.zeros((o_ref.shape[0], 1), F32)
    for c0 in range(0, d, cw):
        acc = None
        off = 0
        for y_ref, wd in zip(y_refs, widths):
            part = _dot(y_ref[...], w_ref[off:off + wd, c0:c0 + cw])
            acc = part if acc is None else acc + part
            off += wd
        ssq = ssq + jnp.sum(acc * acc, axis=-1, keepdims=True)
        o_ref[:, c0:c0 + cw] = acc
    r = lax.rsqrt(ssq * (1.0 / d) + EPS)
    for c0 in range(0, d, cw):
        o_ref[:, c0:c0 + cw] = x_ref[:, c0:c0 + cw] + o_ref[:, c0:c0 + cw] * r * g_ref[:, c0:c0 + cw]


def _out_proj(parts, w_out, post_g, x2d, tm=512):
    t, d = x2d.shape
    widths = tuple(int(p.shape[1]) for p in parts)
    kdim = sum(widths)
    return pl.pallas_call(
        functools.partial(_out_proj_kernel, widths=widths),
        grid=(t // tm,),
        in_specs=[pl.BlockSpec((tm, wd), lambda i: (i, 0)) for wd in widths] + [
            pl.BlockSpec((kdim, d), lambda i: (0, 0)),
            pl.BlockSpec((1, d), lambda i: (0, 0)),
            pl.BlockSpec((tm, d), lambda i: (i, 0)),
        ],
        out_specs=pl.BlockSpec((tm, d), lambda i: (i, 0)),
        out_shape=jax.ShapeDtypeStruct((t, d), F32),
        compiler_params=_cparams("parallel"),
        name="out_proj",
    )(*parts, w_out.astype(BF16), post_g.reshape(1, d).astype(F32), x2d)


def _mem_kv(mem, mem_g, w_mem_kv):
    b, m, d = mem.shape
    kv = _norm_matmul(mem.reshape(b * m, d), mem_g, w_mem_kv, 1024, out_dtype=BF16, name="mem_kv")
    return kv.reshape(b, m, 2 * X_WIDTH)


E_QKV, E_HY, E_GHY, E_GGDN, E_XQ, E_GX, E_BIG = 0, 3072, 6144, 7168, 8192, 8704, 9216


def _even_layer(x, mem, pre_g, post_g, w_in, w_out, hy_conv_w, hy_conv_b, hy_w1, hy_b1, hy_w2, hy_b2, hy_w3, hy_freq,
                hy_skip, gdn_conv_w, gdn_a_log, gdn_dt_bias, gdn_norm_g, mem_g, w_mem_kv):
    b, l, d = x.shape
    x2d = x.reshape(b * l, d)
    w_big = jnp.concatenate([w_in[:, 4096:7168], w_in[:, 0:3072], w_in[:, 3072:4096], w_in[:, 7168:8192],
                             w_in[:, 8224:8736], w_in[:, 8736:9248]], axis=1)
    w_small = jnp.pad(w_in[:, 8192:8224], ((0, 0), (0, LANES - 4 * GDN_HEADS)))
    z = _norm_matmul(x2d, pre_g, w_big, 1024, out_dtype=BF16, tm=1024, name="even_in").reshape(b, l, E_BIG)
    zs = _norm_matmul(x2d, pre_g, w_small, LANES, name="even_in_gates").reshape(b, l, LANES)
    y_a = _hyena_branch(z, l, hy_conv_w, hy_conv_b, hy_w1, hy_b1, hy_w2, hy_b2, hy_w3, hy_freq, hy_skip)
    y_b = _gdn_branch(z, zs, gdn_conv_w, gdn_a_log, gdn_dt_bias, gdn_norm_g, E_GGDN // GDN_WIDTH)
    kv = _mem_kv(mem, mem_g, w_mem_kv)
    y_x = _xattn(z, kv, E_XQ // X_WIDTH, E_GX // X_WIDTH)
    t = b * l
    out = _out_proj([y_a.reshape(t, -1), y_b.reshape(t, -1), y_x.reshape(t, -1)], w_out, post_g, x2d)
    return out.reshape(b, l, d)


DIL_PATTERNS = ((128, 1), (512, 4), (2048, 16))
N_DIL = len(DIL_PATTERNS)
DIL_HEADS = 4
DIL_HEAD_DIM = 128
DIL_WIDTH = DIL_HEADS * DIL_HEAD_DIM
SWA_Q_HEADS = 16
SWA_KV_HEADS = 2
SWA_HEAD_DIM = 64
SWA_WIDTH = SWA_Q_HEADS * SWA_HEAD_DIM
SWA_HALF_WINDOW = 128
O_CQKV, O_GC, O_DQ, O_GD, O_XQ, O_GX, O_DKV, O_ALL = 0, 4608, 5120, 6144, 7168, 7680, 8192, 8448


def _rope_tables(l, dh):
    half = dh // 8
    inv = ROPE_THETA ** (-jnp.arange(half, dtype=F32) / half)
    ang = jnp.arange(l, dtype=F32)[:, None] * inv[None, :]
    cos, sin = jnp.cos(ang), jnp.sin(ang)
    one = jnp.ones((l, dh - 2 * half), F32)
    zero_h = jnp.zeros((l, half), F32)
    zero_r = jnp.zeros((l, dh - 2 * half), F32)
    c = jnp.concatenate([cos, cos, one], axis=1)
    sa = jnp.concatenate([-sin, zero_h, zero_r], axis=1)
    sb = jnp.concatenate([zero_h, sin, zero_r], axis=1)
    rep = LANES // dh
    return tuple(jnp.tile(t, (1, rep)) for t in (c, sa, sb))


def _odd_prep_kernel(c_ref, dq_ref, dkv_ref, c1_ref, a1_ref, b1_ref, c2_ref, a2_ref, b2_ref,
                     cq_ref, ck_ref, cv_ref, dqe_ref, dk_ref, dv_ref):
    tl = c_ref.shape[1]
    lane = lax.broadcasted_iota(jnp.int32, (tl, LANES), 1)
    low = lane < SWA_HEAD_DIM
    c1, a1, b1 = c1_ref[...], a1_ref[...], b1_ref[...]
    c2, a2, b2 = c2_ref[...], a2_ref[...], b2_ref[...]
    h1 = DIL_HEAD_DIM // 8
    h2 = SWA_HEAD_DIM // 8

    def rope1(x):
        return x * c1 + pltpu.roll(x, LANES - h1, 1) * a1 + pltpu.roll(x, h1, 1) * b1

    def rope2(x):
        return x * c2 + pltpu.roll(x, LANES - h2, 1) * a2 + pltpu.roll(x, h2, 1) * b2

    nblk = N_DIL * DIL_HEADS
    for j in range(nblk):
        ls = slice(j * LANES, (j + 1) * LANES)
        cq_ref[0, :, ls] = rope1(c_ref[0, :, ls].astype(F32)).astype(BF16)
        ck_ref[0, :, ls] = rope1(c_ref[0, :, nblk * LANES + j * LANES:nblk * LANES + (j + 1) * LANES].astype(F32)).astype(BF16)
        cv_ref[0, :, ls] = c_ref[0, :, 2 * nblk * LANES + j * LANES:2 * nblk * LANES + (j + 1) * LANES].astype(BF16)
    for j in range(SWA_Q_HEADS // 2):
        xr = rope2(dq_ref[0, :, j * LANES:(j + 1) * LANES].astype(F32))
        dqe_ref[0, :, (2 * j) * LANES:(2 * j + 1) * LANES] = jnp.where(low, xr, 0.0).astype(BF16)
        dqe_ref[0, :, (2 * j + 1) * LANES:(2 * j + 2) * LANES] = jnp.where(low, pltpu.roll(xr, SWA_HEAD_DIM, 1), 0.0).astype(BF16)
    kr = rope2(dkv_ref[0, :, 0:LANES].astype(F32))
    vv = dkv_ref[0, :, LANES:2 * LANES].astype(F32)
    kr_sw = pltpu.roll(kr, SWA_HEAD_DIM, 1)
    vv_sw = pltpu.roll(vv, SWA_HEAD_DIM, 1)
    dk_ref[0, :, 0:LANES] = jnp.where(low, kr, 0.0).astype(BF16)
    dk_ref[0, :, LANES:2 * LANES] = jnp.where(low, kr_sw, 0.0).astype(BF16)
    dv_ref[0, :, 0:LANES] = jnp.where(low, vv, vv_sw).astype(BF16)
    dv_ref[0, :, LANES:2 * LANES] = jnp.where(low, vv_sw, vv).astype(BF16)


def _odd_prep(z, tl=256):
    b, l, _ = z.shape
    wc = 3 * N_DIL * DIL_WIDTH
    t1 = _rope_tables(l, DIL_HEAD_DIM)
    t2 = _rope_tables(l, SWA_HEAD_DIM)
    tab = pl.BlockSpec((tl, LANES), lambda i, t: (t, 0))

    def spec(w):
        return pl.BlockSpec((1, tl, w), lambda i, t: (i, t, 0))

    def shape(w):
        return jax.ShapeDtypeStruct((b, l, w), BF16)

    wd = N_DIL * DIL_WIDTH
    return pl.pallas_call(
        _odd_prep_kernel,
        grid=(b, l // tl),
        in_specs=[
            pl.BlockSpec((1, tl, wc), lambda i, t: (i, t, 0)),
            pl.BlockSpec((1, tl, SWA_WIDTH), lambda i, t: (i, t, O_DQ // SWA_WIDTH)),
            pl.BlockSpec((1, tl, 2 * LANES), lambda i, t: (i, t, O_DKV // (2 * LANES))),
            tab, tab, tab, tab, tab, tab,
        ],
        out_specs=[spec(wd), spec(wd), spec(wd), spec(2 * SWA_WIDTH), spec(2 * LANES), spec(2 * LANES)],
        out_shape=[shape(wd), shape(wd), shape(wd), shape(2 * SWA_WIDTH), shape(2 * LANES), shape(2 * LANES)],
        compiler_params=_cparams("parallel", "arbitrary"),
        name="odd_prep",
    )(z, z, z, *t1, *t2)


def _band_mask(tq, half, ls):
    p0 = pl.program_id(1) * tq
    i = lax.broadcasted_iota(jnp.int32, (tq, tq + 2 * half), 0)
    j = lax.broadcasted_iota(jnp.int32, (tq, tq + 2 * half), 1)
    kpos = p0 - half + j
    return (j >= i) & (j - i <= 2 * half) & (kpos >= 0) & (kpos < ls)


def _band_attn_kernel(q_ref, kp_ref, kc_ref, kn_ref, vp_ref, vc_ref, vn_ref, o_ref, lse_ref, *, half, ls):
    tq = q_ref.shape[1]
    dh = DIL_HEAD_DIM
    valid = _band_mask(tq, half, ls)
    lane = lax.broadcasted_iota(jnp.int32, (tq, LANES), 1)
    lse_all = jnp.zeros((tq, LANES), F32)
    heads = range(DIL_HEADS)
    hsl = [slice(h * dh, (h + 1) * dh) for h in heads]
    kall = [jnp.concatenate([kp_ref[0, :, s], kc_ref[0, :, s], kn_ref[0, :, s]], axis=0) for s in hsl]
    vall = [jnp.concatenate([vp_ref[0, :, s], vc_ref[0, :, s], vn_ref[0, :, s]], axis=0) for s in hsl]
    sc = [jnp.where(valid, _dot_nt(q_ref[0, :, hsl[h]], kall[h]) * (dh ** -0.5), -jnp.inf) for h in heads]
    m = [jnp.max(sc[h], axis=-1, keepdims=True) for h in heads]
    p = [jnp.exp(sc[h] - m[h]) for h in heads]
    den = [jnp.sum(p[h], axis=-1, keepdims=True) for h in heads]
    o = [_dot(p[h].astype(BF16), vall[h]) / den[h] for h in heads]
    for h in heads:
        o_ref[0, :, hsl[h]] = o[h].astype(o_ref.dtype)
        lse_all = jnp.where(lane == h, m[h] + jnp.log(den[h]), lse_all)
    lse_ref[0] = lse_all


def _band_specs(tq, half, ls, w, col):
    r = tq // half
    nb = ls // half
    cur = pl.BlockSpec((1, tq, w), lambda i, t: (i, t, col))
    prev = pl.BlockSpec((1, half, w), lambda i, t: (i, jnp.maximum(t * r - 1, 0), col))
    nxt = pl.BlockSpec((1, half, w), lambda i, t: (i, jnp.minimum((t + 1) * r, nb - 1), col))
    return cur, prev, nxt


def _band_attn(q, k, v, half):
    n, ls, w = q.shape
    tq = min(256, ls)
    cur, prev, nxt = _band_specs(tq, half, ls, w, 0)
    return pl.pallas_call(
        functools.partial(_band_attn_kernel, half=half, ls=ls),
        grid=(n, ls // tq),
        in_specs=[cur, prev, cur, nxt, prev, cur, nxt],
        out_specs=[cur, pl.BlockSpec((1, tq, LANES), lambda i, t: (i, t, 0))],
        out_shape=[jax.ShapeDtypeStruct((n, ls, w), F32), jax.ShapeDtypeStruct((n, ls, LANES), F32)],
        compiler_params=_cparams("parallel", "arbitrary"),
        name="band_attn",
    )(q, k, k, k, v, v, v)


def _dil_merge_kernel(o0_ref, o1_ref, o2_ref, l0_ref, l1_ref, l2_ref, g_ref, y_ref):
    dh = DIL_HEAD_DIM
    o_refs = (o0_ref, o1_ref, o2_ref)
    for h in range(DIL_HEADS):
        hs = slice(h * dh, (h + 1) * dh)
        ls = [r[0, :, h:h + 1] for r in (l0_ref, l1_ref, l2_ref)]
        m = jnp.maximum(jnp.maximum(ls[0], ls[1]), ls[2])
        ws = [jnp.exp(x - m) for x in ls]
        den = ws[0] + ws[1] + ws[2]
        y = (ws[0] / den) * o_refs[0][0, :, hs] + (ws[1] / den) * o_refs[1][0, :, hs] + (ws[2] / den) * o_refs[2][0, :, hs]
        y_ref[0, :, hs] = (y * _silu(g_ref[0, :, hs].astype(F32))).astype(y_ref.dtype)


def _dil_merge(outs, lses, z, tl=512):
    b, l, w = outs[0].shape
    o_spec = pl.BlockSpec((1, tl, w), lambda i, t: (i, t, 0))
    l_spec = pl.BlockSpec((1, tl, LANES), lambda i, t: (i, t, 0))
    return pl.pallas_call(
        _dil_merge_kernel,
        grid=(b, l // tl),
        in_specs=[o_spec, o_spec, o_spec, l_spec, l_spec, l_spec, pl.BlockSpec((1, tl, w), lambda i, t: (i, t, O_GC // DIL_WIDTH))],
        out_specs=o_spec,
        out_shape=jax.ShapeDtypeStruct((b, l, w), BF16),
        compiler_params=_cparams("parallel", "arbitrary"),
        name="dil_merge",
    )(*outs, *lses, z)


def _dilated_branch(cq, ck, cv, z):
    b, l, _ = cq.shape
    w = DIL_WIDTH
    outs, lses = [], []
    for gi, (window, d) in enumerate(DIL_PATTERNS):
        half = window // (2 * d)
        ls = l // d

        def to_res(x):
            x = x[:, :, gi * w:(gi + 1) * w]
            if d == 1:
                return x
            return jnp.swapaxes(x.reshape(b, ls, d, w), 1, 2).reshape(b * d, ls, w)

        def from_res(x):
            if d == 1:
                return x
            return jnp.swapaxes(x.reshape(b, d, ls, x.shape[-1]), 1, 2).reshape(b, l, x.shape[-1])

        o, lse = _band_attn(to_res(cq), to_res(ck), to_res(cv), half)
        outs.append(from_res(o))
        lses.append(from_res(lse))
    return _dil_merge(outs, lses, z)


def _swa_kernel(q_ref, kp_ref, kc_ref, kn_ref, vp_ref, vc_ref, vn_ref, g_ref, sink_ref, o_ref, *, half, ls):
    tq = q_ref.shape[1]
    dh = SWA_HEAD_DIM
    valid = _band_mask(tq, half, ls)
    low = lax.broadcasted_iota(jnp.int32, (tq, LANES), 1) < dh
    grp = SWA_Q_HEADS // SWA_KV_HEADS
    for g in range(SWA_KV_HEADS):
        gs = slice(g * LANES, (g + 1) * LANES)
        kall = jnp.concatenate([kp_ref[0, :, gs], kc_ref[0, :, gs], kn_ref[0, :, gs]], axis=0)
        vall = jnp.concatenate([vp_ref[0, :, gs], vc_ref[0, :, gs], vn_ref[0, :, gs]], axis=0)
        for jp in range(grp // 2):
            pair = []
            for hf in range(2):
                h = g * grp + 2 * jp + hf
                s = jnp.where(valid, _dot_nt(q_ref[0, :, h * LANES:(h + 1) * LANES], kall) * (dh ** -0.5), -jnp.inf)
                sk = sink_ref[:, h:h + 1]
                m = jnp.maximum(jnp.max(s, axis=-1, keepdims=True), sk)
                p = jnp.exp(s - m)
                den = jnp.sum(p, axis=-1, keepdims=True) + jnp.exp(sk - m)
                pair.append(_dot(p.astype(BF16), vall) / den)
            blk = (g * grp) // 2 + jp
            bs = slice(blk * LANES, (blk + 1) * LANES)
            y = jnp.where(low, pair[0], pair[1]) * _silu(g_ref[0, :, bs].astype(F32))
            o_ref[0, :, bs] = y.astype(o_ref.dtype)


def _swa_branch(dqe, dk, dv, z, sink, tq=256):
    b, l, _ = dqe.shape
    half = SWA_HALF_WINDOW
    _, prev, nxt = _band_specs(tq, half, l, 2 * LANES, 0)
    cur = pl.BlockSpec((1, tq, 2 * LANES), lambda i, t: (i, t, 0))
    sink_p = jnp.pad(sink.reshape(1, SWA_Q_HEADS).astype(F32), ((0, 0), (0, LANES - SWA_Q_HEADS)))
    return pl.pallas_call(
        functools.partial(_swa_kernel, half=half, ls=l),
        grid=(b, l // tq),
        in_specs=[
            pl.BlockSpec((1, tq, 2 * SWA_WIDTH), lambda i, t: (i, t, 0)),
            prev, cur, nxt, prev, cur, nxt,
            pl.BlockSpec((1, tq, SWA_WIDTH), lambda i, t: (i, t, O_GD // SWA_WIDTH)),
            pl.BlockSpec((1, LANES), lambda i, t: (0, 0)),
        ],
        out_specs=pl.BlockSpec((1, tq, SWA_WIDTH), lambda i, t: (i, t, 0)),
        out_shape=jax.ShapeDtypeStruct((b, l, SWA_WIDTH), BF16),
        compiler_params=_cparams("parallel", "arbitrary"),
        name="swa",
    )(dqe, dk, dk, dk, dv, dv, dv, z, sink_p)


def _odd_layer(x, mem, pre_g, post_g, w_in, w_out, swa_sink, mem_g, w_mem_kv):
    b, l, d = x.shape
    x2d = x.reshape(b * l, d)
    w_re = jnp.concatenate([w_in[:, 0:6144], w_in[:, 6400:8448], w_in[:, 6144:6400]], axis=1)
    z = _norm_matmul(x2d, pre_g, w_re, 1408, out_dtype=BF16, tm=1024, name="odd_in").reshape(b, l, O_ALL)
    cq, ck, cv, dqe, dk, dv = _odd_prep(z)
    y_c = _dilated_branch(cq, ck, cv, z)
    y_d = _swa_branch(dqe, dk, dv, z, swa_sink)
    kv = _mem_kv(mem, mem_g, w_mem_kv)
    y_x = _xattn(z, kv, O_XQ // X_WIDTH, O_GX // X_WIDTH)
    t = b * l
    out = _out_proj([y_c.reshape(t, -1), y_d.reshape(t, -1), y_x.reshape(t, -1)], w_out, post_g, x2d)
    return out.reshape(b, l, d)


def _trunk(x, mem, even_params, odd_params):
    x = _even_layer(x, mem, *[p[0] for p in even_params])
    return _odd_layer(x, mem, *[p[0] for p in odd_params])


def kernel(x_prompt, x_sample, mem_prompt, mem_sample, e_pre_g, e_post_g, e_w_in, e_w_out, hy_conv_w, hy_conv_b,
           hy_filt_w1, hy_filt_b1, hy_filt_w2, hy_filt_b2, hy_filt_w3, hy_freq, hy_skip, gdn_conv_w, gdn_A_log,
           gdn_dt_bias, gdn_norm_g, e_mem_g, e_w_mem_kv, o_pre_g, o_post_g, o_w_in, o_w_out, swa_sink, o_mem_g,
           o_w_mem_kv):
    even_params = (e_pre_g, e_post_g, e_w_in, e_w_out, hy_conv_w, hy_conv_b, hy_filt_w1, hy_filt_b1, hy_filt_w2,
                   hy_filt_b2, hy_filt_w3, hy_freq, hy_skip, gdn_conv_w, gdn_A_log, gdn_dt_bias, gdn_norm_g,
                   e_mem_g, e_w_mem_kv)
    odd_params = (o_pre_g, o_post_g, o_w_in, o_w_out, swa_sink, o_mem_g, o_w_mem_kv)
    y_prompt = _trunk(x_prompt, mem_prompt, even_params, odd_params)
    y_sample = _trunk(x_sample, mem_sample, even_params, odd_params)
    return (y_prompt, y_sample)
```

```python
import functools
import math

import jax
import jax.numpy as jnp
import numpy as np
from jax import lax
from jax.experimental import pallas as pl
from jax.experimental.pallas import tpu as pltpu

D_MODEL = 1024
EPS = 1e-6
ROPE_THETA = 500000.0
HY_WIDTH = 1024
HY_EMB = 33
HY_BANDS = (HY_EMB - 1) // 2
HY_FILT_HIDDEN = 64
HY_TARGET = 1e-2
HY_FAST_PCT = 0.3
HY_SLOW_PCT = 1.5
GDN_HEADS = 8
GDN_HEAD_DIM = 128
GDN_WIDTH = GDN_HEADS * GDN_HEAD_DIM
GDN_CHUNK = 64
X_HEADS = 4
X_HEAD_DIM = 128
X_WIDTH = X_HEADS * X_HEAD_DIM

LANES = 128
FFT_N2 = 128
VMEM_LIMIT_BYTES = 48 * 1024 * 1024

BF16 = jnp.bfloat16
F32 = jnp.float32


def _cparams(*sem):
    return pltpu.CompilerParams(dimension_semantics=sem, vmem_limit_bytes=VMEM_LIMIT_BYTES)


def _dot(a, b):
    return jnp.dot(a, b, preferred_element_type=F32)


def _silu(x):
    return x * jax.nn.sigmoid(x)


def _norm_matmul_kernel(x_ref, g_ref, w_ref, o_ref, xn_ref):
    @pl.when(pl.program_id(1) == 0)
    def _():
        x = x_ref[...]
        ms = jnp.mean(x * x, axis=-1, keepdims=True)
        xn_ref[...] = (x * lax.rsqrt(ms + EPS) * g_ref[...]).astype(BF16)

    o_ref[...] = _dot(xn_ref[...], w_ref[...]).astype(o_ref.dtype)


def _norm_matmul(x2d, g, w, tn, out_dtype=F32, tm=512, name="norm_matmul"):
    t, d = x2d.shape
    n = w.shape[1]
    assert t % tm == 0 and n % tn == 0
    return pl.pallas_call(
        _norm_matmul_kernel,
        grid=(t // tm, n // tn),
        in_specs=[
            pl.BlockSpec((tm, d), lambda i, j: (i, 0)),
            pl.BlockSpec((1, d), lambda i, j: (0, 0)),
            pl.BlockSpec((d, tn), lambda i, j: (0, j)),
        ],
        out_specs=pl.BlockSpec((tm, tn), lambda i, j: (i, j)),
        out_shape=jax.ShapeDtypeStruct((t, n), out_dtype),
        scratch_shapes=[pltpu.VMEM((tm, d), BF16)],
        compiler_params=_cparams("parallel", "arbitrary"),
        name=name,
    )(x2d, g.reshape(1, d).astype(F32), w.astype(BF16))


HALO_ROWS = 16


def _fill_halo_buf(buf_ref, zc_ref, zp_ref, zn_ref, t, nt):
    tl = zc_ref.shape[1]
    hr = HALO_ROWS
    buf_ref[0:hr, :] = jnp.where(t > 0, zp_ref[0].astype(F32), 0.0)
    buf_ref[hr:hr + tl, :] = zc_ref[0].astype(F32)
    buf_ref[hr + tl:2 * hr + tl, :] = jnp.where(t < nt - 1, zn_ref[0].astype(F32), 0.0)


def _halo_specs(tl, w, l, col):
    r = tl // HALO_ROWS
    nb = l // HALO_ROWS
    cur = pl.BlockSpec((1, tl, w), lambda i, t: (i, t, col))
    prev = pl.BlockSpec((1, HALO_ROWS, w), lambda i, t: (i, jnp.maximum(t * r - 1, 0), col))
    nxt = pl.BlockSpec((1, HALO_ROWS, w), lambda i, t: (i, jnp.minimum((t + 1) * r, nb - 1), col))
    return cur, prev, nxt


def _hy_prep_kernel(zc_ref, zp_ref, zn_ref, g_ref, cw_ref, cb_ref, u_ref, e_ref, buf_ref):
    t = pl.program_id(1)
    nt = pl.num_programs(1)
    tl = zc_ref.shape[1]
    c = HY_WIDTH
    hr = HALO_ROWS
    _fill_halo_buf(buf_ref, zc_ref, zp_ref, zn_ref, t, nt)
    cw = 256
    for c0 in range(0, c, cw):
        parts = []
        for p in range(3):
            lo = p * c + c0
            acc = cb_ref[:, lo:lo + cw]
            for j in range(3):
                acc = acc + buf_ref[pl.ds(hr - 1 + j, tl), lo:lo + cw] * cw_ref[j:j + 1, lo:lo + cw]
            parts.append(acc)
        x0, x1, v = parts
        u_ref[0, :, c0:c0 + cw] = (v * x1).astype(u_ref.dtype)
        e_ref[0, :, c0:c0 + cw] = (x0 * _silu(g_ref[0, :, c0:c0 + cw].astype(F32))).astype(e_ref.dtype)


def _hy_prep(z, conv_w, conv_b, tl=256):
    b, l, _ = z.shape
    c = HY_WIDTH
    assert l % tl == 0
    cur, prev, nxt = _halo_specs(tl, 3 * c, l, E_HY // (3 * c))
    return pl.pallas_call(
        _hy_prep_kernel,
        grid=(b, l // tl),
        in_specs=[
            cur, prev, nxt,
            pl.BlockSpec((1, tl, c), lambda i, t: (i, t, E_GHY // c)),
            pl.BlockSpec((3, 3 * c), lambda i, t: (0, 0)),
            pl.BlockSpec((1, 3 * c), lambda i, t: (0, 0)),
        ],
        out_specs=[
            pl.BlockSpec((1, tl, c), lambda i, t: (i, t, 0)),
            pl.BlockSpec((1, tl, c), lambda i, t: (i, t, 0)),
        ],
        out_shape=[jax.ShapeDtypeStruct((b, l, c), F32), jax.ShapeDtypeStruct((b, l, c), F32)],
        scratch_shapes=[pltpu.VMEM((tl + 2 * HALO_ROWS, 3 * c), F32)],
        compiler_params=_cparams("parallel", "arbitrary"),
        name="hy_prep",
    )(z, z, z, z, conv_w.astype(F32), conv_b.reshape(1, 3 * c).astype(F32))


def _hy_filter_tables(l):
    t = jnp.linspace(0.0, 1.0, l, dtype=F32)[:, None]
    w = (2.0 * math.pi / l) * jnp.arange(l, dtype=F32)[:, None]
    f = jnp.linspace(1e-4, HY_BANDS - 1, HY_BANDS, dtype=F32)[None, :]
    emb = jnp.concatenate([t, jnp.cos(f * w), -jnp.sin(f * w)], axis=-1)
    emb = jnp.pad(emb, ((0, 0), (0, LANES - HY_EMB)))
    deltas = jnp.abs(jnp.linspace(math.log(HY_TARGET) / HY_SLOW_PCT, math.log(HY_TARGET) / HY_FAST_PCT, HY_WIDTH, dtype=F32))
    decay = jnp.exp(-t * jnp.tile(deltas, 2)[None, :])
    return emb, decay


def _hy_filter_kernel(emb_ref, dec_ref, w1_ref, b1_ref, w2_ref, b2_ref, w3_ref, fr_ref, sk_ref, o_ref):
    c = HY_WIDTH
    hp = lax.Precision.HIGHEST
    fr = fr_ref[...]
    hid = jnp.sin(fr * (jnp.dot(emb_ref[...], w1_ref[...], precision=hp, preferred_element_type=F32) + b1_ref[...]))
    hid = jnp.sin(fr * (jnp.dot(hid, w2_ref[...], precision=hp, preferred_element_type=F32) + b2_ref[...]))
    tl = emb_ref.shape[0]
    row = lax.broadcasted_iota(jnp.int32, (tl, 1), 0) + pl.program_id(0) * tl
    first = row == 0
    cw = 512
    for c0 in range(0, 2 * c, cw):
        filt = jnp.dot(hid, w3_ref[:, c0:c0 + cw], precision=hp, preferred_element_type=F32) * dec_ref[:, c0:c0 + cw]
        if c0 < c:
            filt = jnp.where(first, filt + sk_ref[:, c0:c0 + cw], filt)
            o_ref[0, :, c0:c0 + cw] = filt.astype(o_ref.dtype)
        else:
            filt = jnp.where(first, 0.0, filt)
            o_ref[1, :, c0 - c:c0 - c + cw] = filt.astype(o_ref.dtype)


def _hy_filters(l, w1, b1, w2, b2, w3, freq, skip, tl=256):
    c = HY_WIDTH
    hdim = HY_FILT_HIDDEN
    emb, decay = _hy_filter_tables(l)
    w1p = jnp.pad(w1.astype(F32), ((0, LANES - HY_EMB), (0, 0)))
    return pl.pallas_call(
        _hy_filter_kernel,
        grid=(l // tl,),
        in_specs=[
            pl.BlockSpec((tl, LANES), lambda t: (t, 0)),
            pl.BlockSpec((tl, 2 * c), lambda t: (t, 0)),
            pl.BlockSpec((LANES, hdim), lambda t: (0, 0)),
            pl.BlockSpec((1, hdim), lambda t: (0, 0)),
            pl.BlockSpec((hdim, hdim), lambda t: (0, 0)),
            pl.BlockSpec((1, hdim), lambda t: (0, 0)),
            pl.BlockSpec((hdim, 2 * c), lambda t: (0, 0)),
            pl.BlockSpec((1, hdim), lambda t: (0, 0)),
            pl.BlockSpec((1, c), lambda t: (0, 0)),
        ],
        out_specs=pl.BlockSpec((2, tl, c), lambda t: (0, t, 0)),
        out_shape=jax.ShapeDtypeStruct((2, l, c), BF16),
        compiler_params=_cparams("arbitrary"),
        name="hy_filter",
    )(emb, decay, w1p, b1.reshape(1, hdim).astype(F32), w2.astype(F32), b2.reshape(1, hdim).astype(F32),
      w3.astype(F32), freq.reshape(1, hdim).astype(F32), skip.reshape(1, c).astype(F32))


def _fft_consts(l):
    n = 2 * l
    n2 = FFT_N2
    n1 = n // n2
    n1h = n1 // 2
    w0 = 2.0 * math.pi / n
    k1 = jnp.arange(n1, dtype=jnp.int32)
    m1 = jnp.arange(n1h, dtype=jnp.int32)
    m2 = jnp.arange(n2, dtype=jnp.int32)
    ea = (n2 * m1[None, None, :] * k1[None, :, None] + m2[:, None, None] * k1[None, :, None]) % n
    ang = ea.astype(F32) * w0
    fa = jnp.concatenate([jnp.cos(ang), -jnp.sin(ang)], axis=1).astype(BF16)
    eb = (n1 * m2[:, None] * m2[None, :]) % n
    angb = eb.astype(F32) * w0
    fr, fi = jnp.cos(angb), -jnp.sin(angb)
    fb = jnp.concatenate([jnp.concatenate([fr, -fi], axis=1), jnp.concatenate([fi, fr], axis=1)], axis=0).astype(BF16)
    eg = (n1 * m2[None, :, None] * m2[None, None, :] + m2[None, :, None] * k1[:, None, None]) % n
    angg = eg.astype(F32) * w0
    gr, gi = jnp.cos(angg), jnp.sin(angg)
    gb = jnp.concatenate([jnp.concatenate([gr, -gi], axis=2), jnp.concatenate([gi, gr], axis=2)], axis=1).astype(BF16)
    ec = (n2 * m1[:, None] * k1[None, :]) % n
    angc = ec.astype(F32) * w0
    fc = (jnp.concatenate([jnp.cos(angc), -jnp.sin(angc)], axis=1) * (1.0 / n)).astype(BF16)
    return fa, fb, gb, fc


def _fft_a_kernel(u_ref, f_ref, o_ref):
    tn2 = f_ref.shape[0]
    n1 = f_ref.shape[1] // 2
    c = u_ref.shape[2] // tn2
    for j in range(tn2):
        a = _dot(f_ref[j], u_ref[0, :, j * c:(j + 1) * c])
        o_ref[0, 0, :, j * c:(j + 1) * c] = a[:n1].astype(o_ref.dtype)
        o_ref[0, 1, :, j * c:(j + 1) * c] = a[n1:].astype(o_ref.dtype)


def _fft_a(u, fa, tn2=8):
    b, l, c = u.shape
    n2, n1x2, n1h = fa.shape
    n1 = n1x2 // 2
    uv = u.reshape(b, n1h, n2 * c)
    return pl.pallas_call(
        _fft_a_kernel,
        grid=(b, n2 // tn2),
        in_specs=[
            pl.BlockSpec((1, n1h, tn2 * c), lambda i, j: (i, 0, j)),
            pl.BlockSpec((tn2, n1x2, n1h), lambda i, j: (j, 0, 0)),
        ],
        out_specs=pl.BlockSpec((1, 2, n1, tn2 * c), lambda i, j: (i, 0, 0, j)),
        out_shape=jax.ShapeDtypeStruct((b, 2, n1, n2 * c), BF16),
        compiler_params=_cparams("parallel", "arbitrary"),
        name="fft_a",
    )(uv, fa)


def _fft_bf_kernel(a_ref, w_ref, o_ref):
    n2 = a_ref.shape[2]
    ct = a_ref.shape[3]
    cw = 256
    for c0 in range(0, ct, cw):
        xf = _dot(w_ref[...], a_ref[0, :, :, c0:c0 + cw].reshape(2 * n2, cw))
        xb = _dot(w_ref[...], a_ref[1, :, :, c0:c0 + cw].reshape(2 * n2, cw))
        o_ref[0, :, c0:c0 + cw] = xf[:n2] + xb[:n2]
        o_ref[1, :, c0:c0 + cw] = xf[n2:] - xb[n2:]


def _fft_bf(a, fb):
    _, _, n, c = a.shape
    n2 = FFT_N2
    return pl.pallas_call(
        _fft_bf_kernel,
        grid=(n // n2,),
        in_specs=[
            pl.BlockSpec((2, 2, n2, c), lambda k: (0, 0, k, 0)),
            pl.BlockSpec((2 * n2, 2 * n2), lambda k: (0, 0)),
        ],
        out_specs=pl.BlockSpec((2, n2, c), lambda k: (0, k, 0)),
        out_shape=jax.ShapeDtypeStruct((2, n, c), F32),
        compiler_params=_cparams("arbitrary"),
        name="fft_bf",
    )(a, fb)


SUBLANES = 8
CONV_UNROLL = 4


def _conv_pitch(n1):
    return 2 * n1 + SUBLANES


def _hy_conv_kernel(u_ref, h_ref, la_ref, w_ref, g_ref, fc_ref, o_ref, s_ref, *, n1):
    n2 = FFT_N2
    n1h = n1 // 2
    pitch = _conv_pitch(n1)
    grp = SUBLANES

    def stage_a(jj, carry):
        rows = [u_ref[0, pl.ds(pl.multiple_of(m * n2 + jj * grp, grp), grp), :] for m in range(n1h)]
        a = _dot(la_ref[jj], jnp.concatenate(rows, axis=0).astype(BF16))
        for j0 in range(grp):
            dst = pl.multiple_of((jj * grp + j0) * pitch, grp)
            s_ref[pl.ds(dst, 2 * n1), :] = a[j0 * 2 * n1:(j0 + 1) * 2 * n1]
        return carry

    lax.fori_loop(0, n2 // grp, stage_a, 0)

    def stage_b(kk, carry):
        k1s = [kk * CONV_UNROLL + i for i in range(CONV_UNROLL)]
        ar = [s_ref[pl.ds(k, n2, stride=pitch), :] for k in k1s]
        ai = [s_ref[pl.ds(n1 + k, n2, stride=pitch), :] for k in k1s]
        x = [_dot(w_ref[...], jnp.concatenate([ar[i], ai[i]], axis=0).astype(BF16)) for i in range(CONV_UNROLL)]
        y = []
        for i, k in enumerate(k1s):
            off = pl.multiple_of(k * n2, n2)
            hr, hi = h_ref[0, pl.ds(off, n2), :], h_ref[1, pl.ds(off, n2), :]
            xr, xi = x[i][:n2], x[i][n2:]
            y.append(jnp.concatenate([xr * hr - xi * hi, xr * hi + xi * hr], axis=0).astype(BF16))
        z = [_dot(g_ref[k1s[i]], y[i]) for i in range(CONV_UNROLL)]
        for i, k in enumerate(k1s):
            s_ref[pl.ds(k, n2, stride=pitch), :] = z[i][:n2]
            s_ref[pl.ds(n1 + k, n2, stride=pitch), :] = z[i][n2:]
        return carry

    lax.fori_loop(0, n1 // CONV_UNROLL, stage_b, 0)

    def stage_c(jj, carry):
        js = [jj * CONV_UNROLL + i for i in range(CONV_UNROLL)]
        zz = [s_ref[pl.ds(pl.multiple_of(j * pitch, grp), 2 * n1), :].astype(BF16) for j in js]
        y = [_dot(fc_ref[...], zz[i]) for i in range(CONV_UNROLL)]
        for i, j in enumerate(js):
            o_ref[0, pl.ds(j, n1h, stride=n2), :] = y[i]
        return carry

    lax.fori_loop(0, n2 // CONV_UNROLL, stage_c, 0)


def _hy_conv(u, h, fa, fb, gb, fc):
    b, l, c = u.shape
    n2, n1x2, n1h = fa.shape
    n1 = n1x2 // 2
    grp = SUBLANES
    la = (fa.reshape(n2 // grp, grp, n1x2, n1h, 1) * jnp.eye(grp, dtype=fa.dtype).reshape(1, grp, 1, 1, grp))
    la = la.reshape(n2 // grp, grp * n1x2, n1h * grp)
    once = pl.Buffered(1)
    return pl.pallas_call(
        functools.partial(_hy_conv_kernel, n1=n1),
        grid=(c // LANES, b),
        in_specs=[
            pl.BlockSpec((1, l, LANES), lambda j, i: (i, 0, j)),
            pl.BlockSpec((2, 2 * l, LANES), lambda j, i: (0, 0, j), pipeline_mode=once),
            pl.BlockSpec(la.shape, lambda j, i: (0, 0, 0), pipeline_mode=once),
            pl.BlockSpec(fb.shape, lambda j, i: (0, 0), pipeline_mode=once),
            pl.BlockSpec(gb.shape, lambda j, i: (0, 0, 0), pipeline_mode=once),
            pl.BlockSpec(fc.shape, lambda j, i: (0, 0), pipeline_mode=once),
        ],
        out_specs=pl.BlockSpec((1, l, LANES), lambda j, i: (i, 0, j)),
        out_shape=jax.ShapeDtypeStruct((b, l, c), F32),
        scratch_shapes=[pltpu.VMEM((n2 * _conv_pitch(n1), LANES), F32)],
        compiler_params=_cparams("arbitrary", "arbitrary"),
        name="hy_conv",
    )(u, h, la, fb, gb, fc)


def _hyena_branch(z, l, conv_w, conv_b, w1, b1, w2, b2, w3, freq, skip):
    c = HY_WIDTH
    n = 2 * l
    fa, fb, gb, fc = _fft_consts(l)
    filt = _hy_filters(l, w1, b1, w2, b2, w3, freq, skip)
    fa_spec = _fft_a(filt, fa).reshape(2, 2, n, c)
    h = _fft_bf(fa_spec, fb)
    u, e = _hy_prep(z, conv_w, conv_b)
    return _hy_conv(u, h, fa, fb, gb, fc), e


def _gdn_prep_kernel(zc_ref, zp_ref, zn_ref, cw_ref, q_ref, k_ref, v_ref, buf_ref):
    t = pl.program_id(1)
    nt = pl.num_programs(1)
    tl = zc_ref.shape[1]
    dh = GDN_HEAD_DIM
    hr = HALO_ROWS
    _fill_halo_buf(buf_ref, zc_ref, zp_ref, zn_ref, t, nt)
    outs = (q_ref, k_ref, v_ref)
    for p in range(3):
        for h in range(GDN_HEADS):
            lo = p * GDN_WIDTH + h * dh
            acc = buf_ref[pl.ds(hr - 2, tl), lo:lo + dh] * cw_ref[0:1, lo:lo + dh]
            for j in range(1, 5):
                acc = acc + buf_ref[pl.ds(hr - 2 + j, tl), lo:lo + dh] * cw_ref[j:j + 1, lo:lo + dh]
            a = _silu(acc)
            if p < 2:
                a = a * lax.rsqrt(jnp.sum(a * a, axis=-1, keepdims=True) + EPS)
            if p == 0:
                a = a * (dh ** -0.5)
            outs[p][0, :, h * dh:(h + 1) * dh] = a.astype(BF16)


def _gdn_prep(z, conv_w, tl=256):
    b, l, _ = z.shape
    w = 3 * GDN_WIDTH
    cur, prev, nxt = _halo_specs(tl, w, l, E_QKV // w)
    o_spec = pl.BlockSpec((1, tl, GDN_WIDTH), lambda i, t: (i, t, 0))
    o_shape = jax.ShapeDtypeStruct((b, l, GDN_WIDTH), BF16)
    return pl.pallas_call(
        _gdn_prep_kernel,
        grid=(b, l // tl),
        in_specs=[cur, prev, nxt, pl.BlockSpec((5, w), lambda i, t: (0, 0))],
        out_specs=[o_spec, o_spec, o_spec],
        out_shape=[o_shape, o_shape, o_shape],
        scratch_shapes=[pltpu.VMEM((tl + 2 * HALO_ROWS, w), F32)],
        compiler_params=_cparams("parallel", "arbitrary"),
        name="gdn_prep",
    )(z, z, z, conv_w.astype(F32))


def _dot_nt(a, b):
    return lax.dot_general(a, b, (((1,), (1,)), ((), ())), preferred_element_type=F32)


def _dot_tn(a, b):
    return lax.dot_general(a, b, (((0,), (0,)), ((), ())), preferred_element_type=F32)


def _split3(x):
    a = x.astype(BF16)
    r = x - a.astype(F32)
    b = r.astype(BF16)
    c = (r - b.astype(F32)).astype(BF16)
    return a, b, c


def _gdn_scan_kernel(*refs, direction, final):
    if final:
        q_ref, k_ref, v_ref, zs_ref, al_ref, dt_ref, of_ref, gg_ref, ng_ref, o_ref, s_ref = refs
    else:
        q_ref, k_ref, v_ref, zs_ref, al_ref, dt_ref, o_ref, s_ref = refs
    c = GDN_CHUNK
    dh = GDN_HEAD_DIM

    @pl.when(pl.program_id(1) == 0)
    def _():
        s_ref[...] = jnp.zeros_like(s_ref)

    zs = zs_ref[0]
    beta_all = jax.nn.sigmoid(zs)
    gl = -jnp.exp(al_ref[...]) * jax.nn.softplus(zs + dt_ref[...])
    row = lax.broadcasted_iota(jnp.int32, (c, c), 0)
    col = lax.broadcasted_iota(jnp.int32, (c, c), 1)
    if direction == 0:
        incl, strict, last = col <= row, col < row, c - 1
    else:
        incl, strict, last = col >= row, col > row, 0
    tri = incl.astype(BF16)
    g1, g2, g3 = _split3(gl)
    gcum = _dot(tri, g1) + _dot(tri, g2) + _dot(tri, g3)
    gcum_t = jnp.concatenate([gcum, jnp.zeros_like(gcum)], axis=0).T
    eye = (row == col).astype(F32)

    heads = range(GDN_HEADS)
    hsl = [slice(h * dh, (h + 1) * dh) for h in heads]
    lns = [2 * GDN_HEADS + direction * GDN_HEADS + h for h in heads]
    beta = [beta_all[:, direction * GDN_HEADS + h:direction * GDN_HEADS + h + 1] for h in heads]
    gc = [gcum[:, ln:ln + 1] for ln in lns]
    glast = [gcum_t[ln:ln + 1, last:last + 1] for ln in lns]
    qh = [q_ref[0, :, s] for s in hsl]
    kh = [k_ref[0, :, s] for s in hsl]
    vh = [v_ref[0, :, s] for s in hsl]
    kk = [_dot_nt(kh[h], kh[h]) for h in heads]
    qk = [_dot_nt(qh[h], kh[h]) for h in heads]
    dmat = [jnp.exp(jnp.where(incl, gc[h] - gcum_t[lns[h]:lns[h] + 1, 0:c], -jnp.inf)) for h in heads]
    a = [jnp.where(strict, (beta[h] * kk[h]) * dmat[h], 0.0) for h in heads]
    tinv = [eye - a[h] for h in heads]
    ab = [a[h].astype(BF16) for h in heads]
    p = [_dot(ab[h], ab[h]) for h in heads]
    for j in range(5):
        pb = [p[h].astype(BF16) for h in heads]
        tinv = [tinv[h] + _dot(tinv[h].astype(BF16), pb[h]) for h in heads]
        if j < 4:
            p = [_dot(pb[h], pb[h]) for h in heads]
    eg = [jnp.exp(gc[h]) for h in heads]
    kf = [kh[h].astype(F32) for h in heads]
    rhs = [jnp.concatenate([vh[h].astype(F32) * beta[h], kf[h] * (beta[h] * eg[h])], axis=1).astype(BF16) for h in heads]
    uw = [_dot(tinv[h].astype(BF16), rhs[h]) for h in heads]
    s_old = [s_ref[h] for h in heads]
    lhs = [jnp.concatenate([uw[h][:, dh:], qh[h].astype(F32) * eg[h]], axis=0).astype(BF16) for h in heads]
    ws = [_dot(lhs[h], s_old[h].astype(BF16)) for h in heads]
    vnb = [(uw[h][:, :dh] - ws[h][:c]).astype(BF16) for h in heads]
    o = [ws[h][c:] + _dot((qk[h] * dmat[h]).astype(BF16), vnb[h]) for h in heads]
    kd = [(kf[h] * jnp.exp(glast[h] - gc[h])).astype(BF16) for h in heads]
    for h in heads:
        s_ref[h] = s_old[h] * jnp.exp(glast[h]) + _dot_tn(kd[h], vnb[h])
    for h in heads:
        if final:
            tot = of_ref[0, :, hsl[h]] + o[h]
            y = tot * lax.rsqrt(jnp.mean(tot * tot, axis=-1, keepdims=True) + EPS) * ng_ref[...]
            o_ref[0, :, hsl[h]] = (y * _silu(gg_ref[0, :, hsl[h]].astype(F32))).astype(o_ref.dtype)
        else:
            o_ref[0, :, hsl[h]] = o[h].astype(o_ref.dtype)


def _gdn_scan(q, k, v, zs, a_log, dt_bias, direction, o_fwd=None, z=None, gate_blk=None, norm_g=None):
    b, l, w = q.shape
    c = GDN_CHUNK
    n = l // c
    final = o_fwd is not None
    if direction == 0:
        cmap = lambda i, t: (i, t, 0)
    else:
        cmap = lambda i, t: (i, n - 1 - t, 0)
    blk = pl.BlockSpec((1, c, w), cmap)
    vec = pl.BlockSpec((1, LANES), lambda i, t: (0, 0))
    pad = lambda x: jnp.pad(x.reshape(1, -1).astype(F32), ((0, 0), (2 * GDN_HEADS, LANES - 4 * GDN_HEADS)))
    in_specs = [blk, blk, blk, pl.BlockSpec((1, c, LANES), cmap), vec, vec]
    args = [q, k, v, zs, pad(a_log), pad(dt_bias)]
    if final:
        if direction == 0:
            gmap = lambda i, t: (i, t, gate_blk)
        else:
            gmap = lambda i, t: (i, n - 1 - t, gate_blk)
        in_specs += [blk, pl.BlockSpec((1, c, w), gmap), pl.BlockSpec((1, GDN_HEAD_DIM), lambda i, t: (0, 0))]
        args += [o_fwd, z, norm_g.reshape(1, GDN_HEAD_DIM).astype(F32)]
    return pl.pallas_call(
        functools.partial(_gdn_scan_kernel, direction=direction, final=final),
        grid=(b, n),
        in_specs=in_specs,
        out_specs=blk,
        out_shape=jax.ShapeDtypeStruct((b, l, w), BF16 if final else F32),
        scratch_shapes=[pltpu.VMEM((GDN_HEADS, GDN_HEAD_DIM, GDN_HEAD_DIM), F32)],
        compiler_params=_cparams("parallel", "arbitrary"),
        name="gdn_scan_bwd" if direction else "gdn_scan_fwd",
    )(*args)


def _gdn_branch(z, zs, conv_w, a_log, dt_bias, norm_g, gate_blk):
    q, k, v = _gdn_prep(z, conv_w)
    o_f = _gdn_scan(q, k, v, zs, a_log, dt_bias, 0)
    return _gdn_scan(q, k, v, zs, a_log, dt_bias, 1, o_fwd=o_f, z=z, gate_blk=gate_blk, norm_g=norm_g)


def _xattn_kernel(q_ref, g_ref, kv_ref, o_ref):
    dh = X_HEAD_DIM
    for h in range(X_HEADS):
        hs = slice(h * dh, (h + 1) * dh)
        qh = q_ref[0, :, hs].astype(BF16)
        kh = kv_ref[0, :, hs]
        vh = kv_ref[0, :, X_WIDTH + h * dh:X_WIDTH + (h + 1) * dh]
        s = _dot_nt(qh, kh) * (dh ** -0.5)
        p = jnp.exp(s - jnp.max(s, axis=-1, keepdims=True))
        den = jnp.sum(p, axis=-1, keepdims=True)
        o = _dot(p.astype(BF16), vh) / den
        o_ref[0, :, hs] = (o * _silu(g_ref[0, :, hs].astype(F32))).astype(o_ref.dtype)


def _xattn(z, kv, q_blk, g_blk, tq=512):
    b, l, _ = z.shape
    m = kv.shape[1]
    w = X_WIDTH
    return pl.pallas_call(
        _xattn_kernel,
        grid=(b, l // tq),
        in_specs=[
            pl.BlockSpec((1, tq, w), lambda i, t: (i, t, q_blk)),
            pl.BlockSpec((1, tq, w), lambda i, t: (i, t, g_blk)),
            pl.BlockSpec((1, m, 2 * w), lambda i, t: (i, 0, 0)),
        ],
        out_specs=pl.BlockSpec((1, tq, w), lambda i, t: (i, t, 0)),
        out_shape=jax.ShapeDtypeStruct((b, l, w), BF16),
        compiler_params=_cparams("parallel", "arbitrary"),
        name="xattn",
    )(z, z, kv)


def _out_proj_kernel(*refs, widths, gated):
    nparts = len(widths)
    y_refs = refs[:nparts]
    if gated:
        e_ref, w_ref, g_ref, x_ref, o_ref = refs[nparts:]
    else:
        w_ref, g_ref, x_ref, o_ref = refs[nparts:]
    d = o_ref.shape[1]
    cw = 256
    ys = [y_ref[...] for y_ref in y_refs]
    if gated:
        ys[0] = ys[0] * e_ref[...]
    ys = [y.astype(BF16) for y in ys]
    ssq = jnp.zeros((o_ref.shape[0], 1), F32)
    for c0 in range(0, d, cw):
        acc = None
        off = 0
        for y, wd in zip(ys, widths):
            part = _dot(y, w_ref[off:off + wd, c0:c0 + cw])
            acc = part if acc is None else acc + part
            off += wd
        ssq = ssq + jnp.sum(acc * acc, axis=-1, keepdims=True)
        o_ref[:, c0:c0 + cw] = acc
    r = lax.rsqrt(ssq * (1.0 / d) + EPS)
    for c0 in range(0, d, cw):
        o_ref[:, c0:c0 + cw] = x_ref[:, c0:c0 + cw] + o_ref[:, c0:c0 + cw] * r * g_ref[:, c0:c0 + cw]


def _out_proj(parts, w_out, post_g, x2d, gate=None, tm=512):
    t, d = x2d.shape
    widths = tuple(int(p.shape[1]) for p in parts)
    kdim = sum(widths)
    gates = [] if gate is None else [gate]
    return pl.pallas_call(
        functools.partial(_out_proj_kernel, widths=widths, gated=gate is not None),
        grid=(t // tm,),
        in_specs=[pl.BlockSpec((tm, wd), lambda i: (i, 0)) for wd in widths]
        + [pl.BlockSpec((tm, widths[0]), lambda i: (i, 0)) for _ in gates] + [
            pl.BlockSpec((kdim, d), lambda i: (0, 0)),
            pl.BlockSpec((1, d), lambda i: (0, 0)),
            pl.BlockSpec((tm, d), lambda i: (i, 0)),
        ],
        out_specs=pl.BlockSpec((tm, d), lambda i: (i, 0)),
        out_shape=jax.ShapeDtypeStruct((t, d), F32),
        compiler_params=_cparams("parallel"),
        name="out_proj",
    )(*parts, *gates, w_out.astype(BF16), post_g.reshape(1, d).astype(F32), x2d)


def _mem_kv(mem, mem_g, w_mem_kv):
    b, m, d = mem.shape
    kv = _norm_matmul(mem.reshape(b * m, d), mem_g, w_mem_kv, 1024, out_dtype=BF16, name="mem_kv")
    return kv.reshape(b, m, 2 * X_WIDTH)


E_QKV, E_HY, E_GHY, E_GGDN, E_XQ, E_GX, E_BIG = 0, 3072, 6144, 7168, 8192, 8704, 9216


def _even_layer(x, mem, pre_g, post_g, w_in, w_out, hy_conv_w, hy_conv_b, hy_w1, hy_b1, hy_w2, hy_b2, hy_w3, hy_freq,
                hy_skip, gdn_conv_w, gdn_a_log, gdn_dt_bias, gdn_norm_g, mem_g, w_mem_kv):
    b, l, d = x.shape
    x2d = x.reshape(b * l, d)
    w_big = jnp.concatenate([w_in[:, 4096:7168], w_in[:, 0:3072], w_in[:, 3072:4096], w_in[:, 7168:8192],
                             w_in[:, 8224:8736], w_in[:, 8736:9248]], axis=1)
    w_small = jnp.pad(w_in[:, 8192:8224], ((0, 0), (0, LANES - 4 * GDN_HEADS)))
    z = _norm_matmul(x2d, pre_g, w_big, 1024, out_dtype=BF16, tm=1024, name="even_in").reshape(b, l, E_BIG)
    zs = _norm_matmul(x2d, pre_g, w_small, LANES, name="even_in_gates").reshape(b, l, LANES)
    conv, e = _hyena_branch(z, l, hy_conv_w, hy_conv_b, hy_w1, hy_b1, hy_w2, hy_b2, hy_w3, hy_freq, hy_skip)
    y_b = _gdn_branch(z, zs, gdn_conv_w, gdn_a_log, gdn_dt_bias, gdn_norm_g, E_GGDN // GDN_WIDTH)
    kv = _mem_kv(mem, mem_g, w_mem_kv)
    y_x = _xattn(z, kv, E_XQ // X_WIDTH, E_GX // X_WIDTH)
    t = b * l
    out = _out_proj([conv.reshape(t, -1), y_b.reshape(t, -1), y_x.reshape(t, -1)], w_out, post_g, x2d,
                    gate=e.reshape(t, -1))
    return out.reshape(b, l, d)


DIL_PATTERNS = ((128, 1), (512, 4), (2048, 16))
N_DIL = len(DIL_PATTERNS)
DIL_HEADS = 4
DIL_HEAD_DIM = 128
DIL_WIDTH = DIL_HEADS * DIL_HEAD_DIM
SWA_Q_HEADS = 16
SWA_KV_HEADS = 2
SWA_HEAD_DIM = 64
SWA_WIDTH = SWA_Q_HEADS * SWA_HEAD_DIM
SWA_HALF_WINDOW = 128
O_CQKV, O_GC, O_DQ, O_GD, O_XQ, O_GX, O_DKV, O_ALL = 0, 4608, 5120, 6144, 7168, 7680, 8192, 8448


def _rope_tables(l, dh):
    half = dh // 8
    inv = ROPE_THETA ** (-jnp.arange(half, dtype=F32) / half)
    ang = jnp.arange(l, dtype=F32)[:, None] * inv[None, :]
    cos, sin = jnp.cos(ang), jnp.sin(ang)
    one = jnp.ones((l, dh - 2 * half), F32)
    zero_h = jnp.zeros((l, half), F32)
    zero_r = jnp.zeros((l, dh - 2 * half), F32)
    c = jnp.concatenate([cos, cos, one], axis=1)
    sa = jnp.concatenate([-sin, zero_h, zero_r], axis=1)
    sb = jnp.concatenate([zero_h, sin, zero_r], axis=1)
    rep = LANES // dh
    return tuple(jnp.tile(t, (1, rep)) for t in (c, sa, sb))


def _odd_prep_kernel(c_ref, dq_ref, dkv_ref, c1_ref, a1_ref, b1_ref, c2_ref, a2_ref, b2_ref,
                     cq_ref, ck_ref, cv_ref, dqe_ref, dk_ref, dv_ref):
    tl = c_ref.shape[1]
    lane = lax.broadcasted_iota(jnp.int32, (tl, LANES), 1)
    low = lane < SWA_HEAD_DIM
    c1, a1, b1 = c1_ref[...], a1_ref[...], b1_ref[...]
    c2, a2, b2 = c2_ref[...], a2_ref[...], b2_ref[...]
    h1 = DIL_HEAD_DIM // 8
    h2 = SWA_HEAD_DIM // 8

    def rope1(x):
        return x * c1 + pltpu.roll(x, LANES - h1, 1) * a1 + pltpu.roll(x, h1, 1) * b1

    def rope2(x):
        return x * c2 + pltpu.roll(x, LANES - h2, 1) * a2 + pltpu.roll(x, h2, 1) * b2

    nblk = N_DIL * DIL_HEADS
    for j in range(nblk):
        ls = slice(j * LANES, (j + 1) * LANES)
        cq_ref[0, :, ls] = rope1(c_ref[0, :, ls].astype(F32)).astype(BF16)
        ck_ref[0, :, ls] = rope1(c_ref[0, :, nblk * LANES + j * LANES:nblk * LANES + (j + 1) * LANES].astype(F32)).astype(BF16)
        cv_ref[0, :, ls] = c_ref[0, :, 2 * nblk * LANES + j * LANES:2 * nblk * LANES + (j + 1) * LANES].astype(BF16)
    for j in range(SWA_Q_HEADS // 2):
        xr = rope2(dq_ref[0, :, j * LANES:(j + 1) * LANES].astype(F32))
        dqe_ref[0, :, (2 * j) * LANES:(2 * j + 1) * LANES] = jnp.where(low, xr, 0.0).astype(BF16)
        dqe_ref[0, :, (2 * j + 1) * LANES:(2 * j + 2) * LANES] = jnp.where(low, pltpu.roll(xr, SWA_HEAD_DIM, 1), 0.0).astype(BF16)
    kr = rope2(dkv_ref[0, :, 0:LANES].astype(F32))
    vv = dkv_ref[0, :, LANES:2 * LANES].astype(F32)
    kr_sw = pltpu.roll(kr, SWA_HEAD_DIM, 1)
    vv_sw = pltpu.roll(vv, SWA_HEAD_DIM, 1)
    dk_ref[0, :, 0:LANES] = jnp.where(low, kr, 0.0).astype(BF16)
    dk_ref[0, :, LANES:2 * LANES] = jnp.where(low, kr_sw, 0.0).astype(BF16)
    dv_ref[0, :, 0:LANES] = jnp.where(low, vv, vv_sw).astype(BF16)
    dv_ref[0, :, LANES:2 * LANES] = jnp.where(low, vv_sw, vv).astype(BF16)


def _odd_prep(z, tl=256):
    b, l, _ = z.shape
    wc = 3 * N_DIL * DIL_WIDTH
    t1 = _rope_tables(l, DIL_HEAD_DIM)
    t2 = _rope_tables(l, SWA_HEAD_DIM)
    tab = pl.BlockSpec((tl, LANES), lambda i, t: (t, 0))

    def spec(w):
        return pl.BlockSpec((1, tl, w), lambda i, t: (i, t, 0))

    def shape(w):
        return jax.ShapeDtypeStruct((b, l, w), BF16)

    wd = N_DIL * DIL_WIDTH
    return pl.pallas_call(
        _odd_prep_kernel,
        grid=(b, l // tl),
        in_specs=[
            pl.BlockSpec((1, tl, wc), lambda i, t: (i, t, 0)),
            pl.BlockSpec((1, tl, SWA_WIDTH), lambda i, t: (i, t, O_DQ // SWA_WIDTH)),
            pl.BlockSpec((1, tl, 2 * LANES), lambda i, t: (i, t, O_DKV // (2 * LANES))),
            tab, tab, tab, tab, tab, tab,
        ],
        out_specs=[spec(wd), spec(wd), spec(wd), spec(2 * SWA_WIDTH), spec(2 * LANES), spec(2 * LANES)],
        out_shape=[shape(wd), shape(wd), shape(wd), shape(2 * SWA_WIDTH), shape(2 * LANES), shape(2 * LANES)],
        compiler_params=_cparams("parallel", "arbitrary"),
        name="odd_prep",
    )(z, z, z, *t1, *t2)


def _band_mask(tq, half, ls):
    p0 = pl.program_id(1) * tq
    i = lax.broadcasted_iota(jnp.int32, (tq, tq + 2 * half), 0)
    j = lax.broadcasted_iota(jnp.int32, (tq, tq + 2 * half), 1)
    kpos = p0 - half + j
    return (j >= i) & (j - i <= 2 * half) & (kpos >= 0) & (kpos < ls)


def _band_attn_kernel(q_ref, kp_ref, kc_ref, kn_ref, vp_ref, vc_ref, vn_ref, o_ref, lse_ref, *, half, ls):
    tq = q_ref.shape[1]
    dh = DIL_HEAD_DIM
    valid = _band_mask(tq, half, ls)
    lane = lax.broadcasted_iota(jnp.int32, (tq, LANES), 1)
    lse_all = jnp.zeros((tq, LANES), F32)
    heads = range(DIL_HEADS)
    hsl = [slice(h * dh, (h + 1) * dh) for h in heads]
    kall = [jnp.concatenate([kp_ref[0, :, s], kc_ref[0, :, s], kn_ref[0, :, s]], axis=0) for s in hsl]
    vall = [jnp.concatenate([vp_ref[0, :, s], vc_ref[0, :, s], vn_ref[0, :, s]], axis=0) for s in hsl]
    sc = [jnp.where(valid, _dot_nt(q_ref[0, :, hsl[h]], kall[h]) * (dh ** -0.5), -jnp.inf) for h in heads]
    m = [jnp.max(sc[h], axis=-1, keepdims=True) for h in heads]
    p = [jnp.exp(sc[h] - m[h]) for h in heads]
    den = [jnp.sum(p[h], axis=-1, keepdims=True) for h in heads]
    o = [_dot(p[h].astype(BF16), vall[h]) / den[h] for h in heads]
    for h in heads:
        o_ref[0, :, hsl[h]] = o[h].astype(o_ref.dtype)
        lse_all = jnp.where(lane == h, m[h] + jnp.log(den[h]), lse_all)
    lse_ref[0] = lse_all


def _band_specs(tq, half, ls, w, col):
    r = tq // half
    nb = ls // half
    cur = pl.BlockSpec((1, tq, w), lambda i, t: (i, t, col))
    prev = pl.BlockSpec((1, half, w), lambda i, t: (i, jnp.maximum(t * r - 1, 0), col))
    nxt = pl.BlockSpec((1, half, w), lambda i, t: (i, jnp.minimum((t + 1) * r, nb - 1), col))
    return cur, prev, nxt


def _band_attn(q, k, v, half):
    n, ls, w = q.shape
    tq = min(256, ls)
    cur, prev, nxt = _band_specs(tq, half, ls, w, 0)
    return pl.pallas_call(
        functools.partial(_band_attn_kernel, half=half, ls=ls),
        grid=(n, ls // tq),
        in_specs=[cur, prev, cur, nxt, prev, cur, nxt],
        out_specs=[cur, pl.BlockSpec((1, tq, LANES), lambda i, t: (i, t, 0))],
        out_shape=[jax.ShapeDtypeStruct((n, ls, w), F32), jax.ShapeDtypeStruct((n, ls, LANES), F32)],
        compiler_params=_cparams("parallel", "arbitrary"),
        name="band_attn",
    )(q, k, k, k, v, v, v)


def _dil_merge_kernel(o0_ref, o1_ref, o2_ref, l0_ref, l1_ref, l2_ref, g_ref, y_ref):
    dh = DIL_HEAD_DIM
    o_refs = (o0_ref, o1_ref, o2_ref)
    for h in range(DIL_HEADS):
        hs = slice(h * dh, (h + 1) * dh)
        ls = [r[0, :, h:h + 1] for r in (l0_ref, l1_ref, l2_ref)]
        m = jnp.maximum(jnp.maximum(ls[0], ls[1]), ls[2])
        ws = [jnp.exp(x - m) for x in ls]
        den = ws[0] + ws[1] + ws[2]
        y = (ws[0] / den) * o_refs[0][0, :, hs] + (ws[1] / den) * o_refs[1][0, :, hs] + (ws[2] / den) * o_refs[2][0, :, hs]
        y_ref[0, :, hs] = (y * _silu(g_ref[0, :, hs].astype(F32))).astype(y_ref.dtype)


def _dil_merge(outs, lses, z, tl=512):
    b, l, w = outs[0].shape
    o_spec = pl.BlockSpec((1, tl, w), lambda i, t: (i, t, 0))
    l_spec = pl.BlockSpec((1, tl, LANES), lambda i, t: (i, t, 0))
    return pl.pallas_call(
        _dil_merge_kernel,
        grid=(b, l // tl),
        in_specs=[o_spec, o_spec, o_spec, l_spec, l_spec, l_spec, pl.BlockSpec((1, tl, w), lambda i, t: (i, t, O_GC // DIL_WIDTH))],
        out_specs=o_spec,
        out_shape=jax.ShapeDtypeStruct((b, l, w), BF16),
        compiler_params=_cparams("parallel", "arbitrary"),
        name="dil_merge",
    )(*outs, *lses, z)


def _dilated_branch(cq, ck, cv, z):
    b, l, _ = cq.shape
    w = DIL_WIDTH
    outs, lses = [], []
    for gi, (window, d) in enumerate(DIL_PATTERNS):
        half = window // (2 * d)
        ls = l // d

        def to_res(x):
            x = x[:, :, gi * w:(gi + 1) * w]
            if d == 1:
                return x
            return jnp.swapaxes(x.reshape(b, ls, d, w), 1, 2).reshape(b * d, ls, w)

        def from_res(x):
            if d == 1:
                return x
            return jnp.swapaxes(x.reshape(b, d, ls, x.shape[-1]), 1, 2).reshape(b, l, x.shape[-1])

        o, lse = _band_attn(to_res(cq), to_res(ck), to_res(cv), half)
        outs.append(from_res(o))
        lses.append(from_res(lse))
    return _dil_merge(outs, lses, z)


def _swa_kernel(q_ref, kp_ref, kc_ref, kn_ref, vp_ref, vc_ref, vn_ref, g_ref, sink_ref, o_ref, *, half, ls):
    tq = q_ref.shape[1]
    dh = SWA_HEAD_DIM
    valid = _band_mask(tq, half, ls)
    low = lax.broadcasted_iota(jnp.int32, (tq, LANES), 1) < dh
    grp = SWA_Q_HEADS // SWA_KV_HEADS
    for g in range(SWA_KV_HEADS):
        gs = slice(g * LANES, (g + 1) * LANES)
        kall = jnp.concatenate([kp_ref[0, :, gs], kc_ref[0, :, gs], kn_ref[0, :, gs]], axis=0)
        vall = jnp.concatenate([vp_ref[0, :, gs], vc_ref[0, :, gs], vn_ref[0, :, gs]], axis=0)
        for jp in range(grp // 2):
            pair = []
            for hf in range(2):
                h = g * grp + 2 * jp + hf
                s = jnp.where(valid, _dot_nt(q_ref[0, :, h * LANES:(h + 1) * LANES], kall) * (dh ** -0.5), -jnp.inf)
                sk = sink_ref[:, h:h + 1]
                m = jnp.maximum(jnp.max(s, axis=-1, keepdims=True), sk)
                p = jnp.exp(s - m)
                den = jnp.sum(p, axis=-1, keepdims=True) + jnp.exp(sk - m)
                pair.append(_dot(p.astype(BF16), vall) / den)
            blk = (g * grp) // 2 + jp
            bs = slice(blk * LANES, (blk + 1) * LANES)
            y = jnp.where(low, pair[0], pair[1]) * _silu(g_ref[0, :, bs].astype(F32))
            o_ref[0, :, bs] = y.astype(o_ref.dtype)


def _swa_branch(dqe, dk, dv, z, sink, tq=256):
    b, l, _ = dqe.shape
    half = SWA_HALF_WINDOW
    _, prev, nxt = _band_specs(tq, half, l, 2 * LANES, 0)
    cur = pl.BlockSpec((1, tq, 2 * LANES), lambda i, t: (i, t, 0))
    sink_p = jnp.pad(sink.reshape(1, SWA_Q_HEADS).astype(F32), ((0, 0), (0, LANES - SWA_Q_HEADS)))
    return pl.pallas_call(
        functools.partial(_swa_kernel, half=half, ls=l),
        grid=(b, l // tq),
        in_specs=[
            pl.BlockSpec((1, tq, 2 * SWA_WIDTH), lambda i, t: (i, t, 0)),
            prev, cur, nxt, prev, cur, nxt,
            pl.BlockSpec((1, tq, SWA_WIDTH), lambda i, t: (i, t, O_GD // SWA_WIDTH)),
            pl.BlockSpec((1, LANES), lambda i, t: (0, 0)),
        ],
        out_specs=pl.BlockSpec((1, tq, SWA_WIDTH), lambda i, t: (i, t, 0)),
        out_shape=jax.ShapeDtypeStruct((b, l, SWA_WIDTH), BF16),
        compiler_params=_cparams("parallel", "arbitrary"),
        name="swa",
    )(dqe, dk, dk, dk, dv, dv, dv, z, sink_p)


def _odd_layer(x, mem, pre_g, post_g, w_in, w_out, swa_sink, mem_g, w_mem_kv):
    b, l, d = x.shape
    x2d = x.reshape(b * l, d)
    w_re = jnp.concatenate([w_in[:, 0:6144], w_in[:, 6400:8448], w_in[:, 6144:6400]], axis=1)
    z = _norm_matmul(x2d, pre_g, w_re, 1408, out_dtype=BF16, tm=1024, name="odd_in").reshape(b, l, O_ALL)
    cq, ck, cv, dqe, dk, dv = _odd_prep(z)
    y_c = _dilated_branch(cq, ck, cv, z)
    y_d = _swa_branch(dqe, dk, dv, z, swa_sink)
    kv = _mem_kv(mem, mem_g, w_mem_kv)
    y_x = _xattn(z, kv, O_XQ // X_WIDTH, O_GX // X_WIDTH)
    t = b * l
    out = _out_proj([y_c.reshape(t, -1), y_d.reshape(t, -1), y_x.reshape(t, -1)], w_out, post_g, x2d)
    return out.reshape(b, l, d)


def _trunk(x, mem, even_params, odd_params):
    x = _even_layer(x, mem, *[p[0] for p in even_params])
    return _odd_layer(x, mem, *[p[0] for p in odd_params])


def kernel(x_prompt, x_sample, mem_prompt, mem_sample, e_pre_g, e_post_g, e_w_in, e_w_out, hy_conv_w, hy_conv_b,
           hy_filt_w1, hy_filt_b1, hy_filt_w2, hy_filt_b2, hy_filt_w3, hy_freq, hy_skip, gdn_conv_w, gdn_A_log,
           gdn_dt_bias, gdn_norm_g, e_mem_g, e_w_mem_kv, o_pre_g, o_post_g, o_w_in, o_w_out, swa_sink, o_mem_g,
           o_w_mem_kv):
    even_params = (e_pre_g, e_post_g, e_w_in, e_w_out, hy_conv_w, hy_conv_b, hy_filt_w1, hy_filt_b1, hy_filt_w2,
                   hy_filt_b2, hy_filt_w3, hy_freq, hy_skip, gdn_conv_w, gdn_A_log, gdn_dt_bias, gdn_norm_g,
                   e_mem_g, e_w_mem_kv)
    odd_params = (o_pre_g, o_post_g, o_w_in, o_w_out, swa_sink, o_mem_g, o_w_mem_kv)
    y_prompt = _trunk(x_prompt, mem_prompt, even_params, odd_params)
    y_sample = _trunk(x_sample, mem_sample, even_params, odd_params)
    return (y_prompt, y_sample)
```

```python
import functools
import math

import jax
import jax.numpy as jnp
import numpy as np
from jax import lax
from jax.experimental import pallas as pl
from jax.experimental.pallas import tpu as pltpu

D_MODEL = 1024
EPS = 1e-6
ROPE_THETA = 500000.0
HY_WIDTH = 1024
HY_EMB = 33
HY_BANDS = (HY_EMB - 1) // 2
HY_FILT_HIDDEN = 64
HY_TARGET = 1e-2
HY_FAST_PCT = 0.3
HY_SLOW_PCT = 1.5
GDN_HEADS = 8
GDN_HEAD_DIM = 128
GDN_WIDTH = GDN_HEADS * GDN_HEAD_DIM
GDN_CHUNK = 64
X_HEADS = 4
X_HEAD_DIM = 128
X_WIDTH = X_HEADS * X_HEAD_DIM

LANES = 128
FFT_N2 = 128
VMEM_LIMIT_BYTES = 48 * 1024 * 1024

BF16 = jnp.bfloat16
F32 = jnp.float32


def _cparams(*sem):
    return pltpu.CompilerParams(dimension_semantics=sem, vmem_limit_bytes=VMEM_LIMIT_BYTES)


def _dot(a, b):
    return jnp.dot(a, b, preferred_element_type=F32)


def _silu(x):
    return x * jax.nn.sigmoid(x)


def _norm_matmul_kernel(x_ref, g_ref, w_ref, o_ref, xn_ref):
    @pl.when(pl.program_id(1) == 0)
    def _():
        x = x_ref[...]
        ms = jnp.mean(x * x, axis=-1, keepdims=True)
        xn_ref[...] = (x * lax.rsqrt(ms + EPS) * g_ref[...]).astype(BF16)

    o_ref[...] = _dot(xn_ref[...], w_ref[...]).astype(o_ref.dtype)


def _norm_matmul(x2d, g, w, tn, out_dtype=F32, tm=512, name="norm_matmul"):
    t, d = x2d.shape
    n = w.shape[1]
    assert t % tm == 0 and n % tn == 0
    return pl.pallas_call(
        _norm_matmul_kernel,
        grid=(t // tm, n // tn),
        in_specs=[
            pl.BlockSpec((tm, d), lambda i, j: (i, 0)),
            pl.BlockSpec((1, d), lambda i, j: (0, 0)),
            pl.BlockSpec((d, tn), lambda i, j: (0, j)),
        ],
        out_specs=pl.BlockSpec((tm, tn), lambda i, j: (i, j)),
        out_shape=jax.ShapeDtypeStruct((t, n), out_dtype),
        scratch_shapes=[pltpu.VMEM((tm, d), BF16)],
        compiler_params=_cparams("parallel", "arbitrary"),
        name=name,
    )(x2d, g.reshape(1, d).astype(F32), w.astype(BF16))


HALO_ROWS = 16


def _fill_halo_buf(buf_ref, zc_ref, zp_ref, zn_ref, t, nt):
    tl = zc_ref.shape[1]
    hr = HALO_ROWS
    buf_ref[0:hr, :] = jnp.where(t > 0, zp_ref[0].astype(F32), 0.0)
    buf_ref[hr:hr + tl, :] = zc_ref[0].astype(F32)
    buf_ref[hr + tl:2 * hr + tl, :] = jnp.where(t < nt - 1, zn_ref[0].astype(F32), 0.0)


def _halo_specs(tl, w, l, col):
    r = tl // HALO_ROWS
    nb = l // HALO_ROWS
    cur = pl.BlockSpec((1, tl, w), lambda i, t: (i, t, col))
    prev = pl.BlockSpec((1, HALO_ROWS, w), lambda i, t: (i, jnp.maximum(t * r - 1, 0), col))
    nxt = pl.BlockSpec((1, HALO_ROWS, w), lambda i, t: (i, jnp.minimum((t + 1) * r, nb - 1), col))
    return cur, prev, nxt


def _hy_prep_kernel(zc_ref, zp_ref, zn_ref, g_ref, cw_ref, cb_ref, u_ref, e_ref, buf_ref):
    t = pl.program_id(1)
    nt = pl.num_programs(1)
    tl = zc_ref.shape[1]
    c = HY_WIDTH
    hr = HALO_ROWS
    _fill_halo_buf(buf_ref, zc_ref, zp_ref, zn_ref, t, nt)
    cw = 256
    for c0 in range(0, c, cw):
        parts = []
        for p in range(3):
            lo = p * c + c0
            acc = cb_ref[:, lo:lo + cw]
            for j in range(3):
                acc = acc + buf_ref[pl.ds(hr - 1 + j, tl), lo:lo + cw] * cw_ref[j:j + 1, lo:lo + cw]
            parts.append(acc)
        x0, x1, v = parts
        u_ref[0, :, c0:c0 + cw] = (v * x1).astype(u_ref.dtype)
        e_ref[0, :, c0:c0 + cw] = (x0 * _silu(g_ref[0, :, c0:c0 + cw].astype(F32))).astype(e_ref.dtype)


def _hy_prep(z, conv_w, conv_b, tl=256):
    b, l, _ = z.shape
    c = HY_WIDTH
    assert l % tl == 0
    cur, prev, nxt = _halo_specs(tl, 3 * c, l, E_HY // (3 * c))
    return pl.pallas_call(
        _hy_prep_kernel,
        grid=(b, l // tl),
        in_specs=[
            cur, prev, nxt,
            pl.BlockSpec((1, tl, c), lambda i, t: (i, t, E_GHY // c)),
            pl.BlockSpec((3, 3 * c), lambda i, t: (0, 0)),
            pl.BlockSpec((1, 3 * c), lambda i, t: (0, 0)),
        ],
        out_specs=[
            pl.BlockSpec((1, tl, c), lambda i, t: (i, t, 0)),
            pl.BlockSpec((1, tl, c), lambda i, t: (i, t, 0)),
        ],
        out_shape=[jax.ShapeDtypeStruct((b, l, c), F32), jax.ShapeDtypeStruct((b, l, c), F32)],
        scratch_shapes=[pltpu.VMEM((tl + 2 * HALO_ROWS, 3 * c), F32)],
        compiler_params=_cparams("parallel", "arbitrary"),
        name="hy_prep",
    )(z, z, z, z, conv_w.astype(F32), conv_b.reshape(1, 3 * c).astype(F32))


def _hy_filter_tables(l):
    t = jnp.linspace(0.0, 1.0, l, dtype=F32)[:, None]
    w = (2.0 * math.pi / l) * jnp.arange(l, dtype=F32)[:, None]
    f = jnp.linspace(1e-4, HY_BANDS - 1, HY_BANDS, dtype=F32)[None, :]
    emb = jnp.concatenate([t, jnp.cos(f * w), -jnp.sin(f * w)], axis=-1)
    emb = jnp.pad(emb, ((0, 0), (0, LANES - HY_EMB)))
    deltas = jnp.abs(jnp.linspace(math.log(HY_TARGET) / HY_SLOW_PCT, math.log(HY_TARGET) / HY_FAST_PCT, HY_WIDTH, dtype=F32))
    decay = jnp.exp(-t * jnp.tile(deltas, 2)[None, :])
    return emb, decay


def _hy_filter_kernel(emb_ref, dec_ref, w1_ref, b1_ref, w2_ref, b2_ref, w3_ref, fr_ref, sk_ref, o_ref):
    c = HY_WIDTH
    hp = lax.Precision.HIGHEST
    fr = fr_ref[...]
    hid = jnp.sin(fr * (jnp.dot(emb_ref[...], w1_ref[...], precision=hp, preferred_element_type=F32) + b1_ref[...]))
    hid = jnp.sin(fr * (jnp.dot(hid, w2_ref[...], precision=hp, preferred_element_type=F32) + b2_ref[...]))
    tl = emb_ref.shape[0]
    row = lax.broadcasted_iota(jnp.int32, (tl, 1), 0) + pl.program_id(0) * tl
    first = row == 0
    cw = 512
    for c0 in range(0, 2 * c, cw):
        filt = jnp.dot(hid, w3_ref[:, c0:c0 + cw], precision=hp, preferred_element_type=F32) * dec_ref[:, c0:c0 + cw]
        if c0 < c:
            filt = jnp.where(first, filt + sk_ref[:, c0:c0 + cw], filt)
            o_ref[0, :, c0:c0 + cw] = filt.astype(o_ref.dtype)
        else:
            filt = jnp.where(first, 0.0, filt)
            o_ref[1, :, c0 - c:c0 - c + cw] = filt.astype(o_ref.dtype)


def _hy_filters(l, w1, b1, w2, b2, w3, freq, skip, tl=256):
    c = HY_WIDTH
    hdim = HY_FILT_HIDDEN
    emb, decay = _hy_filter_tables(l)
    w1p = jnp.pad(w1.astype(F32), ((0, LANES - HY_EMB), (0, 0)))
    return pl.pallas_call(
        _hy_filter_kernel,
        grid=(l // tl,),
        in_specs=[
            pl.BlockSpec((tl, LANES), lambda t: (t, 0)),
            pl.BlockSpec((tl, 2 * c), lambda t: (t, 0)),
            pl.BlockSpec((LANES, hdim), lambda t: (0, 0)),
            pl.BlockSpec((1, hdim), lambda t: (0, 0)),
            pl.BlockSpec((hdim, hdim), lambda t: (0, 0)),
            pl.BlockSpec((1, hdim), lambda t: (0, 0)),
            pl.BlockSpec((hdim, 2 * c), lambda t: (0, 0)),
            pl.BlockSpec((1, hdim), lambda t: (0, 0)),
            pl.BlockSpec((1, c), lambda t: (0, 0)),
        ],
        out_specs=pl.BlockSpec((2, tl, c), lambda t: (0, t, 0)),
        out_shape=jax.ShapeDtypeStruct((2, l, c), BF16),
        compiler_params=_cparams("arbitrary"),
        name="hy_filter",
    )(emb, decay, w1p, b1.reshape(1, hdim).astype(F32), w2.astype(F32), b2.reshape(1, hdim).astype(F32),
      w3.astype(F32), freq.reshape(1, hdim).astype(F32), skip.reshape(1, c).astype(F32))


def _fft_consts(l):
    n = 2 * l
    n2 = FFT_N2
    n1 = n // n2
    n1h = n1 // 2
    w0 = 2.0 * math.pi / n
    k1 = jnp.arange(n1, dtype=jnp.int32)
    m1 = jnp.arange(n1h, dtype=jnp.int32)
    m2 = jnp.arange(n2, dtype=jnp.int32)
    ea = (n2 * m1[None, None, :] * k1[None, :, None] + m2[:, None, None] * k1[None, :, None]) % n
    ang = ea.astype(F32) * w0
    fa = jnp.concatenate([jnp.cos(ang), -jnp.sin(ang)], axis=1).astype(BF16)
    eb = (n1 * m2[:, None] * m2[None, :]) % n
    angb = eb.astype(F32) * w0
    fr, fi = jnp.cos(angb), -jnp.sin(angb)
    fb = jnp.concatenate([jnp.concatenate([fr, -fi], axis=1), jnp.concatenate([fi, fr], axis=1)], axis=0).astype(BF16)
    eg = (n1 * m2[None, :, None] * m2[None, None, :] + m2[None, :, None] * k1[:, None, None]) % n
    angg = eg.astype(F32) * w0
    gr, gi = jnp.cos(angg), jnp.sin(angg)
    gb = jnp.concatenate([jnp.concatenate([gr, -gi], axis=2), jnp.concatenate([gi, gr], axis=2)], axis=1).astype(BF16)
    ec = (n2 * m1[:, None] * k1[None, :]) % n
    angc = ec.astype(F32) * w0
    fc = (jnp.concatenate([jnp.cos(angc), -jnp.sin(angc)], axis=1) * (1.0 / n)).astype(BF16)
    return fa, fb, gb, fc


def _fft_a_kernel(u_ref, f_ref, o_ref):
    tn2 = f_ref.shape[0]
    n1 = f_ref.shape[1] // 2
    c = u_ref.shape[2] // tn2
    for j in range(tn2):
        a = _dot(f_ref[j], u_ref[0, :, j * c:(j + 1) * c])
        o_ref[0, 0, :, j * c:(j + 1) * c] = a[:n1].astype(o_ref.dtype)
        o_ref[0, 1, :, j * c:(j + 1) * c] = a[n1:].astype(o_ref.dtype)


def _fft_a(u, fa, tn2=8):
    b, l, c = u.shape
    n2, n1x2, n1h = fa.shape
    n1 = n1x2 // 2
    uv = u.reshape(b, n1h, n2 * c)
    return pl.pallas_call(
        _fft_a_kernel,
        grid=(b, n2 // tn2),
        in_specs=[
            pl.BlockSpec((1, n1h, tn2 * c), lambda i, j: (i, 0, j)),
            pl.BlockSpec((tn2, n1x2, n1h), lambda i, j: (j, 0, 0)),
        ],
        out_specs=pl.BlockSpec((1, 2, n1, tn2 * c), lambda i, j: (i, 0, 0, j)),
        out_shape=jax.ShapeDtypeStruct((b, 2, n1, n2 * c), BF16),
        compiler_params=_cparams("parallel", "arbitrary"),
        name="fft_a",
    )(uv, fa)


def _fft_bf_kernel(a_ref, w_ref, o_ref):
    n2 = a_ref.shape[2]
    ct = a_ref.shape[3]
    cw = 256
    for c0 in range(0, ct, cw):
        xf = _dot(w_ref[...], a_ref[0, :, :, c0:c0 + cw].reshape(2 * n2, cw))
        xb = _dot(w_ref[...], a_ref[1, :, :, c0:c0 + cw].reshape(2 * n2, cw))
        o_ref[0, :, c0:c0 + cw] = xf[:n2] + xb[:n2]
        o_ref[1, :, c0:c0 + cw] = xf[n2:] - xb[n2:]


def _fft_bf(a, fb):
    _, _, n, c = a.shape
    n2 = FFT_N2
    return pl.pallas_call(
        _fft_bf_kernel,
        grid=(n // n2,),
        in_specs=[
            pl.BlockSpec((2, 2, n2, c), lambda k: (0, 0, k, 0)),
            pl.BlockSpec((2 * n2, 2 * n2), lambda k: (0, 0)),
        ],
        out_specs=pl.BlockSpec((2, n2, c), lambda k: (0, k, 0)),
        out_shape=jax.ShapeDtypeStruct((2, n, c), F32),
        compiler_params=_cparams("arbitrary"),
        name="fft_bf",
    )(a, fb)


SUBLANES = 8
CONV_UNROLL = 4


def _conv_pitch(n1):
    return 2 * n1 + SUBLANES


def _hy_conv_kernel(u_ref, h_ref, la_ref, w_ref, g_ref, fc_ref, o_ref, s_ref, *, n1):
    n2 = FFT_N2
    n1h = n1 // 2
    pitch = _conv_pitch(n1)
    grp = SUBLANES

    def stage_a(jj, carry):
        rows = [u_ref[0, pl.ds(pl.multiple_of(m * n2 + jj * grp, grp), grp), :] for m in range(n1h)]
        a = _dot(la_ref[jj], jnp.concatenate(rows, axis=0).astype(BF16))
        for j0 in range(grp):
            dst = pl.multiple_of((jj * grp + j0) * pitch, grp)
            s_ref[pl.ds(dst, 2 * n1), :] = a[j0 * 2 * n1:(j0 + 1) * 2 * n1]
        return carry

    lax.fori_loop(0, n2 // grp, stage_a, 0)

    def stage_b(kk, carry):
        k1s = [kk * CONV_UNROLL + i for i in range(CONV_UNROLL)]
        ar = [s_ref[pl.ds(k, n2, stride=pitch), :] for k in k1s]
        ai = [s_ref[pl.ds(n1 + k, n2, stride=pitch), :] for k in k1s]
        x = [_dot(w_ref[...], jnp.concatenate([ar[i], ai[i]], axis=0).astype(BF16)) for i in range(CONV_UNROLL)]
        y = []
        for i, k in enumerate(k1s):
            off = pl.multiple_of(k * n2, n2)
            hr, hi = h_ref[0, pl.ds(off, n2), :], h_ref[1, pl.ds(off, n2), :]
            xr, xi = x[i][:n2], x[i][n2:]
            y.append(jnp.concatenate([xr * hr - xi * hi, xr * hi + xi * hr], axis=0).astype(BF16))
        z = [_dot(g_ref[k1s[i]], y[i]) for i in range(CONV_UNROLL)]
        for i, k in enumerate(k1s):
            s_ref[pl.ds(k, n2, stride=pitch), :] = z[i][:n2]
            s_ref[pl.ds(n1 + k, n2, stride=pitch), :] = z[i][n2:]
        return carry

    lax.fori_loop(0, n1 // CONV_UNROLL, stage_b, 0)

    def stage_c(jj, carry):
        js = [jj * CONV_UNROLL + i for i in range(CONV_UNROLL)]
        zz = [s_ref[pl.ds(pl.multiple_of(j * pitch, grp), 2 * n1), :].astype(BF16) for j in js]
        y = [_dot(fc_ref[...], zz[i]) for i in range(CONV_UNROLL)]
        for i, j in enumerate(js):
            o_ref[0, pl.ds(j, n1h, stride=n2), :] = y[i]
        return carry

    lax.fori_loop(0, n2 // CONV_UNROLL, stage_c, 0)


def _hy_conv(u, h, fa, fb, gb, fc):
    b, l, c = u.shape
    n2, n1x2, n1h = fa.shape
    n1 = n1x2 // 2
    grp = SUBLANES
    la = (fa.reshape(n2 // grp, grp, n1x2, n1h, 1) * jnp.eye(grp, dtype=fa.dtype).reshape(1, grp, 1, 1, grp))
    la = la.reshape(n2 // grp, grp * n1x2, n1h * grp)
    once = pl.Buffered(1)
    return pl.pallas_call(
        functools.partial(_hy_conv_kernel, n1=n1),
        grid=(c // LANES, b),
        in_specs=[
            pl.BlockSpec((1, l, LANES), lambda j, i: (i, 0, j)),
            pl.BlockSpec((2, 2 * l, LANES), lambda j, i: (0, 0, j), pipeline_mode=once),
            pl.BlockSpec(la.shape, lambda j, i: (0, 0, 0), pipeline_mode=once),
            pl.BlockSpec(fb.shape, lambda j, i: (0, 0), pipeline_mode=once),
            pl.BlockSpec(gb.shape, lambda j, i: (0, 0, 0), pipeline_mode=once),
            pl.BlockSpec(fc.shape, lambda j, i: (0, 0), pipeline_mode=once),
        ],
        out_specs=pl.BlockSpec((1, l, LANES), lambda j, i: (i, 0, j)),
        out_shape=jax.ShapeDtypeStruct((b, l, c), F32),
        scratch_shapes=[pltpu.VMEM((n2 * _conv_pitch(n1), LANES), F32)],
        compiler_params=_cparams("arbitrary", "arbitrary"),
        name="hy_conv",
    )(u, h, la, fb, gb, fc)


def _hyena_branch(z, l, conv_w, conv_b, w1, b1, w2, b2, w3, freq, skip):
    c = HY_WIDTH
    n = 2 * l
    fa, fb, gb, fc = _fft_consts(l)
    filt = _hy_filters(l, w1, b1, w2, b2, w3, freq, skip)
    fa_spec = _fft_a(filt, fa).reshape(2, 2, n, c)
    h = _fft_bf(fa_spec, fb)
    u, e = _hy_prep(z, conv_w, conv_b)
    return _hy_conv(u, h, fa, fb, gb, fc), e


def _gdn_prep_kernel(zc_ref, zp_ref, zn_ref, cw_ref, q_ref, k_ref, v_ref, buf_ref):
    t = pl.program_id(1)
    nt = pl.num_programs(1)
    tl = zc_ref.shape[1]
    dh = GDN_HEAD_DIM
    hr = HALO_ROWS
    _fill_halo_buf(buf_ref, zc_ref, zp_ref, zn_ref, t, nt)
    outs = (q_ref, k_ref, v_ref)
    for p in range(3):
        for h in range(GDN_HEADS):
            lo = p * GDN_WIDTH + h * dh
            acc = buf_ref[pl.ds(hr - 2, tl), lo:lo + dh] * cw_ref[0:1, lo:lo + dh]
            for j in range(1, 5):
                acc = acc + buf_ref[pl.ds(hr - 2 + j, tl), lo:lo + dh] * cw_ref[j:j + 1, lo:lo + dh]
            a = _silu(acc)
            if p < 2:
                a = a * lax.rsqrt(jnp.sum(a * a, axis=-1, keepdims=True) + EPS)
            if p == 0:
                a = a * (dh ** -0.5)
            outs[p][0, :, h * dh:(h + 1) * dh] = a.astype(BF16)


def _gdn_prep(z, conv_w, tl=256):
    b, l, _ = z.shape
    w = 3 * GDN_WIDTH
    cur, prev, nxt = _halo_specs(tl, w, l, E_QKV // w)
    o_spec = pl.BlockSpec((1, tl, GDN_WIDTH), lambda i, t: (i, t, 0))
    o_shape = jax.ShapeDtypeStruct((b, l, GDN_WIDTH), BF16)
    return pl.pallas_call(
        _gdn_prep_kernel,
        grid=(b, l // tl),
        in_specs=[cur, prev, nxt, pl.BlockSpec((5, w), lambda i, t: (0, 0))],
        out_specs=[o_spec, o_spec, o_spec],
        out_shape=[o_shape, o_shape, o_shape],
        scratch_shapes=[pltpu.VMEM((tl + 2 * HALO_ROWS, w), F32)],
        compiler_params=_cparams("parallel", "arbitrary"),
        name="gdn_prep",
    )(z, z, z, conv_w.astype(F32))


def _dot_nt(a, b):
    return lax.dot_general(a, b, (((1,), (1,)), ((), ())), preferred_element_type=F32)


def _dot_tn(a, b):
    return lax.dot_general(a, b, (((0,), (0,)), ((), ())), preferred_element_type=F32)


def _split3(x):
    a = x.astype(BF16)
    r = x - a.astype(F32)
    b = r.astype(BF16)
    c = (r - b.astype(F32)).astype(BF16)
    return a, b, c


GDN_CHUNKS_PER_STEP = 4


def _gdn_scan_kernel(*refs, direction, final):
    if final:
        q_ref, k_ref, v_ref, zs_ref, al_ref, dt_ref, of_ref, gg_ref, ng_ref, o_ref, s_ref = refs
    else:
        q_ref, k_ref, v_ref, zs_ref, al_ref, dt_ref, o_ref, s_ref = refs
    c = GDN_CHUNK
    dh = GDN_HEAD_DIM
    nsub = q_ref.shape[1] // c

    @pl.when(pl.program_id(1) == 0)
    def _():
        s_ref[...] = jnp.zeros_like(s_ref)

    row = lax.broadcasted_iota(jnp.int32, (c, c), 0)
    col = lax.broadcasted_iota(jnp.int32, (c, c), 1)
    if direction == 0:
        incl, strict, last = col <= row, col < row, c - 1
    else:
        incl, strict, last = col >= row, col > row, 0
    tri = incl.astype(BF16)
    eye = (row == col).astype(F32)
    heads = range(GDN_HEADS)
    hsl = [slice(h * dh, (h + 1) * dh) for h in heads]
    lns = [2 * GDN_HEADS + direction * GDN_HEADS + h for h in heads]

    rows = [slice(i * c, (i + 1) * c) for i in range(nsub)]
    gcum, gcum_t, beta_all = [], [], []
    for rs in rows:
        zs = zs_ref[0, rs, :]
        beta_all.append(jax.nn.sigmoid(zs))
        gl = -jnp.exp(al_ref[...]) * jax.nn.softplus(zs + dt_ref[...])
        g1, g2, g3 = _split3(gl)
        gsum = _dot(tri, g1) + _dot(tri, g2) + _dot(tri, g3)
        gcum.append(gsum)
        gcum_t.append(jnp.concatenate([gsum, jnp.zeros_like(gsum)], axis=0).T)
    items = [(i, h) for i in range(nsub) for h in heads]
    it = range(len(items))
    beta = [beta_all[i][:, direction * GDN_HEADS + h:direction * GDN_HEADS + h + 1] for i, h in items]
    gc = [gcum[i][:, lns[h]:lns[h] + 1] for i, h in items]
    gct = [gcum_t[i][lns[h]:lns[h] + 1, 0:c] for i, h in items]
    glast = [gcum_t[i][lns[h]:lns[h] + 1, last:last + 1] for i, h in items]
    qh = [q_ref[0, rows[i], hsl[h]] for i, h in items]
    kh = [k_ref[0, rows[i], hsl[h]] for i, h in items]
    vh = [v_ref[0, rows[i], hsl[h]] for i, h in items]
    kq = [_dot_nt(jnp.concatenate([kh[x], qh[x]], axis=0), kh[x]) for x in it]
    dmat = [jnp.exp(jnp.where(incl, gc[x] - gct[x], -jnp.inf)) for x in it]
    a = [jnp.where(strict, (beta[x] * kq[x][:c]) * dmat[x], 0.0) for x in it]
    tinv = [eye - a[x] for x in it]
    ab = [a[x].astype(BF16) for x in it]
    p = [_dot(ab[x], ab[x]) for x in it]
    for j in range(5):
        pb = [p[x].astype(BF16) for x in it]
        if j < 4:
            tp = [_dot(jnp.concatenate([tinv[x].astype(BF16), pb[x]], axis=0), pb[x]) for x in it]
            tinv = [tinv[x] + tp[x][:c] for x in it]
            p = [tp[x][c:] for x in it]
        else:
            tinv = [tinv[x] + _dot(tinv[x].astype(BF16), pb[x]) for x in it]
    eg = [jnp.exp(gc[x]) for x in it]
    kf = [kh[x].astype(F32) for x in it]
    rhs = [jnp.concatenate([vh[x].astype(F32) * beta[x], kf[x] * (beta[x] * eg[x])], axis=1).astype(BF16) for x in it]
    uw = [_dot(tinv[x].astype(BF16), rhs[x]) for x in it]
    wq = [jnp.concatenate([uw[x][:, dh:], qh[x].astype(F32) * eg[x]], axis=0).astype(BF16) for x in it]
    qkd = [(kq[x][c:] * dmat[x]).astype(BF16) for x in it]
    kd = [(kf[x] * jnp.exp(glast[x] - gc[x])).astype(BF16) for x in it]

    order = list(range(nsub)) if direction == 0 else list(range(nsub - 1, -1, -1))
    state = [s_ref[h] for h in heads]
    for i in order:
        rs = rows[i]
        xs = [i * GDN_HEADS + h for h in heads]
        ws = [_dot(wq[xs[h]], state[h].astype(BF16)) for h in heads]
        vnb = [(uw[xs[h]][:, :dh] - ws[h][:c]).astype(BF16) for h in heads]
        o = [ws[h][c:] + _dot(qkd[xs[h]], vnb[h]) for h in heads]
        state = [state[h] * jnp.exp(glast[xs[h]]) + _dot_tn(kd[xs[h]], vnb[h]) for h in heads]
        for h in heads:
            if final:
                tot = of_ref[0, rs, hsl[h]] + o[h]
                y = tot * lax.rsqrt(jnp.mean(tot * tot, axis=-1, keepdims=True) + EPS) * ng_ref[...]
                o_ref[0, rs, hsl[h]] = (y * _silu(gg_ref[0, rs, hsl[h]].astype(F32))).astype(o_ref.dtype)
            else:
                o_ref[0, rs, hsl[h]] = o[h].astype(o_ref.dtype)
    for h in heads:
        s_ref[h] = state[h]


def _gdn_scan(q, k, v, zs, a_log, dt_bias, direction, o_fwd=None, z=None, gate_blk=None, norm_g=None):
    b, l, w = q.shape
    c = GDN_CHUNK * GDN_CHUNKS_PER_STEP
    n = l // c
    final = o_fwd is not None
    if direction == 0:
        cmap = lambda i, t: (i, t, 0)
    else:
        cmap = lambda i, t: (i, n - 1 - t, 0)
    blk = pl.BlockSpec((1, c, w), cmap)
    vec = pl.BlockSpec((1, LANES), lambda i, t: (0, 0))
    pad = lambda x: jnp.pad(x.reshape(1, -1).astype(F32), ((0, 0), (2 * GDN_HEADS, LANES - 4 * GDN_HEADS)))
    in_specs = [blk, blk, blk, pl.BlockSpec((1, c, LANES), cmap), vec, vec]
    args = [q, k, v, zs, pad(a_log), pad(dt_bias)]
    if final:
        if direction == 0:
            gmap = lambda i, t: (i, t, gate_blk)
        else:
            gmap = lambda i, t: (i, n - 1 - t, gate_blk)
        in_specs += [blk, pl.BlockSpec((1, c, w), gmap), pl.BlockSpec((1, GDN_HEAD_DIM), lambda i, t: (0, 0))]
        args += [o_fwd, z, norm_g.reshape(1, GDN_HEAD_DIM).astype(F32)]
    return pl.pallas_call(
        functools.partial(_gdn_scan_kernel, direction=direction, final=final),
        grid=(b, n),
        in_specs=in_specs,
        out_specs=blk,
        out_shape=jax.ShapeDtypeStruct((b, l, w), BF16 if final else F32),
        scratch_shapes=[pltpu.VMEM((GDN_HEADS, GDN_HEAD_DIM, GDN_HEAD_DIM), F32)],
        compiler_params=_cparams("parallel", "arbitrary"),
        name="gdn_scan_bwd" if direction else "gdn_scan_fwd",
    )(*args)


def _gdn_branch(z, zs, conv_w, a_log, dt_bias, norm_g, gate_blk):
    q, k, v = _gdn_prep(z, conv_w)
    o_f = _gdn_scan(q, k, v, zs, a_log, dt_bias, 0)
    return _gdn_scan(q, k, v, zs, a_log, dt_bias, 1, o_fwd=o_f, z=z, gate_blk=gate_blk, norm_g=norm_g)


def _softmax_t(s, extra=None):
    m = jnp.max(s, axis=0, keepdims=True)
    if extra is not None:
        m = jnp.maximum(m, extra)
    p = jnp.exp(s - m)
    den = jnp.sum(p, axis=0, keepdims=True)
    if extra is not None:
        den = den + jnp.exp(extra - m)
    return p.astype(BF16), den, m


def _t_bf16(x):
    return x.astype(F32).T.astype(BF16)


def _xattn_kernel(q_ref, g_ref, kv_ref, o_ref):
    dh = X_HEAD_DIM
    for h in range(X_HEADS):
        hs = slice(h * dh, (h + 1) * dh)
        qh = q_ref[0, :, hs].astype(BF16)
        kh = kv_ref[0, :, hs]
        vh = kv_ref[0, :, X_WIDTH + h * dh:X_WIDTH + (h + 1) * dh]
        s = _dot_nt(qh, kh) * (dh ** -0.5)
        p = jnp.exp(s - jnp.max(s, axis=-1, keepdims=True))
        den = jnp.sum(p, axis=-1, keepdims=True)
        o = _dot(p.astype(BF16), vh) / den
        o_ref[0, :, hs] = (o * _silu(g_ref[0, :, hs].astype(F32))).astype(o_ref.dtype)


def _xattn(z, kv, q_blk, g_blk, tq=512):
    b, l, _ = z.shape
    m = kv.shape[1]
    w = X_WIDTH
    return pl.pallas_call(
        _xattn_kernel,
        grid=(b, l // tq),
        in_specs=[
            pl.BlockSpec((1, tq, w), lambda i, t: (i, t, q_blk)),
            pl.BlockSpec((1, tq, w), lambda i, t: (i, t, g_blk)),
            pl.BlockSpec((1, m, 2 * w), lambda i, t: (i, 0, 0)),
        ],
        out_specs=pl.BlockSpec((1, tq, w), lambda i, t: (i, t, 0)),
        out_shape=jax.ShapeDtypeStruct((b, l, w), BF16),
        compiler_params=_cparams("parallel", "arbitrary"),
        name="xattn",
    )(z, z, kv)


def _out_proj_kernel(*refs, widths, gated):
    nparts = len(widths)
    y_refs = refs[:nparts]
    if gated:
        e_ref, w_ref, g_ref, x_ref, o_ref = refs[nparts:]
    else:
        w_ref, g_ref, x_ref, o_ref = refs[nparts:]
    d = o_ref.shape[1]
    cw = 256
    ys = [y_ref[...] for y_ref in y_refs]
    if gated:
        ys[0] = ys[0] * e_ref[...]
    ys = [y.astype(BF16) for y in ys]
    ssq = jnp.zeros((o_ref.shape[0], 1), F32)
    for c0 in range(0, d, cw):
        acc = None
        off = 0
        for y, wd in zip(ys, widths):
            part = _dot(y, w_ref[off:off + wd, c0:c0 + cw])
            acc = part if acc is None else acc + part
            off += wd
        ssq = ssq + jnp.sum(acc * acc, axis=-1, keepdims=True)
        o_ref[:, c0:c0 + cw] = acc
    r = lax.rsqrt(ssq * (1.0 / d) + EPS)
    for c0 in range(0, d, cw):
        o_ref[:, c0:c0 + cw] = x_ref[:, c0:c0 + cw] + o_ref[:, c0:c0 + cw] * r * g_ref[:, c0:c0 + cw]


def _out_proj(parts, w_out, post_g, x2d, gate=None, tm=512):
    t, d = x2d.shape
    widths = tuple(int(p.shape[1]) for p in parts)
    kdim = sum(widths)
    gates = [] if gate is None else [gate]
    return pl.pallas_call(
        functools.partial(_out_proj_kernel, widths=widths, gated=gate is not None),
        grid=(t // tm,),
        in_specs=[pl.BlockSpec((tm, wd), lambda i: (i, 0)) for wd in widths]
        + [pl.BlockSpec((tm, widths[0]), lambda i: (i, 0)) for _ in gates] + [
            pl.BlockSpec((kdim, d), lambda i: (0, 0)),
            pl.BlockSpec((1, d), lambda i: (0, 0)),
            pl.BlockSpec((tm, d), lambda i: (i, 0)),
        ],
        out_specs=pl.BlockSpec((tm, d), lambda i: (i, 0)),
        out_shape=jax.ShapeDtypeStruct((t, d), F32),
        compiler_params=_cparams("parallel"),
        name="out_proj",
    )(*parts, *gates, w_out.astype(BF16), post_g.reshape(1, d).astype(F32), x2d)


def _mem_kv(mem, mem_g, w_mem_kv):
    b, m, d = mem.shape
    kv = _norm_matmul(mem.reshape(b * m, d), mem_g, w_mem_kv, 1024, out_dtype=BF16, name="mem_kv")
    return kv.reshape(b, m, 2 * X_WIDTH)


E_QKV, E_HY, E_GHY, E_GGDN, E_XQ, E_GX, E_BIG = 0, 3072, 6144, 7168, 8192, 8704, 9216


def _even_layer(x, mem, pre_g, post_g, w_in, w_out, hy_conv_w, hy_conv_b, hy_w1, hy_b1, hy_w2, hy_b2, hy_w3, hy_freq,
                hy_skip, gdn_conv_w, gdn_a_log, gdn_dt_bias, gdn_norm_g, mem_g, w_mem_kv):
    b, l, d = x.shape
    x2d = x.reshape(b * l, d)
    w_big = jnp.concatenate([w_in[:, 4096:7168], w_in[:, 0:3072], w_in[:, 3072:4096], w_in[:, 7168:8192],
                             w_in[:, 8224:8736], w_in[:, 8736:9248]], axis=1)
    w_small = jnp.pad(w_in[:, 8192:8224], ((0, 0), (0, LANES - 4 * GDN_HEADS)))
    z = _norm_matmul(x2d, pre_g, w_big, 1024, out_dtype=BF16, tm=1024, name="even_in").reshape(b, l, E_BIG)
    zs = _norm_matmul(x2d, pre_g, w_small, LANES, name="even_in_gates").reshape(b, l, LANES)
    conv, e = _hyena_branch(z, l, hy_conv_w, hy_conv_b, hy_w1, hy_b1, hy_w2, hy_b2, hy_w3, hy_freq, hy_skip)
    y_b = _gdn_branch(z, zs, gdn_conv_w, gdn_a_log, gdn_dt_bias, gdn_norm_g, E_GGDN // GDN_WIDTH)
    kv = _mem_kv(mem, mem_g, w_mem_kv)
    y_x = _xattn(z, kv, E_XQ // X_WIDTH, E_GX // X_WIDTH)
    t = b * l
    out = _out_proj([conv.reshape(t, -1), y_b.reshape(t, -1), y_x.reshape(t, -1)], w_out, post_g, x2d,
                    gate=e.reshape(t, -1))
    return out.reshape(b, l, d)


DIL_PATTERNS = ((128, 1), (512, 4), (2048, 16))
N_DIL = len(DIL_PATTERNS)
DIL_HEADS = 4
DIL_HEAD_DIM = 128
DIL_WIDTH = DIL_HEADS * DIL_HEAD_DIM
SWA_Q_HEADS = 16
SWA_KV_HEADS = 2
SWA_HEAD_DIM = 64
SWA_WIDTH = SWA_Q_HEADS * SWA_HEAD_DIM
SWA_HALF_WINDOW = 128
O_CQKV, O_GC, O_DQ, O_GD, O_XQ, O_GX, O_DKV, O_ALL = 0, 4608, 5120, 6144, 7168, 7680, 8192, 8448


def _rope_tables(l, dh):
    half = dh // 8
    inv = ROPE_THETA ** (-jnp.arange(half, dtype=F32) / half)
    ang = jnp.arange(l, dtype=F32)[:, None] * inv[None, :]
    cos, sin = jnp.cos(ang), jnp.sin(ang)
    one = jnp.ones((l, dh - 2 * half), F32)
    zero_h = jnp.zeros((l, half), F32)
    zero_r = jnp.zeros((l, dh - 2 * half), F32)
    c = jnp.concatenate([cos, cos, one], axis=1)
    sa = jnp.concatenate([-sin, zero_h, zero_r], axis=1)
    sb = jnp.concatenate([zero_h, sin, zero_r], axis=1)
    rep = LANES // dh
    return tuple(jnp.tile(t, (1, rep)) for t in (c, sa, sb))


def _odd_prep_kernel(c_ref, dq_ref, dkv_ref, c1_ref, a1_ref, b1_ref, c2_ref, a2_ref, b2_ref,
                     cq_ref, ck_ref, cv_ref, dqe_ref, dk_ref, dv_ref):
    tl = c_ref.shape[1]
    lane = lax.broadcasted_iota(jnp.int32, (tl, LANES), 1)
    low = lane < SWA_HEAD_DIM
    c1, a1, b1 = c1_ref[...], a1_ref[...], b1_ref[...]
    c2, a2, b2 = c2_ref[...], a2_ref[...], b2_ref[...]
    h1 = DIL_HEAD_DIM // 8
    h2 = SWA_HEAD_DIM // 8

    def rope1(x):
        return x * c1 + pltpu.roll(x, LANES - h1, 1) * a1 + pltpu.roll(x, h1, 1) * b1

    def rope2(x):
        return x * c2 + pltpu.roll(x, LANES - h2, 1) * a2 + pltpu.roll(x, h2, 1) * b2

    nblk = N_DIL * DIL_HEADS
    for j in range(nblk):
        ls = slice(j * LANES, (j + 1) * LANES)
        cq_ref[0, :, ls] = rope1(c_ref[0, :, ls].astype(F32)).astype(BF16)
        ck_ref[0, :, ls] = rope1(c_ref[0, :, nblk * LANES + j * LANES:nblk * LANES + (j + 1) * LANES].astype(F32)).astype(BF16)
        cv_ref[0, :, ls] = c_ref[0, :, 2 * nblk * LANES + j * LANES:2 * nblk * LANES + (j + 1) * LANES].astype(BF16)
    for j in range(SWA_Q_HEADS // 2):
        xr = rope2(dq_ref[0, :, j * LANES:(j + 1) * LANES].astype(F32))
        dqe_ref[0, :, (2 * j) * LANES:(2 * j + 1) * LANES] = jnp.where(low, xr, 0.0).astype(BF16)
        dqe_ref[0, :, (2 * j + 1) * LANES:(2 * j + 2) * LANES] = jnp.where(low, pltpu.roll(xr, SWA_HEAD_DIM, 1), 0.0).astype(BF16)
    kr = rope2(dkv_ref[0, :, 0:LANES].astype(F32))
    vv = dkv_ref[0, :, LANES:2 * LANES].astype(F32)
    kr_sw = pltpu.roll(kr, SWA_HEAD_DIM, 1)
    vv_sw = pltpu.roll(vv, SWA_HEAD_DIM, 1)
    dk_ref[0, :, 0:LANES] = jnp.where(low, kr, 0.0).astype(BF16)
    dk_ref[0, :, LANES:2 * LANES] = jnp.where(low, kr_sw, 0.0).astype(BF16)
    dv_ref[0, :, 0:LANES] = jnp.where(low, vv, vv_sw).astype(BF16)
    dv_ref[0, :, LANES:2 * LANES] = jnp.where(low, vv_sw, vv).astype(BF16)


def _odd_prep(z, tl=256):
    b, l, _ = z.shape
    wc = 3 * N_DIL * DIL_WIDTH
    t1 = _rope_tables(l, DIL_HEAD_DIM)
    t2 = _rope_tables(l, SWA_HEAD_DIM)
    tab = pl.BlockSpec((tl, LANES), lambda i, t: (t, 0))

    def spec(w):
        return pl.BlockSpec((1, tl, w), lambda i, t: (i, t, 0))

    def shape(w):
        return jax.ShapeDtypeStruct((b, l, w), BF16)

    wd = N_DIL * DIL_WIDTH
    return pl.pallas_call(
        _odd_prep_kernel,
        grid=(b, l // tl),
        in_specs=[
            pl.BlockSpec((1, tl, wc), lambda i, t: (i, t, 0)),
            pl.BlockSpec((1, tl, SWA_WIDTH), lambda i, t: (i, t, O_DQ // SWA_WIDTH)),
            pl.BlockSpec((1, tl, 2 * LANES), lambda i, t: (i, t, O_DKV // (2 * LANES))),
            tab, tab, tab, tab, tab, tab,
        ],
        out_specs=[spec(wd), spec(wd), spec(wd), spec(2 * SWA_WIDTH), spec(2 * LANES), spec(2 * LANES)],
        out_shape=[shape(wd), shape(wd), shape(wd), shape(2 * SWA_WIDTH), shape(2 * LANES), shape(2 * LANES)],
        compiler_params=_cparams("parallel", "arbitrary"),
        name="odd_prep",
    )(z, z, z, *t1, *t2)


def _band_bias_t(tq, half, ls):
    p0 = pl.program_id(1) * tq
    j = lax.broadcasted_iota(jnp.int32, (tq + 2 * half, tq), 0)
    i = lax.broadcasted_iota(jnp.int32, (tq + 2 * half, tq), 1)
    kpos = p0 - half + j
    valid = (j >= i) & (j - i <= 2 * half) & (kpos >= 0) & (kpos < ls)
    return jnp.where(valid, 0.0, -jnp.inf)


def _band_mask(tq, half, ls):
    p0 = pl.program_id(1) * tq
    i = lax.broadcasted_iota(jnp.int32, (tq, tq + 2 * half), 0)
    j = lax.broadcasted_iota(jnp.int32, (tq, tq + 2 * half), 1)
    kpos = p0 - half + j
    return (j >= i) & (j - i <= 2 * half) & (kpos >= 0) & (kpos < ls)


def _band_attn_kernel(q_ref, kp_ref, kc_ref, kn_ref, vp_ref, vc_ref, vn_ref, o_ref, lse_ref, *, half, ls):
    tq = q_ref.shape[1]
    dh = DIL_HEAD_DIM
    valid = _band_mask(tq, half, ls)
    lane = lax.broadcasted_iota(jnp.int32, (tq, LANES), 1)
    lse_all = jnp.zeros((tq, LANES), F32)
    heads = range(DIL_HEADS)
    hsl = [slice(h * dh, (h + 1) * dh) for h in heads]
    kall = [jnp.concatenate([kp_ref[0, :, s], kc_ref[0, :, s], kn_ref[0, :, s]], axis=0) for s in hsl]
    vall = [jnp.concatenate([vp_ref[0, :, s], vc_ref[0, :, s], vn_ref[0, :, s]], axis=0) for s in hsl]
    sc = [jnp.where(valid, _dot_nt(q_ref[0, :, hsl[h]], kall[h]) * (dh ** -0.5), -jnp.inf) for h in heads]
    m = [jnp.max(sc[h], axis=-1, keepdims=True) for h in heads]
    p = [jnp.exp(sc[h] - m[h]) for h in heads]
    den = [jnp.sum(p[h], axis=-1, keepdims=True) for h in heads]
    o = [_dot(p[h].astype(BF16), vall[h]) / den[h] for h in heads]
    for h in heads:
        o_ref[0, :, hsl[h]] = o[h].astype(o_ref.dtype)
        lse_all = jnp.where(lane == h, m[h] + jnp.log(den[h]), lse_all)
    lse_ref[0] = lse_all


def _band_specs(tq, half, ls, w, col):
    r = tq // half
    nb = ls // half
    cur = pl.BlockSpec((1, tq, w), lambda i, t: (i, t, col))
    prev = pl.BlockSpec((1, half, w), lambda i, t: (i, jnp.maximum(t * r - 1, 0), col))
    nxt = pl.BlockSpec((1, half, w), lambda i, t: (i, jnp.minimum((t + 1) * r, nb - 1), col))
    return cur, prev, nxt


def _band_attn(q, k, v, half):
    n, ls, w = q.shape
    tq = min(2 * LANES, ls)
    cur, prev, nxt = _band_specs(tq, half, ls, w, 0)
    return pl.pallas_call(
        functools.partial(_band_attn_kernel, half=half, ls=ls),
        grid=(n, ls // tq),
        in_specs=[cur, prev, cur, nxt, prev, cur, nxt],
        out_specs=[cur, pl.BlockSpec((1, tq, LANES), lambda i, t: (i, t, 0))],
        out_shape=[jax.ShapeDtypeStruct((n, ls, w), F32), jax.ShapeDtypeStruct((n, ls, LANES), F32)],
        compiler_params=_cparams("parallel", "arbitrary"),
        name="band_attn",
    )(q, k, k, k, v, v, v)


def _dil_merge_kernel(o0_ref, o1_ref, o2_ref, l0_ref, l1_ref, l2_ref, g_ref, y_ref):
    dh = DIL_HEAD_DIM
    o_refs = (o0_ref, o1_ref, o2_ref)
    for h in range(DIL_HEADS):
        hs = slice(h * dh, (h + 1) * dh)
        ls = [r[0, :, h:h + 1] for r in (l0_ref, l1_ref, l2_ref)]
        m = jnp.maximum(jnp.maximum(ls[0], ls[1]), ls[2])
        ws = [jnp.exp(x - m) for x in ls]
        den = ws[0] + ws[1] + ws[2]
        y = (ws[0] / den) * o_refs[0][0, :, hs] + (ws[1] / den) * o_refs[1][0, :, hs] + (ws[2] / den) * o_refs[2][0, :, hs]
        y_ref[0, :, hs] = (y * _silu(g_ref[0, :, hs].astype(F32))).astype(y_ref.dtype)


def _dil_merge(outs, lses, z, tl=512):
    b, l, w = outs[0].shape
    o_spec = pl.BlockSpec((1, tl, w), lambda i, t: (i, t, 0))
    l_spec = pl.BlockSpec((1, tl, LANES), lambda i, t: (i, t, 0))
    return pl.pallas_call(
        _dil_merge_kernel,
        grid=(b, l // tl),
        in_specs=[o_spec, o_spec, o_spec, l_spec, l_spec, l_spec, pl.BlockSpec((1, tl, w), lambda i, t: (i, t, O_GC // DIL_WIDTH))],
        out_specs=o_spec,
        out_shape=jax.ShapeDtypeStruct((b, l, w), BF16),
        compiler_params=_cparams("parallel", "arbitrary"),
        name="dil_merge",
    )(*outs, *lses, z)


def _dilated_branch(cq, ck, cv, z):
    b, l, _ = cq.shape
    w = DIL_WIDTH
    outs, lses = [], []
    for gi, (window, d) in enumerate(DIL_PATTERNS):
        half = window // (2 * d)
        ls = l // d

        def to_res(x):
            x = x[:, :, gi * w:(gi + 1) * w]
            if d == 1:
                return x
            return jnp.swapaxes(x.reshape(b, ls, d, w), 1, 2).reshape(b * d, ls, w)

        def from_res(x):
            if d == 1:
                return x
            return jnp.swapaxes(x.reshape(b, d, ls, x.shape[-1]), 1, 2).reshape(b, l, x.shape[-1])

        o, lse = _band_attn(to_res(cq), to_res(ck), to_res(cv), half)
        outs.append(from_res(o))
        lses.append(from_res(lse))
    return _dil_merge(outs, lses, z)


def _swa_kernel(q_ref, kp_ref, kc_ref, kn_ref, vp_ref, vc_ref, vn_ref, g_ref, sink_ref, o_ref, *, half, ls):
    tq = q_ref.shape[1]
    dh = SWA_HEAD_DIM
    bias = _band_bias_t(tq, half, ls)
    low = lax.broadcasted_iota(jnp.int32, (LANES, tq), 0) < dh
    grp = SWA_Q_HEADS // SWA_KV_HEADS
    for g in range(SWA_KV_HEADS):
        gs = slice(g * LANES, (g + 1) * LANES)
        kall = jnp.concatenate([kp_ref[0, :, gs], kc_ref[0, :, gs], kn_ref[0, :, gs]], axis=0)
        vt = _t_bf16(jnp.concatenate([vp_ref[0, :, gs], vc_ref[0, :, gs], vn_ref[0, :, gs]], axis=0))
        hs = range(g * grp, (g + 1) * grp)
        sc = [_dot_nt(kall, q_ref[0, :, h * LANES:(h + 1) * LANES]) * (dh ** -0.5) + bias for h in hs]
        sm = [_softmax_t(sc[i], sink_ref[:, h:h + 1]) for i, h in enumerate(hs)]
        ot = [_dot(vt, sm[i][0]) / sm[i][1] for i in range(grp)]
        for jp in range(grp // 2):
            blk = (g * grp) // 2 + jp
            bs = slice(blk * LANES, (blk + 1) * LANES)
            y = jnp.where(low, ot[2 * jp], ot[2 * jp + 1]).T * _silu(g_ref[0, :, bs].astype(F32))
            o_ref[0, :, bs] = y.astype(o_ref.dtype)


def _swa_branch(dqe, dk, dv, z, sink, tq=128):
    b, l, _ = dqe.shape
    half = SWA_HALF_WINDOW
    _, prev, nxt = _band_specs(tq, half, l, 2 * LANES, 0)
    cur = pl.BlockSpec((1, tq, 2 * LANES), lambda i, t: (i, t, 0))
    sink_p = jnp.pad(sink.reshape(1, SWA_Q_HEADS).astype(F32), ((0, 0), (0, LANES - SWA_Q_HEADS)))
    return pl.pallas_call(
        functools.partial(_swa_kernel, half=half, ls=l),
        grid=(b, l // tq),
        in_specs=[
            pl.BlockSpec((1, tq, 2 * SWA_WIDTH), lambda i, t: (i, t, 0)),
            prev, cur, nxt, prev, cur, nxt,
            pl.BlockSpec((1, tq, SWA_WIDTH), lambda i, t: (i, t, O_GD // SWA_WIDTH)),
            pl.BlockSpec((1, LANES), lambda i, t: (0, 0)),
        ],
        out_specs=pl.BlockSpec((1, tq, SWA_WIDTH), lambda i, t: (i, t, 0)),
        out_shape=jax.ShapeDtypeStruct((b, l, SWA_WIDTH), BF16),
        compiler_params=_cparams("parallel", "arbitrary"),
        name="swa",
    )(dqe, dk, dk, dk, dv, dv, dv, z, sink_p)


def _odd_layer(x, mem, pre_g, post_g, w_in, w_out, swa_sink, mem_g, w_mem_kv):
    b, l, d = x.shape
    x2d = x.reshape(b * l, d)
    w_re = jnp.concatenate([w_in[:, 0:6144], w_in[:, 6400:8448], w_in[:, 6144:6400]], axis=1)
    z = _norm_matmul(x2d, pre_g, w_re, 1408, out_dtype=BF16, tm=1024, name="odd_in").reshape(b, l, O_ALL)
    cq, ck, cv, dqe, dk, dv = _odd_prep(z)
    y_c = _dilated_branch(cq, ck, cv, z)
    y_d = _swa_branch(dqe, dk, dv, z, swa_sink)
    kv = _mem_kv(mem, mem_g, w_mem_kv)
    y_x = _xattn(z, kv, O_XQ // X_WIDTH, O_GX // X_WIDTH)
    t = b * l
    out = _out_proj([y_c.reshape(t, -1), y_d.reshape(t, -1), y_x.reshape(t, -1)], w_out, post_g, x2d)
    return out.reshape(b, l, d)


def _trunk(x, mem, even_params, odd_params):
    x = _even_layer(x, mem, *[p[0] for p in even_params])
    return _odd_layer(x, mem, *[p[0] for p in odd_params])


def kernel(x_prompt, x_sample, mem_prompt, mem_sample, e_pre_g, e_post_g, e_w_in, e_w_out, hy_conv_w, hy_conv_b,
           hy_filt_w1, hy_filt_b1, hy_filt_w2, hy_filt_b2, hy_filt_w3, hy_freq, hy_skip, gdn_conv_w, gdn_A_log,
           gdn_dt_bias, gdn_norm_g, e_mem_g, e_w_mem_kv, o_pre_g, o_post_g, o_w_in, o_w_out, swa_sink, o_mem_g,
           o_w_mem_kv):
    even_params = (e_pre_g, e_post_g, e_w_in, e_w_out, hy_conv_w, hy_conv_b, hy_filt_w1, hy_filt_b1, hy_filt_w2,
                   hy_filt_b2, hy_filt_w3, hy_freq, hy_skip, gdn_conv_w, gdn_A_log, gdn_dt_bias, gdn_norm_g,
                   e_mem_g, e_w_mem_kv)
    odd_params = (o_pre_g, o_post_g, o_w_in, o_w_out, swa_sink, o_mem_g, o_w_mem_kv)
    y_prompt = _trunk(x_prompt, mem_prompt, even_params, odd_params)
    y_sample = _trunk(x_sample, mem_sample, even_params, odd_params)
    return (y_prompt, y_sample)
```

```python
import functools
import math

import jax
import jax.numpy as jnp
import numpy as np
from jax import lax
from jax.experimental import pallas as pl
from jax.experimental.pallas import tpu as pltpu

D_MODEL = 1024
EPS = 1e-6
ROPE_THETA = 500000.0
HY_WIDTH = 1024
HY_EMB = 33
HY_BANDS = (HY_EMB - 1) // 2
HY_FILT_HIDDEN = 64
HY_TARGET = 1e-2
HY_FAST_PCT = 0.3
HY_SLOW_PCT = 1.5
GDN_HEADS = 8
GDN_HEAD_DIM = 128
GDN_WIDTH = GDN_HEADS * GDN_HEAD_DIM
GDN_CHUNK = 64
X_HEADS = 4
X_HEAD_DIM = 128
X_WIDTH = X_HEADS * X_HEAD_DIM

LANES = 128
FFT_N2 = 128
VMEM_LIMIT_BYTES = 48 * 1024 * 1024

BF16 = jnp.bfloat16
F32 = jnp.float32


def _cparams(*sem):
    return pltpu.CompilerParams(dimension_semantics=sem, vmem_limit_bytes=VMEM_LIMIT_BYTES)


def _dot(a, b):
    return jnp.dot(a, b, preferred_element_type=F32)


def _silu(x):
    return x * jax.nn.sigmoid(x)


def _norm_matmul_kernel(x_ref, g_ref, w_ref, o_ref, xn_ref):
    @pl.when(pl.program_id(1) == 0)
    def _():
        x = x_ref[...]
        ms = jnp.mean(x * x, axis=-1, keepdims=True)
        xn_ref[...] = (x * lax.rsqrt(ms + EPS) * g_ref[...]).astype(BF16)

    o_ref[...] = _dot(xn_ref[...], w_ref[...]).astype(o_ref.dtype)


def _norm_matmul(x2d, g, w, tn, out_dtype=F32, tm=512, name="norm_matmul"):
    t, d = x2d.shape
    n = w.shape[1]
    assert t % tm == 0 and n % tn == 0
    return pl.pallas_call(
        _norm_matmul_kernel,
        grid=(t // tm, n // tn),
        in_specs=[
            pl.BlockSpec((tm, d), lambda i, j: (i, 0)),
            pl.BlockSpec((1, d), lambda i, j: (0, 0)),
            pl.BlockSpec((d, tn), lambda i, j: (0, j)),
        ],
        out_specs=pl.BlockSpec((tm, tn), lambda i, j: (i, j)),
        out_shape=jax.ShapeDtypeStruct((t, n), out_dtype),
        scratch_shapes=[pltpu.VMEM((tm, d), BF16)],
        compiler_params=_cparams("parallel", "arbitrary"),
        name=name,
    )(x2d, g.reshape(1, d).astype(F32), w.astype(BF16))


HALO_ROWS = 16


def _fill_halo_buf(buf_ref, zc_ref, zp_ref, zn_ref, t, nt):
    tl = zc_ref.shape[1]
    hr = HALO_ROWS
    buf_ref[0:hr, :] = jnp.where(t > 0, zp_ref[0].astype(F32), 0.0)
    buf_ref[hr:hr + tl, :] = zc_ref[0].astype(F32)
    buf_ref[hr + tl:2 * hr + tl, :] = jnp.where(t < nt - 1, zn_ref[0].astype(F32), 0.0)


def _halo_specs(tl, w, l, col):
    r = tl // HALO_ROWS
    nb = l // HALO_ROWS
    cur = pl.BlockSpec((1, tl, w), lambda i, t: (i, t, col))
    prev = pl.BlockSpec((1, HALO_ROWS, w), lambda i, t: (i, jnp.maximum(t * r - 1, 0), col))
    nxt = pl.BlockSpec((1, HALO_ROWS, w), lambda i, t: (i, jnp.minimum((t + 1) * r, nb - 1), col))
    return cur, prev, nxt


def _hy_prep_kernel(zc_ref, zp_ref, zn_ref, g_ref, cw_ref, cb_ref, u_ref, e_ref, buf_ref):
    t = pl.program_id(1)
    nt = pl.num_programs(1)
    tl = zc_ref.shape[1]
    c = HY_WIDTH
    hr = HALO_ROWS
    _fill_halo_buf(buf_ref, zc_ref, zp_ref, zn_ref, t, nt)
    cw = 256
    for c0 in range(0, c, cw):
        parts = []
        for p in range(3):
            lo = p * c + c0
            acc = cb_ref[:, lo:lo + cw]
            for j in range(3):
                acc = acc + buf_ref[pl.ds(hr - 1 + j, tl), lo:lo + cw] * cw_ref[j:j + 1, lo:lo + cw]
            parts.append(acc)
        x0, x1, v = parts
        u_ref[0, :, c0:c0 + cw] = (v * x1).astype(u_ref.dtype)
        e_ref[0, :, c0:c0 + cw] = (x0 * _silu(g_ref[0, :, c0:c0 + cw].astype(F32))).astype(e_ref.dtype)


def _hy_prep(z, conv_w, conv_b, tl=256):
    b, l, _ = z.shape
    c = HY_WIDTH
    assert l % tl == 0
    cur, prev, nxt = _halo_specs(tl, 3 * c, l, E_HY // (3 * c))
    return pl.pallas_call(
        _hy_prep_kernel,
        grid=(b, l // tl),
        in_specs=[
            cur, prev, nxt,
            pl.BlockSpec((1, tl, c), lambda i, t: (i, t, E_GHY // c)),
            pl.BlockSpec((3, 3 * c), lambda i, t: (0, 0)),
            pl.BlockSpec((1, 3 * c), lambda i, t: (0, 0)),
        ],
        out_specs=[
            pl.BlockSpec((1, tl, c), lambda i, t: (i, t, 0)),
            pl.BlockSpec((1, tl, c), lambda i, t: (i, t, 0)),
        ],
        out_shape=[jax.ShapeDtypeStruct((b, l, c), F32), jax.ShapeDtypeStruct((b, l, c), F32)],
        scratch_shapes=[pltpu.VMEM((tl + 2 * HALO_ROWS, 3 * c), F32)],
        compiler_params=_cparams("parallel", "arbitrary"),
        name="hy_prep",
    )(z, z, z, z, conv_w.astype(F32), conv_b.reshape(1, 3 * c).astype(F32))


def _hy_filter_tables(l):
    t = jnp.linspace(0.0, 1.0, l, dtype=F32)[:, None]
    w = (2.0 * math.pi / l) * jnp.arange(l, dtype=F32)[:, None]
    f = jnp.linspace(1e-4, HY_BANDS - 1, HY_BANDS, dtype=F32)[None, :]
    emb = jnp.concatenate([t, jnp.cos(f * w), -jnp.sin(f * w)], axis=-1)
    emb = jnp.pad(emb, ((0, 0), (0, LANES - HY_EMB)))
    deltas = jnp.abs(jnp.linspace(math.log(HY_TARGET) / HY_SLOW_PCT, math.log(HY_TARGET) / HY_FAST_PCT, HY_WIDTH, dtype=F32))
    decay = jnp.exp(-t * jnp.tile(deltas, 2)[None, :])
    return emb, decay


def _hy_filter_kernel(emb_ref, dec_ref, w1_ref, b1_ref, w2_ref, b2_ref, w3_ref, fr_ref, sk_ref, o_ref):
    c = HY_WIDTH
    hp = lax.Precision.HIGHEST
    fr = fr_ref[...]
    hid = jnp.sin(fr * (jnp.dot(emb_ref[...], w1_ref[...], precision=hp, preferred_element_type=F32) + b1_ref[...]))
    hid = jnp.sin(fr * (jnp.dot(hid, w2_ref[...], precision=hp, preferred_element_type=F32) + b2_ref[...]))
    tl = emb_ref.shape[0]
    row = lax.broadcasted_iota(jnp.int32, (tl, 1), 0) + pl.program_id(0) * tl
    first = row == 0
    cw = 512
    for c0 in range(0, 2 * c, cw):
        filt = jnp.dot(hid, w3_ref[:, c0:c0 + cw], precision=hp, preferred_element_type=F32) * dec_ref[:, c0:c0 + cw]
        if c0 < c:
            filt = jnp.where(first, filt + sk_ref[:, c0:c0 + cw], filt)
            o_ref[0, :, c0:c0 + cw] = filt.astype(o_ref.dtype)
        else:
            filt = jnp.where(first, 0.0, filt)
            o_ref[1, :, c0 - c:c0 - c + cw] = filt.astype(o_ref.dtype)


def _hy_filters(l, w1, b1, w2, b2, w3, freq, skip, tl=256):
    c = HY_WIDTH
    hdim = HY_FILT_HIDDEN
    emb, decay = _hy_filter_tables(l)
    w1p = jnp.pad(w1.astype(F32), ((0, LANES - HY_EMB), (0, 0)))
    return pl.pallas_call(
        _hy_filter_kernel,
        grid=(l // tl,),
        in_specs=[
            pl.BlockSpec((tl, LANES), lambda t: (t, 0)),
            pl.BlockSpec((tl, 2 * c), lambda t: (t, 0)),
            pl.BlockSpec((LANES, hdim), lambda t: (0, 0)),
            pl.BlockSpec((1, hdim), lambda t: (0, 0)),
            pl.BlockSpec((hdim, hdim), lambda t: (0, 0)),
            pl.BlockSpec((1, hdim), lambda t: (0, 0)),
            pl.BlockSpec((hdim, 2 * c), lambda t: (0, 0)),
            pl.BlockSpec((1, hdim), lambda t: (0, 0)),
            pl.BlockSpec((1, c), lambda t: (0, 0)),
        ],
        out_specs=pl.BlockSpec((2, tl, c), lambda t: (0, t, 0)),
        out_shape=jax.ShapeDtypeStruct((2, l, c), BF16),
        compiler_params=_cparams("arbitrary"),
        name="hy_filter",
    )(emb, decay, w1p, b1.reshape(1, hdim).astype(F32), w2.astype(F32), b2.reshape(1, hdim).astype(F32),
      w3.astype(F32), freq.reshape(1, hdim).astype(F32), skip.reshape(1, c).astype(F32))


def _fft_consts(l):
    n = 2 * l
    n2 = FFT_N2
    n1 = n // n2
    n1h = n1 // 2
    w0 = 2.0 * math.pi / n
    k1 = jnp.arange(n1, dtype=jnp.int32)
    m1 = jnp.arange(n1h, dtype=jnp.int32)
    m2 = jnp.arange(n2, dtype=jnp.int32)
    ea = (n2 * m1[None, None, :] * k1[None, :, None] + m2[:, None, None] * k1[None, :, None]) % n
    ang = ea.astype(F32) * w0
    fa = jnp.concatenate([jnp.cos(ang), -jnp.sin(ang)], axis=1).astype(BF16)
    eb = (n1 * m2[:, None] * m2[None, :]) % n
    angb = eb.astype(F32) * w0
    fr, fi = jnp.cos(angb), -jnp.sin(angb)
    fb = jnp.concatenate([jnp.concatenate([fr, -fi], axis=1), jnp.concatenate([fi, fr], axis=1)], axis=0).astype(BF16)
    eg = (n1 * m2[None, :, None] * m2[None, None, :] + m2[None, :, None] * k1[:, None, None]) % n
    angg = eg.astype(F32) * w0
    gr, gi = jnp.cos(angg), jnp.sin(angg)
    gb = jnp.concatenate([jnp.concatenate([gr, -gi], axis=2), jnp.concatenate([gi, gr], axis=2)], axis=1).astype(BF16)
    ec = (n2 * m1[:, None] * k1[None, :]) % n
    angc = ec.astype(F32) * w0
    fc = (jnp.concatenate([jnp.cos(angc), -jnp.sin(angc)], axis=1) * (1.0 / n)).astype(BF16)
    return fa, fb, gb, fc


def _fft_a_kernel(u_ref, f_ref, o_ref):
    tn2 = f_ref.shape[0]
    n1 = f_ref.shape[1] // 2
    c = u_ref.shape[2] // tn2
    for j in range(tn2):
        a = _dot(f_ref[j], u_ref[0, :, j * c:(j + 1) * c])
        o_ref[0, 0, :, j * c:(j + 1) * c] = a[:n1].astype(o_ref.dtype)
        o_ref[0, 1, :, j * c:(j + 1) * c] = a[n1:].astype(o_ref.dtype)


def _fft_a(u, fa, tn2=8):
    b, l, c = u.shape
    n2, n1x2, n1h = fa.shape
    n1 = n1x2 // 2
    uv = u.reshape(b, n1h, n2 * c)
    return pl.pallas_call(
        _fft_a_kernel,
        grid=(b, n2 // tn2),
        in_specs=[
            pl.BlockSpec((1, n1h, tn2 * c), lambda i, j: (i, 0, j)),
            pl.BlockSpec((tn2, n1x2, n1h), lambda i, j: (j, 0, 0)),
        ],
        out_specs=pl.BlockSpec((1, 2, n1, tn2 * c), lambda i, j: (i, 0, 0, j)),
        out_shape=jax.ShapeDtypeStruct((b, 2, n1, n2 * c), BF16),
        compiler_params=_cparams("parallel", "arbitrary"),
        name="fft_a",
    )(uv, fa)


def _fft_bf_kernel(a_ref, w_ref, o_ref):
    n2 = a_ref.shape[2]
    ct = a_ref.shape[3]
    cw = 256
    for c0 in range(0, ct, cw):
        xf = _dot(w_ref[...], a_ref[0, :, :, c0:c0 + cw].reshape(2 * n2, cw))
        xb = _dot(w_ref[...], a_ref[1, :, :, c0:c0 + cw].reshape(2 * n2, cw))
        o_ref[0, :, c0:c0 + cw] = xf[:n2] + xb[:n2]
        o_ref[1, :, c0:c0 + cw] = xf[n2:] - xb[n2:]


def _fft_bf(a, fb):
    _, _, n, c = a.shape
    n2 = FFT_N2
    return pl.pallas_call(
        _fft_bf_kernel,
        grid=(n // n2,),
        in_specs=[
            pl.BlockSpec((2, 2, n2, c), lambda k: (0, 0, k, 0)),
            pl.BlockSpec((2 * n2, 2 * n2), lambda k: (0, 0)),
        ],
        out_specs=pl.BlockSpec((2, n2, c), lambda k: (0, k, 0)),
        out_shape=jax.ShapeDtypeStruct((2, n, c), F32),
        compiler_params=_cparams("arbitrary"),
        name="fft_bf",
    )(a, fb)


SUBLANES = 8
CONV_SLABS = 2


def _conv_dims(n1):
    kh = n1 // 2 + 1
    kp = -(-kh // SUBLANES) * SUBLANES
    return kh, kp, 2 * kp + SUBLANES


def _hy_conv_kernel(u_ref, h_ref, la_ref, w_ref, g_ref, fc_ref, o_ref, s_ref, *, n1, unroll_b):
    n2 = FFT_N2
    n1h = n1 // 2
    kh, kp, pitch = _conv_dims(n1)
    grp = SUBLANES
    slabs = range(CONV_SLABS)

    def lanes(x, sl):
        return x[:, sl * LANES:(sl + 1) * LANES]

    def stage_a(jj, carry):
        rows = [u_ref[0, pl.ds(pl.multiple_of(m * n2 + jj * grp, grp), grp), :] for m in range(n1h)]
        a = _dot(la_ref[jj], jnp.concatenate(rows, axis=0).astype(BF16))
        for j0 in range(grp):
            dst = pl.multiple_of((jj * grp + j0) * pitch, grp)
            for sl in slabs:
                s_ref[sl, pl.ds(dst, 2 * kp), :] = lanes(a[j0 * 2 * kp:(j0 + 1) * 2 * kp], sl)
        return carry

    lax.fori_loop(0, n2 // grp, stage_a, 0)

    def stage_b(kk, carry):
        k1s = [kk * unroll_b + i for i in range(unroll_b)]
        ar = [jnp.concatenate([s_ref[sl, pl.ds(k, n2, stride=pitch), :] for sl in slabs], axis=1) for k in k1s]
        ai = [jnp.concatenate([s_ref[sl, pl.ds(kp + k, n2, stride=pitch), :] for sl in slabs], axis=1) for k in k1s]
        x = [_dot(w_ref[...], jnp.concatenate([ar[i], ai[i]], axis=0).astype(BF16)) for i in range(unroll_b)]
        y = []
        for i, k in enumerate(k1s):
            off = pl.multiple_of(k * n2, n2)
            hr, hi = h_ref[0, pl.ds(off, n2), :], h_ref[1, pl.ds(off, n2), :]
            xr, xi = x[i][:n2], x[i][n2:]
            y.append(jnp.concatenate([xr * hr - xi * hi, xr * hi + xi * hr], axis=0).astype(BF16))
        z = [_dot(g_ref[k1s[i]], y[i]) for i in range(unroll_b)]
        for i, k in enumerate(k1s):
            for sl in slabs:
                s_ref[sl, pl.ds(k, n2, stride=pitch), :] = lanes(z[i][:n2], sl)
                s_ref[sl, pl.ds(kp + k, n2, stride=pitch), :] = lanes(z[i][n2:], sl)
        return carry

    lax.fori_loop(0, kh // unroll_b, stage_b, 0)

    def stage_c(jj, carry):
        zz = [jnp.concatenate([s_ref[sl, pl.ds(pl.multiple_of((jj * grp + j0) * pitch, grp), 2 * kp), :]
                               for sl in slabs], axis=1) for j0 in range(grp)]
        y = _dot(fc_ref[...], jnp.concatenate(zz, axis=0).astype(BF16))
        for m in range(n1h):
            o_ref[0, pl.ds(pl.multiple_of(m * n2 + jj * grp, grp), grp), :] = y[m * grp:(m + 1) * grp]
        return carry

    lax.fori_loop(0, n2 // grp, stage_c, 0)


def _hy_conv(u, h, fa, fb, gb, fc):
    b, l, c = u.shape
    n2, n1x2, n1h = fa.shape
    n1 = n1x2 // 2
    kh, kp, pitch = _conv_dims(n1)
    grp = SUBLANES
    ct = CONV_SLABS * LANES

    def half(t, axis):
        re, im = jnp.split(t, 2, axis=axis)
        pad = [(0, 0)] * t.ndim
        pad[axis] = (0, kp - kh)
        cut = lambda x: jnp.pad(lax.slice_in_dim(x, 0, kh, axis=axis), pad)
        return jnp.concatenate([cut(re), cut(im)], axis=axis)

    fa_h = half(fa, 1)
    la = (fa_h.reshape(n2 // grp, grp, 2 * kp, n1h, 1) * jnp.eye(grp, dtype=fa.dtype).reshape(1, grp, 1, 1, grp))
    la = la.reshape(n2 // grp, grp * 2 * kp, n1h * grp)
    wgt = jnp.ones((kh,), F32).at[1:kh - 1].set(2.0)
    wgt = jnp.pad(wgt, (0, kp - kh))
    fc_h = (half(fc, 1).astype(F32) * jnp.tile(wgt, 2)[None, :]).astype(BF16)
    lc = fc_h.reshape(n1h, 1, 1, 2 * kp) * jnp.eye(grp, dtype=fc_h.dtype).reshape(1, grp, grp, 1)
    fc_h = lc.reshape(n1h * grp, grp * 2 * kp)
    gb_h = gb[:kh]
    unroll_b = 3 if kh % 3 == 0 else 1
    once = pl.Buffered(1)
    return pl.pallas_call(
        functools.partial(_hy_conv_kernel, n1=n1, unroll_b=unroll_b),
        grid=(c // ct, b),
        in_specs=[
            pl.BlockSpec((1, l, ct), lambda j, i: (i, 0, j), pipeline_mode=once),
            pl.BlockSpec((2, kh * n2, ct), lambda j, i: (0, 0, j), pipeline_mode=once),
            pl.BlockSpec(la.shape, lambda j, i: (0, 0, 0), pipeline_mode=once),
            pl.BlockSpec(fb.shape, lambda j, i: (0, 0), pipeline_mode=once),
            pl.BlockSpec(gb_h.shape, lambda j, i: (0, 0, 0), pipeline_mode=once),
            pl.BlockSpec(fc_h.shape, lambda j, i: (0, 0), pipeline_mode=once),
        ],
        out_specs=pl.BlockSpec((1, l, ct), lambda j, i: (i, 0, j)),
        out_shape=jax.ShapeDtypeStruct((b, l, c), F32),
        scratch_shapes=[pltpu.VMEM((CONV_SLABS, n2 * pitch, LANES), F32)],
        compiler_params=_cparams("arbitrary", "arbitrary"),
        name="hy_conv",
    )(u, h, la, fb, gb_h, fc_h)


def _hyena_branch(z, l, conv_w, conv_b, w1, b1, w2, b2, w3, freq, skip):
    c = HY_WIDTH
    n = 2 * l
    fa, fb, gb, fc = _fft_consts(l)
    filt = _hy_filters(l, w1, b1, w2, b2, w3, freq, skip)
    fa_spec = _fft_a(filt, fa).reshape(2, 2, n, c)
    h = _fft_bf(fa_spec, fb)
    u, e = _hy_prep(z, conv_w, conv_b)
    return _hy_conv(u, h, fa, fb, gb, fc), e


def _gdn_prep_kernel(zc_ref, zp_ref, zn_ref, cw_ref, q_ref, k_ref, v_ref, buf_ref):
    t = pl.program_id(1)
    nt = pl.num_programs(1)
    tl = zc_ref.shape[1]
    dh = GDN_HEAD_DIM
    hr = HALO_ROWS
    _fill_halo_buf(buf_ref, zc_ref, zp_ref, zn_ref, t, nt)
    outs = (q_ref, k_ref, v_ref)
    for p in range(3):
        for h in range(GDN_HEADS):
            lo = p * GDN_WIDTH + h * dh
            acc = buf_ref[pl.ds(hr - 2, tl), lo:lo + dh] * cw_ref[0:1, lo:lo + dh]
            for j in range(1, 5):
                acc = acc + buf_ref[pl.ds(hr - 2 + j, tl), lo:lo + dh] * cw_ref[j:j + 1, lo:lo + dh]
            a = _silu(acc)
            if p < 2:
                a = a * lax.rsqrt(jnp.sum(a * a, axis=-1, keepdims=True) + EPS)
            if p == 0:
                a = a * (dh ** -0.5)
            outs[p][0, :, h * dh:(h + 1) * dh] = a.astype(BF16)


def _gdn_prep(z, conv_w, tl=256):
    b, l, _ = z.shape
    w = 3 * GDN_WIDTH
    cur, prev, nxt = _halo_specs(tl, w, l, E_QKV // w)
    o_spec = pl.BlockSpec((1, tl, GDN_WIDTH), lambda i, t: (i, t, 0))
    o_shape = jax.ShapeDtypeStruct((b, l, GDN_WIDTH), BF16)
    return pl.pallas_call(
        _gdn_prep_kernel,
        grid=(b, l // tl),
        in_specs=[cur, prev, nxt, pl.BlockSpec((5, w), lambda i, t: (0, 0))],
        out_specs=[o_spec, o_spec, o_spec],
        out_shape=[o_shape, o_shape, o_shape],
        scratch_shapes=[pltpu.VMEM((tl + 2 * HALO_ROWS, w), F32)],
        compiler_params=_cparams("parallel", "arbitrary"),
        name="gdn_prep",
    )(z, z, z, conv_w.astype(F32))


def _dot_nt(a, b):
    return lax.dot_general(a, b, (((1,), (1,)), ((), ())), preferred_element_type=F32)


def _dot_tn(a, b):
    return lax.dot_general(a, b, (((0,), (0,)), ((), ())), preferred_element_type=F32)


def _split3(x):
    a = x.astype(BF16)
    r = x - a.astype(F32)
    b = r.astype(BF16)
    c = (r - b.astype(F32)).astype(BF16)
    return a, b, c


GDN_CHUNKS_PER_STEP = 4


def _gdn_scan_kernel(*refs, direction, final):
    if final:
        q_ref, k_ref, v_ref, zs_ref, al_ref, dt_ref, of_ref, gg_ref, ng_ref, o_ref, s_ref = refs
    else:
        q_ref, k_ref, v_ref, zs_ref, al_ref, dt_ref, o_ref, s_ref = refs
    c = GDN_CHUNK
    dh = GDN_HEAD_DIM
    nsub = q_ref.shape[1] // c

    @pl.when(pl.program_id(1) == 0)
    def _():
        s_ref[...] = jnp.zeros_like(s_ref)

    row = lax.broadcasted_iota(jnp.int32, (c, c), 0)
    col = lax.broadcasted_iota(jnp.int32, (c, c), 1)
    if direction == 0:
        incl, strict, last = col <= row, col < row, c - 1
    else:
        incl, strict, last = col >= row, col > row, 0
    tri = incl.astype(BF16)
    eye = (row == col).astype(F32)
    heads = range(GDN_HEADS)
    hsl = [slice(h * dh, (h + 1) * dh) for h in heads]
    lns = [2 * GDN_HEADS + direction * GDN_HEADS + h for h in heads]

    rows = [slice(i * c, (i + 1) * c) for i in range(nsub)]
    gcum, gcum_t, beta_all = [], [], []
    for rs in rows:
        zs = zs_ref[0, rs, :]
        beta_all.append(jax.nn.sigmoid(zs))
        gl = -jnp.exp(al_ref[...]) * jax.nn.softplus(zs + dt_ref[...])
        g1, g2, g3 = _split3(gl)
        gsum = _dot(tri, g1) + _dot(tri, g2) + _dot(tri, g3)
        gcum.append(gsum)
        gcum_t.append(jnp.concatenate([gsum, jnp.zeros_like(gsum)], axis=0).T)
    items = [(i, h) for i in range(nsub) for h in heads]
    it = range(len(items))
    beta = [beta_all[i][:, direction * GDN_HEADS + h:direction * GDN_HEADS + h + 1] for i, h in items]
    gc = [gcum[i][:, lns[h]:lns[h] + 1] for i, h in items]
    gct = [gcum_t[i][lns[h]:lns[h] + 1, 0:c] for i, h in items]
    glast = [gcum_t[i][lns[h]:lns[h] + 1, last:last + 1] for i, h in items]
    qh = [q_ref[0, rows[i], hsl[h]] for i, h in items]
    kh = [k_ref[0, rows[i], hsl[h]] for i, h in items]
    vh = [v_ref[0, rows[i], hsl[h]] for i, h in items]
    kq = [_dot_nt(jnp.concatenate([kh[x], qh[x]], axis=0), kh[x]) for x in it]
    dmat = [jnp.exp(jnp.where(incl, gc[x] - gct[x], -jnp.inf)) for x in it]
    a = [jnp.where(strict, (beta[x] * kq[x][:c]) * dmat[x], 0.0) for x in it]
    tinv = [eye - a[x] for x in it]
    ab = [a[x].astype(BF16) for x in it]
    p = [_dot(ab[x], ab[x]) for x in it]
    for j in range(5):
        pb = [p[x].astype(BF16) for x in it]
        if j < 4:
            tp = [_dot(jnp.concatenate([tinv[x].astype(BF16), pb[x]], axis=0), pb[x]) for x in it]
            tinv = [tinv[x] + tp[x][:c] for x in it]
            p = [tp[x][c:] for x in it]
        else:
            tinv = [tinv[x] + _dot(tinv[x].astype(BF16), pb[x]) for x in it]
    eg = [jnp.exp(gc[x]) for x in it]
    kf = [kh[x].astype(F32) for x in it]
    rhs = [jnp.concatenate([vh[x].astype(F32) * beta[x], kf[x] * (beta[x] * eg[x])], axis=1).astype(BF16) for x in it]
    uw = [_dot(tinv[x].astype(BF16), rhs[x]) for x in it]
    wq = [jnp.concatenate([uw[x][:, dh:], qh[x].astype(F32) * eg[x]], axis=0).astype(BF16) for x in it]
    qkd = [(kq[x][c:] * dmat[x]).astype(BF16) for x in it]
    kd = [(kf[x] * jnp.exp(glast[x] - gc[x])).astype(BF16) for x in it]

    order = list(range(nsub)) if direction == 0 else list(range(nsub - 1, -1, -1))
    state = [s_ref[h] for h in heads]
    for i in order:
        rs = rows[i]
        xs = [i * GDN_HEADS + h for h in heads]
        ws = [_dot(wq[xs[h]], state[h].astype(BF16)) for h in heads]
        vnb = [(uw[xs[h]][:, :dh] - ws[h][:c]).astype(BF16) for h in heads]
        o = [ws[h][c:] + _dot(qkd[xs[h]], vnb[h]) for h in heads]
        state = [state[h] * jnp.exp(glast[xs[h]]) + _dot_tn(kd[xs[h]], vnb[h]) for h in heads]
        for h in heads:
            if final:
                tot = of_ref[0, rs, hsl[h]] + o[h]
                y = tot * lax.rsqrt(jnp.mean(tot * tot, axis=-1, keepdims=True) + EPS) * ng_ref[...]
                o_ref[0, rs, hsl[h]] = (y * _silu(gg_ref[0, rs, hsl[h]].astype(F32))).astype(o_ref.dtype)
            else:
                o_ref[0, rs, hsl[h]] = o[h].astype(o_ref.dtype)
    for h in heads:
        s_ref[h] = state[h]


def _gdn_scan(q, k, v, zs, a_log, dt_bias, direction, o_fwd=None, z=None, gate_blk=None, norm_g=None):
    b, l, w = q.shape
    c = GDN_CHUNK * GDN_CHUNKS_PER_STEP
    n = l // c
    final = o_fwd is not None
    if direction == 0:
        cmap = lambda i, t: (i, t, 0)
    else:
        cmap = lambda i, t: (i, n - 1 - t, 0)
    blk = pl.BlockSpec((1, c, w), cmap)
    vec = pl.BlockSpec((1, LANES), lambda i, t: (0, 0))
    pad = lambda x: jnp.pad(x.reshape(1, -1).astype(F32), ((0, 0), (2 * GDN_HEADS, LANES - 4 * GDN_HEADS)))
    in_specs = [blk, blk, blk, pl.BlockSpec((1, c, LANES), cmap), vec, vec]
    args = [q, k, v, zs, pad(a_log), pad(dt_bias)]
    if final:
        if direction == 0:
            gmap = lambda i, t: (i, t, gate_blk)
        else:
            gmap = lambda i, t: (i, n - 1 - t, gate_blk)
        in_specs += [blk, pl.BlockSpec((1, c, w), gmap), pl.BlockSpec((1, GDN_HEAD_DIM), lambda i, t: (0, 0))]
        args += [o_fwd, z, norm_g.reshape(1, GDN_HEAD_DIM).astype(F32)]
    return pl.pallas_call(
        functools.partial(_gdn_scan_kernel, direction=direction, final=final),
        grid=(b, n),
        in_specs=in_specs,
        out_specs=blk,
        out_shape=jax.ShapeDtypeStruct((b, l, w), BF16 if final else F32),
        scratch_shapes=[pltpu.VMEM((GDN_HEADS, GDN_HEAD_DIM, GDN_HEAD_DIM), F32)],
        compiler_params=_cparams("parallel", "arbitrary"),
        name="gdn_scan_bwd" if direction else "gdn_scan_fwd",
    )(*args)


def _gdn_branch(z, zs, conv_w, a_log, dt_bias, norm_g, gate_blk):
    q, k, v = _gdn_prep(z, conv_w)
    o_f = _gdn_scan(q, k, v, zs, a_log, dt_bias, 0)
    return _gdn_scan(q, k, v, zs, a_log, dt_bias, 1, o_fwd=o_f, z=z, gate_blk=gate_blk, norm_g=norm_g)


def _softmax_t(s, extra=None):
    m = jnp.max(s, axis=0, keepdims=True)
    if extra is not None:
        m = jnp.maximum(m, extra)
    p = jnp.exp(s - m)
    den = jnp.sum(p, axis=0, keepdims=True)
    if extra is not None:
        den = den + jnp.exp(extra - m)
    return p.astype(BF16), den, m


def _t_bf16(x):
    return x.astype(F32).T.astype(BF16)


def _xattn_kernel(q_ref, g_ref, kv_ref, o_ref):
    dh = X_HEAD_DIM
    for h in range(X_HEADS):
        hs = slice(h * dh, (h + 1) * dh)
        qh = q_ref[0, :, hs].astype(BF16)
        kh = kv_ref[0, :, hs]
        vh = kv_ref[0, :, X_WIDTH + h * dh:X_WIDTH + (h + 1) * dh]
        s = _dot_nt(qh, kh) * (dh ** -0.5)
        p = jnp.exp(s - jnp.max(s, axis=-1, keepdims=True))
        den = jnp.sum(p, axis=-1, keepdims=True)
        o = _dot(p.astype(BF16), vh) / den
        o_ref[0, :, hs] = (o * _silu(g_ref[0, :, hs].astype(F32))).astype(o_ref.dtype)


def _xattn(z, kv, q_blk, g_blk, tq=512):
    b, l, _ = z.shape
    m = kv.shape[1]
    w = X_WIDTH
    return pl.pallas_call(
        _xattn_kernel,
        grid=(b, l // tq),
        in_specs=[
            pl.BlockSpec((1, tq, w), lambda i, t: (i, t, q_blk)),
            pl.BlockSpec((1, tq, w), lambda i, t: (i, t, g_blk)),
            pl.BlockSpec((1, m, 2 * w), lambda i, t: (i, 0, 0)),
        ],
        out_specs=pl.BlockSpec((1, tq, w), lambda i, t: (i, t, 0)),
        out_shape=jax.ShapeDtypeStruct((b, l, w), BF16),
        compiler_params=_cparams("parallel", "arbitrary"),
        name="xattn",
    )(z, z, kv)


def _out_proj_kernel(*refs, widths, gated):
    nparts = len(widths)
    y_refs = refs[:nparts]
    if gated:
        e_ref, w_ref, g_ref, x_ref, o_ref = refs[nparts:]
    else:
        w_ref, g_ref, x_ref, o_ref = refs[nparts:]
    d = o_ref.shape[1]
    cw = 256
    ys = [y_ref[...] for y_ref in y_refs]
    if gated:
        ys[0] = ys[0] * e_ref[...]
    ys = [y.astype(BF16) for y in ys]
    ssq = jnp.zeros((o_ref.shape[0], 1), F32)
    for c0 in range(0, d, cw):
        acc = None
        off = 0
        for y, wd in zip(ys, widths):
            part = _dot(y, w_ref[off:off + wd, c0:c0 + cw])
            acc = part if acc is None else acc + part
            off += wd
        ssq = ssq + jnp.sum(acc * acc, axis=-1, keepdims=True)
        o_ref[:, c0:c0 + cw] = acc
    r = lax.rsqrt(ssq * (1.0 / d) + EPS)
    for c0 in range(0, d, cw):
        o_ref[:, c0:c0 + cw] = x_ref[:, c0:c0 + cw] + o_ref[:, c0:c0 + cw] * r * g_ref[:, c0:c0 + cw]


def _out_proj(parts, w_out, post_g, x2d, gate=None, tm=512):
    t, d = x2d.shape
    widths = tuple(int(p.shape[1]) for p in parts)
    kdim = sum(widths)
    gates = [] if gate is None else [gate]
    return pl.pallas_call(
        functools.partial(_out_proj_kernel, widths=widths, gated=gate is not None),
        grid=(t // tm,),
        in_specs=[pl.BlockSpec((tm, wd), lambda i: (i, 0)) for wd in widths]
        + [pl.BlockSpec((tm, widths[0]), lambda i: (i, 0)) for _ in gates] + [
            pl.BlockSpec((kdim, d), lambda i: (0, 0)),
            pl.BlockSpec((1, d), lambda i: (0, 0)),
            pl.BlockSpec((tm, d), lambda i: (i, 0)),
        ],
        out_specs=pl.BlockSpec((tm, d), lambda i: (i, 0)),
        out_shape=jax.ShapeDtypeStruct((t, d), F32),
        compiler_params=_cparams("parallel"),
        name="out_proj",
    )(*parts, *gates, w_out.astype(BF16), post_g.reshape(1, d).astype(F32), x2d)


def _mem_kv(mem, mem_g, w_mem_kv):
    b, m, d = mem.shape
    kv = _norm_matmul(mem.reshape(b * m, d), mem_g, w_mem_kv, 1024, out_dtype=BF16, name="mem_kv")
    return kv.reshape(b, m, 2 * X_WIDTH)


E_QKV, E_HY, E_GHY, E_GGDN, E_XQ, E_GX, E_BIG = 0, 3072, 6144, 7168, 8192, 8704, 9216


def _even_layer(x, mem, pre_g, post_g, w_in, w_out, hy_conv_w, hy_conv_b, hy_w1, hy_b1, hy_w2, hy_b2, hy_w3, hy_freq,
                hy_skip, gdn_conv_w, gdn_a_log, gdn_dt_bias, gdn_norm_g, mem_g, w_mem_kv):
    b, l, d = x.shape
    x2d = x.reshape(b * l, d)
    w_big = jnp.concatenate([w_in[:, 4096:7168], w_in[:, 0:3072], w_in[:, 3072:4096], w_in[:, 7168:8192],
                             w_in[:, 8224:8736], w_in[:, 8736:9248]], axis=1)
    w_small = jnp.pad(w_in[:, 8192:8224], ((0, 0), (0, LANES - 4 * GDN_HEADS)))
    z = _norm_matmul(x2d, pre_g, w_big, 1024, out_dtype=BF16, tm=1024, name="even_in").reshape(b, l, E_BIG)
    zs = _norm_matmul(x2d, pre_g, w_small, LANES, name="even_in_gates").reshape(b, l, LANES)
    conv, e = _hyena_branch(z, l, hy_conv_w, hy_conv_b, hy_w1, hy_b1, hy_w2, hy_b2, hy_w3, hy_freq, hy_skip)
    y_b = _gdn_branch(z, zs, gdn_conv_w, gdn_a_log, gdn_dt_bias, gdn_norm_g, E_GGDN // GDN_WIDTH)
    kv = _mem_kv(mem, mem_g, w_mem_kv)
    y_x = _xattn(z, kv, E_XQ // X_WIDTH, E_GX // X_WIDTH)
    t = b * l
    out = _out_proj([conv.reshape(t, -1), y_b.reshape(t, -1), y_x.reshape(t, -1)], w_out, post_g, x2d,
                    gate=e.reshape(t, -1))
    return out.reshape(b, l, d)


DIL_PATTERNS = ((128, 1), (512, 4), (2048, 16))
N_DIL = len(DIL_PATTERNS)
DIL_HEADS = 4
DIL_HEAD_DIM = 128
DIL_WIDTH = DIL_HEADS * DIL_HEAD_DIM
SWA_Q_HEADS = 16
SWA_KV_HEADS = 2
SWA_HEAD_DIM = 64
SWA_WIDTH = SWA_Q_HEADS * SWA_HEAD_DIM
SWA_HALF_WINDOW = 128
O_CQKV, O_GC, O_DQ, O_GD, O_XQ, O_GX, O_DKV, O_ALL = 0, 4608, 5120, 6144, 7168, 7680, 8192, 8448


def _rope_tables(l, dh):
    half = dh // 8
    inv = ROPE_THETA ** (-jnp.arange(half, dtype=F32) / half)
    ang = jnp.arange(l, dtype=F32)[:, None] * inv[None, :]
    cos, sin = jnp.cos(ang), jnp.sin(ang)
    one = jnp.ones((l, dh - 2 * half), F32)
    zero_h = jnp.zeros((l, half), F32)
    zero_r = jnp.zeros((l, dh - 2 * half), F32)
    c = jnp.concatenate([cos, cos, one], axis=1)
    sa = jnp.concatenate([-sin, zero_h, zero_r], axis=1)
    sb = jnp.concatenate([zero_h, sin, zero_r], axis=1)
    rep = LANES // dh
    return tuple(jnp.tile(t, (1, rep)) for t in (c, sa, sb))


def _odd_prep_kernel(c_ref, dq_ref, dkv_ref, c1_ref, a1_ref, b1_ref, c2_ref, a2_ref, b2_ref,
                     cq_ref, ck_ref, cv_ref, dqe_ref, dk_ref, dv_ref):
    tl = c_ref.shape[1]
    lane = lax.broadcasted_iota(jnp.int32, (tl, LANES), 1)
    low = lane < SWA_HEAD_DIM
    c1, a1, b1 = c1_ref[...], a1_ref[...], b1_ref[...]
    c2, a2, b2 = c2_ref[...], a2_ref[...], b2_ref[...]
    h1 = DIL_HEAD_DIM // 8
    h2 = SWA_HEAD_DIM // 8

    def rope1(x):
        return x * c1 + pltpu.roll(x, LANES - h1, 1) * a1 + pltpu.roll(x, h1, 1) * b1

    def rope2(x):
        return x * c2 + pltpu.roll(x, LANES - h2, 1) * a2 + pltpu.roll(x, h2, 1) * b2

    nblk = N_DIL * DIL_HEADS
    for j in range(nblk):
        ls = slice(j * LANES, (j + 1) * LANES)
        cq_ref[0, :, ls] = rope1(c_ref[0, :, ls].astype(F32)).astype(BF16)
        ck_ref[0, :, ls] = rope1(c_ref[0, :, nblk * LANES + j * LANES:nblk * LANES + (j + 1) * LANES].astype(F32)).astype(BF16)
        cv_ref[0, :, ls] = c_ref[0, :, 2 * nblk * LANES + j * LANES:2 * nblk * LANES + (j + 1) * LANES].astype(BF16)
    for j in range(SWA_Q_HEADS // 2):
        xr = rope2(dq_ref[0, :, j * LANES:(j + 1) * LANES].astype(F32))
        dqe_ref[0, :, (2 * j) * LANES:(2 * j + 1) * LANES] = jnp.where(low, xr, 0.0).astype(BF16)
        dqe_ref[0, :, (2 * j + 1) * LANES:(2 * j + 2) * LANES] = jnp.where(low, pltpu.roll(xr, SWA_HEAD_DIM, 1), 0.0).astype(BF16)
    kr = rope2(dkv_ref[0, :, 0:LANES].astype(F32))
    vv = dkv_ref[0, :, LANES:2 * LANES].astype(F32)
    kr_sw = pltpu.roll(kr, SWA_HEAD_DIM, 1)
    vv_sw = pltpu.roll(vv, SWA_HEAD_DIM, 1)
    dk_ref[0, :, 0:LANES] = jnp.where(low, kr, 0.0).astype(BF16)
    dk_ref[0, :, LANES:2 * LANES] = jnp.where(low, kr_sw, 0.0).astype(BF16)
    dv_ref[0, :, 0:LANES] = jnp.where(low, vv, vv_sw).astype(BF16)
    dv_ref[0, :, LANES:2 * LANES] = jnp.where(low, vv_sw, vv).astype(BF16)


def _odd_prep(z, tl=256):
    b, l, _ = z.shape
    wc = 3 * N_DIL * DIL_WIDTH
    t1 = _rope_tables(l, DIL_HEAD_DIM)
    t2 = _rope_tables(l, SWA_HEAD_DIM)
    tab = pl.BlockSpec((tl, LANES), lambda i, t: (t, 0))

    def spec(w):
        return pl.BlockSpec((1, tl, w), lambda i, t: (i, t, 0))

    def shape(w):
        return jax.ShapeDtypeStruct((b, l, w), BF16)

    wd = N_DIL * DIL_WIDTH
    return pl.pallas_call(
        _odd_prep_kernel,
        grid=(b, l // tl),
        in_specs=[
            pl.BlockSpec((1, tl, wc), lambda i, t: (i, t, 0)),
            pl.BlockSpec((1, tl, SWA_WIDTH), lambda i, t: (i, t, O_DQ // SWA_WIDTH)),
            pl.BlockSpec((1, tl, 2 * LANES), lambda i, t: (i, t, O_DKV // (2 * LANES))),
            tab, tab, tab, tab, tab, tab,
        ],
        out_specs=[spec(wd), spec(wd), spec(wd), spec(2 * SWA_WIDTH), spec(2 * LANES), spec(2 * LANES)],
        out_shape=[shape(wd), shape(wd), shape(wd), shape(2 * SWA_WIDTH), shape(2 * LANES), shape(2 * LANES)],
        compiler_params=_cparams("parallel", "arbitrary"),
        name="odd_prep",
    )(z, z, z, *t1, *t2)


def _band_bias_t(tq, half, ls):
    p0 = pl.program_id(1) * tq
    j = lax.broadcasted_iota(jnp.int32, (tq + 2 * half, tq), 0)
    i = lax.broadcasted_iota(jnp.int32, (tq + 2 * half, tq), 1)
    kpos = p0 - half + j
    valid = (j >= i) & (j - i <= 2 * half) & (kpos >= 0) & (kpos < ls)
    return jnp.where(valid, 0.0, -jnp.inf)


def _band_mask(tq, half, ls):
    p0 = pl.program_id(1) * tq
    i = lax.broadcasted_iota(jnp.int32, (tq, tq + 2 * half), 0)
    j = lax.broadcasted_iota(jnp.int32, (tq, tq + 2 * half), 1)
    kpos = p0 - half + j
    return (j >= i) & (j - i <= 2 * half) & (kpos >= 0) & (kpos < ls)


def _band_attn_kernel(q_ref, kp_ref, kc_ref, kn_ref, vp_ref, vc_ref, vn_ref, o_ref, lse_ref, *, half, ls):
    tq = q_ref.shape[1]
    dh = DIL_HEAD_DIM
    valid = _band_mask(tq, half, ls)
    lane = lax.broadcasted_iota(jnp.int32, (tq, LANES), 1)
    lse_all = jnp.zeros((tq, LANES), F32)
    heads = range(DIL_HEADS)
    hsl = [slice(h * dh, (h + 1) * dh) for h in heads]
    kall = [jnp.concatenate([kp_ref[0, :, s], kc_ref[0, :, s], kn_ref[0, :, s]], axis=0) for s in hsl]
    vall = [jnp.concatenate([vp_ref[0, :, s], vc_ref[0, :, s], vn_ref[0, :, s]], axis=0) for s in hsl]
    sc = [jnp.where(valid, _dot_nt(q_ref[0, :, hsl[h]], kall[h]) * (dh ** -0.5), -jnp.inf) for h in heads]
    m = [jnp.max(sc[h], axis=-1, keepdims=True) for h in heads]
    p = [jnp.exp(sc[h] - m[h]) for h in heads]
    den = [jnp.sum(p[h], axis=-1, keepdims=True) for h in heads]
    o = [_dot(p[h].astype(BF16), vall[h]) / den[h] for h in heads]
    for h in heads:
        o_ref[0, :, hsl[h]] = o[h].astype(o_ref.dtype)
        lse_all = jnp.where(lane == h, m[h] + jnp.log(den[h]), lse_all)
    lse_ref[0] = lse_all


def _band_specs(tq, half, ls, w, col):
    r = tq // half
    nb = ls // half
    cur = pl.BlockSpec((1, tq, w), lambda i, t: (i, t, col))
    prev = pl.BlockSpec((1, half, w), lambda i, t: (i, jnp.maximum(t * r - 1, 0), col))
    nxt = pl.BlockSpec((1, half, w), lambda i, t: (i, jnp.minimum((t + 1) * r, nb - 1), col))
    return cur, prev, nxt


def _band_attn(q, k, v, half):
    n, ls, w = q.shape
    tq = min(2 * LANES, ls)
    cur, prev, nxt = _band_specs(tq, half, ls, w, 0)
    return pl.pallas_call(
        functools.partial(_band_attn_kernel, half=half, ls=ls),
        grid=(n, ls // tq),
        in_specs=[cur, prev, cur, nxt, prev, cur, nxt],
        out_specs=[cur, pl.BlockSpec((1, tq, LANES), lambda i, t: (i, t, 0))],
        out_shape=[jax.ShapeDtypeStruct((n, ls, w), F32), jax.ShapeDtypeStruct((n, ls, LANES), F32)],
        compiler_params=_cparams("parallel", "arbitrary"),
        name="band_attn",
    )(q, k, k, k, v, v, v)


def _dil_merge_kernel(o0_ref, o1_ref, o2_ref, l0_ref, l1_ref, l2_ref, g_ref, y_ref):
    dh = DIL_HEAD_DIM
    o_refs = (o0_ref, o1_ref, o2_ref)
    for h in range(DIL_HEADS):
        hs = slice(h * dh, (h + 1) * dh)
        ls = [r[0, :, h:h + 1] for r in (l0_ref, l1_ref, l2_ref)]
        m = jnp.maximum(jnp.maximum(ls[0], ls[1]), ls[2])
        ws = [jnp.exp(x - m) for x in ls]
        den = ws[0] + ws[1] + ws[2]
        y = (ws[0] / den) * o_refs[0][0, :, hs] + (ws[1] / den) * o_refs[1][0, :, hs] + (ws[2] / den) * o_refs[2][0, :, hs]
        y_ref[0, :, hs] = (y * _silu(g_ref[0, :, hs].astype(F32))).astype(y_ref.dtype)


def _dil_merge(outs, lses, z, tl=512):
    b, l, w = outs[0].shape
    o_spec = pl.BlockSpec((1, tl, w), lambda i, t: (i, t, 0))
    l_spec = pl.BlockSpec((1, tl, LANES), lambda i, t: (i, t, 0))
    return pl.pallas_call(
        _dil_merge_kernel,
        grid=(b, l // tl),
        in_specs=[o_spec, o_spec, o_spec, l_spec, l_spec, l_spec, pl.BlockSpec((1, tl, w), lambda i, t: (i, t, O_GC // DIL_WIDTH))],
        out_specs=o_spec,
        out_shape=jax.ShapeDtypeStruct((b, l, w), BF16),
        compiler_params=_cparams("parallel", "arbitrary"),
        name="dil_merge",
    )(*outs, *lses, z)


def _dilated_branch(cq, ck, cv, z):
    b, l, _ = cq.shape
    w = DIL_WIDTH
    outs, lses = [], []
    for gi, (window, d) in enumerate(DIL_PATTERNS):
        half = window // (2 * d)
        ls = l // d

        def to_res(x):
            x = x[:, :, gi * w:(gi + 1) * w]
            if d == 1:
                return x
            return jnp.swapaxes(x.reshape(b, ls, d, w), 1, 2).reshape(b * d, ls, w)

        def from_res(x):
            if d == 1:
                return x
            return jnp.swapaxes(x.reshape(b, d, ls, x.shape[-1]), 1, 2).reshape(b, l, x.shape[-1])

        o, lse = _band_attn(to_res(cq), to_res(ck), to_res(cv), half)
        outs.append(from_res(o))
        lses.append(from_res(lse))
    return _dil_merge(outs, lses, z)


def _swa_kernel(q_ref, kp_ref, kc_ref, kn_ref, vp_ref, vc_ref, vn_ref, g_ref, sink_ref, o_ref, *, half, ls):
    tq = q_ref.shape[1]
    dh = SWA_HEAD_DIM
    bias = _band_bias_t(tq, half, ls)
    low = lax.broadcasted_iota(jnp.int32, (LANES, tq), 0) < dh
    grp = SWA_Q_HEADS // SWA_KV_HEADS
    for g in range(SWA_KV_HEADS):
        gs = slice(g * LANES, (g + 1) * LANES)
        kall = jnp.concatenate([kp_ref[0, :, gs], kc_ref[0, :, gs], kn_ref[0, :, gs]], axis=0)
        vt = _t_bf16(jnp.concatenate([vp_ref[0, :, gs], vc_ref[0, :, gs], vn_ref[0, :, gs]], axis=0))
        hs = range(g * grp, (g + 1) * grp)
        sc = [_dot_nt(kall, q_ref[0, :, h * LANES:(h + 1) * LANES]) * (dh ** -0.5) + bias for h in hs]
        sm = [_softmax_t(sc[i], sink_ref[:, h:h + 1]) for i, h in enumerate(hs)]
        ot = [_dot(vt, sm[i][0]) / sm[i][1] for i in range(grp)]
        for jp in range(grp // 2):
            blk = (g * grp) // 2 + jp
            bs = slice(blk * LANES, (blk + 1) * LANES)
            y = jnp.where(low, ot[2 * jp], ot[2 * jp + 1]).T * _silu(g_ref[0, :, bs].astype(F32))
            o_ref[0, :, bs] = y.astype(o_ref.dtype)


def _swa_branch(dqe, dk, dv, z, sink, tq=128):
    b, l, _ = dqe.shape
    half = SWA_HALF_WINDOW
    _, prev, nxt = _band_specs(tq, half, l, 2 * LANES, 0)
    cur = pl.BlockSpec((1, tq, 2 * LANES), lambda i, t: (i, t, 0))
    sink_p = jnp.pad(sink.reshape(1, SWA_Q_HEADS).astype(F32), ((0, 0), (0, LANES - SWA_Q_HEADS)))
    return pl.pallas_call(
        functools.partial(_swa_kernel, half=half, ls=l),
        grid=(b, l // tq),
        in_specs=[
            pl.BlockSpec((1, tq, 2 * SWA_WIDTH), lambda i, t: (i, t, 0)),
            prev, cur, nxt, prev, cur, nxt,
            pl.BlockSpec((1, tq, SWA_WIDTH), lambda i, t: (i, t, O_GD // SWA_WIDTH)),
            pl.BlockSpec((1, LANES), lambda i, t: (0, 0)),
        ],
        out_specs=pl.BlockSpec((1, tq, SWA_WIDTH), lambda i, t: (i, t, 0)),
        out_shape=jax.ShapeDtypeStruct((b, l, SWA_WIDTH), BF16),
        compiler_params=_cparams("parallel", "arbitrary"),
        name="swa",
    )(dqe, dk, dk, dk, dv, dv, dv, z, sink_p)


def _odd_layer(x, mem, pre_g, post_g, w_in, w_out, swa_sink, mem_g, w_mem_kv):
    b, l, d = x.shape
    x2d = x.reshape(b * l, d)
    w_re = jnp.concatenate([w_in[:, 0:6144], w_in[:, 6400:8448], w_in[:, 6144:6400]], axis=1)
    z = _norm_matmul(x2d, pre_g, w_re, 1408, out_dtype=BF16, tm=1024, name="odd_in").reshape(b, l, O_ALL)
    cq, ck, cv, dqe, dk, dv = _odd_prep(z)
    y_c = _dilated_branch(cq, ck, cv, z)
    y_d = _swa_branch(dqe, dk, dv, z, swa_sink)
    kv = _mem_kv(mem, mem_g, w_mem_kv)
    y_x = _xattn(z, kv, O_XQ // X_WIDTH, O_GX // X_WIDTH)
    t = b * l
    out = _out_proj([y_c.reshape(t, -1), y_d.reshape(t, -1), y_x.reshape(t, -1)], w_out, post_g, x2d)
    return out.reshape(b, l, d)


def _trunk(x, mem, even_params, odd_params):
    x = _even_layer(x, mem, *[p[0] for p in even_params])
    return _odd_layer(x, mem, *[p[0] for p in odd_params])


def kernel(x_prompt, x_sample, mem_prompt, mem_sample, e_pre_g, e_post_g, e_w_in, e_w_out, hy_conv_w, hy_conv_b,
           hy_filt_w1, hy_filt_b1, hy_filt_w2, hy_filt_b2, hy_filt_w3, hy_freq, hy_skip, gdn_conv_w, gdn_A_log,
           gdn_dt_bias, gdn_norm_g, e_mem_g, e_w_mem_kv, o_pre_g, o_post_g, o_w_in, o_w_out, swa_sink, o_mem_g,
           o_w_mem_kv):
    even_params = (e_pre_g, e_post_g, e_w_in, e_w_out, hy_conv_w, hy_conv_b, hy_filt_w1, hy_filt_b1, hy_filt_w2,
                   hy_filt_b2, hy_filt_w3, hy_freq, hy_skip, gdn_conv_w, gdn_A_log, gdn_dt_bias, gdn_norm_g,
                   e_mem_g, e_w_mem_kv)
    odd_params = (o_pre_g, o_post_g, o_w_in, o_w_out, swa_sink, o_mem_g, o_w_mem_kv)
    y_prompt = _trunk(x_prompt, mem_prompt, even_params, odd_params)
    y_sample = _trunk(x_sample, mem_sample, even_params, odd_params)
    return (y_prompt, y_sample)
```

```python
import functools
import math

import jax
import jax.numpy as jnp
import numpy as np
from jax import lax
from jax.experimental import pallas as pl
from jax.experimental.pallas import tpu as pltpu

D_MODEL = 1024
EPS = 1e-6
ROPE_THETA = 500000.0
HY_WIDTH = 1024
HY_EMB = 33
HY_BANDS = (HY_EMB - 1) // 2
HY_FILT_HIDDEN = 64
HY_TARGET = 1e-2
HY_FAST_PCT = 0.3
HY_SLOW_PCT = 1.5
GDN_HEADS = 8
GDN_HEAD_DIM = 128
GDN_WIDTH = GDN_HEADS * GDN_HEAD_DIM
GDN_CHUNK = 64
X_HEADS = 4
X_HEAD_DIM = 128
X_WIDTH = X_HEADS * X_HEAD_DIM

LANES = 128
FFT_N2 = 128
VMEM_LIMIT_BYTES = 48 * 1024 * 1024

BF16 = jnp.bfloat16
F32 = jnp.float32


def _cparams(*sem):
    return pltpu.CompilerParams(dimension_semantics=sem, vmem_limit_bytes=VMEM_LIMIT_BYTES)


def _dot(a, b):
    return jnp.dot(a, b, preferred_element_type=F32)


def _silu(x):
    return x * jax.nn.sigmoid(x)


def _norm_matmul_kernel(x_ref, g_ref, w_ref, o_ref, xn_ref):
    @pl.when(pl.program_id(1) == 0)
    def _():
        x = x_ref[...]
        ms = jnp.mean(x * x, axis=-1, keepdims=True)
        xn_ref[...] = (x * lax.rsqrt(ms + EPS) * g_ref[...]).astype(BF16)

    o_ref[...] = _dot(xn_ref[...], w_ref[...]).astype(o_ref.dtype)


def _norm_matmul(x2d, g, w, tn, out_dtype=F32, tm=512, name="norm_matmul"):
    t, d = x2d.shape
    n = w.shape[1]
    assert t % tm == 0 and n % tn == 0
    return pl.pallas_call(
        _norm_matmul_kernel,
        grid=(t // tm, n // tn),
        in_specs=[
            pl.BlockSpec((tm, d), lambda i, j: (i, 0)),
            pl.BlockSpec((1, d), lambda i, j: (0, 0)),
            pl.BlockSpec((d, tn), lambda i, j: (0, j)),
        ],
        out_specs=pl.BlockSpec((tm, tn), lambda i, j: (i, j)),
        out_shape=jax.ShapeDtypeStruct((t, n), out_dtype),
        scratch_shapes=[pltpu.VMEM((tm, d), BF16)],
        compiler_params=_cparams("parallel", "arbitrary"),
        name=name,
    )(x2d, g.reshape(1, d).astype(F32), w.astype(BF16))


HALO_ROWS = 16


def _fill_halo_buf(buf_ref, zc_ref, zp_ref, zn_ref, t, nt):
    tl = zc_ref.shape[1]
    hr = HALO_ROWS
    buf_ref[0:hr, :] = jnp.where(t > 0, zp_ref[0].astype(F32), 0.0)
    buf_ref[hr:hr + tl, :] = zc_ref[0].astype(F32)
    buf_ref[hr + tl:2 * hr + tl, :] = jnp.where(t < nt - 1, zn_ref[0].astype(F32), 0.0)


def _halo_specs(tl, w, l, col):
    r = tl // HALO_ROWS
    nb = l // HALO_ROWS
    cur = pl.BlockSpec((1, tl, w), lambda i, t: (i, t, col))
    prev = pl.BlockSpec((1, HALO_ROWS, w), lambda i, t: (i, jnp.maximum(t * r - 1, 0), col))
    nxt = pl.BlockSpec((1, HALO_ROWS, w), lambda i, t: (i, jnp.minimum((t + 1) * r, nb - 1), col))
    return cur, prev, nxt


def _hy_prep_kernel(zc_ref, zp_ref, zn_ref, g_ref, cw_ref, cb_ref, u_ref, e_ref, buf_ref):
    t = pl.program_id(1)
    nt = pl.num_programs(1)
    tl = zc_ref.shape[1]
    c = HY_WIDTH
    hr = HALO_ROWS
    _fill_halo_buf(buf_ref, zc_ref, zp_ref, zn_ref, t, nt)
    cw = 256
    for c0 in range(0, c, cw):
        parts = []
        for p in range(3):
            lo = p * c + c0
            acc = cb_ref[:, lo:lo + cw]
            for j in range(3):
                acc = acc + buf_ref[pl.ds(hr - 1 + j, tl), lo:lo + cw] * cw_ref[j:j + 1, lo:lo + cw]
            parts.append(acc)
        x0, x1, v = parts
        u_ref[0, :, c0:c0 + cw] = (v * x1).astype(u_ref.dtype)
        e_ref[0, :, c0:c0 + cw] = (x0 * _silu(g_ref[0, :, c0:c0 + cw].astype(F32))).astype(e_ref.dtype)


def _hy_prep(z, conv_w, conv_b, tl=256):
    b, l, _ = z.shape
    c = HY_WIDTH
    assert l % tl == 0
    cur, prev, nxt = _halo_specs(tl, 3 * c, l, E_HY // (3 * c))
    return pl.pallas_call(
        _hy_prep_kernel,
        grid=(b, l // tl),
        in_specs=[
            cur, prev, nxt,
            pl.BlockSpec((1, tl, c), lambda i, t: (i, t, E_GHY // c)),
            pl.BlockSpec((3, 3 * c), lambda i, t: (0, 0)),
            pl.BlockSpec((1, 3 * c), lambda i, t: (0, 0)),
        ],
        out_specs=[
            pl.BlockSpec((1, tl, c), lambda i, t: (i, t, 0)),
            pl.BlockSpec((1, tl, c), lambda i, t: (i, t, 0)),
        ],
        out_shape=[jax.ShapeDtypeStruct((b, l, c), F32), jax.ShapeDtypeStruct((b, l, c), F32)],
        scratch_shapes=[pltpu.VMEM((tl + 2 * HALO_ROWS, 3 * c), F32)],
        compiler_params=_cparams("parallel", "arbitrary"),
        name="hy_prep",
    )(z, z, z, z, conv_w.astype(F32), conv_b.reshape(1, 3 * c).astype(F32))


def _hy_filter_tables(l):
    t = jnp.linspace(0.0, 1.0, l, dtype=F32)[:, None]
    w = (2.0 * math.pi / l) * jnp.arange(l, dtype=F32)[:, None]
    f = jnp.linspace(1e-4, HY_BANDS - 1, HY_BANDS, dtype=F32)[None, :]
    emb = jnp.concatenate([t, jnp.cos(f * w), -jnp.sin(f * w)], axis=-1)
    emb = jnp.pad(emb, ((0, 0), (0, LANES - HY_EMB)))
    deltas = jnp.abs(jnp.linspace(math.log(HY_TARGET) / HY_SLOW_PCT, math.log(HY_TARGET) / HY_FAST_PCT, HY_WIDTH, dtype=F32))
    decay = jnp.exp(-t * jnp.tile(deltas, 2)[None, :])
    return emb, decay


def _hy_filter_kernel(emb_ref, dec_ref, w1_ref, b1_ref, w2_ref, b2_ref, w3_ref, fr_ref, sk_ref, o_ref):
    c = HY_WIDTH
    hp = lax.Precision.HIGHEST
    fr = fr_ref[...]
    hid = jnp.sin(fr * (jnp.dot(emb_ref[...], w1_ref[...], precision=hp, preferred_element_type=F32) + b1_ref[...]))
    hid = jnp.sin(fr * (jnp.dot(hid, w2_ref[...], precision=hp, preferred_element_type=F32) + b2_ref[...]))
    tl = emb_ref.shape[0]
    row = lax.broadcasted_iota(jnp.int32, (tl, 1), 0) + pl.program_id(0) * tl
    first = row == 0
    cw = 512
    for c0 in range(0, 2 * c, cw):
        filt = jnp.dot(hid, w3_ref[:, c0:c0 + cw], precision=hp, preferred_element_type=F32) * dec_ref[:, c0:c0 + cw]
        if c0 < c:
            filt = jnp.where(first, filt + sk_ref[:, c0:c0 + cw], filt)
            o_ref[0, :, c0:c0 + cw] = filt.astype(o_ref.dtype)
        else:
            filt = jnp.where(first, 0.0, filt)
            o_ref[1, :, c0 - c:c0 - c + cw] = filt.astype(o_ref.dtype)


def _hy_filters(l, w1, b1, w2, b2, w3, freq, skip, tl=256):
    c = HY_WIDTH
    hdim = HY_FILT_HIDDEN
    emb, decay = _hy_filter_tables(l)
    w1p = jnp.pad(w1.astype(F32), ((0, LANES - HY_EMB), (0, 0)))
    return pl.pallas_call(
        _hy_filter_kernel,
        grid=(l // tl,),
        in_specs=[
            pl.BlockSpec((tl, LANES), lambda t: (t, 0)),
            pl.BlockSpec((tl, 2 * c), lambda t: (t, 0)),
            pl.BlockSpec((LANES, hdim), lambda t: (0, 0)),
            pl.BlockSpec((1, hdim), lambda t: (0, 0)),
            pl.BlockSpec((hdim, hdim), lambda t: (0, 0)),
            pl.BlockSpec((1, hdim), lambda t: (0, 0)),
            pl.BlockSpec((hdim, 2 * c), lambda t: (0, 0)),
            pl.BlockSpec((1, hdim), lambda t: (0, 0)),
            pl.BlockSpec((1, c), lambda t: (0, 0)),
        ],
        out_specs=pl.BlockSpec((2, tl, c), lambda t: (0, t, 0)),
        out_shape=jax.ShapeDtypeStruct((2, l, c), BF16),
        compiler_params=_cparams("arbitrary"),
        name="hy_filter",
    )(emb, decay, w1p, b1.reshape(1, hdim).astype(F32), w2.astype(F32), b2.reshape(1, hdim).astype(F32),
      w3.astype(F32), freq.reshape(1, hdim).astype(F32), skip.reshape(1, c).astype(F32))


def _fft_consts(l):
    n = 2 * l
    n2 = FFT_N2
    n1 = n // n2
    n1h = n1 // 2
    w0 = 2.0 * math.pi / n
    k1 = jnp.arange(n1, dtype=jnp.int32)
    m1 = jnp.arange(n1h, dtype=jnp.int32)
    m2 = jnp.arange(n2, dtype=jnp.int32)
    ea = (n2 * m1[None, None, :] * k1[None, :, None] + m2[:, None, None] * k1[None, :, None]) % n
    ang = ea.astype(F32) * w0
    fa = jnp.concatenate([jnp.cos(ang), -jnp.sin(ang)], axis=1).astype(BF16)
    eb = (n1 * m2[:, None] * m2[None, :]) % n
    angb = eb.astype(F32) * w0
    fr, fi = jnp.cos(angb), -jnp.sin(angb)
    fb = jnp.concatenate([jnp.concatenate([fr, -fi], axis=1), jnp.concatenate([fi, fr], axis=1)], axis=0).astype(BF16)
    eg = (n1 * m2[None, :, None] * m2[None, None, :] + m2[None, :, None] * k1[:, None, None]) % n
    angg = eg.astype(F32) * w0
    gr, gi = jnp.cos(angg), jnp.sin(angg)
    gb = jnp.concatenate([jnp.concatenate([gr, -gi], axis=2), jnp.concatenate([gi, gr], axis=2)], axis=1).astype(BF16)
    ec = (n2 * m1[:, None] * k1[None, :]) % n
    angc = ec.astype(F32) * w0
    fc = (jnp.concatenate([jnp.cos(angc), -jnp.sin(angc)], axis=1) * (1.0 / n)).astype(BF16)
    return fa, fb, gb, fc


def _fft_a_kernel(u_ref, f_ref, o_ref):
    tn2 = f_ref.shape[0]
    n1 = f_ref.shape[1] // 2
    c = u_ref.shape[2] // tn2
    for j in range(tn2):
        a = _dot(f_ref[j], u_ref[0, :, j * c:(j + 1) * c])
        o_ref[0, 0, :, j * c:(j + 1) * c] = a[:n1].astype(o_ref.dtype)
        o_ref[0, 1, :, j * c:(j + 1) * c] = a[n1:].astype(o_ref.dtype)


def _fft_a(u, fa, tn2=8):
    b, l, c = u.shape
    n2, n1x2, n1h = fa.shape
    n1 = n1x2 // 2
    uv = u.reshape(b, n1h, n2 * c)
    return pl.pallas_call(
        _fft_a_kernel,
        grid=(b, n2 // tn2),
        in_specs=[
            pl.BlockSpec((1, n1h, tn2 * c), lambda i, j: (i, 0, j)),
            pl.BlockSpec((tn2, n1x2, n1h), lambda i, j: (j, 0, 0)),
        ],
        out_specs=pl.BlockSpec((1, 2, n1, tn2 * c), lambda i, j: (i, 0, 0, j)),
        out_shape=jax.ShapeDtypeStruct((b, 2, n1, n2 * c), BF16),
        compiler_params=_cparams("parallel", "arbitrary"),
        name="fft_a",
    )(uv, fa)


def _fft_bf_kernel(a_ref, w_ref, o_ref):
    n2 = a_ref.shape[2]
    ct = a_ref.shape[3]
    cw = 256
    for c0 in range(0, ct, cw):
        xf = _dot(w_ref[...], a_ref[0, :, :, c0:c0 + cw].reshape(2 * n2, cw))
        xb = _dot(w_ref[...], a_ref[1, :, :, c0:c0 + cw].reshape(2 * n2, cw))
        o_ref[0, :, c0:c0 + cw] = xf[:n2] + xb[:n2]
        o_ref[1, :, c0:c0 + cw] = xf[n2:] - xb[n2:]


def _fft_bf(a, fb):
    _, _, n, c = a.shape
    n2 = FFT_N2
    return pl.pallas_call(
        _fft_bf_kernel,
        grid=(n // n2,),
        in_specs=[
            pl.BlockSpec((2, 2, n2, c), lambda k: (0, 0, k, 0)),
            pl.BlockSpec((2 * n2, 2 * n2), lambda k: (0, 0)),
        ],
        out_specs=pl.BlockSpec((2, n2, c), lambda k: (0, k, 0)),
        out_shape=jax.ShapeDtypeStruct((2, n, c), F32),
        compiler_params=_cparams("arbitrary"),
        name="fft_bf",
    )(a, fb)


SUBLANES = 8
CONV_SLABS = 2


def _conv_dims(n1):
    kh = n1 // 2 + 1
    kp = -(-kh // SUBLANES) * SUBLANES
    return kh, kp, 2 * kp + SUBLANES


def _hy_conv_kernel(u_ref, h_ref, la_ref, w_ref, g_ref, fc_ref, o_ref, s_ref, *, n1, unroll_b):
    n2 = FFT_N2
    n1h = n1 // 2
    kh, kp, pitch = _conv_dims(n1)
    grp = SUBLANES
    slabs = range(CONV_SLABS)

    def lanes(x, sl):
        return x[:, sl * LANES:(sl + 1) * LANES]

    def stage_a(jj, carry):
        rows = [u_ref[0, pl.ds(pl.multiple_of(m * n2 + jj * grp, grp), grp), :] for m in range(n1h)]
        a = _dot(la_ref[jj], jnp.concatenate(rows, axis=0).astype(BF16))
        for j0 in range(grp):
            dst = pl.multiple_of((jj * grp + j0) * pitch, grp)
            for sl in slabs:
                s_ref[sl, pl.ds(dst, 2 * kp), :] = lanes(a[j0 * 2 * kp:(j0 + 1) * 2 * kp], sl)
        return carry

    lax.fori_loop(0, n2 // grp, stage_a, 0)

    def stage_b(kk, carry):
        k1s = [jnp.minimum(kk * unroll_b + i, kh - 1) for i in range(unroll_b)]
        ar = [jnp.concatenate([s_ref[sl, pl.ds(k, n2, stride=pitch), :] for sl in slabs], axis=1) for k in k1s]
        ai = [jnp.concatenate([s_ref[sl, pl.ds(kp + k, n2, stride=pitch), :] for sl in slabs], axis=1) for k in k1s]
        x = [_dot(w_ref[...], jnp.concatenate([ar[i], ai[i]], axis=0).astype(BF16)) for i in range(unroll_b)]
        y = []
        for i, k in enumerate(k1s):
            off = pl.multiple_of(k * n2, n2)
            hr, hi = h_ref[0, pl.ds(off, n2), :], h_ref[1, pl.ds(off, n2), :]
            xr, xi = x[i][:n2], x[i][n2:]
            y.append(jnp.concatenate([xr * hr - xi * hi, xr * hi + xi * hr], axis=0).astype(BF16))
        z = [_dot(g_ref[k1s[i]], y[i]) for i in range(unroll_b)]
        for i, k in enumerate(k1s):
            for sl in slabs:
                s_ref[sl, pl.ds(k, n2, stride=pitch), :] = lanes(z[i][:n2], sl)
                s_ref[sl, pl.ds(kp + k, n2, stride=pitch), :] = lanes(z[i][n2:], sl)
        return carry

    lax.fori_loop(0, -(-kh // unroll_b), stage_b, 0)

    def stage_c(jj, carry):
        zz = [jnp.concatenate([s_ref[sl, pl.ds(pl.multiple_of((jj * grp + j0) * pitch, grp), 2 * kp), :]
                               for sl in slabs], axis=1) for j0 in range(grp)]
        y = _dot(fc_ref[...], jnp.concatenate(zz, axis=0).astype(BF16))
        for m in range(n1h):
            o_ref[0, pl.ds(pl.multiple_of(m * n2 + jj * grp, grp), grp), :] = y[m * grp:(m + 1) * grp]
        return carry

    lax.fori_loop(0, n2 // grp, stage_c, 0)


def _hy_conv(u, h, fa, fb, gb, fc):
    b, l, c = u.shape
    n2, n1x2, n1h = fa.shape
    n1 = n1x2 // 2
    kh, kp, pitch = _conv_dims(n1)
    grp = SUBLANES
    ct = CONV_SLABS * LANES

    def half(t, axis):
        re, im = jnp.split(t, 2, axis=axis)
        pad = [(0, 0)] * t.ndim
        pad[axis] = (0, kp - kh)
        cut = lambda x: jnp.pad(lax.slice_in_dim(x, 0, kh, axis=axis), pad)
        return jnp.concatenate([cut(re), cut(im)], axis=axis)

    fa_h = half(fa, 1)
    la = (fa_h.reshape(n2 // grp, grp, 2 * kp, n1h, 1) * jnp.eye(grp, dtype=fa.dtype).reshape(1, grp, 1, 1, grp))
    la = la.reshape(n2 // grp, grp * 2 * kp, n1h * grp)
    wgt = jnp.ones((kh,), F32).at[1:kh - 1].set(2.0)
    wgt = jnp.pad(wgt, (0, kp - kh))
    fc_h = (half(fc, 1).astype(F32) * jnp.tile(wgt, 2)[None, :]).astype(BF16)
    lc = fc_h.reshape(n1h, 1, 1, 2 * kp) * jnp.eye(grp, dtype=fc_h.dtype).reshape(1, grp, grp, 1)
    fc_h = lc.reshape(n1h * grp, grp * 2 * kp)
    gb_h = gb[:kh]
    unroll_b = 3
    once = pl.Buffered(1)
    return pl.pallas_call(
        functools.partial(_hy_conv_kernel, n1=n1, unroll_b=unroll_b),
        grid=(c // ct, b),
        in_specs=[
            pl.BlockSpec((1, l, ct), lambda j, i: (i, 0, j), pipeline_mode=once),
            pl.BlockSpec((2, kh * n2, ct), lambda j, i: (0, 0, j), pipeline_mode=once),
            pl.BlockSpec(la.shape, lambda j, i: (0, 0, 0), pipeline_mode=once),
            pl.BlockSpec(fb.shape, lambda j, i: (0, 0), pipeline_mode=once),
            pl.BlockSpec(gb_h.shape, lambda j, i: (0, 0, 0), pipeline_mode=once),
            pl.BlockSpec(fc_h.shape, lambda j, i: (0, 0), pipeline_mode=once),
        ],
        out_specs=pl.BlockSpec((1, l, ct), lambda j, i: (i, 0, j)),
        out_shape=jax.ShapeDtypeStruct((b, l, c), F32),
        scratch_shapes=[pltpu.VMEM((CONV_SLABS, n2 * pitch, LANES), F32)],
        compiler_params=_cparams("arbitrary", "arbitrary"),
        name="hy_conv",
    )(u, h, la, fb, gb_h, fc_h)


def _hyena_branch(z, l, conv_w, conv_b, w1, b1, w2, b2, w3, freq, skip):
    c = HY_WIDTH
    n = 2 * l
    fa, fb, gb, fc = _fft_consts(l)
    filt = _hy_filters(l, w1, b1, w2, b2, w3, freq, skip)
    fa_spec = _fft_a(filt, fa).reshape(2, 2, n, c)
    h = _fft_bf(fa_spec, fb)
    u, e = _hy_prep(z, conv_w, conv_b)
    return _hy_conv(u, h, fa, fb, gb, fc), e


def _gdn_prep_kernel(zc_ref, zp_ref, zn_ref, cw_ref, q_ref, k_ref, v_ref, buf_ref):
    t = pl.program_id(1)
    nt = pl.num_programs(1)
    tl = zc_ref.shape[1]
    dh = GDN_HEAD_DIM
    hr = HALO_ROWS
    _fill_halo_buf(buf_ref, zc_ref, zp_ref, zn_ref, t, nt)
    outs = (q_ref, k_ref, v_ref)
    for p in range(3):
        for h in range(GDN_HEADS):
            lo = p * GDN_WIDTH + h * dh
            acc = buf_ref[pl.ds(hr - 2, tl), lo:lo + dh] * cw_ref[0:1, lo:lo + dh]
            for j in range(1, 5):
                acc = acc + buf_ref[pl.ds(hr - 2 + j, tl), lo:lo + dh] * cw_ref[j:j + 1, lo:lo + dh]
            a = _silu(acc)
            if p < 2:
                a = a * lax.rsqrt(jnp.sum(a * a, axis=-1, keepdims=True) + EPS)
            if p == 0:
                a = a * (dh ** -0.5)
            outs[p][0, :, h * dh:(h + 1) * dh] = a.astype(BF16)


def _gdn_prep(z, conv_w, tl=256):
    b, l, _ = z.shape
    w = 3 * GDN_WIDTH
    cur, prev, nxt = _halo_specs(tl, w, l, E_QKV // w)
    o_spec = pl.BlockSpec((1, tl, GDN_WIDTH), lambda i, t: (i, t, 0))
    o_shape = jax.ShapeDtypeStruct((b, l, GDN_WIDTH), BF16)
    return pl.pallas_call(
        _gdn_prep_kernel,
        grid=(b, l // tl),
        in_specs=[cur, prev, nxt, pl.BlockSpec((5, w), lambda i, t: (0, 0))],
        out_specs=[o_spec, o_spec, o_spec],
        out_shape=[o_shape, o_shape, o_shape],
        scratch_shapes=[pltpu.VMEM((tl + 2 * HALO_ROWS, w), F32)],
        compiler_params=_cparams("parallel", "arbitrary"),
        name="gdn_prep",
    )(z, z, z, conv_w.astype(F32))


def _dot_nt(a, b):
    return lax.dot_general(a, b, (((1,), (1,)), ((), ())), preferred_element_type=F32)


def _dot_tn(a, b):
    return lax.dot_general(a, b, (((0,), (0,)), ((), ())), preferred_element_type=F32)


def _split3(x):
    a = x.astype(BF16)
    r = x - a.astype(F32)
    b = r.astype(BF16)
    c = (r - b.astype(F32)).astype(BF16)
    return a, b, c


GDN_CHUNKS_PER_STEP = 4


def _gdn_scan_kernel(*refs, direction, final):
    if final:
        q_ref, k_ref, v_ref, zs_ref, al_ref, dt_ref, of_ref, gg_ref, ng_ref, o_ref, s_ref = refs
    else:
        q_ref, k_ref, v_ref, zs_ref, al_ref, dt_ref, o_ref, s_ref = refs
    c = GDN_CHUNK
    dh = GDN_HEAD_DIM
    nsub = q_ref.shape[1] // c

    @pl.when(pl.program_id(1) == 0)
    def _():
        s_ref[...] = jnp.zeros_like(s_ref)

    row = lax.broadcasted_iota(jnp.int32, (c, c), 0)
    col = lax.broadcasted_iota(jnp.int32, (c, c), 1)
    if direction == 0:
        incl, strict, last = col <= row, col < row, c - 1
    else:
        incl, strict, last = col >= row, col > row, 0
    tri = incl.astype(BF16)
    eye = (row == col).astype(F32)
    heads = range(GDN_HEADS)
    hsl = [slice(h * dh, (h + 1) * dh) for h in heads]
    lns = [2 * GDN_HEADS + direction * GDN_HEADS + h for h in heads]

    rows = [slice(i * c, (i + 1) * c) for i in range(nsub)]
    gcum, gcum_t, beta_all = [], [], []
    for rs in rows:
        zs = zs_ref[0, rs, :]
        beta_all.append(jax.nn.sigmoid(zs))
        gl = -jnp.exp(al_ref[...]) * jax.nn.softplus(zs + dt_ref[...])
        g1, g2, g3 = _split3(gl)
        gsum = _dot(tri, g1) + _dot(tri, g2) + _dot(tri, g3)
        gcum.append(gsum)
        gcum_t.append(jnp.concatenate([gsum, jnp.zeros_like(gsum)], axis=0).T)
    items = [(i, h) for i in range(nsub) for h in heads]
    it = range(len(items))
    beta = [beta_all[i][:, direction * GDN_HEADS + h:direction * GDN_HEADS + h + 1] for i, h in items]
    gc = [gcum[i][:, lns[h]:lns[h] + 1] for i, h in items]
    gct = [gcum_t[i][lns[h]:lns[h] + 1, 0:c] for i, h in items]
    glast = [gcum_t[i][lns[h]:lns[h] + 1, last:last + 1] for i, h in items]
    qh = [q_ref[0, rows[i], hsl[h]] for i, h in items]
    kh = [k_ref[0, rows[i], hsl[h]] for i, h in items]
    vh = [v_ref[0, rows[i], hsl[h]] for i, h in items]
    kq = [_dot_nt(jnp.concatenate([kh[x], qh[x]], axis=0), kh[x]) for x in it]
    dmat = [jnp.exp(jnp.where(incl, gc[x] - gct[x], -jnp.inf)) for x in it]
    a = [jnp.where(strict, (beta[x] * kq[x][:c]) * dmat[x], 0.0) for x in it]
    tinv = [eye - a[x] for x in it]
    ab = [a[x].astype(BF16) for x in it]
    p = [_dot(ab[x], ab[x]) for x in it]
    for j in range(5):
        pb = [p[x].astype(BF16) for x in it]
        if j < 4:
            tp = [_dot(jnp.concatenate([tinv[x].astype(BF16), pb[x]], axis=0), pb[x]) for x in it]
            tinv = [tinv[x] + tp[x][:c] for x in it]
            p = [tp[x][c:] for x in it]
        else:
            tinv = [tinv[x] + _dot(tinv[x].astype(BF16), pb[x]) for x in it]
    eg = [jnp.exp(gc[x]) for x in it]
    kf = [kh[x].astype(F32) for x in it]
    rhs = [jnp.concatenate([vh[x].astype(F32) * beta[x], kf[x] * (beta[x] * eg[x])], axis=1).astype(BF16) for x in it]
    uw = [_dot(tinv[x].astype(BF16), rhs[x]) for x in it]
    wq = [jnp.concatenate([uw[x][:, dh:], qh[x].astype(F32) * eg[x]], axis=0).astype(BF16) for x in it]
    qkd = [(kq[x][c:] * dmat[x]).astype(BF16) for x in it]
    kd = [(kf[x] * jnp.exp(glast[x] - gc[x])).astype(BF16) for x in it]

    order = list(range(nsub)) if direction == 0 else list(range(nsub - 1, -1, -1))
    state = [s_ref[h] for h in heads]
    for i in order:
        rs = rows[i]
        xs = [i * GDN_HEADS + h for h in heads]
        ws = [_dot(wq[xs[h]], state[h].astype(BF16)) for h in heads]
        vnb = [(uw[xs[h]][:, :dh] - ws[h][:c]).astype(BF16) for h in heads]
        o = [ws[h][c:] + _dot(qkd[xs[h]], vnb[h]) for h in heads]
        state = [state[h] * jnp.exp(glast[xs[h]]) + _dot_tn(kd[xs[h]], vnb[h]) for h in heads]
        for h in heads:
            if final:
                tot = of_ref[0, rs, hsl[h]] + o[h]
                y = tot * lax.rsqrt(jnp.mean(tot * tot, axis=-1, keepdims=True) + EPS) * ng_ref[...]
                o_ref[0, rs, hsl[h]] = (y * _silu(gg_ref[0, rs, hsl[h]].astype(F32))).astype(o_ref.dtype)
            else:
                o_ref[0, rs, hsl[h]] = o[h].astype(o_ref.dtype)
    for h in heads:
        s_ref[h] = state[h]


def _gdn_scan(q, k, v, zs, a_log, dt_bias, direction, o_fwd=None, z=None, gate_blk=None, norm_g=None):
    b, l, w = q.shape
    c = GDN_CHUNK * GDN_CHUNKS_PER_STEP
    n = l // c
    final = o_fwd is not None
    if direction == 0:
        cmap = lambda i, t: (i, t, 0)
    else:
        cmap = lambda i, t: (i, n - 1 - t, 0)
    blk = pl.BlockSpec((1, c, w), cmap)
    vec = pl.BlockSpec((1, LANES), lambda i, t: (0, 0))
    pad = lambda x: jnp.pad(x.reshape(1, -1).astype(F32), ((0, 0), (2 * GDN_HEADS, LANES - 4 * GDN_HEADS)))
    in_specs = [blk, blk, blk, pl.BlockSpec((1, c, LANES), cmap), vec, vec]
    args = [q, k, v, zs, pad(a_log), pad(dt_bias)]
    if final:
        if direction == 0:
            gmap = lambda i, t: (i, t, gate_blk)
        else:
            gmap = lambda i, t: (i, n - 1 - t, gate_blk)
        in_specs += [blk, pl.BlockSpec((1, c, w), gmap), pl.BlockSpec((1, GDN_HEAD_DIM), lambda i, t: (0, 0))]
        args += [o_fwd, z, norm_g.reshape(1, GDN_HEAD_DIM).astype(F32)]
    return pl.pallas_call(
        functools.partial(_gdn_scan_kernel, direction=direction, final=final),
        grid=(b, n),
        in_specs=in_specs,
        out_specs=blk,
        out_shape=jax.ShapeDtypeStruct((b, l, w), BF16 if final else F32),
        scratch_shapes=[pltpu.VMEM((GDN_HEADS, GDN_HEAD_DIM, GDN_HEAD_DIM), F32)],
        compiler_params=_cparams("parallel", "arbitrary"),
        name="gdn_scan_bwd" if direction else "gdn_scan_fwd",
    )(*args)


def _gdn_branch(z, zs, conv_w, a_log, dt_bias, norm_g, gate_blk):
    q, k, v = _gdn_prep(z, conv_w)
    o_f = _gdn_scan(q, k, v, zs, a_log, dt_bias, 0)
    return _gdn_scan(q, k, v, zs, a_log, dt_bias, 1, o_fwd=o_f, z=z, gate_blk=gate_blk, norm_g=norm_g)


def _softmax_t(s, extra=None):
    m = jnp.max(s, axis=0, keepdims=True)
    if extra is not None:
        m = jnp.maximum(m, extra)
    p = jnp.exp(s - m)
    den = jnp.sum(p, axis=0, keepdims=True)
    if extra is not None:
        den = den + jnp.exp(extra - m)
    return p.astype(BF16), den, m


def _t_bf16(x):
    return x.astype(F32).T.astype(BF16)


def _xattn_kernel(q_ref, g_ref, kv_ref, o_ref):
    dh = X_HEAD_DIM
    for h in range(X_HEADS):
        hs = slice(h * dh, (h + 1) * dh)
        qh = q_ref[0, :, hs].astype(BF16)
        kh = kv_ref[0, :, hs]
        vh = kv_ref[0, :, X_WIDTH + h * dh:X_WIDTH + (h + 1) * dh]
        s = _dot_nt(qh, kh) * (dh ** -0.5)
        p = jnp.exp(s - jnp.max(s, axis=-1, keepdims=True))
        den = jnp.sum(p, axis=-1, keepdims=True)
        o = _dot(p.astype(BF16), vh) / den
        o_ref[0, :, hs] = (o * _silu(g_ref[0, :, hs].astype(F32))).astype(o_ref.dtype)


def _xattn(z, kv, q_blk, g_blk, tq=512):
    b, l, _ = z.shape
    m = kv.shape[1]
    w = X_WIDTH
    return pl.pallas_call(
        _xattn_kernel,
        grid=(b, l // tq),
        in_specs=[
            pl.BlockSpec((1, tq, w), lambda i, t: (i, t, q_blk)),
            pl.BlockSpec((1, tq, w), lambda i, t: (i, t, g_blk)),
            pl.BlockSpec((1, m, 2 * w), lambda i, t: (i, 0, 0)),
        ],
        out_specs=pl.BlockSpec((1, tq, w), lambda i, t: (i, t, 0)),
        out_shape=jax.ShapeDtypeStruct((b, l, w), BF16),
        compiler_params=_cparams("parallel", "arbitrary"),
        name="xattn",
    )(z, z, kv)


def _out_proj_kernel(*refs, widths, gated):
    nparts = len(widths)
    y_refs = refs[:nparts]
    if gated:
        e_ref, w_ref, g_ref, x_ref, o_ref = refs[nparts:]
    else:
        w_ref, g_ref, x_ref, o_ref = refs[nparts:]
    d = o_ref.shape[1]
    cw = 256
    ys = [y_ref[...] for y_ref in y_refs]
    if gated:
        ys[0] = ys[0] * e_ref[...]
    ys = [y.astype(BF16) for y in ys]
    ssq = jnp.zeros((o_ref.shape[0], 1), F32)
    for c0 in range(0, d, cw):
        acc = None
        off = 0
        for y, wd in zip(ys, widths):
            part = _dot(y, w_ref[off:off + wd, c0:c0 + cw])
            acc = part if acc is None else acc + part
            off += wd
        ssq = ssq + jnp.sum(acc * acc, axis=-1, keepdims=True)
        o_ref[:, c0:c0 + cw] = acc
    r = lax.rsqrt(ssq * (1.0 / d) + EPS)
    for c0 in range(0, d, cw):
        o_ref[:, c0:c0 + cw] = x_ref[:, c0:c0 + cw] + o_ref[:, c0:c0 + cw] * r * g_ref[:, c0:c0 + cw]


def _out_proj(parts, w_out, post_g, x2d, gate=None, tm=512):
    t, d = x2d.shape
    widths = tuple(int(p.shape[1]) for p in parts)
    kdim = sum(widths)
    gates = [] if gate is None else [gate]
    return pl.pallas_call(
        functools.partial(_out_proj_kernel, widths=widths, gated=gate is not None),
        grid=(t // tm,),
        in_specs=[pl.BlockSpec((tm, wd), lambda i: (i, 0)) for wd in widths]
        + [pl.BlockSpec((tm, widths[0]), lambda i: (i, 0)) for _ in gates] + [
            pl.BlockSpec((kdim, d), lambda i: (0, 0)),
            pl.BlockSpec((1, d), lambda i: (0, 0)),
            pl.BlockSpec((tm, d), lambda i: (i, 0)),
        ],
        out_specs=pl.BlockSpec((tm, d), lambda i: (i, 0)),
        out_shape=jax.ShapeDtypeStruct((t, d), F32),
        compiler_params=_cparams("parallel"),
        name="out_proj",
    )(*parts, *gates, w_out.astype(BF16), post_g.reshape(1, d).astype(F32), x2d)


def _mem_kv(mem, mem_g, w_mem_kv):
    b, m, d = mem.shape
    kv = _norm_matmul(mem.reshape(b * m, d), mem_g, w_mem_kv, 1024, out_dtype=BF16, name="mem_kv")
    return kv.reshape(b, m, 2 * X_WIDTH)


E_QKV, E_HY, E_GHY, E_GGDN, E_XQ, E_GX, E_BIG = 0, 3072, 6144, 7168, 8192, 8704, 9216


def _even_layer(x, mem, pre_g, post_g, w_in, w_out, hy_conv_w, hy_conv_b, hy_w1, hy_b1, hy_w2, hy_b2, hy_w3, hy_freq,
                hy_skip, gdn_conv_w, gdn_a_log, gdn_dt_bias, gdn_norm_g, mem_g, w_mem_kv):
    b, l, d = x.shape
    x2d = x.reshape(b * l, d)
    w_big = jnp.concatenate([w_in[:, 4096:7168], w_in[:, 0:3072], w_in[:, 3072:4096], w_in[:, 7168:8192],
                             w_in[:, 8224:8736], w_in[:, 8736:9248]], axis=1)
    w_small = jnp.pad(w_in[:, 8192:8224], ((0, 0), (0, LANES - 4 * GDN_HEADS)))
    z = _norm_matmul(x2d, pre_g, w_big, 1024, out_dtype=BF16, tm=1024, name="even_in").reshape(b, l, E_BIG)
    zs = _norm_matmul(x2d, pre_g, w_small, LANES, name="even_in_gates").reshape(b, l, LANES)
    conv, e = _hyena_branch(z, l, hy_conv_w, hy_conv_b, hy_w1, hy_b1, hy_w2, hy_b2, hy_w3, hy_freq, hy_skip)
    y_b = _gdn_branch(z, zs, gdn_conv_w, gdn_a_log, gdn_dt_bias, gdn_norm_g, E_GGDN // GDN_WIDTH)
    kv = _mem_kv(mem, mem_g, w_mem_kv)
    y_x = _xattn(z, kv, E_XQ // X_WIDTH, E_GX // X_WIDTH)
    t = b * l
    out = _out_proj([conv.reshape(t, -1), y_b.reshape(t, -1), y_x.reshape(t, -1)], w_out, post_g, x2d,
                    gate=e.reshape(t, -1))
    return out.reshape(b, l, d)


DIL_PATTERNS = ((128, 1), (512, 4), (2048, 16))
N_DIL = len(DIL_PATTERNS)
DIL_HEADS = 4
DIL_HEAD_DIM = 128
DIL_WIDTH = DIL_HEADS * DIL_HEAD_DIM
SWA_Q_HEADS = 16
SWA_KV_HEADS = 2
SWA_HEAD_DIM = 64
SWA_WIDTH = SWA_Q_HEADS * SWA_HEAD_DIM
SWA_HALF_WINDOW = 128
O_CQKV, O_GC, O_DQ, O_GD, O_XQ, O_GX, O_DKV, O_ALL = 0, 4608, 5120, 6144, 7168, 7680, 8192, 8448


def _rope_tables(l, dh):
    half = dh // 8
    inv = ROPE_THETA ** (-jnp.arange(half, dtype=F32) / half)
    ang = jnp.arange(l, dtype=F32)[:, None] * inv[None, :]
    cos, sin = jnp.cos(ang), jnp.sin(ang)
    one = jnp.ones((l, dh - 2 * half), F32)
    zero_h = jnp.zeros((l, half), F32)
    zero_r = jnp.zeros((l, dh - 2 * half), F32)
    c = jnp.concatenate([cos, cos, one], axis=1)
    sa = jnp.concatenate([-sin, zero_h, zero_r], axis=1)
    sb = jnp.concatenate([zero_h, sin, zero_r], axis=1)
    rep = LANES // dh
    return tuple(jnp.tile(t, (1, rep)) for t in (c, sa, sb))


def _odd_prep_kernel(c_ref, dq_ref, dkv_ref, c1_ref, a1_ref, b1_ref, c2_ref, a2_ref, b2_ref,
                     cq_ref, ck_ref, cv_ref, dqe_ref, dk_ref, dv_ref):
    tl = c_ref.shape[1]
    lane = lax.broadcasted_iota(jnp.int32, (tl, LANES), 1)
    low = lane < SWA_HEAD_DIM
    c1, a1, b1 = c1_ref[...], a1_ref[...], b1_ref[...]
    c2, a2, b2 = c2_ref[...], a2_ref[...], b2_ref[...]
    h1 = DIL_HEAD_DIM // 8
    h2 = SWA_HEAD_DIM // 8

    def rope1(x):
        return x * c1 + pltpu.roll(x, LANES - h1, 1) * a1 + pltpu.roll(x, h1, 1) * b1

    def rope2(x):
        return x * c2 + pltpu.roll(x, LANES - h2, 1) * a2 + pltpu.roll(x, h2, 1) * b2

    nblk = N_DIL * DIL_HEADS
    for j in range(nblk):
        ls = slice(j * LANES, (j + 1) * LANES)
        cq_ref[0, :, ls] = rope1(c_ref[0, :, ls].astype(F32)).astype(BF16)
        ck_ref[0, :, ls] = rope1(c_ref[0, :, nblk * LANES + j * LANES:nblk * LANES + (j + 1) * LANES].astype(F32)).astype(BF16)
        cv_ref[0, :, ls] = c_ref[0, :, 2 * nblk * LANES + j * LANES:2 * nblk * LANES + (j + 1) * LANES].astype(BF16)
    for j in range(SWA_Q_HEADS // 2):
        xr = rope2(dq_ref[0, :, j * LANES:(j + 1) * LANES].astype(F32))
        dqe_ref[0, :, (2 * j) * LANES:(2 * j + 1) * LANES] = jnp.where(low, xr, 0.0).astype(BF16)
        dqe_ref[0, :, (2 * j + 1) * LANES:(2 * j + 2) * LANES] = jnp.where(low, pltpu.roll(xr, SWA_HEAD_DIM, 1), 0.0).astype(BF16)
    kr = rope2(dkv_ref[0, :, 0:LANES].astype(F32))
    vv = dkv_ref[0, :, LANES:2 * LANES].astype(F32)
    kr_sw = pltpu.roll(kr, SWA_HEAD_DIM, 1)
    vv_sw = pltpu.roll(vv, SWA_HEAD_DIM, 1)
    dk_ref[0, :, 0:LANES] = jnp.where(low, kr, 0.0).astype(BF16)
    dk_ref[0, :, LANES:2 * LANES] = jnp.where(low, kr_sw, 0.0).astype(BF16)
    dv_ref[0, :, 0:LANES] = jnp.where(low, vv, vv_sw).astype(BF16)
    dv_ref[0, :, LANES:2 * LANES] = jnp.where(low, vv_sw, vv).astype(BF16)


def _odd_prep(z, tl=256):
    b, l, _ = z.shape
    wc = 3 * N_DIL * DIL_WIDTH
    t1 = _rope_tables(l, DIL_HEAD_DIM)
    t2 = _rope_tables(l, SWA_HEAD_DIM)
    tab = pl.BlockSpec((tl, LANES), lambda i, t: (t, 0))

    def spec(w):
        return pl.BlockSpec((1, tl, w), lambda i, t: (i, t, 0))

    def shape(w):
        return jax.ShapeDtypeStruct((b, l, w), BF16)

    wd = N_DIL * DIL_WIDTH
    return pl.pallas_call(
        _odd_prep_kernel,
        grid=(b, l // tl),
        in_specs=[
            pl.BlockSpec((1, tl, wc), lambda i, t: (i, t, 0)),
            pl.BlockSpec((1, tl, SWA_WIDTH), lambda i, t: (i, t, O_DQ // SWA_WIDTH)),
            pl.BlockSpec((1, tl, 2 * LANES), lambda i, t: (i, t, O_DKV // (2 * LANES))),
            tab, tab, tab, tab, tab, tab,
        ],
        out_specs=[spec(wd), spec(wd), spec(wd), spec(2 * SWA_WIDTH), spec(2 * LANES), spec(2 * LANES)],
        out_shape=[shape(wd), shape(wd), shape(wd), shape(2 * SWA_WIDTH), shape(2 * LANES), shape(2 * LANES)],
        compiler_params=_cparams("parallel", "arbitrary"),
        name="odd_prep",
    )(z, z, z, *t1, *t2)


def _band_bias_t(tq, half, ls):
    p0 = pl.program_id(1) * tq
    j = lax.broadcasted_iota(jnp.int32, (tq + 2 * half, tq), 0)
    i = lax.broadcasted_iota(jnp.int32, (tq + 2 * half, tq), 1)
    kpos = p0 - half + j
    valid = (j >= i) & (j - i <= 2 * half) & (kpos >= 0) & (kpos < ls)
    return jnp.where(valid, 0.0, -jnp.inf)


def _band_mask(tq, half, ls):
    p0 = pl.program_id(1) * tq
    i = lax.broadcasted_iota(jnp.int32, (tq, tq + 2 * half), 0)
    j = lax.broadcasted_iota(jnp.int32, (tq, tq + 2 * half), 1)
    kpos = p0 - half + j
    return (j >= i) & (j - i <= 2 * half) & (kpos >= 0) & (kpos < ls)


def _band_attn_kernel(q_ref, kp_ref, kc_ref, kn_ref, vp_ref, vc_ref, vn_ref, o_ref, lse_ref, *, half, ls):
    tq = q_ref.shape[1]
    dh = DIL_HEAD_DIM
    valid = _band_mask(tq, half, ls)
    lane = lax.broadcasted_iota(jnp.int32, (tq, LANES), 1)
    lse_all = jnp.zeros((tq, LANES), F32)
    heads = range(DIL_HEADS)
    hsl = [slice(h * dh, (h + 1) * dh) for h in heads]
    kall = [jnp.concatenate([kp_ref[0, :, s], kc_ref[0, :, s], kn_ref[0, :, s]], axis=0) for s in hsl]
    vall = [jnp.concatenate([vp_ref[0, :, s], vc_ref[0, :, s], vn_ref[0, :, s]], axis=0) for s in hsl]
    sc = [jnp.where(valid, _dot_nt(q_ref[0, :, hsl[h]], kall[h]) * (dh ** -0.5), -jnp.inf) for h in heads]
    m = [jnp.max(sc[h], axis=-1, keepdims=True) for h in heads]
    p = [jnp.exp(sc[h] - m[h]) for h in heads]
    den = [jnp.sum(p[h], axis=-1, keepdims=True) for h in heads]
    o = [_dot(p[h].astype(BF16), vall[h]) / den[h] for h in heads]
    for h in heads:
        o_ref[0, :, hsl[h]] = o[h].astype(o_ref.dtype)
        lse_all = jnp.where(lane == h, m[h] + jnp.log(den[h]), lse_all)
    lse_ref[0] = lse_all[:, :LSE_LANES]


LSE_LANES = 8


def _band_specs(tq, half, ls, w, col):
    r = tq // half
    nb = ls // half
    cur = pl.BlockSpec((1, tq, w), lambda i, t: (i, t, col))
    prev = pl.BlockSpec((1, half, w), lambda i, t: (i, jnp.maximum(t * r - 1, 0), col))
    nxt = pl.BlockSpec((1, half, w), lambda i, t: (i, jnp.minimum((t + 1) * r, nb - 1), col))
    return cur, prev, nxt


def _band_attn(q, k, v, half):
    n, ls, w = q.shape
    tq = min(2 * LANES, ls)
    cur, prev, nxt = _band_specs(tq, half, ls, w, 0)
    return pl.pallas_call(
        functools.partial(_band_attn_kernel, half=half, ls=ls),
        grid=(n, ls // tq),
        in_specs=[cur, prev, cur, nxt, prev, cur, nxt],
        out_specs=[cur, pl.BlockSpec((1, tq, LSE_LANES), lambda i, t: (i, t, 0))],
        out_shape=[jax.ShapeDtypeStruct((n, ls, w), BF16), jax.ShapeDtypeStruct((n, ls, LSE_LANES), F32)],
        compiler_params=_cparams("parallel", "arbitrary"),
        name="band_attn",
    )(q, k, k, k, v, v, v)


def _dil_merge_kernel(o0_ref, o1_ref, o2_ref, l0_ref, l1_ref, l2_ref, g_ref, y_ref):
    dh = DIL_HEAD_DIM
    o_refs = (o0_ref, o1_ref, o2_ref)
    for h in range(DIL_HEADS):
        hs = slice(h * dh, (h + 1) * dh)
        ls = [r[0, :, h:h + 1] for r in (l0_ref, l1_ref, l2_ref)]
        m = jnp.maximum(jnp.maximum(ls[0], ls[1]), ls[2])
        ws = [jnp.exp(x - m) for x in ls]
        den = ws[0] + ws[1] + ws[2]
        y = ((ws[0] / den) * o_refs[0][0, :, hs].astype(F32) + (ws[1] / den) * o_refs[1][0, :, hs].astype(F32)
             + (ws[2] / den) * o_refs[2][0, :, hs].astype(F32))
        y_ref[0, :, hs] = (y * _silu(g_ref[0, :, hs].astype(F32))).astype(y_ref.dtype)


def _dil_merge(outs, lses, z, tl=512):
    b, l, w = outs[0].shape
    o_spec = pl.BlockSpec((1, tl, w), lambda i, t: (i, t, 0))
    l_spec = pl.BlockSpec((1, tl, LSE_LANES), lambda i, t: (i, t, 0))
    return pl.pallas_call(
        _dil_merge_kernel,
        grid=(b, l // tl),
        in_specs=[o_spec, o_spec, o_spec, l_spec, l_spec, l_spec, pl.BlockSpec((1, tl, w), lambda i, t: (i, t, O_GC // DIL_WIDTH))],
        out_specs=o_spec,
        out_shape=jax.ShapeDtypeStruct((b, l, w), BF16),
        compiler_params=_cparams("parallel", "arbitrary"),
        name="dil_merge",
    )(*outs, *lses, z)


def _dilated_branch(cq, ck, cv, z):
    b, l, _ = cq.shape
    w = DIL_WIDTH
    outs, lses = [], []
    for gi, (window, d) in enumerate(DIL_PATTERNS):
        half = window // (2 * d)
        ls = l // d

        def to_res(x):
            x = x[:, :, gi * w:(gi + 1) * w]
            if d == 1:
                return x
            return jnp.swapaxes(x.reshape(b, ls, d, w), 1, 2).reshape(b * d, ls, w)

        def from_res(x):
            if d == 1:
                return x
            return jnp.swapaxes(x.reshape(b, d, ls, x.shape[-1]), 1, 2).reshape(b, l, x.shape[-1])

        o, lse = _band_attn(to_res(cq), to_res(ck), to_res(cv), half)
        outs.append(from_res(o))
        lses.append(from_res(lse))
    return _dil_merge(outs, lses, z)


def _swa_kernel(q_ref, kp_ref, kc_ref, kn_ref, vp_ref, vc_ref, vn_ref, g_ref, sink_ref, o_ref, *, half, ls):
    tq = q_ref.shape[1]
    dh = SWA_HEAD_DIM
    bias = _band_bias_t(tq, half, ls)
    low = lax.broadcasted_iota(jnp.int32, (LANES, tq), 0) < dh
    grp = SWA_Q_HEADS // SWA_KV_HEADS
    for g in range(SWA_KV_HEADS):
        gs = slice(g * LANES, (g + 1) * LANES)
        kall = jnp.concatenate([kp_ref[0, :, gs], kc_ref[0, :, gs], kn_ref[0, :, gs]], axis=0)
        vt = _t_bf16(jnp.concatenate([vp_ref[0, :, gs], vc_ref[0, :, gs], vn_ref[0, :, gs]], axis=0))
        hs = range(g * grp, (g + 1) * grp)
        sc = [_dot_nt(kall, q_ref[0, :, h * LANES:(h + 1) * LANES]) * (dh ** -0.5) + bias for h in hs]
        sm = [_softmax_t(sc[i], sink_ref[:, h:h + 1]) for i, h in enumerate(hs)]
        ot = [_dot(vt, sm[i][0]) / sm[i][1] for i in range(grp)]
        for jp in range(grp // 2):
            blk = (g * grp) // 2 + jp
            bs = slice(blk * LANES, (blk + 1) * LANES)
            y = jnp.where(low, ot[2 * jp], ot[2 * jp + 1]).T * _silu(g_ref[0, :, bs].astype(F32))
            o_ref[0, :, bs] = y.astype(o_ref.dtype)


def _swa_branch(dqe, dk, dv, z, sink, tq=128):
    b, l, _ = dqe.shape
    half = SWA_HALF_WINDOW
    _, prev, nxt = _band_specs(tq, half, l, 2 * LANES, 0)
    cur = pl.BlockSpec((1, tq, 2 * LANES), lambda i, t: (i, t, 0))
    sink_p = jnp.pad(sink.reshape(1, SWA_Q_HEADS).astype(F32), ((0, 0), (0, LANES - SWA_Q_HEADS)))
    return pl.pallas_call(
        functools.partial(_swa_kernel, half=half, ls=l),
        grid=(b, l // tq),
        in_specs=[
            pl.BlockSpec((1, tq, 2 * SWA_WIDTH), lambda i, t: (i, t, 0)),
            prev, cur, nxt, prev, cur, nxt,
            pl.BlockSpec((1, tq, SWA_WIDTH), lambda i, t: (i, t, O_GD // SWA_WIDTH)),
            pl.BlockSpec((1, LANES), lambda i, t: (0, 0)),
        ],
        out_specs=pl.BlockSpec((1, tq, SWA_WIDTH), lambda i, t: (i, t, 0)),
        out_shape=jax.ShapeDtypeStruct((b, l, SWA_WIDTH), BF16),
        compiler_params=_cparams("parallel", "arbitrary"),
        name="swa",
    )(dqe, dk, dk, dk, dv, dv, dv, z, sink_p)


def _odd_layer(x, mem, pre_g, post_g, w_in, w_out, swa_sink, mem_g, w_mem_kv):
    b, l, d = x.shape
    x2d = x.reshape(b * l, d)
    w_re = jnp.concatenate([w_in[:, 0:6144], w_in[:, 6400:8448], w_in[:, 6144:6400]], axis=1)
    z = _norm_matmul(x2d, pre_g, w_re, 1408, out_dtype=BF16, tm=1024, name="odd_in").reshape(b, l, O_ALL)
    cq, ck, cv, dqe, dk, dv = _odd_prep(z)
    y_c = _dilated_branch(cq, ck, cv, z)
    y_d = _swa_branch(dqe, dk, dv, z, swa_sink)
    kv = _mem_kv(mem, mem_g, w_mem_kv)
    y_x = _xattn(z, kv, O_XQ // X_WIDTH, O_GX // X_WIDTH)
    t = b * l
    out = _out_proj([y_c.reshape(t, -1), y_d.reshape(t, -1), y_x.reshape(t, -1)], w_out, post_g, x2d)
    return out.reshape(b, l, d)


def _trunk(x, mem, even_params, odd_params):
    x = _even_layer(x, mem, *[p[0] for p in even_params])
    return _odd_layer(x, mem, *[p[0] for p in odd_params])


def kernel(x_prompt, x_sample, mem_prompt, mem_sample, e_pre_g, e_post_g, e_w_in, e_w_out, hy_conv_w, hy_conv_b,
           hy_filt_w1, hy_filt_b1, hy_filt_w2, hy_filt_b2, hy_filt_w3, hy_freq, hy_skip, gdn_conv_w, gdn_A_log,
           gdn_dt_bias, gdn_norm_g, e_mem_g, e_w_mem_kv, o_pre_g, o_post_g, o_w_in, o_w_out, swa_sink, o_mem_g,
           o_w_mem_kv):
    even_params = (e_pre_g, e_post_g, e_w_in, e_w_out, hy_conv_w, hy_conv_b, hy_filt_w1, hy_filt_b1, hy_filt_w2,
                   hy_filt_b2, hy_filt_w3, hy_freq, hy_skip, gdn_conv_w, gdn_A_log, gdn_dt_bias, gdn_norm_g,
                   e_mem_g, e_w_mem_kv)
    odd_params = (o_pre_g, o_post_g, o_w_in, o_w_out, swa_sink, o_mem_g, o_w_mem_kv)
    y_prompt = _trunk(x_prompt, mem_prompt, even_params, odd_params)
    y_sample = _trunk(x_sample, mem_sample, even_params, odd_params)
    return (y_prompt, y_sample)
```

```python
import functools
import math

import jax
import jax.numpy as jnp
import numpy as np
from jax import lax
from jax.experimental import pallas as pl
from jax.experimental.pallas import tpu as pltpu

D_MODEL = 1024
EPS = 1e-6
ROPE_THETA = 500000.0
HY_WIDTH = 1024
HY_EMB = 33
HY_BANDS = (HY_EMB - 1) // 2
HY_FILT_HIDDEN = 64
HY_TARGET = 1e-2
HY_FAST_PCT = 0.3
HY_SLOW_PCT = 1.5
GDN_HEADS = 8
GDN_HEAD_DIM = 128
GDN_WIDTH = GDN_HEADS * GDN_HEAD_DIM
GDN_CHUNK = 64
X_HEADS = 4
X_HEAD_DIM = 128
X_WIDTH = X_HEADS * X_HEAD_DIM

LOG2E = math.log2(math.e)
LANES = 128
FFT_N2 = 128
VMEM_LIMIT_BYTES = 48 * 1024 * 1024

BF16 = jnp.bfloat16
F32 = jnp.float32


def _cparams(*sem):
    return pltpu.CompilerParams(dimension_semantics=sem, vmem_limit_bytes=VMEM_LIMIT_BYTES)


def _dot(a, b):
    return jnp.dot(a, b, preferred_element_type=F32)


def _silu(x):
    return x * jax.nn.sigmoid(x)


def _norm_matmul_kernel(*refs, side):
    if side:
        x_ref, g_ref, w_ref, ws_ref, o_ref, os_ref, xn_ref = refs
    else:
        x_ref, g_ref, w_ref, o_ref, xn_ref = refs

    @pl.when(pl.program_id(1) == 0)
    def _():
        x = x_ref[...]
        ms = jnp.mean(x * x, axis=-1, keepdims=True)
        xn_ref[...] = (x * lax.rsqrt(ms + EPS) * g_ref[...]).astype(BF16)
        if side:
            os_ref[...] = _dot(xn_ref[...], ws_ref[...])

    o_ref[...] = _dot(xn_ref[...], w_ref[...]).astype(o_ref.dtype)


def _norm_matmul(x2d, g, w, tn, out_dtype=F32, tm=512, name="norm_matmul", w_side=None):
    t, d = x2d.shape
    n = w.shape[1]
    assert t % tm == 0 and n % tn == 0
    side = w_side is not None
    in_specs = [
        pl.BlockSpec((tm, d), lambda i, j: (i, 0)),
        pl.BlockSpec((1, d), lambda i, j: (0, 0)),
        pl.BlockSpec((d, tn), lambda i, j: (0, j)),
    ]
    out_specs = [pl.BlockSpec((tm, tn), lambda i, j: (i, j))]
    out_shape = [jax.ShapeDtypeStruct((t, n), out_dtype)]
    args = [x2d, g.reshape(1, d).astype(F32), w.astype(BF16)]
    if side:
        in_specs.append(pl.BlockSpec((d, LANES), lambda i, j: (0, 0)))
        out_specs.append(pl.BlockSpec((tm, LANES), lambda i, j: (i, 0)))
        out_shape.append(jax.ShapeDtypeStruct((t, LANES), F32))
        args.append(w_side.astype(BF16))
    outs = pl.pallas_call(
        functools.partial(_norm_matmul_kernel, side=side),
        grid=(t // tm, n // tn),
        in_specs=in_specs,
        out_specs=out_specs,
        out_shape=out_shape,
        scratch_shapes=[pltpu.VMEM((tm, d), BF16)],
        compiler_params=_cparams("parallel", "arbitrary"),
        name=name,
    )(*args)
    return tuple(outs) if side else outs[0]


HALO_ROWS = 16


def _fill_halo_buf(buf_ref, zc_ref, zp_ref, zn_ref, t, nt):
    tl = zc_ref.shape[1]
    hr = HALO_ROWS
    buf_ref[0:hr, :] = jnp.where(t > 0, zp_ref[0].astype(F32), 0.0)
    buf_ref[hr:hr + tl, :] = zc_ref[0].astype(F32)
    buf_ref[hr + tl:2 * hr + tl, :] = jnp.where(t < nt - 1, zn_ref[0].astype(F32), 0.0)


def _halo_specs(tl, w, l, col):
    r = tl // HALO_ROWS
    nb = l // HALO_ROWS
    cur = pl.BlockSpec((1, tl, w), lambda i, t: (i, t, col))
    prev = pl.BlockSpec((1, HALO_ROWS, w), lambda i, t: (i, jnp.maximum(t * r - 1, 0), col))
    nxt = pl.BlockSpec((1, HALO_ROWS, w), lambda i, t: (i, jnp.minimum((t + 1) * r, nb - 1), col))
    return cur, prev, nxt


def _hy_prep_kernel(zc_ref, zp_ref, zn_ref, g_ref, cw_ref, cb_ref, u_ref, e_ref, buf_ref):
    t = pl.program_id(1)
    nt = pl.num_programs(1)
    tl = zc_ref.shape[1]
    c = HY_WIDTH
    hr = HALO_ROWS
    _fill_halo_buf(buf_ref, zc_ref, zp_ref, zn_ref, t, nt)
    cw = 256
    for c0 in range(0, c, cw):
        parts = []
        for p in range(3):
            lo = p * c + c0
            acc = cb_ref[:, lo:lo + cw]
            for j in range(3):
                acc = acc + buf_ref[pl.ds(hr - 1 + j, tl), lo:lo + cw] * cw_ref[j:j + 1, lo:lo + cw]
            parts.append(acc)
        x0, x1, v = parts
        u_ref[0, :, c0:c0 + cw] = (v * x1).astype(u_ref.dtype)
        e_ref[0, :, c0:c0 + cw] = (x0 * _silu(g_ref[0, :, c0:c0 + cw].astype(F32))).astype(e_ref.dtype)


def _hy_prep(z, conv_w, conv_b, tl=512):
    b, l, _ = z.shape
    c = HY_WIDTH
    assert l % tl == 0
    cur, prev, nxt = _halo_specs(tl, 3 * c, l, E_HY // (3 * c))
    return pl.pallas_call(
        _hy_prep_kernel,
        grid=(b, l // tl),
        in_specs=[
            cur, prev, nxt,
            pl.BlockSpec((1, tl, c), lambda i, t: (i, t, E_GHY // c)),
            pl.BlockSpec((3, 3 * c), lambda i, t: (0, 0)),
            pl.BlockSpec((1, 3 * c), lambda i, t: (0, 0)),
        ],
        out_specs=[
            pl.BlockSpec((1, tl, c), lambda i, t: (i, t, 0)),
            pl.BlockSpec((1, tl, c), lambda i, t: (i, t, 0)),
        ],
        out_shape=[jax.ShapeDtypeStruct((b, l, c), F32), jax.ShapeDtypeStruct((b, l, c), F32)],
        scratch_shapes=[pltpu.VMEM((tl + 2 * HALO_ROWS, 3 * c), F32)],
        compiler_params=_cparams("parallel", "arbitrary"),
        name="hy_prep",
    )(z, z, z, z, conv_w.astype(F32), conv_b.reshape(1, 3 * c).astype(F32))


def _hy_filter_tables(l):
    t = jnp.linspace(0.0, 1.0, l, dtype=F32)[:, None]
    w = (2.0 * math.pi / l) * jnp.arange(l, dtype=F32)[:, None]
    f = jnp.linspace(1e-4, HY_BANDS - 1, HY_BANDS, dtype=F32)[None, :]
    emb = jnp.concatenate([t, jnp.cos(f * w), -jnp.sin(f * w)], axis=-1)
    emb = jnp.pad(emb, ((0, 0), (0, LANES - HY_EMB)))
    deltas = jnp.abs(jnp.linspace(math.log(HY_TARGET) / HY_SLOW_PCT, math.log(HY_TARGET) / HY_FAST_PCT, HY_WIDTH, dtype=F32))
    decay = jnp.exp(-t * jnp.tile(deltas, 2)[None, :])
    return emb, decay


def _hy_filter_kernel(emb_ref, dec_ref, w1_ref, b1_ref, w2_ref, b2_ref, w3_ref, fr_ref, sk_ref, o_ref):
    c = HY_WIDTH
    hp = lax.Precision.HIGHEST
    fr = fr_ref[...]
    hid = jnp.sin(fr * (jnp.dot(emb_ref[...], w1_ref[...], precision=hp, preferred_element_type=F32) + b1_ref[...]))
    hid = jnp.sin(fr * (jnp.dot(hid, w2_ref[...], precision=hp, preferred_element_type=F32) + b2_ref[...]))
    tl = emb_ref.shape[0]
    row = lax.broadcasted_iota(jnp.int32, (tl, 1), 0) + pl.program_id(0) * tl
    first = row == 0
    cw = 512
    for c0 in range(0, 2 * c, cw):
        filt = jnp.dot(hid, w3_ref[:, c0:c0 + cw], precision=hp, preferred_element_type=F32) * dec_ref[:, c0:c0 + cw]
        if c0 < c:
            filt = jnp.where(first, filt + sk_ref[:, c0:c0 + cw], filt)
            o_ref[0, :, c0:c0 + cw] = filt.astype(o_ref.dtype)
        else:
            filt = jnp.where(first, 0.0, filt)
            o_ref[1, :, c0 - c:c0 - c + cw] = filt.astype(o_ref.dtype)


def _hy_filters(l, w1, b1, w2, b2, w3, freq, skip, tl=256):
    c = HY_WIDTH
    hdim = HY_FILT_HIDDEN
    emb, decay = _hy_filter_tables(l)
    w1p = jnp.pad(w1.astype(F32), ((0, LANES - HY_EMB), (0, 0)))
    return pl.pallas_call(
        _hy_filter_kernel,
        grid=(l // tl,),
        in_specs=[
            pl.BlockSpec((tl, LANES), lambda t: (t, 0)),
            pl.BlockSpec((tl, 2 * c), lambda t: (t, 0)),
            pl.BlockSpec((LANES, hdim), lambda t: (0, 0)),
            pl.BlockSpec((1, hdim), lambda t: (0, 0)),
            pl.BlockSpec((hdim, hdim), lambda t: (0, 0)),
            pl.BlockSpec((1, hdim), lambda t: (0, 0)),
            pl.BlockSpec((hdim, 2 * c), lambda t: (0, 0)),
            pl.BlockSpec((1, hdim), lambda t: (0, 0)),
            pl.BlockSpec((1, c), lambda t: (0, 0)),
        ],
        out_specs=pl.BlockSpec((2, tl, c), lambda t: (0, t, 0)),
        out_shape=jax.ShapeDtypeStruct((2, l, c), BF16),
        compiler_params=_cparams("arbitrary"),
        name="hy_filter",
    )(emb, decay, w1p, b1.reshape(1, hdim).astype(F32), w2.astype(F32), b2.reshape(1, hdim).astype(F32),
      w3.astype(F32), freq.reshape(1, hdim).astype(F32), skip.reshape(1, c).astype(F32))


def _fft_consts(l):
    n = 2 * l
    n2 = FFT_N2
    n1 = n // n2
    n1h = n1 // 2
    w0 = 2.0 * math.pi / n
    k1 = jnp.arange(n1, dtype=jnp.int32)
    m1 = jnp.arange(n1h, dtype=jnp.int32)
    m2 = jnp.arange(n2, dtype=jnp.int32)
    ea = (n2 * m1[None, None, :] * k1[None, :, None] + m2[:, None, None] * k1[None, :, None]) % n
    ang = ea.astype(F32) * w0
    fa = jnp.concatenate([jnp.cos(ang), -jnp.sin(ang)], axis=1).astype(BF16)
    eb = (n1 * m2[:, None] * m2[None, :]) % n
    angb = eb.astype(F32) * w0
    fr, fi = jnp.cos(angb), -jnp.sin(angb)
    fb = jnp.concatenate([jnp.concatenate([fr, -fi], axis=1), jnp.concatenate([fi, fr], axis=1)], axis=0).astype(BF16)
    eg = (n1 * m2[None, :, None] * m2[None, None, :] + m2[None, :, None] * k1[:, None, None]) % n
    angg = eg.astype(F32) * w0
    gr, gi = jnp.cos(angg), jnp.sin(angg)
    gb = jnp.concatenate([jnp.concatenate([gr, -gi], axis=2), jnp.concatenate([gi, gr], axis=2)], axis=1).astype(BF16)
    ec = (n2 * m1[:, None] * k1[None, :]) % n
    angc = ec.astype(F32) * w0
    fc = (jnp.concatenate([jnp.cos(angc), -jnp.sin(angc)], axis=1) * (1.0 / n)).astype(BF16)
    return fa, fb, gb, fc


def _fft_a_kernel(u_ref, f_ref, o_ref):
    tn2 = f_ref.shape[0]
    n1 = f_ref.shape[1] // 2
    c = u_ref.shape[2] // tn2
    for j in range(tn2):
        a = _dot(f_ref[j], u_ref[0, :, j * c:(j + 1) * c])
        o_ref[0, 0, :, j * c:(j + 1) * c] = a[:n1].astype(o_ref.dtype)
        o_ref[0, 1, :, j * c:(j + 1) * c] = a[n1:].astype(o_ref.dtype)


def _fft_a(u, fa, tn2=8):
    b, l, c = u.shape
    n2, n1x2, n1h = fa.shape
    n1 = n1x2 // 2
    uv = u.reshape(b, n1h, n2 * c)
    return pl.pallas_call(
        _fft_a_kernel,
        grid=(b, n2 // tn2),
        in_specs=[
            pl.BlockSpec((1, n1h, tn2 * c), lambda i, j: (i, 0, j)),
            pl.BlockSpec((tn2, n1x2, n1h), lambda i, j: (j, 0, 0)),
        ],
        out_specs=pl.BlockSpec((1, 2, n1, tn2 * c), lambda i, j: (i, 0, 0, j)),
        out_shape=jax.ShapeDtypeStruct((b, 2, n1, n2 * c), BF16),
        compiler_params=_cparams("parallel", "arbitrary"),
        name="fft_a",
    )(uv, fa)


def _fft_bf_kernel(a_ref, w_ref, o_ref):
    n2 = a_ref.shape[2]
    ct = a_ref.shape[3]
    cw = 256
    for c0 in range(0, ct, cw):
        xf = _dot(w_ref[...], a_ref[0, :, :, c0:c0 + cw].reshape(2 * n2, cw))
        xb = _dot(w_ref[...], a_ref[1, :, :, c0:c0 + cw].reshape(2 * n2, cw))
        o_ref[0, :, c0:c0 + cw] = xf[:n2] + xb[:n2]
        o_ref[1, :, c0:c0 + cw] = xf[n2:] - xb[n2:]


def _fft_bf(a, fb):
    _, _, n, c = a.shape
    n2 = FFT_N2
    return pl.pallas_call(
        _fft_bf_kernel,
        grid=(n // n2,),
        in_specs=[
            pl.BlockSpec((2, 2, n2, c), lambda k: (0, 0, k, 0)),
            pl.BlockSpec((2 * n2, 2 * n2), lambda k: (0, 0)),
        ],
        out_specs=pl.BlockSpec((2, n2, c), lambda k: (0, k, 0)),
        out_shape=jax.ShapeDtypeStruct((2, n, c), F32),
        compiler_params=_cparams("arbitrary"),
        name="fft_bf",
    )(a, fb)


SUBLANES = 8
CONV_SLABS = 2


def _conv_dims(n1):
    kh = n1 // 2 + 1
    kp = -(-kh // SUBLANES) * SUBLANES
    return kh, kp, 2 * kp + SUBLANES


def _hy_conv_kernel(u_ref, h_ref, la_ref, w_ref, g_ref, fc_ref, o_ref, s_ref, *, n1, unroll_b):
    n2 = FFT_N2
    n1h = n1 // 2
    kh, kp, pitch = _conv_dims(n1)
    grp = SUBLANES
    slabs = range(CONV_SLABS)

    def lanes(x, sl):
        return x[:, sl * LANES:(sl + 1) * LANES]

    def stage_a(jj, carry):
        rows = [u_ref[0, pl.ds(pl.multiple_of(m * n2 + jj * grp, grp), grp), :] for m in range(n1h)]
        a = _dot(la_ref[jj], jnp.concatenate(rows, axis=0).astype(BF16))
        for j0 in range(grp):
            dst = pl.multiple_of((jj * grp + j0) * pitch, grp)
            for sl in slabs:
                s_ref[sl, pl.ds(dst, 2 * kp), :] = lanes(a[j0 * 2 * kp:(j0 + 1) * 2 * kp], sl)
        return carry

    lax.fori_loop(0, n2 // grp, stage_a, 0)

    def stage_b(kk, carry):
        k1s = [jnp.minimum(kk * unroll_b + i, kh - 1) for i in range(unroll_b)]
        ar = [jnp.concatenate([s_ref[sl, pl.ds(k, n2, stride=pitch), :] for sl in slabs], axis=1) for k in k1s]
        ai = [jnp.concatenate([s_ref[sl, pl.ds(kp + k, n2, stride=pitch), :] for sl in slabs], axis=1) for k in k1s]
        x = [_dot(w_ref[...], jnp.concatenate([ar[i], ai[i]], axis=0).astype(BF16)) for i in range(unroll_b)]
        y = []
        for i, k in enumerate(k1s):
            off = pl.multiple_of(k * n2, n2)
            hr, hi = h_ref[0, pl.ds(off, n2), :], h_ref[1, pl.ds(off, n2), :]
            xr, xi = x[i][:n2], x[i][n2:]
            y.append(jnp.concatenate([xr * hr - xi * hi, xr * hi + xi * hr], axis=0).astype(BF16))
        z = [_dot(g_ref[k1s[i]], y[i]) for i in range(unroll_b)]
        for i, k in enumerate(k1s):
            for sl in slabs:
                s_ref[sl, pl.ds(k, n2, stride=pitch), :] = lanes(z[i][:n2], sl)
                s_ref[sl, pl.ds(kp + k, n2, stride=pitch), :] = lanes(z[i][n2:], sl)
        return carry

    lax.fori_loop(0, -(-kh // unroll_b), stage_b, 0)

    def stage_c(jj, carry):
        zz = [jnp.concatenate([s_ref[sl, pl.ds(pl.multiple_of((jj * grp + j0) * pitch, grp), 2 * kp), :]
                               for sl in slabs], axis=1) for j0 in range(grp)]
        y = _dot(fc_ref[...], jnp.concatenate(zz, axis=0).astype(BF16))
        for m in range(n1h):
            o_ref[0, pl.ds(pl.multiple_of(m * n2 + jj * grp, grp), grp), :] = y[m * grp:(m + 1) * grp]
        return carry

    lax.fori_loop(0, n2 // grp, stage_c, 0)


def _hy_conv(u, h, fa, fb, gb, fc):
    b, l, c = u.shape
    n2, n1x2, n1h = fa.shape
    n1 = n1x2 // 2
    kh, kp, pitch = _conv_dims(n1)
    grp = SUBLANES
    ct = CONV_SLABS * LANES

    def half(t, axis):
        re, im = jnp.split(t, 2, axis=axis)
        pad = [(0, 0)] * t.ndim
        pad[axis] = (0, kp - kh)
        cut = lambda x: jnp.pad(lax.slice_in_dim(x, 0, kh, axis=axis), pad)
        return jnp.concatenate([cut(re), cut(im)], axis=axis)

    fa_h = half(fa, 1)
    la = (fa_h.reshape(n2 // grp, grp, 2 * kp, n1h, 1) * jnp.eye(grp, dtype=fa.dtype).reshape(1, grp, 1, 1, grp))
    la = la.reshape(n2 // grp, grp * 2 * kp, n1h * grp)
    wgt = jnp.ones((kh,), F32).at[1:kh - 1].set(2.0)
    wgt = jnp.pad(wgt, (0, kp - kh))
    fc_h = (half(fc, 1).astype(F32) * jnp.tile(wgt, 2)[None, :]).astype(BF16)
    lc = fc_h.reshape(n1h, 1, 1, 2 * kp) * jnp.eye(grp, dtype=fc_h.dtype).reshape(1, grp, grp, 1)
    fc_h = lc.reshape(n1h * grp, grp * 2 * kp)
    gb_h = gb[:kh]
    unroll_b = 3
    once = pl.Buffered(1)
    return pl.pallas_call(
        functools.partial(_hy_conv_kernel, n1=n1, unroll_b=unroll_b),
        grid=(c // ct, b),
        in_specs=[
            pl.BlockSpec((1, l, ct), lambda j, i: (i, 0, j)),
            pl.BlockSpec((2, kh * n2, ct), lambda j, i: (0, 0, j), pipeline_mode=once),
            pl.BlockSpec(la.shape, lambda j, i: (0, 0, 0), pipeline_mode=once),
            pl.BlockSpec(fb.shape, lambda j, i: (0, 0), pipeline_mode=once),
            pl.BlockSpec(gb_h.shape, lambda j, i: (0, 0, 0), pipeline_mode=once),
            pl.BlockSpec(fc_h.shape, lambda j, i: (0, 0), pipeline_mode=once),
        ],
        out_specs=pl.BlockSpec((1, l, ct), lambda j, i: (i, 0, j)),
        out_shape=jax.ShapeDtypeStruct((b, l, c), F32),
        scratch_shapes=[pltpu.VMEM((CONV_SLABS, n2 * pitch, LANES), F32)],
        compiler_params=_cparams("arbitrary", "arbitrary"),
        name="hy_conv",
    )(u, h, la, fb, gb_h, fc_h)


def _hyena_branch(z, l, conv_w, conv_b, w1, b1, w2, b2, w3, freq, skip):
    c = HY_WIDTH
    n = 2 * l
    fa, fb, gb, fc = _fft_consts(l)
    filt = _hy_filters(l, w1, b1, w2, b2, w3, freq, skip)
    fa_spec = _fft_a(filt, fa).reshape(2, 2, n, c)
    h = _fft_bf(fa_spec, fb)
    u, e = _hy_prep(z, conv_w, conv_b)
    return _hy_conv(u, h, fa, fb, gb, fc), e


def _gdn_prep_kernel(zc_ref, zp_ref, zn_ref, cw_ref, q_ref, k_ref, v_ref, buf_ref):
    t = pl.program_id(1)
    nt = pl.num_programs(1)
    tl = zc_ref.shape[1]
    dh = GDN_HEAD_DIM
    hr = HALO_ROWS
    _fill_halo_buf(buf_ref, zc_ref, zp_ref, zn_ref, t, nt)
    outs = (q_ref, k_ref, v_ref)
    for p in range(3):
        for h in range(GDN_HEADS):
            lo = p * GDN_WIDTH + h * dh
            acc = buf_ref[pl.ds(hr - 2, tl), lo:lo + dh] * cw_ref[0:1, lo:lo + dh]
            for j in range(1, 5):
                acc = acc + buf_ref[pl.ds(hr - 2 + j, tl), lo:lo + dh] * cw_ref[j:j + 1, lo:lo + dh]
            a = _silu(acc)
            if p < 2:
                a = a * lax.rsqrt(jnp.sum(a * a, axis=-1, keepdims=True) + EPS)
            if p == 0:
                a = a * (dh ** -0.5)
            outs[p][0, :, h * dh:(h + 1) * dh] = a.astype(BF16)


def _gdn_prep(z, conv_w, tl=512):
    b, l, _ = z.shape
    w = 3 * GDN_WIDTH
    cur, prev, nxt = _halo_specs(tl, w, l, E_QKV // w)
    o_spec = pl.BlockSpec((1, tl, GDN_WIDTH), lambda i, t: (i, t, 0))
    o_shape = jax.ShapeDtypeStruct((b, l, GDN_WIDTH), BF16)
    return pl.pallas_call(
        _gdn_prep_kernel,
        grid=(b, l // tl),
        in_specs=[cur, prev, nxt, pl.BlockSpec((5, w), lambda i, t: (0, 0))],
        out_specs=[o_spec, o_spec, o_spec],
        out_shape=[o_shape, o_shape, o_shape],
        scratch_shapes=[pltpu.VMEM((tl + 2 * HALO_ROWS, w), F32)],
        compiler_params=_cparams("parallel", "arbitrary"),
        name="gdn_prep",
    )(z, z, z, conv_w.astype(F32))


def _dot_nt(a, b):
    return lax.dot_general(a, b, (((1,), (1,)), ((), ())), preferred_element_type=F32)


def _dot_tn(a, b):
    return lax.dot_general(a, b, (((0,), (0,)), ((), ())), preferred_element_type=F32)


def _split3(x):
    a = x.astype(BF16)
    r = x - a.astype(F32)
    b = r.astype(BF16)
    c = (r - b.astype(F32)).astype(BF16)
    return a, b, c


GDN_CHUNKS_PER_STEP = 4


def _gdn_scan_kernel(*refs, direction, final):
    if final:
        q_ref, k_ref, v_ref, zs_ref, al_ref, dt_ref, of_ref, gg_ref, ng_ref, o_ref, s_ref = refs
    else:
        q_ref, k_ref, v_ref, zs_ref, al_ref, dt_ref, o_ref, s_ref = refs
    c = GDN_CHUNK
    dh = GDN_HEAD_DIM
    nsub = q_ref.shape[1] // c

    @pl.when(pl.program_id(1) == 0)
    def _():
        s_ref[...] = jnp.zeros_like(s_ref)

    row = lax.broadcasted_iota(jnp.int32, (c, c), 0)
    col = lax.broadcasted_iota(jnp.int32, (c, c), 1)
    if direction == 0:
        incl, strict, last = col <= row, col < row, c - 1
    else:
        incl, strict, last = col >= row, col > row, 0
    tri = incl.astype(BF16)
    eye = (row == col).astype(F32)
    heads = range(GDN_HEADS)
    hsl = [slice(h * dh, (h + 1) * dh) for h in heads]
    lns = [2 * GDN_HEADS + direction * GDN_HEADS + h for h in heads]

    rows = [slice(i * c, (i + 1) * c) for i in range(nsub)]
    gcum, gcum_t, beta_all = [], [], []
    for rs in rows:
        zs = zs_ref[0, rs, :]
        beta_all.append(jax.nn.sigmoid(zs))
        gl = -jnp.exp(al_ref[...]) * jax.nn.softplus(zs + dt_ref[...])
        g1, g2, g3 = _split3(gl)
        gsum = _dot(tri, g1) + _dot(tri, g2) + _dot(tri, g3)
        gcum.append(gsum)
        gcum_t.append(jnp.concatenate([gsum, jnp.zeros_like(gsum)], axis=0).T)
    items = [(i, h) for i in range(nsub) for h in heads]
    it = range(len(items))
    beta = [beta_all[i][:, direction * GDN_HEADS + h:direction * GDN_HEADS + h + 1] for i, h in items]
    gc = [gcum[i][:, lns[h]:lns[h] + 1] for i, h in items]
    gct = [gcum_t[i][lns[h]:lns[h] + 1, 0:c] for i, h in items]
    glast = [gcum_t[i][lns[h]:lns[h] + 1, last:last + 1] for i, h in items]
    qh = [q_ref[0, rows[i], hsl[h]] for i, h in items]
    kh = [k_ref[0, rows[i], hsl[h]] for i, h in items]
    vh = [v_ref[0, rows[i], hsl[h]] for i, h in items]
    kq = [_dot_nt(jnp.concatenate([kh[x], qh[x]], axis=0), kh[x]) for x in it]
    dmat = [jnp.exp(jnp.where(incl, gc[x] - gct[x], -jnp.inf)) for x in it]
    a = [jnp.where(strict, (beta[x] * kq[x][:c]) * dmat[x], 0.0) for x in it]
    tinv = [eye - a[x] for x in it]
    ab = [a[x].astype(BF16) for x in it]
    p = [_dot(ab[x], ab[x]) for x in it]
    for j in range(5):
        pb = [p[x].astype(BF16) for x in it]
        if j < 4:
            tp = [_dot(jnp.concatenate([tinv[x].astype(BF16), pb[x]], axis=0), pb[x]) for x in it]
            tinv = [tinv[x] + tp[x][:c] for x in it]
            p = [tp[x][c:] for x in it]
        else:
            tinv = [tinv[x] + _dot(tinv[x].astype(BF16), pb[x]) for x in it]
    eg = [jnp.exp(gc[x]) for x in it]
    kf = [kh[x].astype(F32) for x in it]
    rhs = [jnp.concatenate([vh[x].astype(F32) * beta[x], kf[x] * (beta[x] * eg[x])], axis=1).astype(BF16) for x in it]
    uw = [_dot(tinv[x].astype(BF16), rhs[x]) for x in it]
    wq = [jnp.concatenate([uw[x][:, dh:], qh[x].astype(F32) * eg[x]], axis=0).astype(BF16) for x in it]
    qkd = [(kq[x][c:] * dmat[x]).astype(BF16) for x in it]
    kd = [(kf[x] * jnp.exp(glast[x] - gc[x])).astype(BF16) for x in it]

    order = list(range(nsub)) if direction == 0 else list(range(nsub - 1, -1, -1))
    state = [s_ref[h] for h in heads]
    for i in order:
        rs = rows[i]
        xs = [i * GDN_HEADS + h for h in heads]
        ws = [_dot(wq[xs[h]], state[h].astype(BF16)) for h in heads]
        vnb = [(uw[xs[h]][:, :dh] - ws[h][:c]).astype(BF16) for h in heads]
        o = [ws[h][c:] + _dot(qkd[xs[h]], vnb[h]) for h in heads]
        state = [state[h] * jnp.exp(glast[xs[h]]) + _dot_tn(kd[xs[h]], vnb[h]) for h in heads]
        for h in heads:
            if final:
                tot = of_ref[0, rs, hsl[h]] + o[h]
                y = tot * lax.rsqrt(jnp.mean(tot * tot, axis=-1, keepdims=True) + EPS) * ng_ref[...]
                o_ref[0, rs, hsl[h]] = (y * _silu(gg_ref[0, rs, hsl[h]].astype(F32))).astype(o_ref.dtype)
            else:
                o_ref[0, rs, hsl[h]] = o[h].astype(o_ref.dtype)
    for h in heads:
        s_ref[h] = state[h]


def _gdn_scan(q, k, v, zs, a_log, dt_bias, direction, o_fwd=None, z=None, gate_blk=None, norm_g=None):
    b, l, w = q.shape
    c = GDN_CHUNK * GDN_CHUNKS_PER_STEP
    n = l // c
    final = o_fwd is not None
    if direction == 0:
        cmap = lambda i, t: (i, t, 0)
    else:
        cmap = lambda i, t: (i, n - 1 - t, 0)
    blk = pl.BlockSpec((1, c, w), cmap)
    vec = pl.BlockSpec((1, LANES), lambda i, t: (0, 0))
    pad = lambda x: jnp.pad(x.reshape(1, -1).astype(F32), ((0, 0), (2 * GDN_HEADS, LANES - 4 * GDN_HEADS)))
    in_specs = [blk, blk, blk, pl.BlockSpec((1, c, LANES), cmap), vec, vec]
    args = [q, k, v, zs, pad(a_log), pad(dt_bias)]
    if final:
        if direction == 0:
            gmap = lambda i, t: (i, t, gate_blk)
        else:
            gmap = lambda i, t: (i, n - 1 - t, gate_blk)
        in_specs += [blk, pl.BlockSpec((1, c, w), gmap), pl.BlockSpec((1, GDN_HEAD_DIM), lambda i, t: (0, 0))]
        args += [o_fwd, z, norm_g.reshape(1, GDN_HEAD_DIM).astype(F32)]
    return pl.pallas_call(
        functools.partial(_gdn_scan_kernel, direction=direction, final=final),
        grid=(b, n),
        in_specs=in_specs,
        out_specs=blk,
        out_shape=jax.ShapeDtypeStruct((b, l, w), BF16 if final else F32),
        scratch_shapes=[pltpu.VMEM((GDN_HEADS, GDN_HEAD_DIM, GDN_HEAD_DIM), F32)],
        compiler_params=_cparams("parallel", "arbitrary"),
        name="gdn_scan_bwd" if direction else "gdn_scan_fwd",
    )(*args)


def _gdn_branch(z, zs, conv_w, a_log, dt_bias, norm_g, gate_blk):
    q, k, v = _gdn_prep(z, conv_w)
    o_f = _gdn_scan(q, k, v, zs, a_log, dt_bias, 0)
    return _gdn_scan(q, k, v, zs, a_log, dt_bias, 1, o_fwd=o_f, z=z, gate_blk=gate_blk, norm_g=norm_g)


def _softmax_t(s, extra=None):
    m = jnp.max(s, axis=0, keepdims=True)
    if extra is not None:
        m = jnp.maximum(m, extra)
    p = jnp.exp2(s - m)
    den = jnp.sum(p, axis=0, keepdims=True)
    if extra is not None:
        den = den + jnp.exp2(extra - m)
    return p.astype(BF16), den, m


def _t_bf16(x):
    return x.astype(F32).T.astype(BF16)


def _xattn_kernel(q_ref, g_ref, kv_ref, o_ref):
    dh = X_HEAD_DIM
    for h in range(X_HEADS):
        hs = slice(h * dh, (h + 1) * dh)
        qh = q_ref[0, :, hs].astype(BF16)
        kh = kv_ref[0, :, hs]
        vh = kv_ref[0, :, X_WIDTH + h * dh:X_WIDTH + (h + 1) * dh]
        s = _dot_nt(qh, kh) * (dh ** -0.5)
        p = jnp.exp(s - jnp.max(s, axis=-1, keepdims=True))
        den = jnp.sum(p, axis=-1, keepdims=True)
        o = _dot(p.astype(BF16), vh) / den
        o_ref[0, :, hs] = (o * _silu(g_ref[0, :, hs].astype(F32))).astype(o_ref.dtype)


def _xattn(z, kv, q_blk, g_blk, tq=512):
    b, l, _ = z.shape
    m = kv.shape[1]
    w = X_WIDTH
    return pl.pallas_call(
        _xattn_kernel,
        grid=(b, l // tq),
        in_specs=[
            pl.BlockSpec((1, tq, w), lambda i, t: (i, t, q_blk)),
            pl.BlockSpec((1, tq, w), lambda i, t: (i, t, g_blk)),
            pl.BlockSpec((1, m, 2 * w), lambda i, t: (i, 0, 0)),
        ],
        out_specs=pl.BlockSpec((1, tq, w), lambda i, t: (i, t, 0)),
        out_shape=jax.ShapeDtypeStruct((b, l, w), BF16),
        compiler_params=_cparams("parallel", "arbitrary"),
        name="xattn",
    )(z, z, kv)


def _out_proj_kernel(*refs, widths, gated):
    nparts = len(widths)
    y_refs = refs[:nparts]
    if gated:
        e_ref, w_ref, g_ref, x_ref, o_ref = refs[nparts:]
    else:
        w_ref, g_ref, x_ref, o_ref = refs[nparts:]
    d = o_ref.shape[1]
    cw = 256
    ys = [y_ref[...] for y_ref in y_refs]
    if gated:
        ys[0] = ys[0] * e_ref[...]
    ys = [y.astype(BF16) for y in ys]
    ssq = jnp.zeros((o_ref.shape[0], 1), F32)
    for c0 in range(0, d, cw):
        acc = None
        off = 0
        for y, wd in zip(ys, widths):
            part = _dot(y, w_ref[off:off + wd, c0:c0 + cw])
            acc = part if acc is None else acc + part
            off += wd
        ssq = ssq + jnp.sum(acc * acc, axis=-1, keepdims=True)
        o_ref[:, c0:c0 + cw] = acc
    r = lax.rsqrt(ssq * (1.0 / d) + EPS)
    for c0 in range(0, d, cw):
        o_ref[:, c0:c0 + cw] = x_ref[:, c0:c0 + cw] + o_ref[:, c0:c0 + cw] * r * g_ref[:, c0:c0 + cw]


def _out_proj(parts, w_out, post_g, x2d, gate=None, tm=512):
    t, d = x2d.shape
    widths = tuple(int(p.shape[1]) for p in parts)
    kdim = sum(widths)
    gates = [] if gate is None else [gate]
    return pl.pallas_call(
        functools.partial(_out_proj_kernel, widths=widths, gated=gate is not None),
        grid=(t // tm,),
        in_specs=[pl.BlockSpec((tm, wd), lambda i: (i, 0)) for wd in widths]
        + [pl.BlockSpec((tm, widths[0]), lambda i: (i, 0)) for _ in gates] + [
            pl.BlockSpec((kdim, d), lambda i: (0, 0)),
            pl.BlockSpec((1, d), lambda i: (0, 0)),
            pl.BlockSpec((tm, d), lambda i: (i, 0)),
        ],
        out_specs=pl.BlockSpec((tm, d), lambda i: (i, 0)),
        out_shape=jax.ShapeDtypeStruct((t, d), F32),
        compiler_params=_cparams("parallel"),
        name="out_proj",
    )(*parts, *gates, w_out.astype(BF16), post_g.reshape(1, d).astype(F32), x2d)


def _mem_kv(mem, mem_g, w_mem_kv):
    b, m, d = mem.shape
    kv = _norm_matmul(mem.reshape(b * m, d), mem_g, w_mem_kv, 1024, out_dtype=BF16, name="mem_kv")
    return kv.reshape(b, m, 2 * X_WIDTH)


E_QKV, E_HY, E_GHY, E_GGDN, E_XQ, E_GX, E_BIG = 0, 3072, 6144, 7168, 8192, 8704, 9216


def _even_layer(x, mem, pre_g, post_g, w_in, w_out, hy_conv_w, hy_conv_b, hy_w1, hy_b1, hy_w2, hy_b2, hy_w3, hy_freq,
                hy_skip, gdn_conv_w, gdn_a_log, gdn_dt_bias, gdn_norm_g, mem_g, w_mem_kv):
    b, l, d = x.shape
    x2d = x.reshape(b * l, d)
    w_big = jnp.concatenate([w_in[:, 4096:7168], w_in[:, 0:3072], w_in[:, 3072:4096], w_in[:, 7168:8192],
                             w_in[:, 8224:8736], w_in[:, 8736:9248]], axis=1)
    w_small = jnp.pad(w_in[:, 8192:8224], ((0, 0), (0, LANES - 4 * GDN_HEADS)))
    z, zs = _norm_matmul(x2d, pre_g, w_big, 1024, out_dtype=BF16, tm=1024, name="even_in", w_side=w_small)
    z = z.reshape(b, l, E_BIG)
    zs = zs.reshape(b, l, LANES)
    conv, e = _hyena_branch(z, l, hy_conv_w, hy_conv_b, hy_w1, hy_b1, hy_w2, hy_b2, hy_w3, hy_freq, hy_skip)
    y_b = _gdn_branch(z, zs, gdn_conv_w, gdn_a_log, gdn_dt_bias, gdn_norm_g, E_GGDN // GDN_WIDTH)
    kv = _mem_kv(mem, mem_g, w_mem_kv)
    y_x = _xattn(z, kv, E_XQ // X_WIDTH, E_GX // X_WIDTH)
    t = b * l
    out = _out_proj([conv.reshape(t, -1), y_b.reshape(t, -1), y_x.reshape(t, -1)], w_out, post_g, x2d,
                    gate=e.reshape(t, -1))
    return out.reshape(b, l, d)


DIL_PATTERNS = ((128, 1), (512, 4), (2048, 16))
N_DIL = len(DIL_PATTERNS)
DIL_HEADS = 4
DIL_HEAD_DIM = 128
DIL_WIDTH = DIL_HEADS * DIL_HEAD_DIM
SWA_Q_HEADS = 16
SWA_KV_HEADS = 2
SWA_HEAD_DIM = 64
SWA_WIDTH = SWA_Q_HEADS * SWA_HEAD_DIM
SWA_HALF_WINDOW = 128
O_CQKV, O_GC, O_DQ, O_GD, O_XQ, O_GX, O_DKV, O_ALL = 0, 4608, 5120, 6144, 7168, 7680, 8192, 8448


def _rope_tables(l, dh):
    half = dh // 8
    inv = ROPE_THETA ** (-jnp.arange(half, dtype=F32) / half)
    ang = jnp.arange(l, dtype=F32)[:, None] * inv[None, :]
    cos, sin = jnp.cos(ang), jnp.sin(ang)
    one = jnp.ones((l, dh - 2 * half), F32)
    zero_h = jnp.zeros((l, half), F32)
    zero_r = jnp.zeros((l, dh - 2 * half), F32)
    c = jnp.concatenate([cos, cos, one], axis=1)
    sa = jnp.concatenate([-sin, zero_h, zero_r], axis=1)
    sb = jnp.concatenate([zero_h, sin, zero_r], axis=1)
    rep = LANES // dh
    return tuple(jnp.tile(t, (1, rep)) for t in (c, sa, sb))


def _odd_prep_kernel(c_ref, dq_ref, dkv_ref, c1_ref, a1_ref, b1_ref, c2_ref, a2_ref, b2_ref,
                     cq_ref, ck_ref, cv_ref, dqe_ref, dk_ref, dv_ref):
    tl = c_ref.shape[1]
    lane = lax.broadcasted_iota(jnp.int32, (tl, LANES), 1)
    low = lane < SWA_HEAD_DIM
    c1, a1, b1 = c1_ref[...], a1_ref[...], b1_ref[...]
    c2, a2, b2 = c2_ref[...], a2_ref[...], b2_ref[...]
    h1 = DIL_HEAD_DIM // 8
    h2 = SWA_HEAD_DIM // 8

    def rope1(x):
        return x * c1 + pltpu.roll(x, LANES - h1, 1) * a1 + pltpu.roll(x, h1, 1) * b1

    def rope2(x):
        return x * c2 + pltpu.roll(x, LANES - h2, 1) * a2 + pltpu.roll(x, h2, 1) * b2

    nblk = N_DIL * DIL_HEADS
    for j in range(nblk):
        ls = slice(j * LANES, (j + 1) * LANES)
        cq_ref[0, :, ls] = rope1(c_ref[0, :, ls].astype(F32)).astype(BF16)
        ck_ref[0, :, ls] = rope1(c_ref[0, :, nblk * LANES + j * LANES:nblk * LANES + (j + 1) * LANES].astype(F32)).astype(BF16)
        cv_ref[0, :, ls] = c_ref[0, :, 2 * nblk * LANES + j * LANES:2 * nblk * LANES + (j + 1) * LANES].astype(BF16)
    for j in range(SWA_Q_HEADS // 2):
        xr = rope2(dq_ref[0, :, j * LANES:(j + 1) * LANES].astype(F32)) * (SWA_HEAD_DIM ** -0.5 * LOG2E)
        dqe_ref[0, :, (2 * j) * LANES:(2 * j + 1) * LANES] = jnp.where(low, xr, 0.0).astype(BF16)
        dqe_ref[0, :, (2 * j + 1) * LANES:(2 * j + 2) * LANES] = jnp.where(low, pltpu.roll(xr, SWA_HEAD_DIM, 1), 0.0).astype(BF16)
    kr = rope2(dkv_ref[0, :, 0:LANES].astype(F32))
    vv = dkv_ref[0, :, LANES:2 * LANES].astype(F32)
    kr_sw = pltpu.roll(kr, SWA_HEAD_DIM, 1)
    vv_sw = pltpu.roll(vv, SWA_HEAD_DIM, 1)
    dk_ref[0, :, 0:LANES] = jnp.where(low, kr, 0.0).astype(BF16)
    dk_ref[0, :, LANES:2 * LANES] = jnp.where(low, kr_sw, 0.0).astype(BF16)
    dv_ref[0, :, 0:LANES] = jnp.where(low, vv, vv_sw).astype(BF16)
    dv_ref[0, :, LANES:2 * LANES] = jnp.where(low, vv_sw, vv).astype(BF16)


def _odd_prep(z, tl=256):
    b, l, _ = z.shape
    wc = 3 * N_DIL * DIL_WIDTH
    t1 = _rope_tables(l, DIL_HEAD_DIM)
    t2 = _rope_tables(l, SWA_HEAD_DIM)
    tab = pl.BlockSpec((tl, LANES), lambda i, t: (t, 0))

    def spec(w):
        return pl.BlockSpec((1, tl, w), lambda i, t: (i, t, 0))

    def shape(w):
        return jax.ShapeDtypeStruct((b, l, w), BF16)

    wd = N_DIL * DIL_WIDTH
    return pl.pallas_call(
        _odd_prep_kernel,
        grid=(b, l // tl),
        in_specs=[
            pl.BlockSpec((1, tl, wc), lambda i, t: (i, t, 0)),
            pl.BlockSpec((1, tl, SWA_WIDTH), lambda i, t: (i, t, O_DQ // SWA_WIDTH)),
            pl.BlockSpec((1, tl, 2 * LANES), lambda i, t: (i, t, O_DKV // (2 * LANES))),
            tab, tab, tab, tab, tab, tab,
        ],
        out_specs=[spec(wd), spec(wd), spec(wd), spec(2 * SWA_WIDTH), spec(2 * LANES), spec(2 * LANES)],
        out_shape=[shape(wd), shape(wd), shape(wd), shape(2 * SWA_WIDTH), shape(2 * LANES), shape(2 * LANES)],
        compiler_params=_cparams("parallel", "arbitrary"),
        name="odd_prep",
    )(z, z, z, *t1, *t2)


def _band_bias_t(tq, half, ls):
    p0 = pl.program_id(1) * tq
    j = lax.broadcasted_iota(jnp.int32, (tq + 2 * half, tq), 0)
    i = lax.broadcasted_iota(jnp.int32, (tq + 2 * half, tq), 1)
    kpos = p0 - half + j
    valid = (j >= i) & (j - i <= 2 * half) & (kpos >= 0) & (kpos < ls)
    return jnp.where(valid, 0.0, -jnp.inf)


def _band_mask(tq, half, ls):
    p0 = pl.program_id(1) * tq
    i = lax.broadcasted_iota(jnp.int32, (tq, tq + 2 * half), 0)
    j = lax.broadcasted_iota(jnp.int32, (tq, tq + 2 * half), 1)
    kpos = p0 - half + j
    return (j >= i) & (j - i <= 2 * half) & (kpos >= 0) & (kpos < ls)


def _band_attn_kernel(q_ref, kp_ref, kc_ref, kn_ref, vp_ref, vc_ref, vn_ref, o_ref, lse_ref, *, half, ls):
    tq = q_ref.shape[1]
    dh = DIL_HEAD_DIM
    valid = _band_mask(tq, half, ls)
    lane = lax.broadcasted_iota(jnp.int32, (tq, LANES), 1)
    lse_all = jnp.zeros((tq, LANES), F32)
    heads = range(DIL_HEADS)
    hsl = [slice(h * dh, (h + 1) * dh) for h in heads]
    kall = [jnp.concatenate([kp_ref[0, :, s], kc_ref[0, :, s], kn_ref[0, :, s]], axis=0) for s in hsl]
    vall = [jnp.concatenate([vp_ref[0, :, s], vc_ref[0, :, s], vn_ref[0, :, s]], axis=0) for s in hsl]
    sc = [jnp.where(valid, _dot_nt(q_ref[0, :, hsl[h]], kall[h]) * (dh ** -0.5), -jnp.inf) for h in heads]
    m = [jnp.max(sc[h], axis=-1, keepdims=True) for h in heads]
    p = [jnp.exp(sc[h] - m[h]) for h in heads]
    den = [jnp.sum(p[h], axis=-1, keepdims=True) for h in heads]
    o = [_dot(p[h].astype(BF16), vall[h]) / den[h] for h in heads]
    for h in heads:
        o_ref[0, :, hsl[h]] = o[h].astype(o_ref.dtype)
        lse_all = jnp.where(lane == h, m[h] + jnp.log(den[h]), lse_all)
    lse_ref[0] = lse_all[:, :LSE_LANES]


LSE_LANES = 8


def _band_specs(tq, half, ls, w, col):
    r = tq // half
    nb = ls // half
    cur = pl.BlockSpec((1, tq, w), lambda i, t: (i, t, col))
    prev = pl.BlockSpec((1, half, w), lambda i, t: (i, jnp.maximum(t * r - 1, 0), col))
    nxt = pl.BlockSpec((1, half, w), lambda i, t: (i, jnp.minimum((t + 1) * r, nb - 1), col))
    return cur, prev, nxt


def _band_attn(q, k, v, half):
    n, ls, w = q.shape
    tq = min(2 * LANES, ls)
    cur, prev, nxt = _band_specs(tq, half, ls, w, 0)
    return pl.pallas_call(
        functools.partial(_band_attn_kernel, half=half, ls=ls),
        grid=(n, ls // tq),
        in_specs=[cur, prev, cur, nxt, prev, cur, nxt],
        out_specs=[cur, pl.BlockSpec((1, tq, LSE_LANES), lambda i, t: (i, t, 0))],
        out_shape=[jax.ShapeDtypeStruct((n, ls, w), BF16), jax.ShapeDtypeStruct((n, ls, LSE_LANES), F32)],
        compiler_params=_cparams("parallel", "arbitrary"),
        name="band_attn",
    )(q, k, k, k, v, v, v)


def _dil_merge_kernel(o0_ref, o1_ref, o2_ref, l0_ref, l1_ref, l2_ref, g_ref, y_ref):
    dh = DIL_HEAD_DIM
    o_refs = (o0_ref, o1_ref, o2_ref)
    for h in range(DIL_HEADS):
        hs = slice(h * dh, (h + 1) * dh)
        ls = [r[0, :, h:h + 1] for r in (l0_ref, l1_ref, l2_ref)]
        m = jnp.maximum(jnp.maximum(ls[0], ls[1]), ls[2])
        ws = [jnp.exp(x - m) for x in ls]
        den = ws[0] + ws[1] + ws[2]
        y = ((ws[0] / den) * o_refs[0][0, :, hs].astype(F32) + (ws[1] / den) * o_refs[1][0, :, hs].astype(F32)
             + (ws[2] / den) * o_refs[2][0, :, hs].astype(F32))
        y_ref[0, :, hs] = (y * _silu(g_ref[0, :, hs].astype(F32))).astype(y_ref.dtype)


def _dil_merge(outs, lses, z, tl=512):
    b, l, w = outs[0].shape
    o_spec = pl.BlockSpec((1, tl, w), lambda i, t: (i, t, 0))
    l_spec = pl.BlockSpec((1, tl, LSE_LANES), lambda i, t: (i, t, 0))
    return pl.pallas_call(
        _dil_merge_kernel,
        grid=(b, l // tl),
        in_specs=[o_spec, o_spec, o_spec, l_spec, l_spec, l_spec, pl.BlockSpec((1, tl, w), lambda i, t: (i, t, O_GC // DIL_WIDTH))],
        out_specs=o_spec,
        out_shape=jax.ShapeDtypeStruct((b, l, w), BF16),
        compiler_params=_cparams("parallel", "arbitrary"),
        name="dil_merge",
    )(*outs, *lses, z)


def _dilated_branch(cq, ck, cv, z):
    b, l, _ = cq.shape
    w = DIL_WIDTH
    outs, lses = [], []
    for gi, (window, d) in enumerate(DIL_PATTERNS):
        half = window // (2 * d)
        ls = l // d

        def to_res(x):
            x = x[:, :, gi * w:(gi + 1) * w]
            if d == 1:
                return x
            return jnp.swapaxes(x.reshape(b, ls, d, w), 1, 2).reshape(b * d, ls, w)

        def from_res(x):
            if d == 1:
                return x
            return jnp.swapaxes(x.reshape(b, d, ls, x.shape[-1]), 1, 2).reshape(b, l, x.shape[-1])

        o, lse = _band_attn(to_res(cq), to_res(ck), to_res(cv), half)
        outs.append(from_res(o))
        lses.append(from_res(lse))
    return _dil_merge(outs, lses, z)


def _swa_kernel(q_ref, kp_ref, kc_ref, kn_ref, vp_ref, vc_ref, vn_ref, g_ref, sink_ref, o_ref, *, half, ls):
    tq = q_ref.shape[1]
    dh = SWA_HEAD_DIM
    bias1 = _band_bias_t(tq, half, ls)
    bias = jnp.concatenate([bias1, bias1], axis=1)
    b_prev, b_next = bias[:half], bias[half + tq:]
    low = lax.broadcasted_iota(jnp.int32, (LANES, tq), 0) < dh
    first = lax.broadcasted_iota(jnp.int32, (1, 2 * tq), 1) < tq
    grp = SWA_Q_HEADS // SWA_KV_HEADS
    for g in range(SWA_KV_HEADS):
        gs = slice(g * LANES, (g + 1) * LANES)
        kall = jnp.concatenate([kp_ref[0, :, gs], kc_ref[0, :, gs], kn_ref[0, :, gs]], axis=0)
        vt = _t_bf16(jnp.concatenate([vp_ref[0, :, gs], vc_ref[0, :, gs], vn_ref[0, :, gs]], axis=0))
        pairs = range(grp // 2)
        h0 = [g * grp + 2 * jp for jp in pairs]
        q2 = [q_ref[0, :, h * LANES:(h + 2) * LANES] for h in h0]
        q2 = [jnp.concatenate([x[:, :LANES], x[:, LANES:]], axis=0) for x in q2]
        sc = [_dot_nt(kall, q2[jp]) for jp in pairs]
        sc = [jnp.concatenate([s[:half] + b_prev, s[half:half + tq], s[half + tq:] + b_next], axis=0) for s in sc]
        sk = [jnp.where(first, sink_ref[:, h:h + 1], sink_ref[:, h + 1:h + 2]) * LOG2E for h in h0]
        sm = [_softmax_t(sc[jp], sk[jp]) for jp in pairs]
        ot = [_dot(vt, sm[jp][0]) / sm[jp][1] for jp in pairs]
        for jp in pairs:
            blk = (g * grp) // 2 + jp
            bs = slice(blk * LANES, (blk + 1) * LANES)
            y = jnp.where(low, ot[jp][:, :tq], ot[jp][:, tq:]).T * _silu(g_ref[0, :, bs].astype(F32))
            o_ref[0, :, bs] = y.astype(o_ref.dtype)


def _swa_branch(dqe, dk, dv, z, sink, tq=128):
    b, l, _ = dqe.shape
    half = SWA_HALF_WINDOW
    assert tq == half == LANES
    _, prev, nxt = _band_specs(tq, half, l, 2 * LANES, 0)
    cur = pl.BlockSpec((1, tq, 2 * LANES), lambda i, t: (i, t, 0))
    sink_p = jnp.pad(sink.reshape(1, SWA_Q_HEADS).astype(F32), ((0, 0), (0, LANES - SWA_Q_HEADS)))
    return pl.pallas_call(
        functools.partial(_swa_kernel, half=half, ls=l),
        grid=(b, l // tq),
        in_specs=[
            pl.BlockSpec((1, tq, 2 * SWA_WIDTH), lambda i, t: (i, t, 0)),
            prev, cur, nxt, prev, cur, nxt,
            pl.BlockSpec((1, tq, SWA_WIDTH), lambda i, t: (i, t, O_GD // SWA_WIDTH)),
            pl.BlockSpec((1, LANES), lambda i, t: (0, 0)),
        ],
        out_specs=pl.BlockSpec((1, tq, SWA_WIDTH), lambda i, t: (i, t, 0)),
        out_shape=jax.ShapeDtypeStruct((b, l, SWA_WIDTH), BF16),
        compiler_params=_cparams("parallel", "arbitrary"),
        name="swa",
    )(dqe, dk, dk, dk, dv, dv, dv, z, sink_p)


def _odd_layer(x, mem, pre_g, post_g, w_in, w_out, swa_sink, mem_g, w_mem_kv):
    b, l, d = x.shape
    x2d = x.reshape(b * l, d)
    w_re = jnp.concatenate([w_in[:, 0:6144], w_in[:, 6400:8448], w_in[:, 6144:6400]], axis=1)
    z = _norm_matmul(x2d, pre_g, w_re, 1408, out_dtype=BF16, tm=1024, name="odd_in").reshape(b, l, O_ALL)
    cq, ck, cv, dqe, dk, dv = _odd_prep(z)
    y_c = _dilated_branch(cq, ck, cv, z)
    y_d = _swa_branch(dqe, dk, dv, z, swa_sink)
    kv = _mem_kv(mem, mem_g, w_mem_kv)
    y_x = _xattn(z, kv, O_XQ // X_WIDTH, O_GX // X_WIDTH)
    t = b * l
    out = _out_proj([y_c.reshape(t, -1), y_d.reshape(t, -1), y_x.reshape(t, -1)], w_out, post_g, x2d)
    return out.reshape(b, l, d)


def _trunk(x, mem, even_params, odd_params):
    x = _even_layer(x, mem, *[p[0] for p in even_params])
    return _odd_layer(x, mem, *[p[0] for p in odd_params])


def kernel(x_prompt, x_sample, mem_prompt, mem_sample, e_pre_g, e_post_g, e_w_in, e_w_out, hy_conv_w, hy_conv_b,
           hy_filt_w1, hy_filt_b1, hy_filt_w2, hy_filt_b2, hy_filt_w3, hy_freq, hy_skip, gdn_conv_w, gdn_A_log,
           gdn_dt_bias, gdn_norm_g, e_mem_g, e_w_mem_kv, o_pre_g, o_post_g, o_w_in, o_w_out, swa_sink, o_mem_g,
           o_w_mem_kv):
    even_params = (e_pre_g, e_post_g, e_w_in, e_w_out, hy_conv_w, hy_conv_b, hy_filt_w1, hy_filt_b1, hy_filt_w2,
                   hy_filt_b2, hy_filt_w3, hy_freq, hy_skip, gdn_conv_w, gdn_A_log, gdn_dt_bias, gdn_norm_g,
                   e_mem_g, e_w_mem_kv)
    odd_params = (o_pre_g, o_post_g, o_w_in, o_w_out, swa_sink, o_mem_g, o_w_mem_kv)
    y_prompt = _trunk(x_prompt, mem_prompt, even_params, odd_params)
    y_sample = _trunk(x_sample, mem_sample, even_params, odd_params)
    return (y_prompt, y_sample)
```

```python
import functools
import math

import jax
import jax.numpy as jnp
import numpy as np
from jax import lax
from jax.experimental import pallas as pl
from jax.experimental.pallas import tpu as pltpu

D_MODEL = 1024
EPS = 1e-6
ROPE_THETA = 500000.0
HY_WIDTH = 1024
HY_EMB = 33
HY_BANDS = (HY_EMB - 1) // 2
HY_FILT_HIDDEN = 64
HY_TARGET = 1e-2
HY_FAST_PCT = 0.3
HY_SLOW_PCT = 1.5
GDN_HEADS = 8
GDN_HEAD_DIM = 128
GDN_WIDTH = GDN_HEADS * GDN_HEAD_DIM
GDN_CHUNK = 64
X_HEADS = 4
X_HEAD_DIM = 128
X_WIDTH = X_HEADS * X_HEAD_DIM

LOG2E = math.log2(math.e)
LANES = 128
FFT_N2 = 128
VMEM_LIMIT_BYTES = 48 * 1024 * 1024

BF16 = jnp.bfloat16
F32 = jnp.float32


def _cparams(*sem):
    return pltpu.CompilerParams(dimension_semantics=sem, vmem_limit_bytes=VMEM_LIMIT_BYTES)


def _dot(a, b):
    return jnp.dot(a, b, preferred_element_type=F32)


def _silu(x):
    return x * jax.nn.sigmoid(x)


def _norm_matmul_kernel(*refs, side):
    if side:
        x_ref, g_ref, w_ref, ws_ref, o_ref, os_ref, xn_ref = refs
    else:
        x_ref, g_ref, w_ref, o_ref, xn_ref = refs

    @pl.when(pl.program_id(1) == 0)
    def _():
        x = x_ref[...]
        ms = jnp.mean(x * x, axis=-1, keepdims=True)
        xn_ref[...] = (x * lax.rsqrt(ms + EPS) * g_ref[...]).astype(BF16)
        if side:
            os_ref[...] = _dot(xn_ref[...], ws_ref[...])

    o_ref[...] = _dot(xn_ref[...], w_ref[...]).astype(o_ref.dtype)


def _norm_matmul(x2d, g, w, tn, out_dtype=F32, tm=512, name="norm_matmul", w_side=None):
    t, d = x2d.shape
    n = w.shape[1]
    assert t % tm == 0 and n % tn == 0
    side = w_side is not None
    in_specs = [
        pl.BlockSpec((tm, d), lambda i, j: (i, 0)),
        pl.BlockSpec((1, d), lambda i, j: (0, 0)),
        pl.BlockSpec((d, tn), lambda i, j: (0, j)),
    ]
    out_specs = [pl.BlockSpec((tm, tn), lambda i, j: (i, j))]
    out_shape = [jax.ShapeDtypeStruct((t, n), out_dtype)]
    args = [x2d, g.reshape(1, d).astype(F32), w.astype(BF16)]
    if side:
        in_specs.append(pl.BlockSpec((d, LANES), lambda i, j: (0, 0)))
        out_specs.append(pl.BlockSpec((tm, LANES), lambda i, j: (i, 0)))
        out_shape.append(jax.ShapeDtypeStruct((t, LANES), F32))
        args.append(w_side.astype(BF16))
    outs = pl.pallas_call(
        functools.partial(_norm_matmul_kernel, side=side),
        grid=(t // tm, n // tn),
        in_specs=in_specs,
        out_specs=out_specs,
        out_shape=out_shape,
        scratch_shapes=[pltpu.VMEM((tm, d), BF16)],
        compiler_params=_cparams("parallel", "arbitrary"),
        name=name,
    )(*args)
    return tuple(outs) if side else outs[0]


HALO_ROWS = 16


def _fill_halo_buf(buf_ref, zc_ref, zp_ref, zn_ref, t, nt):
    tl = zc_ref.shape[1]
    hr = HALO_ROWS
    buf_ref[0:hr, :] = jnp.where(t > 0, zp_ref[0].astype(F32), 0.0)
    buf_ref[hr:hr + tl, :] = zc_ref[0].astype(F32)
    buf_ref[hr + tl:2 * hr + tl, :] = jnp.where(t < nt - 1, zn_ref[0].astype(F32), 0.0)


def _halo_specs(tl, w, l, col):
    r = tl // HALO_ROWS
    nb = l // HALO_ROWS
    cur = pl.BlockSpec((1, tl, w), lambda i, t: (i, t, col))
    prev = pl.BlockSpec((1, HALO_ROWS, w), lambda i, t: (i, jnp.maximum(t * r - 1, 0), col))
    nxt = pl.BlockSpec((1, HALO_ROWS, w), lambda i, t: (i, jnp.minimum((t + 1) * r, nb - 1), col))
    return cur, prev, nxt


def _hy_prep_kernel(zc_ref, zp_ref, zn_ref, g_ref, cw_ref, cb_ref, u_ref, e_ref, buf_ref):
    t = pl.program_id(1)
    nt = pl.num_programs(1)
    tl = zc_ref.shape[1]
    c = HY_WIDTH
    hr = HALO_ROWS
    _fill_halo_buf(buf_ref, zc_ref, zp_ref, zn_ref, t, nt)
    cw = 256
    for c0 in range(0, c, cw):
        parts = []
        for p in range(3):
            lo = p * c + c0
            acc = cb_ref[:, lo:lo + cw]
            for j in range(3):
                acc = acc + buf_ref[pl.ds(hr - 1 + j, tl), lo:lo + cw] * cw_ref[j:j + 1, lo:lo + cw]
            parts.append(acc)
        x0, x1, v = parts
        u_ref[0, :, c0:c0 + cw] = (v * x1).astype(u_ref.dtype)
        e_ref[0, :, c0:c0 + cw] = (x0 * _silu(g_ref[0, :, c0:c0 + cw].astype(F32))).astype(e_ref.dtype)


def _hy_prep(z, conv_w, conv_b, tl=512):
    b, l, _ = z.shape
    c = HY_WIDTH
    assert l % tl == 0
    cur, prev, nxt = _halo_specs(tl, 3 * c, l, E_HY // (3 * c))
    return pl.pallas_call(
        _hy_prep_kernel,
        grid=(b, l // tl),
        in_specs=[
            cur, prev, nxt,
            pl.BlockSpec((1, tl, c), lambda i, t: (i, t, E_GHY // c)),
            pl.BlockSpec((3, 3 * c), lambda i, t: (0, 0)),
            pl.BlockSpec((1, 3 * c), lambda i, t: (0, 0)),
        ],
        out_specs=[
            pl.BlockSpec((1, tl, c), lambda i, t: (i, t, 0)),
            pl.BlockSpec((1, tl, c), lambda i, t: (i, t, 0)),
        ],
        out_shape=[jax.ShapeDtypeStruct((b, l, c), F32), jax.ShapeDtypeStruct((b, l, c), F32)],
        scratch_shapes=[pltpu.VMEM((tl + 2 * HALO_ROWS, 3 * c), F32)],
        compiler_params=_cparams("parallel", "arbitrary"),
        name="hy_prep",
    )(z, z, z, z, conv_w.astype(F32), conv_b.reshape(1, 3 * c).astype(F32))


def _hy_filter_tables(l):
    t = jnp.linspace(0.0, 1.0, l, dtype=F32)[:, None]
    w = (2.0 * math.pi / l) * jnp.arange(l, dtype=F32)[:, None]
    f = jnp.linspace(1e-4, HY_BANDS - 1, HY_BANDS, dtype=F32)[None, :]
    emb = jnp.concatenate([t, jnp.cos(f * w), -jnp.sin(f * w)], axis=-1)
    emb = jnp.pad(emb, ((0, 0), (0, LANES - HY_EMB)))
    deltas = jnp.abs(jnp.linspace(math.log(HY_TARGET) / HY_SLOW_PCT, math.log(HY_TARGET) / HY_FAST_PCT, HY_WIDTH, dtype=F32))
    decay = jnp.exp(-t * jnp.tile(deltas, 2)[None, :])
    return emb, decay


def _hy_filter_kernel(emb_ref, dec_ref, w1_ref, b1_ref, w2_ref, b2_ref, w3_ref, fr_ref, sk_ref, o_ref):
    c = HY_WIDTH
    hp = lax.Precision.HIGHEST
    fr = fr_ref[...]
    hid = jnp.sin(fr * (jnp.dot(emb_ref[...], w1_ref[...], precision=hp, preferred_element_type=F32) + b1_ref[...]))
    hid = jnp.sin(fr * (jnp.dot(hid, w2_ref[...], precision=hp, preferred_element_type=F32) + b2_ref[...]))
    tl = emb_ref.shape[0]
    row = lax.broadcasted_iota(jnp.int32, (tl, 1), 0) + pl.program_id(0) * tl
    first = row == 0
    cw = 512
    for c0 in range(0, 2 * c, cw):
        filt = jnp.dot(hid, w3_ref[:, c0:c0 + cw], precision=hp, preferred_element_type=F32) * dec_ref[:, c0:c0 + cw]
        if c0 < c:
            filt = jnp.where(first, filt + sk_ref[:, c0:c0 + cw], filt)
            o_ref[0, :, c0:c0 + cw] = filt.astype(o_ref.dtype)
        else:
            filt = jnp.where(first, 0.0, filt)
            o_ref[1, :, c0 - c:c0 - c + cw] = filt.astype(o_ref.dtype)


def _hy_filters(l, w1, b1, w2, b2, w3, freq, skip, tl=256):
    c = HY_WIDTH
    hdim = HY_FILT_HIDDEN
    emb, decay = _hy_filter_tables(l)
    w1p = jnp.pad(w1.astype(F32), ((0, LANES - HY_EMB), (0, 0)))
    return pl.pallas_call(
        _hy_filter_kernel,
        grid=(l // tl,),
        in_specs=[
            pl.BlockSpec((tl, LANES), lambda t: (t, 0)),
            pl.BlockSpec((tl, 2 * c), lambda t: (t, 0)),
            pl.BlockSpec((LANES, hdim), lambda t: (0, 0)),
            pl.BlockSpec((1, hdim), lambda t: (0, 0)),
            pl.BlockSpec((hdim, hdim), lambda t: (0, 0)),
            pl.BlockSpec((1, hdim), lambda t: (0, 0)),
            pl.BlockSpec((hdim, 2 * c), lambda t: (0, 0)),
            pl.BlockSpec((1, hdim), lambda t: (0, 0)),
            pl.BlockSpec((1, c), lambda t: (0, 0)),
        ],
        out_specs=pl.BlockSpec((2, tl, c), lambda t: (0, t, 0)),
        out_shape=jax.ShapeDtypeStruct((2, l, c), BF16),
        compiler_params=_cparams("arbitrary"),
        name="hy_filter",
    )(emb, decay, w1p, b1.reshape(1, hdim).astype(F32), w2.astype(F32), b2.reshape(1, hdim).astype(F32),
      w3.astype(F32), freq.reshape(1, hdim).astype(F32), skip.reshape(1, c).astype(F32))


def _fft_consts(l):
    n = 2 * l
    n2 = FFT_N2
    n1 = n // n2
    n1h = n1 // 2
    w0 = 2.0 * math.pi / n
    k1 = jnp.arange(n1, dtype=jnp.int32)
    m1 = jnp.arange(n1h, dtype=jnp.int32)
    m2 = jnp.arange(n2, dtype=jnp.int32)
    ea = (n2 * m1[None, None, :] * k1[None, :, None] + m2[:, None, None] * k1[None, :, None]) % n
    ang = ea.astype(F32) * w0
    fa = jnp.concatenate([jnp.cos(ang), -jnp.sin(ang)], axis=1).astype(BF16)
    eb = (n1 * m2[:, None] * m2[None, :]) % n
    angb = eb.astype(F32) * w0
    fr, fi = jnp.cos(angb), -jnp.sin(angb)
    fb = jnp.concatenate([jnp.concatenate([fr, -fi], axis=1), jnp.concatenate([fi, fr], axis=1)], axis=0).astype(BF16)
    eg = (n1 * m2[None, :, None] * m2[None, None, :] + m2[None, :, None] * k1[:, None, None]) % n
    angg = eg.astype(F32) * w0
    gr, gi = jnp.cos(angg), jnp.sin(angg)
    gb = jnp.concatenate([jnp.concatenate([gr, -gi], axis=2), jnp.concatenate([gi, gr], axis=2)], axis=1).astype(BF16)
    ec = (n2 * m1[:, None] * k1[None, :]) % n
    angc = ec.astype(F32) * w0
    fc = (jnp.concatenate([jnp.cos(angc), -jnp.sin(angc)], axis=1) * (1.0 / n)).astype(BF16)
    return fa, fb, gb, fc


def _fft_a_kernel(u_ref, f_ref, o_ref):
    tn2 = f_ref.shape[0]
    n1 = f_ref.shape[1] // 2
    c = u_ref.shape[2] // tn2
    for j in range(tn2):
        a = _dot(f_ref[j], u_ref[0, :, j * c:(j + 1) * c])
        o_ref[0, 0, :, j * c:(j + 1) * c] = a[:n1].astype(o_ref.dtype)
        o_ref[0, 1, :, j * c:(j + 1) * c] = a[n1:].astype(o_ref.dtype)


def _fft_a(u, fa, tn2=8):
    b, l, c = u.shape
    n2, n1x2, n1h = fa.shape
    n1 = n1x2 // 2
    uv = u.reshape(b, n1h, n2 * c)
    return pl.pallas_call(
        _fft_a_kernel,
        grid=(b, n2 // tn2),
        in_specs=[
            pl.BlockSpec((1, n1h, tn2 * c), lambda i, j: (i, 0, j)),
            pl.BlockSpec((tn2, n1x2, n1h), lambda i, j: (j, 0, 0)),
        ],
        out_specs=pl.BlockSpec((1, 2, n1, tn2 * c), lambda i, j: (i, 0, 0, j)),
        out_shape=jax.ShapeDtypeStruct((b, 2, n1, n2 * c), BF16),
        compiler_params=_cparams("parallel", "arbitrary"),
        name="fft_a",
    )(uv, fa)


def _fft_bf_kernel(a_ref, w_ref, o_ref):
    n2 = a_ref.shape[2]
    ct = a_ref.shape[3]
    cw = 256
    for c0 in range(0, ct, cw):
        xf = _dot(w_ref[...], a_ref[0, :, :, c0:c0 + cw].reshape(2 * n2, cw))
        xb = _dot(w_ref[...], a_ref[1, :, :, c0:c0 + cw].reshape(2 * n2, cw))
        o_ref[0, :, c0:c0 + cw] = xf[:n2] + xb[:n2]
        o_ref[1, :, c0:c0 + cw] = xf[n2:] - xb[n2:]


def _fft_bf(a, fb):
    _, _, n, c = a.shape
    n2 = FFT_N2
    return pl.pallas_call(
        _fft_bf_kernel,
        grid=(n // n2,),
        in_specs=[
            pl.BlockSpec((2, 2, n2, c), lambda k: (0, 0, k, 0)),
            pl.BlockSpec((2 * n2, 2 * n2), lambda k: (0, 0)),
        ],
        out_specs=pl.BlockSpec((2, n2, c), lambda k: (0, k, 0)),
        out_shape=jax.ShapeDtypeStruct((2, n, c), F32),
        compiler_params=_cparams("arbitrary"),
        name="fft_bf",
    )(a, fb)


SUBLANES = 8
CONV_SLABS = 2


def _conv_dims(n1):
    kh = n1 // 2 + 1
    kp = -(-kh // SUBLANES) * SUBLANES
    return kh, kp, 2 * kp + SUBLANES


def _hy_conv_kernel(u_ref, h_ref, la_ref, w_ref, g_ref, fc_ref, o_ref, s_ref, *, n1, unroll_b):
    n2 = FFT_N2
    n1h = n1 // 2
    kh, kp, pitch = _conv_dims(n1)
    grp = SUBLANES
    slabs = range(CONV_SLABS)

    def lanes(x, sl):
        return x[:, sl * LANES:(sl + 1) * LANES]

    def stage_a(jj, carry):
        rows = [u_ref[0, pl.ds(pl.multiple_of(m * n2 + jj * grp, grp), grp), :] for m in range(n1h)]
        a = _dot(la_ref[jj], jnp.concatenate(rows, axis=0).astype(BF16))
        for j0 in range(grp):
            dst = pl.multiple_of((jj * grp + j0) * pitch, grp)
            for sl in slabs:
                s_ref[sl, pl.ds(dst, 2 * kp), :] = lanes(a[j0 * 2 * kp:(j0 + 1) * 2 * kp], sl)
        return carry

    lax.fori_loop(0, n2 // grp, stage_a, 0)

    def stage_b(kk, carry):
        k1s = [jnp.minimum(kk * unroll_b + i, kh - 1) for i in range(unroll_b)]
        ar = [jnp.concatenate([s_ref[sl, pl.ds(k, n2, stride=pitch), :] for sl in slabs], axis=1) for k in k1s]
        ai = [jnp.concatenate([s_ref[sl, pl.ds(kp + k, n2, stride=pitch), :] for sl in slabs], axis=1) for k in k1s]
        x = [_dot(w_ref[...], jnp.concatenate([ar[i], ai[i]], axis=0).astype(BF16)) for i in range(unroll_b)]
        y = []
        for i, k in enumerate(k1s):
            off = pl.multiple_of(k * n2, n2)
            hr, hi = h_ref[0, pl.ds(off, n2), :], h_ref[1, pl.ds(off, n2), :]
            xr, xi = x[i][:n2], x[i][n2:]
            y.append(jnp.concatenate([xr * hr - xi * hi, xr * hi + xi * hr], axis=0).astype(BF16))
        z = [_dot(g_ref[k1s[i]], y[i]) for i in range(unroll_b)]
        for i, k in enumerate(k1s):
            for sl in slabs:
                s_ref[sl, pl.ds(k, n2, stride=pitch), :] = lanes(z[i][:n2], sl)
                s_ref[sl, pl.ds(kp + k, n2, stride=pitch), :] = lanes(z[i][n2:], sl)
        return carry

    lax.fori_loop(0, -(-kh // unroll_b), stage_b, 0)

    def stage_c(jj, carry):
        zz = [jnp.concatenate([s_ref[sl, pl.ds(pl.multiple_of((jj * grp + j0) * pitch, grp), 2 * kp), :]
                               for sl in slabs], axis=1) for j0 in range(grp)]
        y = _dot(fc_ref[...], jnp.concatenate(zz, axis=0).astype(BF16))
        for m in range(n1h):
            o_ref[0, pl.ds(pl.multiple_of(m * n2 + jj * grp, grp), grp), :] = y[m * grp:(m + 1) * grp]
        return carry

    lax.fori_loop(0, n2 // grp, stage_c, 0)


def _hy_conv(u, h, fa, fb, gb, fc):
    b, l, c = u.shape
    n2, n1x2, n1h = fa.shape
    n1 = n1x2 // 2
    kh, kp, pitch = _conv_dims(n1)
    grp = SUBLANES
    ct = CONV_SLABS * LANES

    def half(t, axis):
        re, im = jnp.split(t, 2, axis=axis)
        pad = [(0, 0)] * t.ndim
        pad[axis] = (0, kp - kh)
        cut = lambda x: jnp.pad(lax.slice_in_dim(x, 0, kh, axis=axis), pad)
        return jnp.concatenate([cut(re), cut(im)], axis=axis)

    fa_h = half(fa, 1)
    la = (fa_h.reshape(n2 // grp, grp, 2 * kp, n1h, 1) * jnp.eye(grp, dtype=fa.dtype).reshape(1, grp, 1, 1, grp))
    la = la.reshape(n2 // grp, grp * 2 * kp, n1h * grp)
    wgt = jnp.ones((kh,), F32).at[1:kh - 1].set(2.0)
    wgt = jnp.pad(wgt, (0, kp - kh))
    fc_h = (half(fc, 1).astype(F32) * jnp.tile(wgt, 2)[None, :]).astype(BF16)
    lc = fc_h.reshape(n1h, 1, 1, 2 * kp) * jnp.eye(grp, dtype=fc_h.dtype).reshape(1, grp, grp, 1)
    fc_h = lc.reshape(n1h * grp, grp * 2 * kp)
    gb_h = gb[:kh]
    unroll_b = 3
    once = pl.Buffered(1)
    return pl.pallas_call(
        functools.partial(_hy_conv_kernel, n1=n1, unroll_b=unroll_b),
        grid=(c // ct, b),
        in_specs=[
            pl.BlockSpec((1, l, ct), lambda j, i: (i, 0, j)),
            pl.BlockSpec((2, kh * n2, ct), lambda j, i: (0, 0, j), pipeline_mode=once),
            pl.BlockSpec(la.shape, lambda j, i: (0, 0, 0), pipeline_mode=once),
            pl.BlockSpec(fb.shape, lambda j, i: (0, 0), pipeline_mode=once),
            pl.BlockSpec(gb_h.shape, lambda j, i: (0, 0, 0), pipeline_mode=once),
            pl.BlockSpec(fc_h.shape, lambda j, i: (0, 0), pipeline_mode=once),
        ],
        out_specs=pl.BlockSpec((1, l, ct), lambda j, i: (i, 0, j)),
        out_shape=jax.ShapeDtypeStruct((b, l, c), F32),
        scratch_shapes=[pltpu.VMEM((CONV_SLABS, n2 * pitch, LANES), F32)],
        compiler_params=_cparams("arbitrary", "arbitrary"),
        name="hy_conv",
    )(u, h, la, fb, gb_h, fc_h)


def _hyena_branch(z, l, conv_w, conv_b, w1, b1, w2, b2, w3, freq, skip):
    c = HY_WIDTH
    n = 2 * l
    fa, fb, gb, fc = _fft_consts(l)
    filt = _hy_filters(l, w1, b1, w2, b2, w3, freq, skip)
    fa_spec = _fft_a(filt, fa).reshape(2, 2, n, c)
    h = _fft_bf(fa_spec, fb)
    u, e = _hy_prep(z, conv_w, conv_b)
    return _hy_conv(u, h, fa, fb, gb, fc), e


def _gdn_prep_kernel(zc_ref, zp_ref, zn_ref, cw_ref, q_ref, k_ref, v_ref, buf_ref):
    t = pl.program_id(1)
    nt = pl.num_programs(1)
    tl = zc_ref.shape[1]
    dh = GDN_HEAD_DIM
    hr = HALO_ROWS
    _fill_halo_buf(buf_ref, zc_ref, zp_ref, zn_ref, t, nt)
    outs = (q_ref, k_ref, v_ref)
    for p in range(3):
        for h in range(GDN_HEADS):
            lo = p * GDN_WIDTH + h * dh
            acc = buf_ref[pl.ds(hr - 2, tl), lo:lo + dh] * cw_ref[0:1, lo:lo + dh]
            for j in range(1, 5):
                acc = acc + buf_ref[pl.ds(hr - 2 + j, tl), lo:lo + dh] * cw_ref[j:j + 1, lo:lo + dh]
            a = _silu(acc)
            if p < 2:
                a = a * lax.rsqrt(jnp.sum(a * a, axis=-1, keepdims=True) + EPS)
            if p == 0:
                a = a * (dh ** -0.5)
            outs[p][0, :, h * dh:(h + 1) * dh] = a.astype(BF16)


def _gdn_prep(z, conv_w, tl=512):
    b, l, _ = z.shape
    w = 3 * GDN_WIDTH
    cur, prev, nxt = _halo_specs(tl, w, l, E_QKV // w)
    o_spec = pl.BlockSpec((1, tl, GDN_WIDTH), lambda i, t: (i, t, 0))
    o_shape = jax.ShapeDtypeStruct((b, l, GDN_WIDTH), BF16)
    return pl.pallas_call(
        _gdn_prep_kernel,
        grid=(b, l // tl),
        in_specs=[cur, prev, nxt, pl.BlockSpec((5, w), lambda i, t: (0, 0))],
        out_specs=[o_spec, o_spec, o_spec],
        out_shape=[o_shape, o_shape, o_shape],
        scratch_shapes=[pltpu.VMEM((tl + 2 * HALO_ROWS, w), F32)],
        compiler_params=_cparams("parallel", "arbitrary"),
        name="gdn_prep",
    )(z, z, z, conv_w.astype(F32))


def _dot_nt(a, b):
    return lax.dot_general(a, b, (((1,), (1,)), ((), ())), preferred_element_type=F32)


def _dot_tn(a, b):
    return lax.dot_general(a, b, (((0,), (0,)), ((), ())), preferred_element_type=F32)


def _split3(x):
    a = x.astype(BF16)
    r = x - a.astype(F32)
    b = r.astype(BF16)
    c = (r - b.astype(F32)).astype(BF16)
    return a, b, c


GDN_CHUNKS_PER_STEP = 4


def _gdn_scan_kernel(*refs, direction, final):
    if final:
        q_ref, k_ref, v_ref, zs_ref, al_ref, dt_ref, of_ref, gg_ref, ng_ref, o_ref, s_ref = refs
    else:
        q_ref, k_ref, v_ref, zs_ref, al_ref, dt_ref, o_ref, s_ref = refs
    c = GDN_CHUNK
    dh = GDN_HEAD_DIM
    nsub = q_ref.shape[1] // c

    @pl.when(pl.program_id(1) == 0)
    def _():
        s_ref[...] = jnp.zeros_like(s_ref)

    row = lax.broadcasted_iota(jnp.int32, (c, c), 0)
    col = lax.broadcasted_iota(jnp.int32, (c, c), 1)
    if direction == 0:
        incl, strict, last = col <= row, col < row, c - 1
    else:
        incl, strict, last = col >= row, col > row, 0
    tri = incl.astype(BF16)
    eye = (row == col).astype(F32)
    heads = range(GDN_HEADS)
    hsl = [slice(h * dh, (h + 1) * dh) for h in heads]
    lns = [2 * GDN_HEADS + direction * GDN_HEADS + h for h in heads]

    rows = [slice(i * c, (i + 1) * c) for i in range(nsub)]
    gcum, gcum_t, beta_all = [], [], []
    for rs in rows:
        zs = zs_ref[0, rs, :]
        beta_all.append(jax.nn.sigmoid(zs))
        gl = -jnp.exp(al_ref[...]) * jax.nn.softplus(zs + dt_ref[...])
        g1, g2, g3 = _split3(gl)
        gsum = _dot(tri, g1) + _dot(tri, g2) + _dot(tri, g3)
        gcum.append(gsum)
        gcum_t.append(jnp.concatenate([gsum, jnp.zeros_like(gsum)], axis=0).T)
    items = [(i, h) for i in range(nsub) for h in heads]
    it = range(len(items))
    beta = [beta_all[i][:, direction * GDN_HEADS + h:direction * GDN_HEADS + h + 1] for i, h in items]
    gc = [gcum[i][:, lns[h]:lns[h] + 1] for i, h in items]
    gct = [gcum_t[i][lns[h]:lns[h] + 1, 0:c] for i, h in items]
    glast = [gcum_t[i][lns[h]:lns[h] + 1, last:last + 1] for i, h in items]
    qh = [q_ref[0, rows[i], hsl[h]] for i, h in items]
    kh = [k_ref[0, rows[i], hsl[h]] for i, h in items]
    vh = [v_ref[0, rows[i], hsl[h]] for i, h in items]
    kq = [_dot_nt(jnp.concatenate([kh[x], qh[x]], axis=0), kh[x]) for x in it]
    dmat = [jnp.exp(jnp.where(incl, gc[x] - gct[x], -jnp.inf)) for x in it]
    a = [jnp.where(strict, (beta[x] * kq[x][:c]) * dmat[x], 0.0) for x in it]
    tinv = [eye - a[x] for x in it]
    ab = [a[x].astype(BF16) for x in it]
    p = [_dot(ab[x], ab[x]) for x in it]
    for j in range(5):
        pb = [p[x].astype(BF16) for x in it]
        if j < 4:
            tp = [_dot(jnp.concatenate([tinv[x].astype(BF16), pb[x]], axis=0), pb[x]) for x in it]
            tinv = [tinv[x] + tp[x][:c] for x in it]
            p = [tp[x][c:] for x in it]
        else:
            tinv = [tinv[x] + _dot(tinv[x].astype(BF16), pb[x]) for x in it]
    eg = [jnp.exp(gc[x]) for x in it]
    kf = [kh[x].astype(F32) for x in it]
    rhs = [jnp.concatenate([vh[x].astype(F32) * beta[x], kf[x] * (beta[x] * eg[x])], axis=1).astype(BF16) for x in it]
    uw = [_dot(tinv[x].astype(BF16), rhs[x]) for x in it]
    wq = [jnp.concatenate([uw[x][:, dh:], qh[x].astype(F32) * eg[x]], axis=0).astype(BF16) for x in it]
    qkd = [(kq[x][c:] * dmat[x]).astype(BF16) for x in it]
    kd = [(kf[x] * jnp.exp(glast[x] - gc[x])).astype(BF16) for x in it]

    order = list(range(nsub)) if direction == 0 else list(range(nsub - 1, -1, -1))
    state = [s_ref[h] for h in heads]
    for i in order:
        rs = rows[i]
        xs = [i * GDN_HEADS + h for h in heads]
        ws = [_dot(wq[xs[h]], state[h].astype(BF16)) for h in heads]
        vnb = [(uw[xs[h]][:, :dh] - ws[h][:c]).astype(BF16) for h in heads]
        o = [ws[h][c:] + _dot(qkd[xs[h]], vnb[h]) for h in heads]
        state = [state[h] * jnp.exp(glast[xs[h]]) + _dot_tn(kd[xs[h]], vnb[h]) for h in heads]
        for h in heads:
            if final:
                tot = of_ref[0, rs, hsl[h]] + o[h]
                y = tot * lax.rsqrt(jnp.mean(tot * tot, axis=-1, keepdims=True) + EPS) * ng_ref[...]
                o_ref[0, rs, hsl[h]] = (y * _silu(gg_ref[0, rs, hsl[h]].astype(F32))).astype(o_ref.dtype)
            else:
                o_ref[0, rs, hsl[h]] = o[h].astype(o_ref.dtype)
    for h in heads:
        s_ref[h] = state[h]


def _gdn_scan(q, k, v, zs, a_log, dt_bias, direction, o_fwd=None, z=None, gate_blk=None, norm_g=None):
    b, l, w = q.shape
    c = GDN_CHUNK * GDN_CHUNKS_PER_STEP
    n = l // c
    final = o_fwd is not None
    if direction == 0:
        cmap = lambda i, t: (i, t, 0)
    else:
        cmap = lambda i, t: (i, n - 1 - t, 0)
    blk = pl.BlockSpec((1, c, w), cmap)
    vec = pl.BlockSpec((1, LANES), lambda i, t: (0, 0))
    pad = lambda x: jnp.pad(x.reshape(1, -1).astype(F32), ((0, 0), (2 * GDN_HEADS, LANES - 4 * GDN_HEADS)))
    in_specs = [blk, blk, blk, pl.BlockSpec((1, c, LANES), cmap), vec, vec]
    args = [q, k, v, zs, pad(a_log), pad(dt_bias)]
    if final:
        if direction == 0:
            gmap = lambda i, t: (i, t, gate_blk)
        else:
            gmap = lambda i, t: (i, n - 1 - t, gate_blk)
        in_specs += [blk, pl.BlockSpec((1, c, w), gmap), pl.BlockSpec((1, GDN_HEAD_DIM), lambda i, t: (0, 0))]
        args += [o_fwd, z, norm_g.reshape(1, GDN_HEAD_DIM).astype(F32)]
    return pl.pallas_call(
        functools.partial(_gdn_scan_kernel, direction=direction, final=final),
        grid=(b, n),
        in_specs=in_specs,
        out_specs=blk,
        out_shape=jax.ShapeDtypeStruct((b, l, w), BF16 if final else F32),
        scratch_shapes=[pltpu.VMEM((GDN_HEADS, GDN_HEAD_DIM, GDN_HEAD_DIM), F32)],
        compiler_params=_cparams("parallel", "arbitrary"),
        name="gdn_scan_bwd" if direction else "gdn_scan_fwd",
    )(*args)


def _gdn_branch(z, zs, conv_w, a_log, dt_bias, norm_g, gate_blk):
    q, k, v = _gdn_prep(z, conv_w)
    o_f = _gdn_scan(q, k, v, zs, a_log, dt_bias, 0)
    return _gdn_scan(q, k, v, zs, a_log, dt_bias, 1, o_fwd=o_f, z=z, gate_blk=gate_blk, norm_g=norm_g)


def _softmax_t(s, extra=None):
    m = jnp.max(s, axis=0, keepdims=True)
    if extra is not None:
        m = jnp.maximum(m, extra)
    p = jnp.exp2(s - m)
    den = jnp.sum(p, axis=0, keepdims=True)
    if extra is not None:
        den = den + jnp.exp2(extra - m)
    return p.astype(BF16), den, m


def _t_bf16(x):
    return x.astype(F32).T.astype(BF16)


def _xattn_kernel(q_ref, g_ref, kv_ref, o_ref):
    dh = X_HEAD_DIM
    for h in range(X_HEADS):
        hs = slice(h * dh, (h + 1) * dh)
        qh = q_ref[0, :, hs].astype(BF16)
        kh = kv_ref[0, :, hs]
        vh = kv_ref[0, :, X_WIDTH + h * dh:X_WIDTH + (h + 1) * dh]
        s = _dot_nt(qh, kh) * (dh ** -0.5)
        p = jnp.exp(s - jnp.max(s, axis=-1, keepdims=True))
        den = jnp.sum(p, axis=-1, keepdims=True)
        o = _dot(p.astype(BF16), vh) / den
        o_ref[0, :, hs] = (o * _silu(g_ref[0, :, hs].astype(F32))).astype(o_ref.dtype)


def _xattn(z, kv, q_blk, g_blk, tq=512):
    b, l, _ = z.shape
    m = kv.shape[1]
    w = X_WIDTH
    return pl.pallas_call(
        _xattn_kernel,
        grid=(b, l // tq),
        in_specs=[
            pl.BlockSpec((1, tq, w), lambda i, t: (i, t, q_blk)),
            pl.BlockSpec((1, tq, w), lambda i, t: (i, t, g_blk)),
            pl.BlockSpec((1, m, 2 * w), lambda i, t: (i, 0, 0)),
        ],
        out_specs=pl.BlockSpec((1, tq, w), lambda i, t: (i, t, 0)),
        out_shape=jax.ShapeDtypeStruct((b, l, w), BF16),
        compiler_params=_cparams("parallel", "arbitrary"),
        name="xattn",
    )(z, z, kv)


def _out_proj_kernel(*refs, widths, gated):
    nparts = len(widths)
    y_refs = refs[:nparts]
    if gated:
        e_ref, w_ref, g_ref, x_ref, o_ref = refs[nparts:]
    else:
        w_ref, g_ref, x_ref, o_ref = refs[nparts:]
    d = o_ref.shape[1]
    cw = 256
    ys = [y_ref[...] for y_ref in y_refs]
    if gated:
        ys[0] = ys[0] * e_ref[...]
    ys = [y.astype(BF16) for y in ys]
    ssq = jnp.zeros((o_ref.shape[0], 1), F32)
    for c0 in range(0, d, cw):
        acc = None
        off = 0
        for y, wd in zip(ys, widths):
            part = _dot(y, w_ref[off:off + wd, c0:c0 + cw])
            acc = part if acc is None else acc + part
            off += wd
        ssq = ssq + jnp.sum(acc * acc, axis=-1, keepdims=True)
        o_ref[:, c0:c0 + cw] = acc
    r = lax.rsqrt(ssq * (1.0 / d) + EPS)
    for c0 in range(0, d, cw):
        o_ref[:, c0:c0 + cw] = x_ref[:, c0:c0 + cw] + o_ref[:, c0:c0 + cw] * r * g_ref[:, c0:c0 + cw]


def _out_proj(parts, w_out, post_g, x2d, gate=None, tm=512):
    t, d = x2d.shape
    widths = tuple(int(p.shape[1]) for p in parts)
    kdim = sum(widths)
    gates = [] if gate is None else [gate]
    return pl.pallas_call(
        functools.partial(_out_proj_kernel, widths=widths, gated=gate is not None),
        grid=(t // tm,),
        in_specs=[pl.BlockSpec((tm, wd), lambda i: (i, 0)) for wd in widths]
        + [pl.BlockSpec((tm, widths[0]), lambda i: (i, 0)) for _ in gates] + [
            pl.BlockSpec((kdim, d), lambda i: (0, 0)),
            pl.BlockSpec((1, d), lambda i: (0, 0)),
            pl.BlockSpec((tm, d), lambda i: (i, 0)),
        ],
        out_specs=pl.BlockSpec((tm, d), lambda i: (i, 0)),
        out_shape=jax.ShapeDtypeStruct((t, d), F32),
        compiler_params=_cparams("parallel"),
        name="out_proj",
    )(*parts, *gates, w_out.astype(BF16), post_g.reshape(1, d).astype(F32), x2d)


def _mem_kv(mem, mem_g, w_mem_kv):
    b, m, d = mem.shape
    kv = _norm_matmul(mem.reshape(b * m, d), mem_g, w_mem_kv, 1024, out_dtype=BF16, name="mem_kv")
    return kv.reshape(b, m, 2 * X_WIDTH)


E_QKV, E_HY, E_GHY, E_GGDN, E_XQ, E_GX, E_BIG = 0, 3072, 6144, 7168, 8192, 8704, 9216


def _even_layer(x, mem, pre_g, post_g, w_in, w_out, hy_conv_w, hy_conv_b, hy_w1, hy_b1, hy_w2, hy_b2, hy_w3, hy_freq,
                hy_skip, gdn_conv_w, gdn_a_log, gdn_dt_bias, gdn_norm_g, mem_g, w_mem_kv):
    b, l, d = x.shape
    x2d = x.reshape(b * l, d)
    w_big = jnp.concatenate([w_in[:, 4096:7168], w_in[:, 0:3072], w_in[:, 3072:4096], w_in[:, 7168:8192],
                             w_in[:, 8224:8736], w_in[:, 8736:9248]], axis=1)
    w_small = jnp.pad(w_in[:, 8192:8224], ((0, 0), (0, LANES - 4 * GDN_HEADS)))
    z, zs = _norm_matmul(x2d, pre_g, w_big, 1024, out_dtype=BF16, tm=1024, name="even_in", w_side=w_small)
    z = z.reshape(b, l, E_BIG)
    zs = zs.reshape(b, l, LANES)
    conv, e = _hyena_branch(z, l, hy_conv_w, hy_conv_b, hy_w1, hy_b1, hy_w2, hy_b2, hy_w3, hy_freq, hy_skip)
    y_b = _gdn_branch(z, zs, gdn_conv_w, gdn_a_log, gdn_dt_bias, gdn_norm_g, E_GGDN // GDN_WIDTH)
    kv = _mem_kv(mem, mem_g, w_mem_kv)
    y_x = _xattn(z, kv, E_XQ // X_WIDTH, E_GX // X_WIDTH)
    t = b * l
    out = _out_proj([conv.reshape(t, -1), y_b.reshape(t, -1), y_x.reshape(t, -1)], w_out, post_g, x2d,
                    gate=e.reshape(t, -1))
    return out.reshape(b, l, d)


DIL_PATTERNS = ((128, 1), (512, 4), (2048, 16))
N_DIL = len(DIL_PATTERNS)
DIL_HEADS = 4
DIL_HEAD_DIM = 128
DIL_WIDTH = DIL_HEADS * DIL_HEAD_DIM
SWA_Q_HEADS = 16
SWA_KV_HEADS = 2
SWA_HEAD_DIM = 64
SWA_WIDTH = SWA_Q_HEADS * SWA_HEAD_DIM
SWA_HALF_WINDOW = 128
O_CQKV, O_GC, O_DQ, O_GD, O_XQ, O_GX, O_DKV, O_ALL = 0, 4608, 5120, 6144, 7168, 7680, 8192, 8448


def _rope_tables(l, dh):
    half = dh // 8
    inv = ROPE_THETA ** (-jnp.arange(half, dtype=F32) / half)
    ang = jnp.arange(l, dtype=F32)[:, None] * inv[None, :]
    cos, sin = jnp.cos(ang), jnp.sin(ang)
    one = jnp.ones((l, dh - 2 * half), F32)
    zero_h = jnp.zeros((l, half), F32)
    zero_r = jnp.zeros((l, dh - 2 * half), F32)
    c = jnp.concatenate([cos, cos, one], axis=1)
    sa = jnp.concatenate([-sin, zero_h, zero_r], axis=1)
    sb = jnp.concatenate([zero_h, sin, zero_r], axis=1)
    rep = LANES // dh
    return tuple(jnp.tile(t, (1, rep)) for t in (c, sa, sb))


def _odd_prep_kernel(c_ref, dq_ref, dkv_ref, c1_ref, a1_ref, b1_ref, c2_ref, a2_ref, b2_ref, *refs):
    dil_refs = refs[:3 * N_DIL]
    dqe_ref, dk_ref, dv_ref, xs_ref = refs[3 * N_DIL:]
    tl = c_ref.shape[1]
    lane = lax.broadcasted_iota(jnp.int32, (tl, LANES), 1)
    low = lane < SWA_HEAD_DIM
    c1, a1, b1 = c1_ref[...], a1_ref[...], b1_ref[...]
    c2, a2, b2 = c2_ref[...], a2_ref[...], b2_ref[...]
    h1 = DIL_HEAD_DIM // 8
    h2 = SWA_HEAD_DIM // 8

    def rope1(x):
        return x * c1 + pltpu.roll(x, LANES - h1, 1) * a1 + pltpu.roll(x, h1, 1) * b1

    def rope2(x):
        return x * c2 + pltpu.roll(x, LANES - h2, 1) * a2 + pltpu.roll(x, h2, 1) * b2

    nblk = N_DIL * DIL_HEADS
    slot = 0
    for s in range(3):
        for gi, (_, d) in enumerate(DIL_PATTERNS):
            dst = dil_refs[3 * gi + s]
            for h in range(DIL_HEADS):
                j = s * nblk + gi * DIL_HEADS + h
                x = c_ref[0, :, j * LANES:(j + 1) * LANES]
                if s < 2:
                    x = rope1(x.astype(F32))
                hs = slice(h * LANES, (h + 1) * LANES)
                if d == 1:
                    dst[0, 0, :, hs] = x.astype(BF16)
                else:
                    xs_ref[slot] = x.astype(F32)
                    for r in range(d):
                        dst[0, r, :, hs] = xs_ref[slot, pl.ds(r, tl // d, stride=d), :].astype(BF16)
                    slot += 1
    for j in range(SWA_Q_HEADS // 2):
        xr = rope2(dq_ref[0, :, j * LANES:(j + 1) * LANES].astype(F32)) * (SWA_HEAD_DIM ** -0.5 * LOG2E)
        dqe_ref[0, :, (2 * j) * LANES:(2 * j + 1) * LANES] = jnp.where(low, xr, 0.0).astype(BF16)
        dqe_ref[0, :, (2 * j + 1) * LANES:(2 * j + 2) * LANES] = jnp.where(low, pltpu.roll(xr, SWA_HEAD_DIM, 1), 0.0).astype(BF16)
    kr = rope2(dkv_ref[0, :, 0:LANES].astype(F32))
    vv = dkv_ref[0, :, LANES:2 * LANES].astype(F32)
    kr_sw = pltpu.roll(kr, SWA_HEAD_DIM, 1)
    vv_sw = pltpu.roll(vv, SWA_HEAD_DIM, 1)
    dk_ref[0, :, 0:LANES] = jnp.where(low, kr, 0.0).astype(BF16)
    dk_ref[0, :, LANES:2 * LANES] = jnp.where(low, kr_sw, 0.0).astype(BF16)
    dv_ref[0, :, 0:LANES] = jnp.where(low, vv, vv_sw).astype(BF16)
    dv_ref[0, :, LANES:2 * LANES] = jnp.where(low, vv_sw, vv).astype(BF16)


def _odd_prep(z, tl=256):
    b, l, _ = z.shape
    wc = 3 * N_DIL * DIL_WIDTH
    t1 = _rope_tables(l, DIL_HEAD_DIM)
    t2 = _rope_tables(l, SWA_HEAD_DIM)
    tab = pl.BlockSpec((tl, LANES), lambda i, t: (t, 0))

    def spec(w):
        return pl.BlockSpec((1, tl, w), lambda i, t: (i, t, 0))

    def shape(w):
        return jax.ShapeDtypeStruct((b, l, w), BF16)

    dil_specs, dil_shapes = [], []
    for _, d in DIL_PATTERNS:
        for _ in range(3):
            dil_specs.append(pl.BlockSpec((1, d, tl // d, DIL_WIDTH), lambda i, t: (i, 0, t, 0)))
            dil_shapes.append(jax.ShapeDtypeStruct((b, d, l // d, DIL_WIDTH), BF16))
    n_staged = 3 * DIL_HEADS * sum(1 for _, d in DIL_PATTERNS if d > 1)
    outs = pl.pallas_call(
        _odd_prep_kernel,
        grid=(b, l // tl),
        in_specs=[
            pl.BlockSpec((1, tl, wc), lambda i, t: (i, t, 0)),
            pl.BlockSpec((1, tl, SWA_WIDTH), lambda i, t: (i, t, O_DQ // SWA_WIDTH)),
            pl.BlockSpec((1, tl, 2 * LANES), lambda i, t: (i, t, O_DKV // (2 * LANES))),
            tab, tab, tab, tab, tab, tab,
        ],
        out_specs=dil_specs + [spec(2 * SWA_WIDTH), spec(2 * LANES), spec(2 * LANES)],
        out_shape=dil_shapes + [shape(2 * SWA_WIDTH), shape(2 * LANES), shape(2 * LANES)],
        scratch_shapes=[pltpu.VMEM((n_staged, tl, LANES), F32)],
        compiler_params=_cparams("parallel", "arbitrary"),
        name="odd_prep",
    )(z, z, z, *t1, *t2)
    dil = [tuple(outs[3 * gi:3 * gi + 3]) for gi in range(N_DIL)]
    return dil, outs[3 * N_DIL], outs[3 * N_DIL + 1], outs[3 * N_DIL + 2]


def _band_bias_t(tq, half, ls):
    p0 = pl.program_id(1) * tq
    j = lax.broadcasted_iota(jnp.int32, (tq + 2 * half, tq), 0)
    i = lax.broadcasted_iota(jnp.int32, (tq + 2 * half, tq), 1)
    kpos = p0 - half + j
    valid = (j >= i) & (j - i <= 2 * half) & (kpos >= 0) & (kpos < ls)
    return jnp.where(valid, 0.0, -jnp.inf)


def _band_mask(tq, half, ls):
    p0 = pl.program_id(1) * tq
    i = lax.broadcasted_iota(jnp.int32, (tq, tq + 2 * half), 0)
    j = lax.broadcasted_iota(jnp.int32, (tq, tq + 2 * half), 1)
    kpos = p0 - half + j
    return (j >= i) & (j - i <= 2 * half) & (kpos >= 0) & (kpos < ls)


def _band_attn_kernel(q_ref, kp_ref, kc_ref, kn_ref, vp_ref, vc_ref, vn_ref, o_ref, lse_ref, *, half, ls):
    tq = q_ref.shape[1]
    dh = DIL_HEAD_DIM
    valid = _band_mask(tq, half, ls)
    lane = lax.broadcasted_iota(jnp.int32, (tq, LANES), 1)
    lse_all = jnp.zeros((tq, LANES), F32)
    heads = range(DIL_HEADS)
    hsl = [slice(h * dh, (h + 1) * dh) for h in heads]
    kall = [jnp.concatenate([kp_ref[0, :, s], kc_ref[0, :, s], kn_ref[0, :, s]], axis=0) for s in hsl]
    vall = [jnp.concatenate([vp_ref[0, :, s], vc_ref[0, :, s], vn_ref[0, :, s]], axis=0) for s in hsl]
    sc = [jnp.where(valid, _dot_nt(q_ref[0, :, hsl[h]], kall[h]) * (dh ** -0.5), -jnp.inf) for h in heads]
    m = [jnp.max(sc[h], axis=-1, keepdims=True) for h in heads]
    p = [jnp.exp(sc[h] - m[h]) for h in heads]
    den = [jnp.sum(p[h], axis=-1, keepdims=True) for h in heads]
    o = [_dot(p[h].astype(BF16), vall[h]) / den[h] for h in heads]
    for h in heads:
        o_ref[0, :, hsl[h]] = o[h].astype(o_ref.dtype)
        lse_all = jnp.where(lane == h, m[h] + jnp.log(den[h]), lse_all)
    lse_ref[0] = lse_all


def _band_specs(tq, half, ls, w, col):
    r = tq // half
    nb = ls // half
    cur = pl.BlockSpec((1, tq, w), lambda i, t: (i, t, col))
    prev = pl.BlockSpec((1, half, w), lambda i, t: (i, jnp.maximum(t * r - 1, 0), col))
    nxt = pl.BlockSpec((1, half, w), lambda i, t: (i, jnp.minimum((t + 1) * r, nb - 1), col))
    return cur, prev, nxt


def _band_attn(q, k, v, half):
    n, ls, w = q.shape
    tq = min(2 * LANES, ls)
    cur, prev, nxt = _band_specs(tq, half, ls, w, 0)
    return pl.pallas_call(
        functools.partial(_band_attn_kernel, half=half, ls=ls),
        grid=(n, ls // tq),
        in_specs=[cur, prev, cur, nxt, prev, cur, nxt],
        out_specs=[cur, pl.BlockSpec((1, tq, LANES), lambda i, t: (i, t, 0))],
        out_shape=[jax.ShapeDtypeStruct((n, ls, w), BF16), jax.ShapeDtypeStruct((n, ls, LANES), F32)],
        compiler_params=_cparams("parallel", "arbitrary"),
        name="band_attn",
    )(q, k, k, k, v, v, v)


def _dil_merge_kernel(o0_ref, o1_ref, o2_ref, l0_ref, l1_ref, l2_ref, g_ref, y_ref, os_ref, ls_ref):
    dh = DIL_HEAD_DIM
    tl = y_ref.shape[1]
    o_refs = (o0_ref, o1_ref, o2_ref)
    l_refs = (l0_ref, l1_ref, l2_ref)
    slot = 0
    lse, outs = [], []
    for gi, (_, d) in enumerate(DIL_PATTERNS):
        if d == 1:
            lse.append(l_refs[gi][0, 0])
            outs.append([o_refs[gi][0, 0, :, h * dh:(h + 1) * dh].astype(F32) for h in range(DIL_HEADS)])
            continue
        for r in range(d):
            ls_ref[gi, pl.ds(r, tl // d, stride=d), :] = l_refs[gi][0, r]
        lse.append(ls_ref[gi])
        per_head = []
        for h in range(DIL_HEADS):
            for r in range(d):
                os_ref[slot, pl.ds(r, tl // d, stride=d), :] = o_refs[gi][0, r, :, h * dh:(h + 1) * dh].astype(F32)
            per_head.append(os_ref[slot])
            slot += 1
        outs.append(per_head)
    for h in range(DIL_HEADS):
        hs = slice(h * dh, (h + 1) * dh)
        ls = [x[:, h:h + 1] for x in lse]
        m = jnp.maximum(jnp.maximum(ls[0], ls[1]), ls[2])
        ws = [jnp.exp(x - m) for x in ls]
        den = ws[0] + ws[1] + ws[2]
        y = (ws[0] / den) * outs[0][h] + (ws[1] / den) * outs[1][h] + (ws[2] / den) * outs[2][h]
        y_ref[0, :, hs] = (y * _silu(g_ref[0, :, hs].astype(F32))).astype(y_ref.dtype)


def _dil_merge(outs, lses, z, tl=512):
    b, _, l, w = outs[0].shape
    o_specs = [pl.BlockSpec((1, d, tl // d, w), lambda i, t: (i, 0, t, 0)) for _, d in DIL_PATTERNS]
    l_specs = [pl.BlockSpec((1, d, tl // d, LANES), lambda i, t: (i, 0, t, 0)) for _, d in DIL_PATTERNS]
    n_staged = DIL_HEADS * sum(1 for _, d in DIL_PATTERNS if d > 1)
    return pl.pallas_call(
        _dil_merge_kernel,
        grid=(b, l // tl),
        in_specs=o_specs + l_specs + [pl.BlockSpec((1, tl, w), lambda i, t: (i, t, O_GC // DIL_WIDTH))],
        out_specs=pl.BlockSpec((1, tl, w), lambda i, t: (i, t, 0)),
        out_shape=jax.ShapeDtypeStruct((b, l, w), BF16),
        scratch_shapes=[pltpu.VMEM((n_staged, tl, LANES), F32), pltpu.VMEM((N_DIL, tl, LANES), F32)],
        compiler_params=_cparams("parallel", "arbitrary"),
        name="dil_merge",
    )(*outs, *lses, z)


def _dilated_branch(dil, z):
    outs, lses = [], []
    for (window, d), (q, k, v) in zip(DIL_PATTERNS, dil):
        b, _, ls, w = q.shape
        half = window // (2 * d)
        o, lse = _band_attn(q.reshape(b * d, ls, w), k.reshape(b * d, ls, w), v.reshape(b * d, ls, w), half)
        outs.append(o.reshape(b, d, ls, w))
        lses.append(lse.reshape(b, d, ls, LANES))
    return _dil_merge(outs, lses, z)


def _swa_kernel(q_ref, kp_ref, kc_ref, kn_ref, vp_ref, vc_ref, vn_ref, g_ref, sink_ref, o_ref, *, half, ls):
    tq = q_ref.shape[1]
    dh = SWA_HEAD_DIM
    bias1 = _band_bias_t(tq, half, ls)
    bias = jnp.concatenate([bias1, bias1], axis=1)
    b_prev, b_next = bias[:half], bias[half + tq:]
    low = lax.broadcasted_iota(jnp.int32, (LANES, tq), 0) < dh
    first = lax.broadcasted_iota(jnp.int32, (1, 2 * tq), 1) < tq
    grp = SWA_Q_HEADS // SWA_KV_HEADS
    for g in range(SWA_KV_HEADS):
        gs = slice(g * LANES, (g + 1) * LANES)
        kall = jnp.concatenate([kp_ref[0, :, gs], kc_ref[0, :, gs], kn_ref[0, :, gs]], axis=0)
        vt = _t_bf16(jnp.concatenate([vp_ref[0, :, gs], vc_ref[0, :, gs], vn_ref[0, :, gs]], axis=0))
        pairs = range(grp // 2)
        h0 = [g * grp + 2 * jp for jp in pairs]
        q2 = [q_ref[0, :, h * LANES:(h + 2) * LANES] for h in h0]
        q2 = [jnp.concatenate([x[:, :LANES], x[:, LANES:]], axis=0) for x in q2]
        sc = [_dot_nt(kall, q2[jp]) for jp in pairs]
        sc = [jnp.concatenate([s[:half] + b_prev, s[half:half + tq], s[half + tq:] + b_next], axis=0) for s in sc]
        sk = [jnp.where(first, sink_ref[:, h:h + 1], sink_ref[:, h + 1:h + 2]) * LOG2E for h in h0]
        sm = [_softmax_t(sc[jp], sk[jp]) for jp in pairs]
        ot = [_dot(vt, sm[jp][0]) / sm[jp][1] for jp in pairs]
        for jp in pairs:
            blk = (g * grp) // 2 + jp
            bs = slice(blk * LANES, (blk + 1) * LANES)
            y = jnp.where(low, ot[jp][:, :tq], ot[jp][:, tq:]).T * _silu(g_ref[0, :, bs].astype(F32))
            o_ref[0, :, bs] = y.astype(o_ref.dtype)


def _swa_branch(dqe, dk, dv, z, sink, tq=128):
    b, l, _ = dqe.shape
    half = SWA_HALF_WINDOW
    assert tq == half == LANES
    _, prev, nxt = _band_specs(tq, half, l, 2 * LANES, 0)
    cur = pl.BlockSpec((1, tq, 2 * LANES), lambda i, t: (i, t, 0))
    sink_p = jnp.pad(sink.reshape(1, SWA_Q_HEADS).astype(F32), ((0, 0), (0, LANES - SWA_Q_HEADS)))
    return pl.pallas_call(
        functools.partial(_swa_kernel, half=half, ls=l),
        grid=(b, l // tq),
        in_specs=[
            pl.BlockSpec((1, tq, 2 * SWA_WIDTH), lambda i, t: (i, t, 0)),
            prev, cur, nxt, prev, cur, nxt,
            pl.BlockSpec((1, tq, SWA_WIDTH), lambda i, t: (i, t, O_GD // SWA_WIDTH)),
            pl.BlockSpec((1, LANES), lambda i, t: (0, 0)),
        ],
        out_specs=pl.BlockSpec((1, tq, SWA_WIDTH), lambda i, t: (i, t, 0)),
        out_shape=jax.ShapeDtypeStruct((b, l, SWA_WIDTH), BF16),
        compiler_params=_cparams("parallel", "arbitrary"),
        name="swa",
    )(dqe, dk, dk, dk, dv, dv, dv, z, sink_p)


def _odd_layer(x, mem, pre_g, post_g, w_in, w_out, swa_sink, mem_g, w_mem_kv):
    b, l, d = x.shape
    x2d = x.reshape(b * l, d)
    w_re = jnp.concatenate([w_in[:, 0:6144], w_in[:, 6400:8448], w_in[:, 6144:6400]], axis=1)
    z = _norm_matmul(x2d, pre_g, w_re, 1408, out_dtype=BF16, tm=1024, name="odd_in").reshape(b, l, O_ALL)
    dil, dqe, dk, dv = _odd_prep(z)
    y_c = _dilated_branch(dil, z)
    y_d = _swa_branch(dqe, dk, dv, z, swa_sink)
    kv = _mem_kv(mem, mem_g, w_mem_kv)
    y_x = _xattn(z, kv, O_XQ // X_WIDTH, O_GX // X_WIDTH)
    t = b * l
    out = _out_proj([y_c.reshape(t, -1), y_d.reshape(t, -1), y_x.reshape(t, -1)], w_out, post_g, x2d)
    return out.reshape(b, l, d)


def _trunk(x, mem, even_params, odd_params):
    x = _even_layer(x, mem, *[p[0] for p in even_params])
    return _odd_layer(x, mem, *[p[0] for p in odd_params])


def kernel(x_prompt, x_sample, mem_prompt, mem_sample, e_pre_g, e_post_g, e_w_in, e_w_out, hy_conv_w, hy_conv_b,
           hy_filt_w1, hy_filt_b1, hy_filt_w2, hy_filt_b2, hy_filt_w3, hy_freq, hy_skip, gdn_conv_w, gdn_A_log,
           gdn_dt_bias, gdn_norm_g, e_mem_g, e_w_mem_kv, o_pre_g, o_post_g, o_w_in, o_w_out, swa_sink, o_mem_g,
           o_w_mem_kv):
    even_params = (e_pre_g, e_post_g, e_w_in, e_w_out, hy_conv_w, hy_conv_b, hy_filt_w1, hy_filt_b1, hy_filt_w2,
                   hy_filt_b2, hy_filt_w3, hy_freq, hy_skip, gdn_conv_w, gdn_A_log, gdn_dt_bias, gdn_norm_g,
                   e_mem_g, e_w_mem_kv)
    odd_params = (o_pre_g, o_post_g, o_w_in, o_w_out, swa_sink, o_mem_g, o_w_mem_kv)
    y_prompt = _trunk(x_prompt, mem_prompt, even_params, odd_params)
    y_sample = _trunk(x_sample, mem_sample, even_params, odd_params)
    return (y_prompt, y_sample)
```

```python
import functools
import math

import jax
import jax.numpy as jnp
import numpy as np
from jax import lax
from jax.experimental import pallas as pl
from jax.experimental.pallas import tpu as pltpu

D_MODEL = 1024
EPS = 1e-6
ROPE_THETA = 500000.0
HY_WIDTH = 1024
HY_EMB = 33
HY_BANDS = (HY_EMB - 1) // 2
HY_FILT_HIDDEN = 64
HY_TARGET = 1e-2
HY_FAST_PCT = 0.3
HY_SLOW_PCT = 1.5
GDN_HEADS = 8
GDN_HEAD_DIM = 128
GDN_WIDTH = GDN_HEADS * GDN_HEAD_DIM
GDN_CHUNK = 64
X_HEADS = 4
X_HEAD_DIM = 128
X_WIDTH = X_HEADS * X_HEAD_DIM

LOG2E = math.log2(math.e)
LANES = 128
FFT_N2 = 128
VMEM_LIMIT_BYTES = 48 * 1024 * 1024

BF16 = jnp.bfloat16
F32 = jnp.float32


def _cparams(*sem):
    return pltpu.CompilerParams(dimension_semantics=sem, vmem_limit_bytes=VMEM_LIMIT_BYTES)


def _dot(a, b):
    return jnp.dot(a, b, preferred_element_type=F32)


def _silu(x):
    hx = 0.5 * x
    return hx + hx * jnp.tanh(hx)


def _norm_matmul_kernel(*refs, side):
    if side:
        x_ref, g_ref, w_ref, ws_ref, o_ref, os_ref, xn_ref = refs
    else:
        x_ref, g_ref, w_ref, o_ref, xn_ref = refs

    @pl.when(pl.program_id(1) == 0)
    def _():
        x = x_ref[...]
        ms = jnp.mean(x * x, axis=-1, keepdims=True)
        xn_ref[...] = (x * lax.rsqrt(ms + EPS) * g_ref[...]).astype(BF16)
        if side:
            os_ref[...] = _dot(xn_ref[...], ws_ref[...])

    o_ref[...] = _dot(xn_ref[...], w_ref[...]).astype(o_ref.dtype)


def _norm_matmul(x2d, g, w, tn, out_dtype=F32, tm=512, name="norm_matmul", w_side=None):
    t, d = x2d.shape
    n = w.shape[1]
    assert t % tm == 0 and n % tn == 0
    side = w_side is not None
    in_specs = [
        pl.BlockSpec((tm, d), lambda i, j: (i, 0)),
        pl.BlockSpec((1, d), lambda i, j: (0, 0)),
        pl.BlockSpec((d, tn), lambda i, j: (0, j)),
    ]
    out_specs = [pl.BlockSpec((tm, tn), lambda i, j: (i, j))]
    out_shape = [jax.ShapeDtypeStruct((t, n), out_dtype)]
    args = [x2d, g.reshape(1, d).astype(F32), w.astype(BF16)]
    if side:
        in_specs.append(pl.BlockSpec((d, LANES), lambda i, j: (0, 0)))
        out_specs.append(pl.BlockSpec((tm, LANES), lambda i, j: (i, 0)))
        out_shape.append(jax.ShapeDtypeStruct((t, LANES), F32))
        args.append(w_side.astype(BF16))
    outs = pl.pallas_call(
        functools.partial(_norm_matmul_kernel, side=side),
        grid=(t // tm, n // tn),
        in_specs=in_specs,
        out_specs=out_specs,
        out_shape=out_shape,
        scratch_shapes=[pltpu.VMEM((tm, d), BF16)],
        compiler_params=_cparams("parallel", "arbitrary"),
        name=name,
    )(*args)
    return tuple(outs) if side else outs[0]


HALO_ROWS = 16


def _fill_halo_buf(buf_ref, zc_ref, zp_ref, zn_ref, t, nt):
    tl = zc_ref.shape[1]
    hr = HALO_ROWS
    buf_ref[0:hr, :] = jnp.where(t > 0, zp_ref[0].astype(F32), 0.0)
    buf_ref[hr:hr + tl, :] = zc_ref[0].astype(F32)
    buf_ref[hr + tl:2 * hr + tl, :] = jnp.where(t < nt - 1, zn_ref[0].astype(F32), 0.0)


def _halo_specs(tl, w, l, col):
    r = tl // HALO_ROWS
    nb = l // HALO_ROWS
    cur = pl.BlockSpec((1, tl, w), lambda i, t: (i, t, col))
    prev = pl.BlockSpec((1, HALO_ROWS, w), lambda i, t: (i, jnp.maximum(t * r - 1, 0), col))
    nxt = pl.BlockSpec((1, HALO_ROWS, w), lambda i, t: (i, jnp.minimum((t + 1) * r, nb - 1), col))
    return cur, prev, nxt


def _hy_prep_kernel(zc_ref, zp_ref, zn_ref, g_ref, cw_ref, cb_ref, u_ref, e_ref, buf_ref):
    t = pl.program_id(1)
    nt = pl.num_programs(1)
    tl = zc_ref.shape[1]
    c = HY_WIDTH
    hr = HALO_ROWS
    _fill_halo_buf(buf_ref, zc_ref, zp_ref, zn_ref, t, nt)
    cw = 256
    for c0 in range(0, c, cw):
        parts = []
        for p in range(3):
            lo = p * c + c0
            acc = cb_ref[:, lo:lo + cw]
            for j in range(3):
                acc = acc + buf_ref[pl.ds(hr - 1 + j, tl), lo:lo + cw] * cw_ref[j:j + 1, lo:lo + cw]
            parts.append(acc)
        x0, x1, v = parts
        u_ref[0, :, c0:c0 + cw] = (v * x1).astype(u_ref.dtype)
        e_ref[0, :, c0:c0 + cw] = (x0 * _silu(g_ref[0, :, c0:c0 + cw].astype(F32))).astype(e_ref.dtype)


def _hy_prep(z, conv_w, conv_b, tl=512):
    b, l, _ = z.shape
    c = HY_WIDTH
    assert l % tl == 0
    cur, prev, nxt = _halo_specs(tl, 3 * c, l, E_HY // (3 * c))
    return pl.pallas_call(
        _hy_prep_kernel,
        grid=(b, l // tl),
        in_specs=[
            cur, prev, nxt,
            pl.BlockSpec((1, tl, c), lambda i, t: (i, t, E_GHY // c)),
            pl.BlockSpec((3, 3 * c), lambda i, t: (0, 0)),
            pl.BlockSpec((1, 3 * c), lambda i, t: (0, 0)),
        ],
        out_specs=[
            pl.BlockSpec((1, tl, c), lambda i, t: (i, t, 0)),
            pl.BlockSpec((1, tl, c), lambda i, t: (i, t, 0)),
        ],
        out_shape=[jax.ShapeDtypeStruct((b, l, c), F32), jax.ShapeDtypeStruct((b, l, c), F32)],
        scratch_shapes=[pltpu.VMEM((tl + 2 * HALO_ROWS, 3 * c), F32)],
        compiler_params=_cparams("parallel", "arbitrary"),
        name="hy_prep",
    )(z, z, z, z, conv_w.astype(F32), conv_b.reshape(1, 3 * c).astype(F32))


def _hy_filter_tables(l):
    t = jnp.linspace(0.0, 1.0, l, dtype=F32)[:, None]
    w = (2.0 * math.pi / l) * jnp.arange(l, dtype=F32)[:, None]
    f = jnp.linspace(1e-4, HY_BANDS - 1, HY_BANDS, dtype=F32)[None, :]
    emb = jnp.concatenate([t, jnp.cos(f * w), -jnp.sin(f * w)], axis=-1)
    emb = jnp.pad(emb, ((0, 0), (0, LANES - HY_EMB)))
    deltas = jnp.abs(jnp.linspace(math.log(HY_TARGET) / HY_SLOW_PCT, math.log(HY_TARGET) / HY_FAST_PCT, HY_WIDTH, dtype=F32))
    decay = jnp.exp(-t * jnp.tile(deltas, 2)[None, :])
    return emb, decay


def _hy_filter_kernel(emb_ref, dec_ref, w1_ref, b1_ref, w2_ref, b2_ref, w3_ref, fr_ref, sk_ref, o_ref):
    c = HY_WIDTH
    hp = lax.Precision.HIGHEST
    fr = fr_ref[...]
    hid = jnp.sin(fr * (jnp.dot(emb_ref[...], w1_ref[...], precision=hp, preferred_element_type=F32) + b1_ref[...]))
    hid = jnp.sin(fr * (jnp.dot(hid, w2_ref[...], precision=hp, preferred_element_type=F32) + b2_ref[...]))
    tl = emb_ref.shape[0]
    row = lax.broadcasted_iota(jnp.int32, (tl, 1), 0) + pl.program_id(0) * tl
    first = row == 0
    cw = 512
    for c0 in range(0, 2 * c, cw):
        filt = jnp.dot(hid, w3_ref[:, c0:c0 + cw], precision=hp, preferred_element_type=F32) * dec_ref[:, c0:c0 + cw]
        if c0 < c:
            filt = jnp.where(first, filt + sk_ref[:, c0:c0 + cw], filt)
            o_ref[0, :, c0:c0 + cw] = filt.astype(o_ref.dtype)
        else:
            filt = jnp.where(first, 0.0, filt)
            o_ref[1, :, c0 - c:c0 - c + cw] = filt.astype(o_ref.dtype)


def _hy_filters(l, w1, b1, w2, b2, w3, freq, skip, tl=256):
    c = HY_WIDTH
    hdim = HY_FILT_HIDDEN
    emb, decay = _hy_filter_tables(l)
    w1p = jnp.pad(w1.astype(F32), ((0, LANES - HY_EMB), (0, 0)))
    return pl.pallas_call(
        _hy_filter_kernel,
        grid=(l // tl,),
        in_specs=[
            pl.BlockSpec((tl, LANES), lambda t: (t, 0)),
            pl.BlockSpec((tl, 2 * c), lambda t: (t, 0)),
            pl.BlockSpec((LANES, hdim), lambda t: (0, 0)),
            pl.BlockSpec((1, hdim), lambda t: (0, 0)),
            pl.BlockSpec((hdim, hdim), lambda t: (0, 0)),
            pl.BlockSpec((1, hdim), lambda t: (0, 0)),
            pl.BlockSpec((hdim, 2 * c), lambda t: (0, 0)),
            pl.BlockSpec((1, hdim), lambda t: (0, 0)),
            pl.BlockSpec((1, c), lambda t: (0, 0)),
        ],
        out_specs=pl.BlockSpec((2, tl, c), lambda t: (0, t, 0)),
        out_shape=jax.ShapeDtypeStruct((2, l, c), BF16),
        compiler_params=_cparams("arbitrary"),
        name="hy_filter",
    )(emb, decay, w1p, b1.reshape(1, hdim).astype(F32), w2.astype(F32), b2.reshape(1, hdim).astype(F32),
      w3.astype(F32), freq.reshape(1, hdim).astype(F32), skip.reshape(1, c).astype(F32))


def _fft_consts(l):
    n = 2 * l
    n2 = FFT_N2
    n1 = n // n2
    n1h = n1 // 2
    w0 = 2.0 * math.pi / n
    k1 = jnp.arange(n1, dtype=jnp.int32)
    m1 = jnp.arange(n1h, dtype=jnp.int32)
    m2 = jnp.arange(n2, dtype=jnp.int32)
    ea = (n2 * m1[None, None, :] * k1[None, :, None] + m2[:, None, None] * k1[None, :, None]) % n
    ang = ea.astype(F32) * w0
    fa = jnp.concatenate([jnp.cos(ang), -jnp.sin(ang)], axis=1).astype(BF16)
    eb = (n1 * m2[:, None] * m2[None, :]) % n
    angb = eb.astype(F32) * w0
    fr, fi = jnp.cos(angb), -jnp.sin(angb)
    fb = jnp.concatenate([jnp.concatenate([fr, -fi], axis=1), jnp.concatenate([fi, fr], axis=1)], axis=0).astype(BF16)
    eg = (n1 * m2[None, :, None] * m2[None, None, :] + m2[None, :, None] * k1[:, None, None]) % n
    angg = eg.astype(F32) * w0
    gr, gi = jnp.cos(angg), jnp.sin(angg)
    gb = jnp.concatenate([jnp.concatenate([gr, -gi], axis=2), jnp.concatenate([gi, gr], axis=2)], axis=1).astype(BF16)
    ec = (n2 * m1[:, None] * k1[None, :]) % n
    angc = ec.astype(F32) * w0
    fc = (jnp.concatenate([jnp.cos(angc), -jnp.sin(angc)], axis=1) * (1.0 / n)).astype(BF16)
    return fa, fb, gb, fc


def _fft_a_kernel(u_ref, f_ref, o_ref):
    tn2 = f_ref.shape[0]
    n1 = f_ref.shape[1] // 2
    c = u_ref.shape[2] // tn2
    for j in range(tn2):
        a = _dot(f_ref[j], u_ref[0, :, j * c:(j + 1) * c])
        o_ref[0, 0, :, j * c:(j + 1) * c] = a[:n1].astype(o_ref.dtype)
        o_ref[0, 1, :, j * c:(j + 1) * c] = a[n1:].astype(o_ref.dtype)


def _fft_a(u, fa, tn2=8):
    b, l, c = u.shape
    n2, n1x2, n1h = fa.shape
    n1 = n1x2 // 2
    uv = u.reshape(b, n1h, n2 * c)
    return pl.pallas_call(
        _fft_a_kernel,
        grid=(b, n2 // tn2),
        in_specs=[
            pl.BlockSpec((1, n1h, tn2 * c), lambda i, j: (i, 0, j)),
            pl.BlockSpec((tn2, n1x2, n1h), lambda i, j: (j, 0, 0)),
        ],
        out_specs=pl.BlockSpec((1, 2, n1, tn2 * c), lambda i, j: (i, 0, 0, j)),
        out_shape=jax.ShapeDtypeStruct((b, 2, n1, n2 * c), BF16),
        compiler_params=_cparams("parallel", "arbitrary"),
        name="fft_a",
    )(uv, fa)


def _fft_bf_kernel(a_ref, w_ref, o_ref):
    n2 = a_ref.shape[2]
    ct = a_ref.shape[3]
    cw = 256
    for c0 in range(0, ct, cw):
        xf = _dot(w_ref[...], a_ref[0, :, :, c0:c0 + cw].reshape(2 * n2, cw))
        xb = _dot(w_ref[...], a_ref[1, :, :, c0:c0 + cw].reshape(2 * n2, cw))
        o_ref[0, :, c0:c0 + cw] = xf[:n2] + xb[:n2]
        o_ref[1, :, c0:c0 + cw] = xf[n2:] - xb[n2:]


def _fft_bf(a, fb):
    _, _, n, c = a.shape
    n2 = FFT_N2
    return pl.pallas_call(
        _fft_bf_kernel,
        grid=(n // n2,),
        in_specs=[
            pl.BlockSpec((2, 2, n2, c), lambda k: (0, 0, k, 0)),
            pl.BlockSpec((2 * n2, 2 * n2), lambda k: (0, 0)),
        ],
        out_specs=pl.BlockSpec((2, n2, c), lambda k: (0, k, 0)),
        out_shape=jax.ShapeDtypeStruct((2, n, c), F32),
        compiler_params=_cparams("arbitrary"),
        name="fft_bf",
    )(a, fb)


SUBLANES = 8
CONV_SLABS = 2


def _conv_dims(n1):
    kh = n1 // 2 + 1
    kp = -(-kh // SUBLANES) * SUBLANES
    return kh, kp, 2 * kp + SUBLANES


def _hy_conv_kernel(u_ref, h_ref, la_ref, w_ref, g_ref, fc_ref, o_ref, s_ref, *, n1, unroll_b):
    n2 = FFT_N2
    n1h = n1 // 2
    kh, kp, pitch = _conv_dims(n1)
    grp = SUBLANES
    slabs = range(CONV_SLABS)

    def lanes(x, sl):
        return x[:, sl * LANES:(sl + 1) * LANES]

    def stage_a(jj, carry):
        rows = [u_ref[0, pl.ds(pl.multiple_of(m * n2 + jj * grp, grp), grp), :] for m in range(n1h)]
        a = _dot(la_ref[jj], jnp.concatenate(rows, axis=0).astype(BF16))
        for j0 in range(grp):
            dst = pl.multiple_of((jj * grp + j0) * pitch, grp)
            for sl in slabs:
                s_ref[sl, pl.ds(dst, 2 * kp), :] = lanes(a[j0 * 2 * kp:(j0 + 1) * 2 * kp], sl)
        return carry

    lax.fori_loop(0, n2 // grp, stage_a, 0)

    def stage_b(kk, carry):
        k1s = [jnp.minimum(kk * unroll_b + i, kh - 1) for i in range(unroll_b)]
        ar = [jnp.concatenate([s_ref[sl, pl.ds(k, n2, stride=pitch), :] for sl in slabs], axis=1) for k in k1s]
        ai = [jnp.concatenate([s_ref[sl, pl.ds(kp + k, n2, stride=pitch), :] for sl in slabs], axis=1) for k in k1s]
        x = [_dot(w_ref[...], jnp.concatenate([ar[i], ai[i]], axis=0).astype(BF16)) for i in range(unroll_b)]
        y = []
        for i, k in enumerate(k1s):
            off = pl.multiple_of(k * n2, n2)
            hr, hi = h_ref[0, pl.ds(off, n2), :], h_ref[1, pl.ds(off, n2), :]
            xr, xi = x[i][:n2], x[i][n2:]
            y.append(jnp.concatenate([xr * hr - xi * hi, xr * hi + xi * hr], axis=0).astype(BF16))
        z = [_dot(g_ref[k1s[i]], y[i]) for i in range(unroll_b)]
        for i, k in enumerate(k1s):
            for sl in slabs:
                s_ref[sl, pl.ds(k, n2, stride=pitch), :] = lanes(z[i][:n2], sl)
                s_ref[sl, pl.ds(kp + k, n2, stride=pitch), :] = lanes(z[i][n2:], sl)
        return carry

    lax.fori_loop(0, -(-kh // unroll_b), stage_b, 0)

    def stage_c(jj, carry):
        zz = [jnp.concatenate([s_ref[sl, pl.ds(pl.multiple_of((jj * grp + j0) * pitch, grp), 2 * kp), :]
                               for sl in slabs], axis=1) for j0 in range(grp)]
        y = _dot(fc_ref[...], jnp.concatenate(zz, axis=0).astype(BF16))
        for m in range(n1h):
            o_ref[0, pl.ds(pl.multiple_of(m * n2 + jj * grp, grp), grp), :] = y[m * grp:(m + 1) * grp]
        return carry

    lax.fori_loop(0, n2 // grp, stage_c, 0)


def _hy_conv(u, h, fa, fb, gb, fc):
    b, l, c = u.shape
    n2, n1x2, n1h = fa.shape
    n1 = n1x2 // 2
    kh, kp, pitch = _conv_dims(n1)
    grp = SUBLANES
    ct = CONV_SLABS * LANES

    def half(t, axis):
        re, im = jnp.split(t, 2, axis=axis)
        pad = [(0, 0)] * t.ndim
        pad[axis] = (0, kp - kh)
        cut = lambda x: jnp.pad(lax.slice_in_dim(x, 0, kh, axis=axis), pad)
        return jnp.concatenate([cut(re), cut(im)], axis=axis)

    fa_h = half(fa, 1)
    la = (fa_h.reshape(n2 // grp, grp, 2 * kp, n1h, 1) * jnp.eye(grp, dtype=fa.dtype).reshape(1, grp, 1, 1, grp))
    la = la.reshape(n2 // grp, grp * 2 * kp, n1h * grp)
    wgt = jnp.ones((kh,), F32).at[1:kh - 1].set(2.0)
    wgt = jnp.pad(wgt, (0, kp - kh))
    fc_h = (half(fc, 1).astype(F32) * jnp.tile(wgt, 2)[None, :]).astype(BF16)
    lc = fc_h.reshape(n1h, 1, 1, 2 * kp) * jnp.eye(grp, dtype=fc_h.dtype).reshape(1, grp, grp, 1)
    fc_h = lc.reshape(n1h * grp, grp * 2 * kp)
    gb_h = gb[:kh]
    unroll_b = 3
    once = pl.Buffered(1)
    return pl.pallas_call(
        functools.partial(_hy_conv_kernel, n1=n1, unroll_b=unroll_b),
        grid=(c // ct, b),
        in_specs=[
            pl.BlockSpec((1, l, ct), lambda j, i: (i, 0, j)),
            pl.BlockSpec((2, kh * n2, ct), lambda j, i: (0, 0, j), pipeline_mode=once),
            pl.BlockSpec(la.shape, lambda j, i: (0, 0, 0), pipeline_mode=once),
            pl.BlockSpec(fb.shape, lambda j, i: (0, 0), pipeline_mode=once),
            pl.BlockSpec(gb_h.shape, lambda j, i: (0, 0, 0), pipeline_mode=once),
            pl.BlockSpec(fc_h.shape, lambda j, i: (0, 0), pipeline_mode=once),
        ],
        out_specs=pl.BlockSpec((1, l, ct), lambda j, i: (i, 0, j)),
        out_shape=jax.ShapeDtypeStruct((b, l, c), F32),
        scratch_shapes=[pltpu.VMEM((CONV_SLABS, n2 * pitch, LANES), F32)],
        compiler_params=_cparams("arbitrary", "arbitrary"),
        name="hy_conv",
    )(u, h, la, fb, gb_h, fc_h)


def _hyena_branch(z, l, conv_w, conv_b, w1, b1, w2, b2, w3, freq, skip):
    c = HY_WIDTH
    n = 2 * l
    fa, fb, gb, fc = _fft_consts(l)
    filt = _hy_filters(l, w1, b1, w2, b2, w3, freq, skip)
    fa_spec = _fft_a(filt, fa).reshape(2, 2, n, c)
    h = _fft_bf(fa_spec, fb)
    u, e = _hy_prep(z, conv_w, conv_b)
    return _hy_conv(u, h, fa, fb, gb, fc), e


def _gdn_prep_kernel(zc_ref, zp_ref, zn_ref, cw_ref, q_ref, k_ref, v_ref, buf_ref):
    t = pl.program_id(1)
    nt = pl.num_programs(1)
    tl = zc_ref.shape[1]
    dh = GDN_HEAD_DIM
    hr = HALO_ROWS
    _fill_halo_buf(buf_ref, zc_ref, zp_ref, zn_ref, t, nt)
    outs = (q_ref, k_ref, v_ref)
    for p in range(3):
        for h in range(GDN_HEADS):
            lo = p * GDN_WIDTH + h * dh
            acc = buf_ref[pl.ds(hr - 2, tl), lo:lo + dh] * cw_ref[0:1, lo:lo + dh]
            for j in range(1, 5):
                acc = acc + buf_ref[pl.ds(hr - 2 + j, tl), lo:lo + dh] * cw_ref[j:j + 1, lo:lo + dh]
            a = _silu(acc)
            if p < 2:
                a = a * lax.rsqrt(jnp.sum(a * a, axis=-1, keepdims=True) + EPS)
            if p == 0:
                a = a * (dh ** -0.5)
            outs[p][0, :, h * dh:(h + 1) * dh] = a.astype(BF16)


def _gdn_prep(z, conv_w, tl=512):
    b, l, _ = z.shape
    w = 3 * GDN_WIDTH
    cur, prev, nxt = _halo_specs(tl, w, l, E_QKV // w)
    o_spec = pl.BlockSpec((1, tl, GDN_WIDTH), lambda i, t: (i, t, 0))
    o_shape = jax.ShapeDtypeStruct((b, l, GDN_WIDTH), BF16)
    return pl.pallas_call(
        _gdn_prep_kernel,
        grid=(b, l // tl),
        in_specs=[cur, prev, nxt, pl.BlockSpec((5, w), lambda i, t: (0, 0))],
        out_specs=[o_spec, o_spec, o_spec],
        out_shape=[o_shape, o_shape, o_shape],
        scratch_shapes=[pltpu.VMEM((tl + 2 * HALO_ROWS, w), F32)],
        compiler_params=_cparams("parallel", "arbitrary"),
        name="gdn_prep",
    )(z, z, z, conv_w.astype(F32))


def _dot_nt(a, b):
    return lax.dot_general(a, b, (((1,), (1,)), ((), ())), preferred_element_type=F32)


def _dot_tn(a, b):
    return lax.dot_general(a, b, (((0,), (0,)), ((), ())), preferred_element_type=F32)


def _split3(x):
    a = x.astype(BF16)
    r = x - a.astype(F32)
    b = r.astype(BF16)
    c = (r - b.astype(F32)).astype(BF16)
    return a, b, c


GDN_CHUNKS_PER_STEP = 8


def _gdn_scan_kernel(*refs, direction, final):
    if final:
        q_ref, k_ref, v_ref, zs_ref, al_ref, dt_ref, of_ref, gg_ref, ng_ref, o_ref, s_ref = refs
    else:
        q_ref, k_ref, v_ref, zs_ref, al_ref, dt_ref, o_ref, s_ref = refs
    c = GDN_CHUNK
    dh = GDN_HEAD_DIM
    nsub = q_ref.shape[1] // c

    @pl.when(pl.program_id(1) == 0)
    def _():
        s_ref[...] = jnp.zeros_like(s_ref)

    row = lax.broadcasted_iota(jnp.int32, (c, c), 0)
    col = lax.broadcasted_iota(jnp.int32, (c, c), 1)
    if direction == 0:
        incl, strict, last = col <= row, col < row, c - 1
    else:
        incl, strict, last = col >= row, col > row, 0
    tri = incl.astype(BF16)
    eye = (row == col).astype(F32)
    heads = range(GDN_HEADS)
    hsl = [slice(h * dh, (h + 1) * dh) for h in heads]
    lns = [2 * GDN_HEADS + direction * GDN_HEADS + h for h in heads]

    rows = [slice(i * c, (i + 1) * c) for i in range(nsub)]
    gcum, gcum_t, beta_all = [], [], []
    for rs in rows:
        zs = zs_ref[0, rs, :]
        beta_all.append(jax.nn.sigmoid(zs))
        gl = -jnp.exp(al_ref[...]) * jax.nn.softplus(zs + dt_ref[...])
        g1, g2, g3 = _split3(gl)
        gsum = _dot(tri, g1) + _dot(tri, g2) + _dot(tri, g3)
        gcum.append(gsum)
        gcum_t.append(jnp.concatenate([gsum, jnp.zeros_like(gsum)], axis=0).T)
    items = [(i, h) for i in range(nsub) for h in heads]
    it = range(len(items))
    beta = [beta_all[i][:, direction * GDN_HEADS + h:direction * GDN_HEADS + h + 1] for i, h in items]
    gc = [gcum[i][:, lns[h]:lns[h] + 1] for i, h in items]
    gct = [gcum_t[i][lns[h]:lns[h] + 1, 0:c] for i, h in items]
    glast = [gcum_t[i][lns[h]:lns[h] + 1, last:last + 1] for i, h in items]
    qh = [q_ref[0, rows[i], hsl[h]] for i, h in items]
    kh = [k_ref[0, rows[i], hsl[h]] for i, h in items]
    vh = [v_ref[0, rows[i], hsl[h]] for i, h in items]
    kq = [_dot_nt(jnp.concatenate([kh[x], qh[x]], axis=0), kh[x]) for x in it]
    dmat = [jnp.exp(jnp.where(incl, gc[x] - gct[x], -jnp.inf)) for x in it]
    a = [jnp.where(strict, (beta[x] * kq[x][:c]) * dmat[x], 0.0) for x in it]
    tinv = [eye - a[x] for x in it]
    ab = [a[x].astype(BF16) for x in it]
    p = [_dot(ab[x], ab[x]) for x in it]
    for j in range(5):
        pb = [p[x].astype(BF16) for x in it]
        if j < 4:
            tp = [_dot(jnp.concatenate([tinv[x].astype(BF16), pb[x]], axis=0), pb[x]) for x in it]
            tinv = [tinv[x] + tp[x][:c] for x in it]
            p = [tp[x][c:] for x in it]
        else:
            tinv = [tinv[x] + _dot(tinv[x].astype(BF16), pb[x]) for x in it]
    eg = [jnp.exp(gc[x]) for x in it]
    kf = [kh[x].astype(F32) for x in it]
    rhs = [jnp.concatenate([vh[x].astype(F32) * beta[x], kf[x] * (beta[x] * eg[x])], axis=1).astype(BF16) for x in it]
    uw = [_dot(tinv[x].astype(BF16), rhs[x]) for x in it]
    wq = [jnp.concatenate([uw[x][:, dh:], qh[x].astype(F32) * eg[x]], axis=0).astype(BF16) for x in it]
    qkd = [(kq[x][c:] * dmat[x]).astype(BF16) for x in it]
    kd = [(kf[x] * jnp.exp(glast[x] - gc[x])).astype(BF16) for x in it]

    order = list(range(nsub)) if direction == 0 else list(range(nsub - 1, -1, -1))
    state = [s_ref[h] for h in heads]
    for i in order:
        rs = rows[i]
        xs = [i * GDN_HEADS + h for h in heads]
        ws = [_dot(wq[xs[h]], state[h].astype(BF16)) for h in heads]
        vnb = [(uw[xs[h]][:, :dh] - ws[h][:c]).astype(BF16) for h in heads]
        o = [ws[h][c:] + _dot(qkd[xs[h]], vnb[h]) for h in heads]
        state = [state[h] * jnp.exp(glast[xs[h]]) + _dot_tn(kd[xs[h]], vnb[h]) for h in heads]
        for h in heads:
            if final:
                tot = of_ref[0, rs, hsl[h]] + o[h]
                y = tot * lax.rsqrt(jnp.mean(tot * tot, axis=-1, keepdims=True) + EPS) * ng_ref[...]
                o_ref[0, rs, hsl[h]] = (y * _silu(gg_ref[0, rs, hsl[h]].astype(F32))).astype(o_ref.dtype)
            else:
                o_ref[0, rs, hsl[h]] = o[h].astype(o_ref.dtype)
    for h in heads:
        s_ref[h] = state[h]


def _gdn_scan(q, k, v, zs, a_log, dt_bias, direction, o_fwd=None, z=None, gate_blk=None, norm_g=None):
    b, l, w = q.shape
    c = GDN_CHUNK * GDN_CHUNKS_PER_STEP
    n = l // c
    final = o_fwd is not None
    if direction == 0:
        cmap = lambda i, t: (i, t, 0)
    else:
        cmap = lambda i, t: (i, n - 1 - t, 0)
    blk = pl.BlockSpec((1, c, w), cmap)
    vec = pl.BlockSpec((1, LANES), lambda i, t: (0, 0))
    pad = lambda x: jnp.pad(x.reshape(1, -1).astype(F32), ((0, 0), (2 * GDN_HEADS, LANES - 4 * GDN_HEADS)))
    in_specs = [blk, blk, blk, pl.BlockSpec((1, c, LANES), cmap), vec, vec]
    args = [q, k, v, zs, pad(a_log), pad(dt_bias)]
    if final:
        if direction == 0:
            gmap = lambda i, t: (i, t, gate_blk)
        else:
            gmap = lambda i, t: (i, n - 1 - t, gate_blk)
        in_specs += [blk, pl.BlockSpec((1, c, w), gmap), pl.BlockSpec((1, GDN_HEAD_DIM), lambda i, t: (0, 0))]
        args += [o_fwd, z, norm_g.reshape(1, GDN_HEAD_DIM).astype(F32)]
    return pl.pallas_call(
        functools.partial(_gdn_scan_kernel, direction=direction, final=final),
        grid=(b, n),
        in_specs=in_specs,
        out_specs=blk,
        out_shape=jax.ShapeDtypeStruct((b, l, w), BF16 if final else F32),
        scratch_shapes=[pltpu.VMEM((GDN_HEADS, GDN_HEAD_DIM, GDN_HEAD_DIM), F32)],
        compiler_params=_cparams("parallel", "arbitrary"),
        name="gdn_scan_bwd" if direction else "gdn_scan_fwd",
    )(*args)


def _gdn_branch(z, zs, conv_w, a_log, dt_bias, norm_g, gate_blk):
    q, k, v = _gdn_prep(z, conv_w)
    o_f = _gdn_scan(q, k, v, zs, a_log, dt_bias, 0)
    return _gdn_scan(q, k, v, zs, a_log, dt_bias, 1, o_fwd=o_f, z=z, gate_blk=gate_blk, norm_g=norm_g)


def _softmax_t(s, extra=None):
    m = jnp.max(s, axis=0, keepdims=True)
    if extra is not None:
        m = jnp.maximum(m, extra)
    p = jnp.exp2(s - m)
    den = jnp.sum(p, axis=0, keepdims=True)
    if extra is not None:
        den = den + jnp.exp2(extra - m)
    return p.astype(BF16), den, m


def _t_bf16(x):
    return x.astype(F32).T.astype(BF16)


def _xattn_kernel(q_ref, g_ref, kv_ref, o_ref):
    dh = X_HEAD_DIM
    for h in range(X_HEADS):
        hs = slice(h * dh, (h + 1) * dh)
        qh = q_ref[0, :, hs].astype(BF16)
        kh = kv_ref[0, :, hs]
        vh = kv_ref[0, :, X_WIDTH + h * dh:X_WIDTH + (h + 1) * dh]
        s = _dot_nt(qh, kh) * (dh ** -0.5)
        p = jnp.exp(s - jnp.max(s, axis=-1, keepdims=True))
        den = jnp.sum(p, axis=-1, keepdims=True)
        o = _dot(p.astype(BF16), vh) / den
        o_ref[0, :, hs] = (o * _silu(g_ref[0, :, hs].astype(F32))).astype(o_ref.dtype)


def _xattn(z, kv, q_blk, g_blk, tq=512):
    b, l, _ = z.shape
    m = kv.shape[1]
    w = X_WIDTH
    return pl.pallas_call(
        _xattn_kernel,
        grid=(b, l // tq),
        in_specs=[
            pl.BlockSpec((1, tq, w), lambda i, t: (i, t, q_blk)),
            pl.BlockSpec((1, tq, w), lambda i, t: (i, t, g_blk)),
            pl.BlockSpec((1, m, 2 * w), lambda i, t: (i, 0, 0)),
        ],
        out_specs=pl.BlockSpec((1, tq, w), lambda i, t: (i, t, 0)),
        out_shape=jax.ShapeDtypeStruct((b, l, w), BF16),
        compiler_params=_cparams("parallel", "arbitrary"),
        name="xattn",
    )(z, z, kv)


def _out_proj_kernel(*refs, widths, gated):
    nparts = len(widths)
    y_refs = refs[:nparts]
    if gated:
        e_ref, w_ref, g_ref, x_ref, o_ref = refs[nparts:]
    else:
        w_ref, g_ref, x_ref, o_ref = refs[nparts:]
    d = o_ref.shape[1]
    cw = 256
    ys = [y_ref[...] for y_ref in y_refs]
    if gated:
        ys[0] = ys[0] * e_ref[...]
    ys = [y.astype(BF16) for y in ys]
    ssq = jnp.zeros((o_ref.shape[0], 1), F32)
    for c0 in range(0, d, cw):
        acc = None
        off = 0
        for y, wd in zip(ys, widths):
            part = _dot(y, w_ref[off:off + wd, c0:c0 + cw])
            acc = part if acc is None else acc + part
            off += wd
        ssq = ssq + jnp.sum(acc * acc, axis=-1, keepdims=True)
        o_ref[:, c0:c0 + cw] = acc
    r = lax.rsqrt(ssq * (1.0 / d) + EPS)
    for c0 in range(0, d, cw):
        o_ref[:, c0:c0 + cw] = x_ref[:, c0:c0 + cw] + o_ref[:, c0:c0 + cw] * r * g_ref[:, c0:c0 + cw]


def _out_proj(parts, w_out, post_g, x2d, gate=None, tm=512):
    t, d = x2d.shape
    widths = tuple(int(p.shape[1]) for p in parts)
    kdim = sum(widths)
    gates = [] if gate is None else [gate]
    return pl.pallas_call(
        functools.partial(_out_proj_kernel, widths=widths, gated=gate is not None),
        grid=(t // tm,),
        in_specs=[pl.BlockSpec((tm, wd), lambda i: (i, 0)) for wd in widths]
        + [pl.BlockSpec((tm, widths[0]), lambda i: (i, 0)) for _ in gates] + [
            pl.BlockSpec((kdim, d), lambda i: (0, 0)),
            pl.BlockSpec((1, d), lambda i: (0, 0)),
            pl.BlockSpec((tm, d), lambda i: (i, 0)),
        ],
        out_specs=pl.BlockSpec((tm, d), lambda i: (i, 0)),
        out_shape=jax.ShapeDtypeStruct((t, d), F32),
        compiler_params=_cparams("parallel"),
        name="out_proj",
    )(*parts, *gates, w_out.astype(BF16), post_g.reshape(1, d).astype(F32), x2d)


def _mem_kv(mem, mem_g, w_mem_kv):
    b, m, d = mem.shape
    kv = _norm_matmul(mem.reshape(b * m, d), mem_g, w_mem_kv, 1024, out_dtype=BF16, name="mem_kv")
    return kv.reshape(b, m, 2 * X_WIDTH)


E_QKV, E_HY, E_GHY, E_GGDN, E_XQ, E_GX, E_BIG = 0, 3072, 6144, 7168, 8192, 8704, 9216


def _even_layer(x, mem, pre_g, post_g, w_in, w_out, hy_conv_w, hy_conv_b, hy_w1, hy_b1, hy_w2, hy_b2, hy_w3, hy_freq,
                hy_skip, gdn_conv_w, gdn_a_log, gdn_dt_bias, gdn_norm_g, mem_g, w_mem_kv):
    b, l, d = x.shape
    x2d = x.reshape(b * l, d)
    w_big = jnp.concatenate([w_in[:, 4096:7168], w_in[:, 0:3072], w_in[:, 3072:4096], w_in[:, 7168:8192],
                             w_in[:, 8224:8736], w_in[:, 8736:9248]], axis=1)
    w_small = jnp.pad(w_in[:, 8192:8224], ((0, 0), (0, LANES - 4 * GDN_HEADS)))
    z, zs = _norm_matmul(x2d, pre_g, w_big, E_BIG // 3, out_dtype=BF16, tm=1024, name="even_in", w_side=w_small)
    z = z.reshape(b, l, E_BIG)
    zs = zs.reshape(b, l, LANES)
    conv, e = _hyena_branch(z, l, hy_conv_w, hy_conv_b, hy_w1, hy_b1, hy_w2, hy_b2, hy_w3, hy_freq, hy_skip)
    y_b = _gdn_branch(z, zs, gdn_conv_w, gdn_a_log, gdn_dt_bias, gdn_norm_g, E_GGDN // GDN_WIDTH)
    kv = _mem_kv(mem, mem_g, w_mem_kv)
    y_x = _xattn(z, kv, E_XQ // X_WIDTH, E_GX // X_WIDTH)
    t = b * l
    out = _out_proj([conv.reshape(t, -1), y_b.reshape(t, -1), y_x.reshape(t, -1)], w_out, post_g, x2d,
                    gate=e.reshape(t, -1))
    return out.reshape(b, l, d)


DIL_PATTERNS = ((128, 1), (512, 4), (2048, 16))
N_DIL = len(DIL_PATTERNS)
DIL_HEADS = 4
DIL_HEAD_DIM = 128
DIL_WIDTH = DIL_HEADS * DIL_HEAD_DIM
SWA_Q_HEADS = 16
SWA_KV_HEADS = 2
SWA_HEAD_DIM = 64
SWA_WIDTH = SWA_Q_HEADS * SWA_HEAD_DIM
SWA_HALF_WINDOW = 128
O_CQKV, O_GC, O_DQ, O_GD, O_XQ, O_GX, O_DKV, O_ALL = 0, 4608, 5120, 6144, 7168, 7680, 8192, 8448


def _rope_tables(l, dh):
    half = dh // 8
    inv = ROPE_THETA ** (-jnp.arange(half, dtype=F32) / half)
    ang = jnp.arange(l, dtype=F32)[:, None] * inv[None, :]
    cos, sin = jnp.cos(ang), jnp.sin(ang)
    one = jnp.ones((l, dh - 2 * half), F32)
    zero_h = jnp.zeros((l, half), F32)
    zero_r = jnp.zeros((l, dh - 2 * half), F32)
    c = jnp.concatenate([cos, cos, one], axis=1)
    sa = jnp.concatenate([-sin, zero_h, zero_r], axis=1)
    sb = jnp.concatenate([zero_h, sin, zero_r], axis=1)
    rep = LANES // dh
    return tuple(jnp.tile(t, (1, rep)) for t in (c, sa, sb))


def _odd_prep_kernel(c_ref, dq_ref, dkv_ref, c1_ref, a1_ref, b1_ref, c2_ref, a2_ref, b2_ref, *refs):
    dil_refs = refs[:3 * N_DIL]
    dqe_ref, dk_ref, dv_ref, xs_ref = refs[3 * N_DIL:]
    tl = c_ref.shape[1]
    lane = lax.broadcasted_iota(jnp.int32, (tl, LANES), 1)
    low = lane < SWA_HEAD_DIM
    c1, a1, b1 = c1_ref[...], a1_ref[...], b1_ref[...]
    c2, a2, b2 = c2_ref[...], a2_ref[...], b2_ref[...]
    h1 = DIL_HEAD_DIM // 8
    h2 = SWA_HEAD_DIM // 8

    def rope1(x):
        return x * c1 + pltpu.roll(x, LANES - h1, 1) * a1 + pltpu.roll(x, h1, 1) * b1

    def rope2(x):
        return x * c2 + pltpu.roll(x, LANES - h2, 1) * a2 + pltpu.roll(x, h2, 1) * b2

    nblk = N_DIL * DIL_HEADS
    slot = 0
    for s in range(3):
        for gi, (_, d) in enumerate(DIL_PATTERNS):
            dst = dil_refs[3 * gi + s]
            for h in range(DIL_HEADS):
                j = s * nblk + gi * DIL_HEADS + h
                x = c_ref[0, :, j * LANES:(j + 1) * LANES]
                if s < 2:
                    x = rope1(x.astype(F32))
                hs = slice(h * LANES, (h + 1) * LANES)
                if d == 1:
                    dst[0, 0, :, hs] = x.astype(BF16)
                else:
                    xs_ref[slot] = x.astype(F32)
                    for r in range(d):
                        dst[0, r, :, hs] = xs_ref[slot, pl.ds(r, tl // d, stride=d), :].astype(BF16)
                    slot += 1
    for j in range(SWA_Q_HEADS // 2):
        xr = rope2(dq_ref[0, :, j * LANES:(j + 1) * LANES].astype(F32)) * (SWA_HEAD_DIM ** -0.5 * LOG2E)
        dqe_ref[0, :, (2 * j) * LANES:(2 * j + 1) * LANES] = jnp.where(low, xr, 0.0).astype(BF16)
        dqe_ref[0, :, (2 * j + 1) * LANES:(2 * j + 2) * LANES] = jnp.where(low, pltpu.roll(xr, SWA_HEAD_DIM, 1), 0.0).astype(BF16)
    kr = rope2(dkv_ref[0, :, 0:LANES].astype(F32))
    vv = dkv_ref[0, :, LANES:2 * LANES].astype(F32)
    kr_sw = pltpu.roll(kr, SWA_HEAD_DIM, 1)
    vv_sw = pltpu.roll(vv, SWA_HEAD_DIM, 1)
    dk_ref[0, :, 0:LANES] = jnp.where(low, kr, 0.0).astype(BF16)
    dk_ref[0, :, LANES:2 * LANES] = jnp.where(low, kr_sw, 0.0).astype(BF16)
    dv_ref[0, :, 0:LANES] = jnp.where(low, vv, vv_sw).astype(BF16)
    dv_ref[0, :, LANES:2 * LANES] = jnp.where(low, vv_sw, vv).astype(BF16)


def _odd_prep(z, tl=256):
    b, l, _ = z.shape
    wc = 3 * N_DIL * DIL_WIDTH
    t1 = _rope_tables(l, DIL_HEAD_DIM)
    t2 = _rope_tables(l, SWA_HEAD_DIM)
    tab = pl.BlockSpec((tl, LANES), lambda i, t: (t, 0))

    def spec(w):
        return pl.BlockSpec((1, tl, w), lambda i, t: (i, t, 0))

    def shape(w):
        return jax.ShapeDtypeStruct((b, l, w), BF16)

    dil_specs, dil_shapes = [], []
    for _, d in DIL_PATTERNS:
        for _ in range(3):
            dil_specs.append(pl.BlockSpec((1, d, tl // d, DIL_WIDTH), lambda i, t: (i, 0, t, 0)))
            dil_shapes.append(jax.ShapeDtypeStruct((b, d, l // d, DIL_WIDTH), BF16))
    n_staged = 3 * DIL_HEADS * sum(1 for _, d in DIL_PATTERNS if d > 1)
    outs = pl.pallas_call(
        _odd_prep_kernel,
        grid=(b, l // tl),
        in_specs=[
            pl.BlockSpec((1, tl, wc), lambda i, t: (i, t, 0)),
            pl.BlockSpec((1, tl, SWA_WIDTH), lambda i, t: (i, t, O_DQ // SWA_WIDTH)),
            pl.BlockSpec((1, tl, 2 * LANES), lambda i, t: (i, t, O_DKV // (2 * LANES))),
            tab, tab, tab, tab, tab, tab,
        ],
        out_specs=dil_specs + [spec(2 * SWA_WIDTH), spec(2 * LANES), spec(2 * LANES)],
        out_shape=dil_shapes + [shape(2 * SWA_WIDTH), shape(2 * LANES), shape(2 * LANES)],
        scratch_shapes=[pltpu.VMEM((n_staged, tl, LANES), F32)],
        compiler_params=_cparams("parallel", "arbitrary"),
        name="odd_prep",
    )(z, z, z, *t1, *t2)
    dil = [tuple(outs[3 * gi:3 * gi + 3]) for gi in range(N_DIL)]
    return dil, outs[3 * N_DIL], outs[3 * N_DIL + 1], outs[3 * N_DIL + 2]


def _band_bias_t(tq, half, ls):
    p0 = pl.program_id(1) * tq
    j = lax.broadcasted_iota(jnp.int32, (tq + 2 * half, tq), 0)
    i = lax.broadcasted_iota(jnp.int32, (tq + 2 * half, tq), 1)
    kpos = p0 - half + j
    valid = (j >= i) & (j - i <= 2 * half) & (kpos >= 0) & (kpos < ls)
    return jnp.where(valid, 0.0, -jnp.inf)


def _band_mask(tq, half, ls):
    p0 = pl.program_id(1) * tq
    i = lax.broadcasted_iota(jnp.int32, (tq, tq + 2 * half), 0)
    j = lax.broadcasted_iota(jnp.int32, (tq, tq + 2 * half), 1)
    kpos = p0 - half + j
    return (j >= i) & (j - i <= 2 * half) & (kpos >= 0) & (kpos < ls)


def _band_attn_kernel(q_ref, kp_ref, kc_ref, kn_ref, vp_ref, vc_ref, vn_ref, o_ref, lse_ref, *, half, ls):
    tq = q_ref.shape[1]
    dh = DIL_HEAD_DIM
    valid = _band_mask(tq, half, ls)
    lane = lax.broadcasted_iota(jnp.int32, (tq, LANES), 1)
    lse_all = jnp.zeros((tq, LANES), F32)
    heads = range(DIL_HEADS)
    hsl = [slice(h * dh, (h + 1) * dh) for h in heads]
    kall = [jnp.concatenate([kp_ref[0, :, s], kc_ref[0, :, s], kn_ref[0, :, s]], axis=0) for s in hsl]
    vall = [jnp.concatenate([vp_ref[0, :, s], vc_ref[0, :, s], vn_ref[0, :, s]], axis=0) for s in hsl]
    sc = [jnp.where(valid, _dot_nt(q_ref[0, :, hsl[h]], kall[h]) * (dh ** -0.5), -jnp.inf) for h in heads]
    m = [jnp.max(sc[h], axis=-1, keepdims=True) for h in heads]
    p = [jnp.exp(sc[h] - m[h]) for h in heads]
    den = [jnp.sum(p[h], axis=-1, keepdims=True) for h in heads]
    o = [_dot(p[h].astype(BF16), vall[h]) / den[h] for h in heads]
    for h in heads:
        o_ref[0, :, hsl[h]] = o[h].astype(o_ref.dtype)
        lse_all = jnp.where(lane == h, m[h] + jnp.log(den[h]), lse_all)
    lse_ref[0] = lse_all


def _band_specs(tq, half, ls, w, col):
    r = tq // half
    nb = ls // half
    cur = pl.BlockSpec((1, tq, w), lambda i, t: (i, t, col))
    prev = pl.BlockSpec((1, half, w), lambda i, t: (i, jnp.maximum(t * r - 1, 0), col))
    nxt = pl.BlockSpec((1, half, w), lambda i, t: (i, jnp.minimum((t + 1) * r, nb - 1), col))
    return cur, prev, nxt


def _band_attn(q, k, v, half):
    n, ls, w = q.shape
    tq = min(2 * LANES, ls)
    cur, prev, nxt = _band_specs(tq, half, ls, w, 0)
    return pl.pallas_call(
        functools.partial(_band_attn_kernel, half=half, ls=ls),
        grid=(n, ls // tq),
        in_specs=[cur, prev, cur, nxt, prev, cur, nxt],
        out_specs=[cur, pl.BlockSpec((1, tq, LANES), lambda i, t: (i, t, 0))],
        out_shape=[jax.ShapeDtypeStruct((n, ls, w), BF16), jax.ShapeDtypeStruct((n, ls, LANES), F32)],
        compiler_params=_cparams("parallel", "arbitrary"),
        name="band_attn",
    )(q, k, k, k, v, v, v)


def _dil_merge_kernel(o0_ref, o1_ref, o2_ref, l0_ref, l1_ref, l2_ref, g_ref, y_ref, os_ref, ls_ref):
    dh = DIL_HEAD_DIM
    tl = y_ref.shape[1]
    o_refs = (o0_ref, o1_ref, o2_ref)
    l_refs = (l0_ref, l1_ref, l2_ref)
    slot = 0
    lse, outs = [], []
    for gi, (_, d) in enumerate(DIL_PATTERNS):
        if d == 1:
            lse.append(l_refs[gi][0, 0])
            outs.append([o_refs[gi][0, 0, :, h * dh:(h + 1) * dh].astype(F32) for h in range(DIL_HEADS)])
            continue
        for r in range(d):
            ls_ref[gi, pl.ds(r, tl // d, stride=d), :] = l_refs[gi][0, r]
        lse.append(ls_ref[gi])
        per_head = []
        for h in range(DIL_HEADS):
            for r in range(d):
                os_ref[slot, pl.ds(r, tl // d, stride=d), :] = o_refs[gi][0, r, :, h * dh:(h + 1) * dh].astype(F32)
            per_head.append(os_ref[slot])
            slot += 1
        outs.append(per_head)
    for h in range(DIL_HEADS):
        hs = slice(h * dh, (h + 1) * dh)
        ls = [x[:, h:h + 1] for x in lse]
        m = jnp.maximum(jnp.maximum(ls[0], ls[1]), ls[2])
        ws = [jnp.exp(x - m) for x in ls]
        den = ws[0] + ws[1] + ws[2]
        y = (ws[0] / den) * outs[0][h] + (ws[1] / den) * outs[1][h] + (ws[2] / den) * outs[2][h]
        y_ref[0, :, hs] = (y * _silu(g_ref[0, :, hs].astype(F32))).astype(y_ref.dtype)


def _dil_merge(outs, lses, z, tl=512):
    b, _, l, w = outs[0].shape
    o_specs = [pl.BlockSpec((1, d, tl // d, w), lambda i, t: (i, 0, t, 0)) for _, d in DIL_PATTERNS]
    l_specs = [pl.BlockSpec((1, d, tl // d, LANES), lambda i, t: (i, 0, t, 0)) for _, d in DIL_PATTERNS]
    n_staged = DIL_HEADS * sum(1 for _, d in DIL_PATTERNS if d > 1)
    return pl.pallas_call(
        _dil_merge_kernel,
        grid=(b, l // tl),
        in_specs=o_specs + l_specs + [pl.BlockSpec((1, tl, w), lambda i, t: (i, t, O_GC // DIL_WIDTH))],
        out_specs=pl.BlockSpec((1, tl, w), lambda i, t: (i, t, 0)),
        out_shape=jax.ShapeDtypeStruct((b, l, w), BF16),
        scratch_shapes=[pltpu.VMEM((n_staged, tl, LANES), F32), pltpu.VMEM((N_DIL, tl, LANES), F32)],
        compiler_params=_cparams("parallel", "arbitrary"),
        name="dil_merge",
    )(*outs, *lses, z)


def _dilated_branch(dil, z):
    outs, lses = [], []
    for (window, d), (q, k, v) in zip(DIL_PATTERNS, dil):
        b, _, ls, w = q.shape
        half = window // (2 * d)
        o, lse = _band_attn(q.reshape(b * d, ls, w), k.reshape(b * d, ls, w), v.reshape(b * d, ls, w), half)
        outs.append(o.reshape(b, d, ls, w))
        lses.append(lse.reshape(b, d, ls, LANES))
    return _dil_merge(outs, lses, z)


def _swa_kernel(q_ref, kp_ref, kc_ref, kn_ref, vp_ref, vc_ref, vn_ref, g_ref, sink_ref, o_ref, *, half, ls):
    tq = q_ref.shape[1]
    dh = SWA_HEAD_DIM
    bias1 = _band_bias_t(tq, half, ls)
    bias = jnp.concatenate([bias1, bias1], axis=1)
    b_prev, b_next = bias[:half], bias[half + tq:]
    low = lax.broadcasted_iota(jnp.int32, (LANES, tq), 0) < dh
    first = lax.broadcasted_iota(jnp.int32, (1, 2 * tq), 1) < tq
    grp = SWA_Q_HEADS // SWA_KV_HEADS
    for g in range(SWA_KV_HEADS):
        gs = slice(g * LANES, (g + 1) * LANES)
        kall = jnp.concatenate([kp_ref[0, :, gs], kc_ref[0, :, gs], kn_ref[0, :, gs]], axis=0)
        vt = _t_bf16(jnp.concatenate([vp_ref[0, :, gs], vc_ref[0, :, gs], vn_ref[0, :, gs]], axis=0))
        pairs = range(grp // 2)
        h0 = [g * grp + 2 * jp for jp in pairs]
        q2 = [q_ref[0, :, h * LANES:(h + 2) * LANES] for h in h0]
        q2 = [jnp.concatenate([x[:, :LANES], x[:, LANES:]], axis=0) for x in q2]
        sc = [_dot_nt(kall, q2[jp]) for jp in pairs]
        sc = [jnp.concatenate([s[:half] + b_prev, s[half:half + tq], s[half + tq:] + b_next], axis=0) for s in sc]
        sk = [jnp.where(first, sink_ref[:, h:h + 1], sink_ref[:, h + 1:h + 2]) * LOG2E for h in h0]
        sm = [_softmax_t(sc[jp], sk[jp]) for jp in pairs]
        ot = [_dot(vt, sm[jp][0]) / sm[jp][1] for jp in pairs]
        for jp in pairs:
            blk = (g * grp) // 2 + jp
            bs = slice(blk * LANES, (blk + 1) * LANES)
            y = jnp.where(low, ot[jp][:, :tq], ot[jp][:, tq:]).T * _silu(g_ref[0, :, bs].astype(F32))
            o_ref[0, :, bs] = y.astype(o_ref.dtype)


def _swa_branch(dqe, dk, dv, z, sink, tq=128):
    b, l, _ = dqe.shape
    half = SWA_HALF_WINDOW
    assert tq == half == LANES
    _, prev, nxt = _band_specs(tq, half, l, 2 * LANES, 0)
    cur = pl.BlockSpec((1, tq, 2 * LANES), lambda i, t: (i, t, 0))
    sink_p = jnp.pad(sink.reshape(1, SWA_Q_HEADS).astype(F32), ((0, 0), (0, LANES - SWA_Q_HEADS)))
    return pl.pallas_call(
        functools.partial(_swa_kernel, half=half, ls=l),
        grid=(b, l // tq),
        in_specs=[
            pl.BlockSpec((1, tq, 2 * SWA_WIDTH), lambda i, t: (i, t, 0)),
            prev, cur, nxt, prev, cur, nxt,
            pl.BlockSpec((1, tq, SWA_WIDTH), lambda i, t: (i, t, O_GD // SWA_WIDTH)),
            pl.BlockSpec((1, LANES), lambda i, t: (0, 0)),
        ],
        out_specs=pl.BlockSpec((1, tq, SWA_WIDTH), lambda i, t: (i, t, 0)),
        out_shape=jax.ShapeDtypeStruct((b, l, SWA_WIDTH), BF16),
        compiler_params=_cparams("parallel", "arbitrary"),
        name="swa",
    )(dqe, dk, dk, dk, dv, dv, dv, z, sink_p)


def _odd_layer(x, mem, pre_g, post_g, w_in, w_out, swa_sink, mem_g, w_mem_kv):
    b, l, d = x.shape
    x2d = x.reshape(b * l, d)
    w_re = jnp.concatenate([w_in[:, 0:6144], w_in[:, 6400:8448], w_in[:, 6144:6400]], axis=1)
    z = _norm_matmul(x2d, pre_g, w_re, O_ALL // 3, out_dtype=BF16, tm=1024, name="odd_in").reshape(b, l, O_ALL)
    dil, dqe, dk, dv = _odd_prep(z)
    y_c = _dilated_branch(dil, z)
    y_d = _swa_branch(dqe, dk, dv, z, swa_sink)
    kv = _mem_kv(mem, mem_g, w_mem_kv)
    y_x = _xattn(z, kv, O_XQ // X_WIDTH, O_GX // X_WIDTH)
    t = b * l
    out = _out_proj([y_c.reshape(t, -1), y_d.reshape(t, -1), y_x.reshape(t, -1)], w_out, post_g, x2d)
    return out.reshape(b, l, d)


def _trunk(x, mem, even_params, odd_params):
    x = _even_layer(x, mem, *[p[0] for p in even_params])
    return _odd_layer(x, mem, *[p[0] for p in odd_params])


def kernel(x_prompt, x_sample, mem_prompt, mem_sample, e_pre_g, e_post_g, e_w_in, e_w_out, hy_conv_w, hy_conv_b,
           hy_filt_w1, hy_filt_b1, hy_filt_w2, hy_filt_b2, hy_filt_w3, hy_freq, hy_skip, gdn_conv_w, gdn_A_log,
           gdn_dt_bias, gdn_norm_g, e_mem_g, e_w_mem_kv, o_pre_g, o_post_g, o_w_in, o_w_out, swa_sink, o_mem_g,
           o_w_mem_kv):
    even_params = (e_pre_g, e_post_g, e_w_in, e_w_out, hy_conv_w, hy_conv_b, hy_filt_w1, hy_filt_b1, hy_filt_w2,
                   hy_filt_b2, hy_filt_w3, hy_freq, hy_skip, gdn_conv_w, gdn_A_log, gdn_dt_bias, gdn_norm_g,
                   e_mem_g, e_w_mem_kv)
    odd_params = (o_pre_g, o_post_g, o_w_in, o_w_out, swa_sink, o_mem_g, o_w_mem_kv)
    y_prompt = _trunk(x_prompt, mem_prompt, even_params, odd_params)
    y_sample = _trunk(x_sample, mem_sample, even_params, odd_params)
    return (y_prompt, y_sample)
```

```python
import functools
import math

import jax
import jax.numpy as jnp
import numpy as np
from jax import lax
from jax.experimental import pallas as pl
from jax.experimental.pallas import tpu as pltpu

D_MODEL = 1024
EPS = 1e-6
ROPE_THETA = 500000.0
HY_WIDTH = 1024
HY_EMB = 33
HY_BANDS = (HY_EMB - 1) // 2
HY_FILT_HIDDEN = 64
HY_TARGET = 1e-2
HY_FAST_PCT = 0.3
HY_SLOW_PCT = 1.5
GDN_HEADS = 8
GDN_HEAD_DIM = 128
GDN_WIDTH = GDN_HEADS * GDN_HEAD_DIM
GDN_CHUNK = 64
X_HEADS = 4
X_HEAD_DIM = 128
X_WIDTH = X_HEADS * X_HEAD_DIM

LOG2E = math.log2(math.e)
LANES = 128
FFT_N2 = 128
VMEM_LIMIT_BYTES = 48 * 1024 * 1024

BF16 = jnp.bfloat16
F32 = jnp.float32


def _cparams(*sem):
    return pltpu.CompilerParams(dimension_semantics=sem, vmem_limit_bytes=VMEM_LIMIT_BYTES)


def _dot(a, b):
    return jnp.dot(a, b, preferred_element_type=F32)


def _silu(x):
    hx = 0.5 * x
    return hx + hx * jnp.tanh(hx)


def _norm_matmul_kernel(*refs, side):
    if side:
        x_ref, g_ref, w_ref, ws_ref, o_ref, os_ref, xn_ref = refs
    else:
        x_ref, g_ref, w_ref, o_ref, xn_ref = refs

    @pl.when(pl.program_id(1) == 0)
    def _():
        x = x_ref[...]
        ms = jnp.mean(x * x, axis=-1, keepdims=True)
        xn_ref[...] = (x * lax.rsqrt(ms + EPS) * g_ref[...]).astype(BF16)
        if side:
            os_ref[...] = _dot(xn_ref[...], ws_ref[...])

    o_ref[...] = _dot(xn_ref[...], w_ref[...]).astype(o_ref.dtype)


def _norm_matmul(x2d, g, w, tn, out_dtype=F32, tm=512, name="norm_matmul", w_side=None):
    t, d = x2d.shape
    n = w.shape[1]
    assert t % tm == 0 and n % tn == 0
    side = w_side is not None
    in_specs = [
        pl.BlockSpec((tm, d), lambda i, j: (i, 0)),
        pl.BlockSpec((1, d), lambda i, j: (0, 0)),
        pl.BlockSpec((d, tn), lambda i, j: (0, j)),
    ]
    out_specs = [pl.BlockSpec((tm, tn), lambda i, j: (i, j))]
    out_shape = [jax.ShapeDtypeStruct((t, n), out_dtype)]
    args = [x2d, g.reshape(1, d).astype(F32), w.astype(BF16)]
    if side:
        in_specs.append(pl.BlockSpec((d, LANES), lambda i, j: (0, 0)))
        out_specs.append(pl.BlockSpec((tm, LANES), lambda i, j: (i, 0)))
        out_shape.append(jax.ShapeDtypeStruct((t, LANES), F32))
        args.append(w_side.astype(BF16))
    outs = pl.pallas_call(
        functools.partial(_norm_matmul_kernel, side=side),
        grid=(t // tm, n // tn),
        in_specs=in_specs,
        out_specs=out_specs,
        out_shape=out_shape,
        scratch_shapes=[pltpu.VMEM((tm, d), BF16)],
        compiler_params=_cparams("parallel", "arbitrary"),
        name=name,
    )(*args)
    return tuple(outs) if side else outs[0]


HALO_ROWS = 16


def _fill_halo_buf(buf_ref, zc_ref, zp_ref, zn_ref, t, nt):
    tl = zc_ref.shape[1]
    hr = HALO_ROWS
    buf_ref[0:hr, :] = jnp.where(t > 0, zp_ref[0].astype(F32), 0.0)
    buf_ref[hr:hr + tl, :] = zc_ref[0].astype(F32)
    buf_ref[hr + tl:2 * hr + tl, :] = jnp.where(t < nt - 1, zn_ref[0].astype(F32), 0.0)


def _halo_specs(tl, w, l, col):
    r = tl // HALO_ROWS
    nb = l // HALO_ROWS
    cur = pl.BlockSpec((1, tl, w), lambda i, t: (i, t, col))
    prev = pl.BlockSpec((1, HALO_ROWS, w), lambda i, t: (i, jnp.maximum(t * r - 1, 0), col))
    nxt = pl.BlockSpec((1, HALO_ROWS, w), lambda i, t: (i, jnp.minimum((t + 1) * r, nb - 1), col))
    return cur, prev, nxt


def _hy_prep_kernel(zc_ref, zp_ref, zn_ref, g_ref, cw_ref, cb_ref, u_ref, e_ref, buf_ref):
    t = pl.program_id(1)
    nt = pl.num_programs(1)
    tl = zc_ref.shape[1]
    c = HY_WIDTH
    hr = HALO_ROWS
    _fill_halo_buf(buf_ref, zc_ref, zp_ref, zn_ref, t, nt)
    cw = 256
    for c0 in range(0, c, cw):
        parts = []
        for p in range(3):
            lo = p * c + c0
            acc = cb_ref[:, lo:lo + cw]
            for j in range(3):
                acc = acc + buf_ref[pl.ds(hr - 1 + j, tl), lo:lo + cw] * cw_ref[j:j + 1, lo:lo + cw]
            parts.append(acc)
        x0, x1, v = parts
        u_ref[0, :, c0:c0 + cw] = (v * x1).astype(u_ref.dtype)
        e_ref[0, :, c0:c0 + cw] = (x0 * _silu(g_ref[0, :, c0:c0 + cw].astype(F32))).astype(e_ref.dtype)


def _hy_prep(z, conv_w, conv_b, tl=512):
    b, l, _ = z.shape
    c = HY_WIDTH
    assert l % tl == 0
    cur, prev, nxt = _halo_specs(tl, 3 * c, l, E_HY // (3 * c))
    return pl.pallas_call(
        _hy_prep_kernel,
        grid=(b, l // tl),
        in_specs=[
            cur, prev, nxt,
            pl.BlockSpec((1, tl, c), lambda i, t: (i, t, E_GHY // c)),
            pl.BlockSpec((3, 3 * c), lambda i, t: (0, 0)),
            pl.BlockSpec((1, 3 * c), lambda i, t: (0, 0)),
        ],
        out_specs=[
            pl.BlockSpec((1, tl, c), lambda i, t: (i, t, 0)),
            pl.BlockSpec((1, tl, c), lambda i, t: (i, t, 0)),
        ],
        out_shape=[jax.ShapeDtypeStruct((b, l, c), F32), jax.ShapeDtypeStruct((b, l, c), F32)],
        scratch_shapes=[pltpu.VMEM((tl + 2 * HALO_ROWS, 3 * c), F32)],
        compiler_params=_cparams("parallel", "arbitrary"),
        name="hy_prep",
    )(z, z, z, z, conv_w.astype(F32), conv_b.reshape(1, 3 * c).astype(F32))


def _hy_filter_tables(l):
    t = jnp.linspace(0.0, 1.0, l, dtype=F32)[:, None]
    w = (2.0 * math.pi / l) * jnp.arange(l, dtype=F32)[:, None]
    f = jnp.linspace(1e-4, HY_BANDS - 1, HY_BANDS, dtype=F32)[None, :]
    emb = jnp.concatenate([t, jnp.cos(f * w), -jnp.sin(f * w)], axis=-1)
    emb = jnp.pad(emb, ((0, 0), (0, LANES - HY_EMB)))
    deltas = jnp.abs(jnp.linspace(math.log(HY_TARGET) / HY_SLOW_PCT, math.log(HY_TARGET) / HY_FAST_PCT, HY_WIDTH, dtype=F32))
    decay = jnp.exp(-t * jnp.tile(deltas, 2)[None, :])
    return emb, decay


def _hy_filter_kernel(emb_ref, dec_ref, w1_ref, b1_ref, w2_ref, b2_ref, w3_ref, fr_ref, sk_ref, o_ref):
    c = HY_WIDTH
    hp = lax.Precision.HIGHEST
    fr = fr_ref[...]
    hid = jnp.sin(fr * (jnp.dot(emb_ref[...], w1_ref[...], precision=hp, preferred_element_type=F32) + b1_ref[...]))
    hid = jnp.sin(fr * (jnp.dot(hid, w2_ref[...], precision=hp, preferred_element_type=F32) + b2_ref[...]))
    tl = emb_ref.shape[0]
    row = lax.broadcasted_iota(jnp.int32, (tl, 1), 0) + pl.program_id(0) * tl
    first = row == 0
    cw = 512
    for c0 in range(0, 2 * c, cw):
        filt = jnp.dot(hid, w3_ref[:, c0:c0 + cw], precision=hp, preferred_element_type=F32) * dec_ref[:, c0:c0 + cw]
        if c0 < c:
            filt = jnp.where(first, filt + sk_ref[:, c0:c0 + cw], filt)
            o_ref[0, :, c0:c0 + cw] = filt.astype(o_ref.dtype)
        else:
            filt = jnp.where(first, 0.0, filt)
            o_ref[1, :, c0 - c:c0 - c + cw] = filt.astype(o_ref.dtype)


def _hy_filters(l, w1, b1, w2, b2, w3, freq, skip, tl=256):
    c = HY_WIDTH
    hdim = HY_FILT_HIDDEN
    emb, decay = _hy_filter_tables(l)
    w1p = jnp.pad(w1.astype(F32), ((0, LANES - HY_EMB), (0, 0)))
    return pl.pallas_call(
        _hy_filter_kernel,
        grid=(l // tl,),
        in_specs=[
            pl.BlockSpec((tl, LANES), lambda t: (t, 0)),
            pl.BlockSpec((tl, 2 * c), lambda t: (t, 0)),
            pl.BlockSpec((LANES, hdim), lambda t: (0, 0)),
            pl.BlockSpec((1, hdim), lambda t: (0, 0)),
            pl.BlockSpec((hdim, hdim), lambda t: (0, 0)),
            pl.BlockSpec((1, hdim), lambda t: (0, 0)),
            pl.BlockSpec((hdim, 2 * c), lambda t: (0, 0)),
            pl.BlockSpec((1, hdim), lambda t: (0, 0)),
            pl.BlockSpec((1, c), lambda t: (0, 0)),
        ],
        out_specs=pl.BlockSpec((2, tl, c), lambda t: (0, t, 0)),
        out_shape=jax.ShapeDtypeStruct((2, l, c), BF16),
        compiler_params=_cparams("arbitrary"),
        name="hy_filter",
    )(emb, decay, w1p, b1.reshape(1, hdim).astype(F32), w2.astype(F32), b2.reshape(1, hdim).astype(F32),
      w3.astype(F32), freq.reshape(1, hdim).astype(F32), skip.reshape(1, c).astype(F32))


def _fft_consts(l):
    n = 2 * l
    n2 = FFT_N2
    n1 = n // n2
    n1h = n1 // 2
    w0 = 2.0 * math.pi / n
    k1 = jnp.arange(n1, dtype=jnp.int32)
    m1 = jnp.arange(n1h, dtype=jnp.int32)
    m2 = jnp.arange(n2, dtype=jnp.int32)
    ea = (n2 * m1[None, None, :] * k1[None, :, None] + m2[:, None, None] * k1[None, :, None]) % n
    ang = ea.astype(F32) * w0
    fa = jnp.concatenate([jnp.cos(ang), -jnp.sin(ang)], axis=1).astype(BF16)
    eb = (n1 * m2[:, None] * m2[None, :]) % n
    angb = eb.astype(F32) * w0
    fr, fi = jnp.cos(angb), -jnp.sin(angb)
    fb = jnp.concatenate([jnp.concatenate([fr, -fi], axis=1), jnp.concatenate([fi, fr], axis=1)], axis=0).astype(BF16)
    eg = (n1 * m2[None, :, None] * m2[None, None, :] + m2[None, :, None] * k1[:, None, None]) % n
    angg = eg.astype(F32) * w0
    gr, gi = jnp.cos(angg), jnp.sin(angg)
    gb = jnp.concatenate([jnp.concatenate([gr, -gi], axis=2), jnp.concatenate([gi, gr], axis=2)], axis=1).astype(BF16)
    ec = (n2 * m1[:, None] * k1[None, :]) % n
    angc = ec.astype(F32) * w0
    fc = (jnp.concatenate([jnp.cos(angc), -jnp.sin(angc)], axis=1) * (1.0 / n)).astype(BF16)
    return fa, fb, gb, fc


def _fft_a_kernel(u_ref, f_ref, o_ref):
    tn2 = f_ref.shape[0]
    n1 = f_ref.shape[1] // 2
    c = u_ref.shape[2] // tn2
    for j in range(tn2):
        a = _dot(f_ref[j], u_ref[0, :, j * c:(j + 1) * c])
        o_ref[0, 0, :, j * c:(j + 1) * c] = a[:n1].astype(o_ref.dtype)
        o_ref[0, 1, :, j * c:(j + 1) * c] = a[n1:].astype(o_ref.dtype)


def _fft_a(u, fa, tn2=8):
    b, l, c = u.shape
    n2, n1x2, n1h = fa.shape
    n1 = n1x2 // 2
    uv = u.reshape(b, n1h, n2 * c)
    return pl.pallas_call(
        _fft_a_kernel,
        grid=(b, n2 // tn2),
        in_specs=[
            pl.BlockSpec((1, n1h, tn2 * c), lambda i, j: (i, 0, j)),
            pl.BlockSpec((tn2, n1x2, n1h), lambda i, j: (j, 0, 0)),
        ],
        out_specs=pl.BlockSpec((1, 2, n1, tn2 * c), lambda i, j: (i, 0, 0, j)),
        out_shape=jax.ShapeDtypeStruct((b, 2, n1, n2 * c), BF16),
        compiler_params=_cparams("parallel", "arbitrary"),
        name="fft_a",
    )(uv, fa)


def _fft_bf_kernel(a_ref, w_ref, o_ref):
    n2 = a_ref.shape[2]
    ct = a_ref.shape[3]
    cw = 256
    for c0 in range(0, ct, cw):
        xf = _dot(w_ref[...], a_ref[0, :, :, c0:c0 + cw].reshape(2 * n2, cw))
        xb = _dot(w_ref[...], a_ref[1, :, :, c0:c0 + cw].reshape(2 * n2, cw))
        o_ref[0, :, c0:c0 + cw] = xf[:n2] + xb[:n2]
        o_ref[1, :, c0:c0 + cw] = xf[n2:] - xb[n2:]


def _fft_bf(a, fb):
    _, _, n, c = a.shape
    n2 = FFT_N2
    return pl.pallas_call(
        _fft_bf_kernel,
        grid=(n // n2,),
        in_specs=[
            pl.BlockSpec((2, 2, n2, c), lambda k: (0, 0, k, 0)),
            pl.BlockSpec((2 * n2, 2 * n2), lambda k: (0, 0)),
        ],
        out_specs=pl.BlockSpec((2, n2, c), lambda k: (0, k, 0)),
        out_shape=jax.ShapeDtypeStruct((2, n, c), F32),
        compiler_params=_cparams("arbitrary"),
        name="fft_bf",
    )(a, fb)


SUBLANES = 8
CONV_SLABS = 2


def _conv_dims(n1):
    kh = n1 // 2 + 1
    kp = -(-kh // SUBLANES) * SUBLANES
    return kh, kp, 2 * kp + SUBLANES


def _hy_conv_kernel(u_ref, h_ref, la_ref, w_ref, g_ref, fc_ref, o_ref, s_ref, *, n1, unroll_b):
    n2 = FFT_N2
    n1h = n1 // 2
    kh, kp, pitch = _conv_dims(n1)
    grp = SUBLANES
    slabs = range(CONV_SLABS)

    def lanes(x, sl):
        return x[:, sl * LANES:(sl + 1) * LANES]

    def stage_a(jj, carry):
        rows = [u_ref[0, pl.ds(pl.multiple_of(m * n2 + jj * grp, grp), grp), :] for m in range(n1h)]
        a = _dot(la_ref[jj], jnp.concatenate(rows, axis=0).astype(BF16))
        for j0 in range(grp):
            dst = pl.multiple_of((jj * grp + j0) * pitch, grp)
            for sl in slabs:
                s_ref[sl, pl.ds(dst, 2 * kp), :] = lanes(a[j0 * 2 * kp:(j0 + 1) * 2 * kp], sl)
        return carry

    lax.fori_loop(0, n2 // grp, stage_a, 0)

    def stage_b(kk, carry):
        k1s = [jnp.minimum(kk * unroll_b + i, kh - 1) for i in range(unroll_b)]
        ar = [jnp.concatenate([s_ref[sl, pl.ds(k, n2, stride=pitch), :] for sl in slabs], axis=1) for k in k1s]
        ai = [jnp.concatenate([s_ref[sl, pl.ds(kp + k, n2, stride=pitch), :] for sl in slabs], axis=1) for k in k1s]
        x = [_dot(w_ref[...], jnp.concatenate([ar[i], ai[i]], axis=0).astype(BF16)) for i in range(unroll_b)]
        y = []
        for i, k in enumerate(k1s):
            off = pl.multiple_of(k * n2, n2)
            hr, hi = h_ref[0, pl.ds(off, n2), :], h_ref[1, pl.ds(off, n2), :]
            xr, xi = x[i][:n2], x[i][n2:]
            y.append(jnp.concatenate([xr * hr - xi * hi, xr * hi + xi * hr], axis=0).astype(BF16))
        z = [_dot(g_ref[k1s[i]], y[i]) for i in range(unroll_b)]
        for i, k in enumerate(k1s):
            for sl in slabs:
                s_ref[sl, pl.ds(k, n2, stride=pitch), :] = lanes(z[i][:n2], sl)
                s_ref[sl, pl.ds(kp + k, n2, stride=pitch), :] = lanes(z[i][n2:], sl)
        return carry

    lax.fori_loop(0, -(-kh // unroll_b), stage_b, 0)

    def stage_c(jj, carry):
        zz = [jnp.concatenate([s_ref[sl, pl.ds(pl.multiple_of((jj * grp + j0) * pitch, grp), 2 * kp), :]
                               for sl in slabs], axis=1) for j0 in range(grp)]
        y = _dot(fc_ref[...], jnp.concatenate(zz, axis=0).astype(BF16))
        for m in range(n1h):
            o_ref[0, pl.ds(pl.multiple_of(m * n2 + jj * grp, grp), grp), :] = y[m * grp:(m + 1) * grp]
        return carry

    lax.fori_loop(0, n2 // grp, stage_c, 0)


def _hy_conv(u, h, fa, fb, gb, fc):
    b, l, c = u.shape
    n2, n1x2, n1h = fa.shape
    n1 = n1x2 // 2
    kh, kp, pitch = _conv_dims(n1)
    grp = SUBLANES
    ct = CONV_SLABS * LANES

    def half(t, axis):
        re, im = jnp.split(t, 2, axis=axis)
        pad = [(0, 0)] * t.ndim
        pad[axis] = (0, kp - kh)
        cut = lambda x: jnp.pad(lax.slice_in_dim(x, 0, kh, axis=axis), pad)
        return jnp.concatenate([cut(re), cut(im)], axis=axis)

    fa_h = half(fa, 1)
    la = (fa_h.reshape(n2 // grp, grp, 2 * kp, n1h, 1) * jnp.eye(grp, dtype=fa.dtype).reshape(1, grp, 1, 1, grp))
    la = la.reshape(n2 // grp, grp * 2 * kp, n1h * grp)
    wgt = jnp.ones((kh,), F32).at[1:kh - 1].set(2.0)
    wgt = jnp.pad(wgt, (0, kp - kh))
    fc_h = (half(fc, 1).astype(F32) * jnp.tile(wgt, 2)[None, :]).astype(BF16)
    lc = fc_h.reshape(n1h, 1, 1, 2 * kp) * jnp.eye(grp, dtype=fc_h.dtype).reshape(1, grp, grp, 1)
    fc_h = lc.reshape(n1h * grp, grp * 2 * kp)
    gb_h = gb[:kh]
    unroll_b = 3
    once = pl.Buffered(1)
    return pl.pallas_call(
        functools.partial(_hy_conv_kernel, n1=n1, unroll_b=unroll_b),
        grid=(c // ct, b),
        in_specs=[
            pl.BlockSpec((1, l, ct), lambda j, i: (i, 0, j)),
            pl.BlockSpec((2, kh * n2, ct), lambda j, i: (0, 0, j), pipeline_mode=once),
            pl.BlockSpec(la.shape, lambda j, i: (0, 0, 0), pipeline_mode=once),
            pl.BlockSpec(fb.shape, lambda j, i: (0, 0), pipeline_mode=once),
            pl.BlockSpec(gb_h.shape, lambda j, i: (0, 0, 0), pipeline_mode=once),
            pl.BlockSpec(fc_h.shape, lambda j, i: (0, 0), pipeline_mode=once),
        ],
        out_specs=pl.BlockSpec((1, l, ct), lambda j, i: (i, 0, j)),
        out_shape=jax.ShapeDtypeStruct((b, l, c), F32),
        scratch_shapes=[pltpu.VMEM((CONV_SLABS, n2 * pitch, LANES), F32)],
        compiler_params=_cparams("arbitrary", "arbitrary"),
        name="hy_conv",
    )(u, h, la, fb, gb_h, fc_h)


def _hyena_branch(z, l, conv_w, conv_b, w1, b1, w2, b2, w3, freq, skip):
    c = HY_WIDTH
    n = 2 * l
    fa, fb, gb, fc = _fft_consts(l)
    filt = _hy_filters(l, w1, b1, w2, b2, w3, freq, skip)
    fa_spec = _fft_a(filt, fa).reshape(2, 2, n, c)
    h = _fft_bf(fa_spec, fb)
    u, e = _hy_prep(z, conv_w, conv_b)
    return _hy_conv(u, h, fa, fb, gb, fc), e


def _gdn_prep_kernel(zc_ref, zp_ref, zn_ref, cw_ref, q_ref, k_ref, v_ref, buf_ref):
    t = pl.program_id(1)
    nt = pl.num_programs(1)
    tl = zc_ref.shape[1]
    dh = GDN_HEAD_DIM
    hr = HALO_ROWS
    _fill_halo_buf(buf_ref, zc_ref, zp_ref, zn_ref, t, nt)
    outs = (q_ref, k_ref, v_ref)
    for p in range(3):
        for h in range(GDN_HEADS):
            lo = p * GDN_WIDTH + h * dh
            acc = buf_ref[pl.ds(hr - 2, tl), lo:lo + dh] * cw_ref[0:1, lo:lo + dh]
            for j in range(1, 5):
                acc = acc + buf_ref[pl.ds(hr - 2 + j, tl), lo:lo + dh] * cw_ref[j:j + 1, lo:lo + dh]
            a = _silu(acc)
            if p < 2:
                a = a * lax.rsqrt(jnp.sum(a * a, axis=-1, keepdims=True) + EPS)
            if p == 0:
                a = a * (dh ** -0.5)
            outs[p][0, :, h * dh:(h + 1) * dh] = a.astype(BF16)


def _gdn_prep(z, conv_w, tl=512):
    b, l, _ = z.shape
    w = 3 * GDN_WIDTH
    cur, prev, nxt = _halo_specs(tl, w, l, E_QKV // w)
    o_spec = pl.BlockSpec((1, tl, GDN_WIDTH), lambda i, t: (i, t, 0))
    o_shape = jax.ShapeDtypeStruct((b, l, GDN_WIDTH), BF16)
    return pl.pallas_call(
        _gdn_prep_kernel,
        grid=(b, l // tl),
        in_specs=[cur, prev, nxt, pl.BlockSpec((5, w), lambda i, t: (0, 0))],
        out_specs=[o_spec, o_spec, o_spec],
        out_shape=[o_shape, o_shape, o_shape],
        scratch_shapes=[pltpu.VMEM((tl + 2 * HALO_ROWS, w), F32)],
        compiler_params=_cparams("parallel", "arbitrary"),
        name="gdn_prep",
    )(z, z, z, conv_w.astype(F32))


def _dot_nt(a, b):
    return lax.dot_general(a, b, (((1,), (1,)), ((), ())), preferred_element_type=F32)


def _dot_tn(a, b):
    return lax.dot_general(a, b, (((0,), (0,)), ((), ())), preferred_element_type=F32)


def _split3(x):
    a = x.astype(BF16)
    r = x - a.astype(F32)
    b = r.astype(BF16)
    c = (r - b.astype(F32)).astype(BF16)
    return a, b, c


GDN_CHUNKS_PER_STEP = 8


def _gdn_scan_kernel(*refs, direction, final):
    if final:
        q_ref, k_ref, v_ref, zs_ref, al_ref, dt_ref, of_ref, gg_ref, ng_ref, o_ref, s_ref = refs
    else:
        q_ref, k_ref, v_ref, zs_ref, al_ref, dt_ref, o_ref, s_ref = refs
    c = GDN_CHUNK
    dh = GDN_HEAD_DIM
    nsub = q_ref.shape[1] // c

    @pl.when(pl.program_id(1) == 0)
    def _():
        s_ref[...] = jnp.zeros_like(s_ref)

    row = lax.broadcasted_iota(jnp.int32, (c, LANES), 0)
    col = lax.broadcasted_iota(jnp.int32, (c, LANES), 1)
    if direction == 0:
        incl, strict, last = col <= row, col < row, c - 1
    else:
        incl, strict, last = (col >= row) & (col < c), (col > row) & (col < c), 0
    tri = incl[:, :c].astype(BF16)
    eye = (row == col).astype(F32)
    zc = jnp.zeros((c, LANES), BF16)
    zd = jnp.zeros((dh, dh), BF16)

    def lanes2(xa, xb):
        return jnp.concatenate([xa, xb], axis=1)

    def blockdiag_slots(pa, pb):
        return jnp.concatenate([lanes2(pa, zc), lanes2(zc, zc), lanes2(zc, pb), lanes2(zc, zc)], axis=0)

    heads = range(GDN_HEADS)
    hsl = [slice(h * dh, (h + 1) * dh) for h in heads]
    lns = [2 * GDN_HEADS + direction * GDN_HEADS + h for h in heads]

    rows = [slice(i * c, (i + 1) * c) for i in range(nsub)]
    gcum, gcum_t, beta_all = [], [], []
    for rs in rows:
        zs = zs_ref[0, rs, :]
        beta_all.append(jax.nn.sigmoid(zs))
        gl = -jnp.exp(al_ref[...]) * jax.nn.softplus(zs + dt_ref[...])
        g1, g2, g3 = _split3(gl)
        gsum = _dot(tri, g1) + _dot(tri, g2) + _dot(tri, g3)
        gcum.append(gsum)
        gcum_t.append(jnp.concatenate([gsum, jnp.zeros_like(gsum)], axis=0).T)
    items = [(i, h) for i in range(nsub) for h in heads]
    it = range(len(items))
    beta = [beta_all[i][:, direction * GDN_HEADS + h:direction * GDN_HEADS + h + 1] for i, h in items]
    gc = [gcum[i][:, lns[h]:lns[h] + 1] for i, h in items]
    gct = [gcum_t[i][lns[h]:lns[h] + 1, :] for i, h in items]
    glast = [gcum_t[i][lns[h]:lns[h] + 1, last:last + 1] for i, h in items]
    qh = [q_ref[0, rows[i], hsl[h]] for i, h in items]
    kh = [k_ref[0, rows[i], hsl[h]] for i, h in items]
    vh = [v_ref[0, rows[i], hsl[h]] for i, h in items]
    pr = range(len(items) // 2)
    kq2 = [_dot_nt(lanes2(jnp.concatenate([kh[2 * y], qh[2 * y]], axis=0),
                          jnp.concatenate([kh[2 * y + 1], qh[2 * y + 1]], axis=0)),
                   jnp.concatenate([lanes2(kh[2 * y], zd[:c]), lanes2(zc, zc),
                                    lanes2(zd[:c], kh[2 * y + 1]), lanes2(zc, zc)], axis=0)) for y in pr]
    kq = [kq2[x // 2][:, (x % 2) * LANES:(x % 2 + 1) * LANES] for x in it]
    dmat = [jnp.exp(jnp.where(incl, gc[x] - gct[x], -jnp.inf)) for x in it]
    a = [jnp.where(strict, (beta[x] * kq[x][:c]) * dmat[x], 0.0) for x in it]
    tinv2 = [lanes2(eye - a[2 * y], eye - a[2 * y + 1]) for y in pr]
    ab = [a[x].astype(BF16) for x in it]
    p2 = [_dot(lanes2(ab[2 * y], ab[2 * y + 1]), blockdiag_slots(ab[2 * y], ab[2 * y + 1])) for y in pr]
    for j in range(5):
        pb2 = [p2[y].astype(BF16) for y in pr]
        bd = [blockdiag_slots(pb2[y][:, :LANES], pb2[y][:, LANES:]) for y in pr]
        if j < 4:
            tp = [_dot(jnp.concatenate([tinv2[y].astype(BF16), pb2[y]], axis=0), bd[y]) for y in pr]
            tinv2 = [tinv2[y] + tp[y][:c] for y in pr]
            p2 = [tp[y][c:] for y in pr]
        else:
            tinv2 = [tinv2[y] + _dot(tinv2[y].astype(BF16), bd[y]) for y in pr]
    tinv = [tinv2[x // 2][:, (x % 2) * LANES:(x % 2 + 1) * LANES].astype(BF16) for x in it]
    eg = [jnp.exp(gc[x]) for x in it]
    kf = [kh[x].astype(F32) for x in it]
    rhs = [jnp.concatenate([vh[x].astype(F32) * beta[x], kf[x] * (beta[x] * eg[x])], axis=1).astype(BF16) for x in it]
    zr = jnp.zeros((LANES - c, 2 * dh), BF16)
    uw = [_dot(tinv[x], jnp.concatenate([rhs[x], zr], axis=0)) for x in it]
    wq = [jnp.concatenate([uw[x][:, dh:], qh[x].astype(F32) * eg[x]], axis=0).astype(BF16) for x in it]
    qkd = [(kq[x][c:] * dmat[x]).astype(BF16) for x in it]
    kd = [(kf[x] * jnp.exp(glast[x] - gc[x])).astype(BF16) for x in it]

    order = list(range(nsub)) if direction == 0 else list(range(nsub - 1, -1, -1))
    state = [s_ref[h] for h in heads]
    hp = range(GDN_HEADS // 2)
    for i in order:
        rs = rows[i]
        xs = [i * GDN_HEADS + h for h in heads]
        sb = [state[h].astype(BF16) for h in heads]
        ws2 = [_dot(lanes2(wq[xs[2 * y]], wq[xs[2 * y + 1]]),
                    jnp.concatenate([lanes2(sb[2 * y], zd), lanes2(zd, sb[2 * y + 1])], axis=0)) for y in hp]
        ws = [ws2[h // 2][:, (h % 2) * dh:(h % 2 + 1) * dh] for h in heads]
        vnb = [(uw[xs[h]][:, :dh] - ws[h][:c]).astype(BF16) for h in heads]
        o2 = [_dot(lanes2(qkd[xs[2 * y]], qkd[xs[2 * y + 1]]),
                   jnp.concatenate([lanes2(vnb[2 * y], zd[:c]), lanes2(zc, zc),
                                    lanes2(zd[:c], vnb[2 * y + 1]), lanes2(zc, zc)], axis=0)) for y in hp]
        o = [ws[h][c:] + o2[h // 2][:, (h % 2) * dh:(h % 2 + 1) * dh] for h in heads]
        kv2 = [_dot_tn(jnp.concatenate([kd[xs[2 * y]], kd[xs[2 * y + 1]]], axis=0),
                       jnp.concatenate([lanes2(vnb[2 * y], zd[:c]), lanes2(zd[:c], vnb[2 * y + 1])], axis=0)) for y in hp]
        state = [state[h] * jnp.exp(glast[xs[h]]) + kv2[h // 2][:, (h % 2) * dh:(h % 2 + 1) * dh] for h in heads]
        for h in heads:
            if final:
                tot = of_ref[0, rs, hsl[h]] + o[h]
                y = tot * lax.rsqrt(jnp.mean(tot * tot, axis=-1, keepdims=True) + EPS) * ng_ref[...]
                o_ref[0, rs, hsl[h]] = (y * _silu(gg_ref[0, rs, hsl[h]].astype(F32))).astype(o_ref.dtype)
            else:
                o_ref[0, rs, hsl[h]] = o[h].astype(o_ref.dtype)
    for h in heads:
        s_ref[h] = state[h]


def _gdn_scan(q, k, v, zs, a_log, dt_bias, direction, o_fwd=None, z=None, gate_blk=None, norm_g=None):
    b, l, w = q.shape
    c = GDN_CHUNK * GDN_CHUNKS_PER_STEP
    n = l // c
    final = o_fwd is not None
    if direction == 0:
        cmap = lambda i, t: (i, t, 0)
    else:
        cmap = lambda i, t: (i, n - 1 - t, 0)
    blk = pl.BlockSpec((1, c, w), cmap)
    vec = pl.BlockSpec((1, LANES), lambda i, t: (0, 0))
    pad = lambda x: jnp.pad(x.reshape(1, -1).astype(F32), ((0, 0), (2 * GDN_HEADS, LANES - 4 * GDN_HEADS)))
    in_specs = [blk, blk, blk, pl.BlockSpec((1, c, LANES), cmap), vec, vec]
    args = [q, k, v, zs, pad(a_log), pad(dt_bias)]
    if final:
        if direction == 0:
            gmap = lambda i, t: (i, t, gate_blk)
        else:
            gmap = lambda i, t: (i, n - 1 - t, gate_blk)
        in_specs += [blk, pl.BlockSpec((1, c, w), gmap), pl.BlockSpec((1, GDN_HEAD_DIM), lambda i, t: (0, 0))]
        args += [o_fwd, z, norm_g.reshape(1, GDN_HEAD_DIM).astype(F32)]
    return pl.pallas_call(
        functools.partial(_gdn_scan_kernel, direction=direction, final=final),
        grid=(b, n),
        in_specs=in_specs,
        out_specs=blk,
        out_shape=jax.ShapeDtypeStruct((b, l, w), BF16 if final else F32),
        scratch_shapes=[pltpu.VMEM((GDN_HEADS, GDN_HEAD_DIM, GDN_HEAD_DIM), F32)],
        compiler_params=_cparams("parallel", "arbitrary"),
        name="gdn_scan_bwd" if direction else "gdn_scan_fwd",
    )(*args)


def _gdn_branch(z, zs, conv_w, a_log, dt_bias, norm_g, gate_blk):
    q, k, v = _gdn_prep(z, conv_w)
    o_f = _gdn_scan(q, k, v, zs, a_log, dt_bias, 0)
    return _gdn_scan(q, k, v, zs, a_log, dt_bias, 1, o_fwd=o_f, z=z, gate_blk=gate_blk, norm_g=norm_g)


def _softmax_t(s, extra=None):
    m = jnp.max(s, axis=0, keepdims=True)
    if extra is not None:
        m = jnp.maximum(m, extra)
    p = jnp.exp2(s - m)
    den = jnp.sum(p, axis=0, keepdims=True)
    if extra is not None:
        den = den + jnp.exp2(extra - m)
    return p.astype(BF16), den, m


def _t_bf16(x):
    return x.astype(F32).T.astype(BF16)


def _xattn_kernel(q_ref, g_ref, kv_ref, o_ref):
    dh = X_HEAD_DIM
    for h in range(X_HEADS):
        hs = slice(h * dh, (h + 1) * dh)
        qh = q_ref[0, :, hs].astype(BF16)
        kh = kv_ref[0, :, hs]
        vh = kv_ref[0, :, X_WIDTH + h * dh:X_WIDTH + (h + 1) * dh]
        s = _dot_nt(qh, kh) * (dh ** -0.5)
        p = jnp.exp(s - jnp.max(s, axis=-1, keepdims=True))
        den = jnp.sum(p, axis=-1, keepdims=True)
        o = _dot(p.astype(BF16), vh) / den
        o_ref[0, :, hs] = (o * _silu(g_ref[0, :, hs].astype(F32))).astype(o_ref.dtype)


def _xattn(z, kv, q_blk, g_blk, tq=512):
    b, l, _ = z.shape
    m = kv.shape[1]
    w = X_WIDTH
    return pl.pallas_call(
        _xattn_kernel,
        grid=(b, l // tq),
        in_specs=[
            pl.BlockSpec((1, tq, w), lambda i, t: (i, t, q_blk)),
            pl.BlockSpec((1, tq, w), lambda i, t: (i, t, g_blk)),
            pl.BlockSpec((1, m, 2 * w), lambda i, t: (i, 0, 0)),
        ],
        out_specs=pl.BlockSpec((1, tq, w), lambda i, t: (i, t, 0)),
        out_shape=jax.ShapeDtypeStruct((b, l, w), BF16),
        compiler_params=_cparams("parallel", "arbitrary"),
        name="xattn",
    )(z, z, kv)


def _out_proj_kernel(*refs, widths, gated):
    nparts = len(widths)
    y_refs = refs[:nparts]
    if gated:
        e_ref, w_ref, g_ref, x_ref, o_ref = refs[nparts:]
    else:
        w_ref, g_ref, x_ref, o_ref = refs[nparts:]
    d = o_ref.shape[1]
    cw = 256
    ys = [y_ref[...] for y_ref in y_refs]
    if gated:
        ys[0] = ys[0] * e_ref[...]
    ys = [y.astype(BF16) for y in ys]
    ssq = jnp.zeros((o_ref.shape[0], 1), F32)
    for c0 in range(0, d, cw):
        acc = None
        off = 0
        for y, wd in zip(ys, widths):
            part = _dot(y, w_ref[off:off + wd, c0:c0 + cw])
            acc = part if acc is None else acc + part
            off += wd
        ssq = ssq + jnp.sum(acc * acc, axis=-1, keepdims=True)
        o_ref[:, c0:c0 + cw] = acc
    r = lax.rsqrt(ssq * (1.0 / d) + EPS)
    for c0 in range(0, d, cw):
        o_ref[:, c0:c0 + cw] = x_ref[:, c0:c0 + cw] + o_ref[:, c0:c0 + cw] * r * g_ref[:, c0:c0 + cw]


def _out_proj(parts, w_out, post_g, x2d, gate=None, tm=512):
    t, d = x2d.shape
    widths = tuple(int(p.shape[1]) for p in parts)
    kdim = sum(widths)
    gates = [] if gate is None else [gate]
    return pl.pallas_call(
        functools.partial(_out_proj_kernel, widths=widths, gated=gate is not None),
        grid=(t // tm,),
        in_specs=[pl.BlockSpec((tm, wd), lambda i: (i, 0)) for wd in widths]
        + [pl.BlockSpec((tm, widths[0]), lambda i: (i, 0)) for _ in gates] + [
            pl.BlockSpec((kdim, d), lambda i: (0, 0)),
            pl.BlockSpec((1, d), lambda i: (0, 0)),
            pl.BlockSpec((tm, d), lambda i: (i, 0)),
        ],
        out_specs=pl.BlockSpec((tm, d), lambda i: (i, 0)),
        out_shape=jax.ShapeDtypeStruct((t, d), F32),
        compiler_params=_cparams("parallel"),
        name="out_proj",
    )(*parts, *gates, w_out.astype(BF16), post_g.reshape(1, d).astype(F32), x2d)


def _mem_kv(mem, mem_g, w_mem_kv):
    b, m, d = mem.shape
    kv = _norm_matmul(mem.reshape(b * m, d), mem_g, w_mem_kv, 1024, out_dtype=BF16, name="mem_kv")
    return kv.reshape(b, m, 2 * X_WIDTH)


E_QKV, E_HY, E_GHY, E_GGDN, E_XQ, E_GX, E_BIG = 0, 3072, 6144, 7168, 8192, 8704, 9216


def _even_layer(x, mem, pre_g, post_g, w_in, w_out, hy_conv_w, hy_conv_b, hy_w1, hy_b1, hy_w2, hy_b2, hy_w3, hy_freq,
                hy_skip, gdn_conv_w, gdn_a_log, gdn_dt_bias, gdn_norm_g, mem_g, w_mem_kv):
    b, l, d = x.shape
    x2d = x.reshape(b * l, d)
    w_big = jnp.concatenate([w_in[:, 4096:7168], w_in[:, 0:3072], w_in[:, 3072:4096], w_in[:, 7168:8192],
                             w_in[:, 8224:8736], w_in[:, 8736:9248]], axis=1)
    w_small = jnp.pad(w_in[:, 8192:8224], ((0, 0), (0, LANES - 4 * GDN_HEADS)))
    z, zs = _norm_matmul(x2d, pre_g, w_big, E_BIG // 3, out_dtype=BF16, tm=1024, name="even_in", w_side=w_small)
    z = z.reshape(b, l, E_BIG)
    zs = zs.reshape(b, l, LANES)
    conv, e = _hyena_branch(z, l, hy_conv_w, hy_conv_b, hy_w1, hy_b1, hy_w2, hy_b2, hy_w3, hy_freq, hy_skip)
    y_b = _gdn_branch(z, zs, gdn_conv_w, gdn_a_log, gdn_dt_bias, gdn_norm_g, E_GGDN // GDN_WIDTH)
    kv = _mem_kv(mem, mem_g, w_mem_kv)
    y_x = _xattn(z, kv, E_XQ // X_WIDTH, E_GX // X_WIDTH)
    t = b * l
    out = _out_proj([conv.reshape(t, -1), y_b.reshape(t, -1), y_x.reshape(t, -1)], w_out, post_g, x2d,
                    gate=e.reshape(t, -1))
    return out.reshape(b, l, d)


DIL_PATTERNS = ((128, 1), (512, 4), (2048, 16))
N_DIL = len(DIL_PATTERNS)
DIL_HEADS = 4
DIL_HEAD_DIM = 128
DIL_WIDTH = DIL_HEADS * DIL_HEAD_DIM
SWA_Q_HEADS = 16
SWA_KV_HEADS = 2
SWA_HEAD_DIM = 64
SWA_WIDTH = SWA_Q_HEADS * SWA_HEAD_DIM
SWA_HALF_WINDOW = 128
O_CQKV, O_GC, O_DQ, O_GD, O_XQ, O_GX, O_DKV, O_ALL = 0, 4608, 5120, 6144, 7168, 7680, 8192, 8448


def _rope_tables(l, dh):
    half = dh // 8
    inv = ROPE_THETA ** (-jnp.arange(half, dtype=F32) / half)
    ang = jnp.arange(l, dtype=F32)[:, None] * inv[None, :]
    cos, sin = jnp.cos(ang), jnp.sin(ang)
    one = jnp.ones((l, dh - 2 * half), F32)
    zero_h = jnp.zeros((l, half), F32)
    zero_r = jnp.zeros((l, dh - 2 * half), F32)
    c = jnp.concatenate([cos, cos, one], axis=1)
    sa = jnp.concatenate([-sin, zero_h, zero_r], axis=1)
    sb = jnp.concatenate([zero_h, sin, zero_r], axis=1)
    rep = LANES // dh
    return tuple(jnp.tile(t, (1, rep)) for t in (c, sa, sb))


def _odd_prep_kernel(c_ref, dq_ref, dkv_ref, c1_ref, a1_ref, b1_ref, c2_ref, a2_ref, b2_ref, *refs):
    dil_refs = refs[:3 * N_DIL]
    dqe_ref, dk_ref, dv_ref, xs_ref = refs[3 * N_DIL:]
    tl = c_ref.shape[1]
    lane = lax.broadcasted_iota(jnp.int32, (tl, LANES), 1)
    low = lane < SWA_HEAD_DIM
    c1, a1, b1 = c1_ref[...], a1_ref[...], b1_ref[...]
    c2, a2, b2 = c2_ref[...], a2_ref[...], b2_ref[...]
    h1 = DIL_HEAD_DIM // 8
    h2 = SWA_HEAD_DIM // 8

    def rope1(x):
        return x * c1 + pltpu.roll(x, LANES - h1, 1) * a1 + pltpu.roll(x, h1, 1) * b1

    def rope2(x):
        return x * c2 + pltpu.roll(x, LANES - h2, 1) * a2 + pltpu.roll(x, h2, 1) * b2

    nblk = N_DIL * DIL_HEADS
    slot = 0
    for s in range(3):
        for gi, (_, d) in enumerate(DIL_PATTERNS):
            dst = dil_refs[3 * gi + s]
            for h in range(DIL_HEADS):
                j = s * nblk + gi * DIL_HEADS + h
                x = c_ref[0, :, j * LANES:(j + 1) * LANES]
                if s < 2:
                    x = rope1(x.astype(F32))
                hs = slice(h * LANES, (h + 1) * LANES)
                if d == 1:
                    dst[0, 0, :, hs] = x.astype(BF16)
                else:
                    xs_ref[slot] = x.astype(F32)
                    for r in range(d):
                        dst[0, r, :, hs] = xs_ref[slot, pl.ds(r, tl // d, stride=d), :].astype(BF16)
                    slot += 1
    for j in range(SWA_Q_HEADS // 2):
        xr = rope2(dq_ref[0, :, j * LANES:(j + 1) * LANES].astype(F32)) * (SWA_HEAD_DIM ** -0.5 * LOG2E)
        dqe_ref[0, :, (2 * j) * LANES:(2 * j + 1) * LANES] = jnp.where(low, xr, 0.0).astype(BF16)
        dqe_ref[0, :, (2 * j + 1) * LANES:(2 * j + 2) * LANES] = jnp.where(low, pltpu.roll(xr, SWA_HEAD_DIM, 1), 0.0).astype(BF16)
    kr = rope2(dkv_ref[0, :, 0:LANES].astype(F32))
    vv = dkv_ref[0, :, LANES:2 * LANES].astype(F32)
    kr_sw = pltpu.roll(kr, SWA_HEAD_DIM, 1)
    vv_sw = pltpu.roll(vv, SWA_HEAD_DIM, 1)
    dk_ref[0, :, 0:LANES] = jnp.where(low, kr, 0.0).astype(BF16)
    dk_ref[0, :, LANES:2 * LANES] = jnp.where(low, kr_sw, 0.0).astype(BF16)
    dv_ref[0, :, 0:LANES] = jnp.where(low, vv, vv_sw).astype(BF16)
    dv_ref[0, :, LANES:2 * LANES] = jnp.where(low, vv_sw, vv).astype(BF16)


def _odd_prep(z, tl=256):
    b, l, _ = z.shape
    wc = 3 * N_DIL * DIL_WIDTH
    t1 = _rope_tables(l, DIL_HEAD_DIM)
    t2 = _rope_tables(l, SWA_HEAD_DIM)
    tab = pl.BlockSpec((tl, LANES), lambda i, t: (t, 0))

    def spec(w):
        return pl.BlockSpec((1, tl, w), lambda i, t: (i, t, 0))

    def shape(w):
        return jax.ShapeDtypeStruct((b, l, w), BF16)

    dil_specs, dil_shapes = [], []
    for _, d in DIL_PATTERNS:
        for _ in range(3):
            dil_specs.append(pl.BlockSpec((1, d, tl // d, DIL_WIDTH), lambda i, t: (i, 0, t, 0)))
            dil_shapes.append(jax.ShapeDtypeStruct((b, d, l // d, DIL_WIDTH), BF16))
    n_staged = 3 * DIL_HEADS * sum(1 for _, d in DIL_PATTERNS if d > 1)
    outs = pl.pallas_call(
        _odd_prep_kernel,
        grid=(b, l // tl),
        in_specs=[
            pl.BlockSpec((1, tl, wc), lambda i, t: (i, t, 0)),
            pl.BlockSpec((1, tl, SWA_WIDTH), lambda i, t: (i, t, O_DQ // SWA_WIDTH)),
            pl.BlockSpec((1, tl, 2 * LANES), lambda i, t: (i, t, O_DKV // (2 * LANES))),
            tab, tab, tab, tab, tab, tab,
        ],
        out_specs=dil_specs + [spec(2 * SWA_WIDTH), spec(2 * LANES), spec(2 * LANES)],
        out_shape=dil_shapes + [shape(2 * SWA_WIDTH), shape(2 * LANES), shape(2 * LANES)],
        scratch_shapes=[pltpu.VMEM((n_staged, tl, LANES), F32)],
        compiler_params=_cparams("parallel", "arbitrary"),
        name="odd_prep",
    )(z, z, z, *t1, *t2)
    dil = [tuple(outs[3 * gi:3 * gi + 3]) for gi in range(N_DIL)]
    return dil, outs[3 * N_DIL], outs[3 * N_DIL + 1], outs[3 * N_DIL + 2]


def _band_bias_t(tq, half, ls):
    p0 = pl.program_id(1) * tq
    j = lax.broadcasted_iota(jnp.int32, (tq + 2 * half, tq), 0)
    i = lax.broadcasted_iota(jnp.int32, (tq + 2 * half, tq), 1)
    kpos = p0 - half + j
    valid = (j >= i) & (j - i <= 2 * half) & (kpos >= 0) & (kpos < ls)
    return jnp.where(valid, 0.0, -jnp.inf)


def _band_mask(tq, half, ls):
    p0 = pl.program_id(1) * tq
    i = lax.broadcasted_iota(jnp.int32, (tq, tq + 2 * half), 0)
    j = lax.broadcasted_iota(jnp.int32, (tq, tq + 2 * half), 1)
    kpos = p0 - half + j
    return (j >= i) & (j - i <= 2 * half) & (kpos >= 0) & (kpos < ls)


def _band_attn_kernel(q_ref, kp_ref, kc_ref, kn_ref, vp_ref, vc_ref, vn_ref, o_ref, lse_ref, *, half, ls):
    tq = q_ref.shape[1]
    dh = DIL_HEAD_DIM
    valid = _band_mask(tq, half, ls)
    lane = lax.broadcasted_iota(jnp.int32, (tq, LANES), 1)
    lse_all = jnp.zeros((tq, LANES), F32)
    heads = range(DIL_HEADS)
    hsl = [slice(h * dh, (h + 1) * dh) for h in heads]
    kall = [jnp.concatenate([kp_ref[0, :, s], kc_ref[0, :, s], kn_ref[0, :, s]], axis=0) for s in hsl]
    vall = [jnp.concatenate([vp_ref[0, :, s], vc_ref[0, :, s], vn_ref[0, :, s]], axis=0) for s in hsl]
    sc = [jnp.where(valid, _dot_nt(q_ref[0, :, hsl[h]], kall[h]) * (dh ** -0.5), -jnp.inf) for h in heads]
    m = [jnp.max(sc[h], axis=-1, keepdims=True) for h in heads]
    p = [jnp.exp(sc[h] - m[h]) for h in heads]
    den = [jnp.sum(p[h], axis=-1, keepdims=True) for h in heads]
    o = [_dot(p[h].astype(BF16), vall[h]) / den[h] for h in heads]
    for h in heads:
        o_ref[0, :, hsl[h]] = o[h].astype(o_ref.dtype)
        lse_all = jnp.where(lane == h, m[h] + jnp.log(den[h]), lse_all)
    lse_ref[0] = lse_all


def _band_specs(tq, half, ls, w, col):
    r = tq // half
    nb = ls // half
    cur = pl.BlockSpec((1, tq, w), lambda i, t: (i, t, col))
    prev = pl.BlockSpec((1, half, w), lambda i, t: (i, jnp.maximum(t * r - 1, 0), col))
    nxt = pl.BlockSpec((1, half, w), lambda i, t: (i, jnp.minimum((t + 1) * r, nb - 1), col))
    return cur, prev, nxt


def _band_attn(q, k, v, half):
    n, ls, w = q.shape
    tq = min(2 * LANES, ls)
    cur, prev, nxt = _band_specs(tq, half, ls, w, 0)
    return pl.pallas_call(
        functools.partial(_band_attn_kernel, half=half, ls=ls),
        grid=(n, ls // tq),
        in_specs=[cur, prev, cur, nxt, prev, cur, nxt],
        out_specs=[cur, pl.BlockSpec((1, tq, LANES), lambda i, t: (i, t, 0))],
        out_shape=[jax.ShapeDtypeStruct((n, ls, w), BF16), jax.ShapeDtypeStruct((n, ls, LANES), F32)],
        compiler_params=_cparams("parallel", "arbitrary"),
        name="band_attn",
    )(q, k, k, k, v, v, v)


def _dil_merge_kernel(o0_ref, o1_ref, o2_ref, l0_ref, l1_ref, l2_ref, g_ref, y_ref, os_ref, ls_ref):
    dh = DIL_HEAD_DIM
    tl = y_ref.shape[1]
    o_refs = (o0_ref, o1_ref, o2_ref)
    l_refs = (l0_ref, l1_ref, l2_ref)
    slot = 0
    lse, outs = [], []
    for gi, (_, d) in enumerate(DIL_PATTERNS):
        if d == 1:
            lse.append(l_refs[gi][0, 0])
            outs.append([o_refs[gi][0, 0, :, h * dh:(h + 1) * dh].astype(F32) for h in range(DIL_HEADS)])
            continue
        for r in range(d):
            ls_ref[gi, pl.ds(r, tl // d, stride=d), :] = l_refs[gi][0, r]
        lse.append(ls_ref[gi])
        per_head = []
        for h in range(DIL_HEADS):
            for r in range(d):
                os_ref[slot, pl.ds(r, tl // d, stride=d), :] = o_refs[gi][0, r, :, h * dh:(h + 1) * dh].astype(F32)
            per_head.append(os_ref[slot])
            slot += 1
        outs.append(per_head)
    for h in range(DIL_HEADS):
        hs = slice(h * dh, (h + 1) * dh)
        ls = [x[:, h:h + 1] for x in lse]
        m = jnp.maximum(jnp.maximum(ls[0], ls[1]), ls[2])
        ws = [jnp.exp(x - m) for x in ls]
        den = ws[0] + ws[1] + ws[2]
        y = (ws[0] / den) * outs[0][h] + (ws[1] / den) * outs[1][h] + (ws[2] / den) * outs[2][h]
        y_ref[0, :, hs] = (y * _silu(g_ref[0, :, hs].astype(F32))).astype(y_ref.dtype)


def _dil_merge(outs, lses, z, tl=512):
    b, _, l, w = outs[0].shape
    o_specs = [pl.BlockSpec((1, d, tl // d, w), lambda i, t: (i, 0, t, 0)) for _, d in DIL_PATTERNS]
    l_specs = [pl.BlockSpec((1, d, tl // d, LANES), lambda i, t: (i, 0, t, 0)) for _, d in DIL_PATTERNS]
    n_staged = DIL_HEADS * sum(1 for _, d in DIL_PATTERNS if d > 1)
    return pl.pallas_call(
        _dil_merge_kernel,
        grid=(b, l // tl),
        in_specs=o_specs + l_specs + [pl.BlockSpec((1, tl, w), lambda i, t: (i, t, O_GC // DIL_WIDTH))],
        out_specs=pl.BlockSpec((1, tl, w), lambda i, t: (i, t, 0)),
        out_shape=jax.ShapeDtypeStruct((b, l, w), BF16),
        scratch_shapes=[pltpu.VMEM((n_staged, tl, LANES), F32), pltpu.VMEM((N_DIL, tl, LANES), F32)],
        compiler_params=_cparams("parallel", "arbitrary"),
        name="dil_merge",
    )(*outs, *lses, z)


def _dilated_branch(dil, z):
    outs, lses = [], []
    for (window, d), (q, k, v) in zip(DIL_PATTERNS, dil):
        b, _, ls, w = q.shape
        half = window // (2 * d)
        o, lse = _band_attn(q.reshape(b * d, ls, w), k.reshape(b * d, ls, w), v.reshape(b * d, ls, w), half)
        outs.append(o.reshape(b, d, ls, w))
        lses.append(lse.reshape(b, d, ls, LANES))
    return _dil_merge(outs, lses, z)


def _swa_kernel(q_ref, kp_ref, kc_ref, kn_ref, vp_ref, vc_ref, vn_ref, g_ref, sink_ref, o_ref, *, half, ls):
    tq = q_ref.shape[1]
    dh = SWA_HEAD_DIM
    bias1 = _band_bias_t(tq, half, ls)
    bias = jnp.concatenate([bias1, bias1], axis=1)
    b_prev, b_next = bias[:half], bias[half + tq:]
    low = lax.broadcasted_iota(jnp.int32, (LANES, tq), 0) < dh
    first = lax.broadcasted_iota(jnp.int32, (1, 2 * tq), 1) < tq
    grp = SWA_Q_HEADS // SWA_KV_HEADS
    for g in range(SWA_KV_HEADS):
        gs = slice(g * LANES, (g + 1) * LANES)
        kall = jnp.concatenate([kp_ref[0, :, gs], kc_ref[0, :, gs], kn_ref[0, :, gs]], axis=0)
        vt = _t_bf16(jnp.concatenate([vp_ref[0, :, gs], vc_ref[0, :, gs], vn_ref[0, :, gs]], axis=0))
        pairs = range(grp // 2)
        h0 = [g * grp + 2 * jp for jp in pairs]
        q2 = [q_ref[0, :, h * LANES:(h + 2) * LANES] for h in h0]
        q2 = [jnp.concatenate([x[:, :LANES], x[:, LANES:]], axis=0) for x in q2]
        sc = [_dot_nt(kall, q2[jp]) for jp in pairs]
        sc = [jnp.concatenate([s[:half] + b_prev, s[half:half + tq], s[half + tq:] + b_next], axis=0) for s in sc]
        sk = [jnp.where(first, sink_ref[:, h:h + 1], sink_ref[:, h + 1:h + 2]) * LOG2E for h in h0]
        sm = [_softmax_t(sc[jp], sk[jp]) for jp in pairs]
        ot = [_dot(vt, sm[jp][0]) / sm[jp][1] for jp in pairs]
        for jp in pairs:
            blk = (g * grp) // 2 + jp
            bs = slice(blk * LANES, (blk + 1) * LANES)
            y = jnp.where(low, ot[jp][:, :tq], ot[jp][:, tq:]).T * _silu(g_ref[0, :, bs].astype(F32))
            o_ref[0, :, bs] = y.astype(o_ref.dtype)


def _swa_branch(dqe, dk, dv, z, sink, tq=128):
    b, l, _ = dqe.shape
    half = SWA_HALF_WINDOW
    assert tq == half == LANES
    _, prev, nxt = _band_specs(tq, half, l, 2 * LANES, 0)
    cur = pl.BlockSpec((1, tq, 2 * LANES), lambda i, t: (i, t, 0))
    sink_p = jnp.pad(sink.reshape(1, SWA_Q_HEADS).astype(F32), ((0, 0), (0, LANES - SWA_Q_HEADS)))
    return pl.pallas_call(
        functools.partial(_swa_kernel, half=half, ls=l),
        grid=(b, l // tq),
        in_specs=[
            pl.BlockSpec((1, tq, 2 * SWA_WIDTH), lambda i, t: (i, t, 0)),
            prev, cur, nxt, prev, cur, nxt,
            pl.BlockSpec((1, tq, SWA_WIDTH), lambda i, t: (i, t, O_GD // SWA_WIDTH)),
            pl.BlockSpec((1, LANES), lambda i, t: (0, 0)),
        ],
        out_specs=pl.BlockSpec((1, tq, SWA_WIDTH), lambda i, t: (i, t, 0)),
        out_shape=jax.ShapeDtypeStruct((b, l, SWA_WIDTH), BF16),
        compiler_params=_cparams("parallel", "arbitrary"),
        name="swa",
    )(dqe, dk, dk, dk, dv, dv, dv, z, sink_p)


def _odd_layer(x, mem, pre_g, post_g, w_in, w_out, swa_sink, mem_g, w_mem_kv):
    b, l, d = x.shape
    x2d = x.reshape(b * l, d)
    w_re = jnp.concatenate([w_in[:, 0:6144], w_in[:, 6400:8448], w_in[:, 6144:6400]], axis=1)
    z = _norm_matmul(x2d, pre_g, w_re, O_ALL // 3, out_dtype=BF16, tm=1024, name="odd_in").reshape(b, l, O_ALL)
    dil, dqe, dk, dv = _odd_prep(z)
    y_c = _dilated_branch(dil, z)
    y_d = _swa_branch(dqe, dk, dv, z, swa_sink)
    kv = _mem_kv(mem, mem_g, w_mem_kv)
    y_x = _xattn(z, kv, O_XQ // X_WIDTH, O_GX // X_WIDTH)
    t = b * l
    out = _out_proj([y_c.reshape(t, -1), y_d.reshape(t, -1), y_x.reshape(t, -1)], w_out, post_g, x2d)
    return out.reshape(b, l, d)


def _trunk(x, mem, even_params, odd_params):
    x = _even_layer(x, mem, *[p[0] for p in even_params])
    return _odd_layer(x, mem, *[p[0] for p in odd_params])


def kernel(x_prompt, x_sample, mem_prompt, mem_sample, e_pre_g, e_post_g, e_w_in, e_w_out, hy_conv_w, hy_conv_b,
           hy_filt_w1, hy_filt_b1, hy_filt_w2, hy_filt_b2, hy_filt_w3, hy_freq, hy_skip, gdn_conv_w, gdn_A_log,
           gdn_dt_bias, gdn_norm_g, e_mem_g, e_w_mem_kv, o_pre_g, o_post_g, o_w_in, o_w_out, swa_sink, o_mem_g,
           o_w_mem_kv):
    even_params = (e_pre_g, e_post_g, e_w_in, e_w_out, hy_conv_w, hy_conv_b, hy_filt_w1, hy_filt_b1, hy_filt_w2,
                   hy_filt_b2, hy_filt_w3, hy_freq, hy_skip, gdn_conv_w, gdn_A_log, gdn_dt_bias, gdn_norm_g,
                   e_mem_g, e_w_mem_kv)
    odd_params = (o_pre_g, o_post_g, o_w_in, o_w_out, swa_sink, o_mem_g, o_w_mem_kv)
    y_prompt = _trunk(x_prompt, mem_prompt, even_params, odd_params)
    y_sample = _trunk(x_sample, mem_sample, even_params, odd_params)
    return (y_prompt, y_sample)
```

```python
import functools
import math

import jax
import jax.numpy as jnp
import numpy as np
from jax import lax
from jax.experimental import pallas as pl
from jax.experimental.pallas import tpu as pltpu

D_MODEL = 1024
EPS = 1e-6
ROPE_THETA = 500000.0
HY_WIDTH = 1024
HY_EMB = 33
HY_BANDS = (HY_EMB - 1) // 2
HY_FILT_HIDDEN = 64
HY_TARGET = 1e-2
HY_FAST_PCT = 0.3
HY_SLOW_PCT = 1.5
GDN_HEADS = 8
GDN_HEAD_DIM = 128
GDN_WIDTH = GDN_HEADS * GDN_HEAD_DIM
GDN_CHUNK = 64
X_HEADS = 4
X_HEAD_DIM = 128
X_WIDTH = X_HEADS * X_HEAD_DIM

LOG2E = math.log2(math.e)
LANES = 128
FFT_N2 = 128
VMEM_LIMIT_BYTES = 48 * 1024 * 1024

BF16 = jnp.bfloat16
F32 = jnp.float32


def _cparams(*sem):
    return pltpu.CompilerParams(dimension_semantics=sem, vmem_limit_bytes=VMEM_LIMIT_BYTES)


def _dot(a, b):
    return jnp.dot(a, b, preferred_element_type=F32)


def _silu(x):
    hx = 0.5 * x
    return hx + hx * jnp.tanh(hx)


def _norm_matmul_kernel(*refs, side):
    if side:
        x_ref, g_ref, w_ref, ws_ref, o_ref, os_ref, xn_ref = refs
    else:
        x_ref, g_ref, w_ref, o_ref, xn_ref = refs

    @pl.when(pl.program_id(1) == 0)
    def _():
        x = x_ref[...]
        ms = jnp.mean(x * x, axis=-1, keepdims=True)
        xn_ref[...] = (x * lax.rsqrt(ms + EPS) * g_ref[...]).astype(BF16)
        if side:
            os_ref[...] = _dot(xn_ref[...], ws_ref[...])

    o_ref[...] = _dot(xn_ref[...], w_ref[...]).astype(o_ref.dtype)


def _norm_matmul(x2d, g, w, tn, out_dtype=F32, tm=512, name="norm_matmul", w_side=None):
    t, d = x2d.shape
    n = w.shape[1]
    assert t % tm == 0 and n % tn == 0
    side = w_side is not None
    in_specs = [
        pl.BlockSpec((tm, d), lambda i, j: (i, 0)),
        pl.BlockSpec((1, d), lambda i, j: (0, 0)),
        pl.BlockSpec((d, tn), lambda i, j: (0, j)),
    ]
    out_specs = [pl.BlockSpec((tm, tn), lambda i, j: (i, j))]
    out_shape = [jax.ShapeDtypeStruct((t, n), out_dtype)]
    args = [x2d, g.reshape(1, d).astype(F32), w.astype(BF16)]
    if side:
        in_specs.append(pl.BlockSpec((d, LANES), lambda i, j: (0, 0)))
        out_specs.append(pl.BlockSpec((tm, LANES), lambda i, j: (i, 0)))
        out_shape.append(jax.ShapeDtypeStruct((t, LANES), F32))
        args.append(w_side.astype(BF16))
    outs = pl.pallas_call(
        functools.partial(_norm_matmul_kernel, side=side),
        grid=(t // tm, n // tn),
        in_specs=in_specs,
        out_specs=out_specs,
        out_shape=out_shape,
        scratch_shapes=[pltpu.VMEM((tm, d), BF16)],
        compiler_params=_cparams("parallel", "arbitrary"),
        name=name,
    )(*args)
    return tuple(outs) if side else outs[0]


HALO_ROWS = 16


TAP_ROWS = 128
TAP_WINDOW = 256


def _fill_halo_buf(buf_ref, zc_ref, zp_ref, zn_ref, t, nt):
    tl = zc_ref.shape[1]
    hr = HALO_ROWS
    buf_ref[0:hr, :] = jnp.where(t > 0, zp_ref[0].astype(F32), 0.0).astype(BF16)
    buf_ref[hr:hr + tl, :] = zc_ref[0].astype(BF16)
    buf_ref[hr + tl:2 * hr + tl, :] = jnp.where(t < nt - 1, zn_ref[0].astype(F32), 0.0).astype(BF16)
    buf_ref[2 * hr + tl:, :] = jnp.zeros((buf_ref.shape[0] - 2 * hr - tl, buf_ref.shape[1]), BF16)


def _halo_buf_rows(tl):
    return tl - TAP_ROWS + TAP_WINDOW


def _tap_shift_matrix(ntaps):
    n = lax.broadcasted_iota(jnp.int32, (ntaps, TAP_ROWS, TAP_WINDOW), 1)
    j = lax.broadcasted_iota(jnp.int32, (ntaps, TAP_ROWS, TAP_WINDOW), 0)
    m = lax.broadcasted_iota(jnp.int32, (ntaps, TAP_ROWS, TAP_WINDOW), 2)
    return (m == HALO_ROWS + n + j - ntaps // 2).astype(BF16).reshape(ntaps * TAP_ROWS, TAP_WINDOW)


def _taps(buf_ref, s_ref, r0, lanes):
    return _dot(s_ref[...], buf_ref[r0:r0 + TAP_WINDOW, lanes])


def _halo_specs(tl, w, l, col):
    r = tl // HALO_ROWS
    nb = l // HALO_ROWS
    cur = pl.BlockSpec((1, tl, w), lambda i, t: (i, t, col))
    prev = pl.BlockSpec((1, HALO_ROWS, w), lambda i, t: (i, jnp.maximum(t * r - 1, 0), col))
    nxt = pl.BlockSpec((1, HALO_ROWS, w), lambda i, t: (i, jnp.minimum((t + 1) * r, nb - 1), col))
    return cur, prev, nxt


def _hy_prep_kernel(zc_ref, zp_ref, zn_ref, g_ref, cw_ref, cb_ref, s_ref, u_ref, e_ref, buf_ref):
    t = pl.program_id(1)
    nt = pl.num_programs(1)
    tl = zc_ref.shape[1]
    c = HY_WIDTH
    tr = TAP_ROWS
    _fill_halo_buf(buf_ref, zc_ref, zp_ref, zn_ref, t, nt)
    cw = 256
    for r0 in range(0, tl, tr):
        rs = slice(r0, r0 + tr)
        for c0 in range(0, c, cw):
            parts = []
            for p in range(3):
                lo = p * c + c0
                xs = _taps(buf_ref, s_ref, r0, slice(lo, lo + cw))
                acc = cb_ref[:, lo:lo + cw]
                for j in range(3):
                    acc = acc + xs[j * tr:(j + 1) * tr] * cw_ref[j:j + 1, lo:lo + cw]
                parts.append(acc)
            x0, x1, v = parts
            u_ref[0, rs, c0:c0 + cw] = (v * x1).astype(u_ref.dtype)
            e_ref[0, rs, c0:c0 + cw] = (x0 * _silu(g_ref[0, rs, c0:c0 + cw].astype(F32))).astype(e_ref.dtype)


def _hy_prep(z, conv_w, conv_b, tl=512):
    b, l, _ = z.shape
    c = HY_WIDTH
    assert l % tl == 0
    cur, prev, nxt = _halo_specs(tl, 3 * c, l, E_HY // (3 * c))
    return pl.pallas_call(
        _hy_prep_kernel,
        grid=(b, l // tl),
        in_specs=[
            cur, prev, nxt,
            pl.BlockSpec((1, tl, c), lambda i, t: (i, t, E_GHY // c)),
            pl.BlockSpec((3, 3 * c), lambda i, t: (0, 0)),
            pl.BlockSpec((1, 3 * c), lambda i, t: (0, 0)),
            pl.BlockSpec((3 * TAP_ROWS, TAP_WINDOW), lambda i, t: (0, 0)),
        ],
        out_specs=[
            pl.BlockSpec((1, tl, c), lambda i, t: (i, t, 0)),
            pl.BlockSpec((1, tl, c), lambda i, t: (i, t, 0)),
        ],
        out_shape=[jax.ShapeDtypeStruct((b, l, c), F32), jax.ShapeDtypeStruct((b, l, c), F32)],
        scratch_shapes=[pltpu.VMEM((_halo_buf_rows(tl), 3 * c), BF16)],
        compiler_params=_cparams("parallel", "arbitrary"),
        name="hy_prep",
    )(z, z, z, z, conv_w.astype(F32), conv_b.reshape(1, 3 * c).astype(F32), _tap_shift_matrix(3))


def _hy_filter_tables(l):
    t = jnp.linspace(0.0, 1.0, l, dtype=F32)[:, None]
    w = (2.0 * math.pi / l) * jnp.arange(l, dtype=F32)[:, None]
    f = jnp.linspace(1e-4, HY_BANDS - 1, HY_BANDS, dtype=F32)[None, :]
    emb = jnp.concatenate([t, jnp.cos(f * w), -jnp.sin(f * w)], axis=-1)
    emb = jnp.pad(emb, ((0, 0), (0, LANES - HY_EMB)))
    deltas = jnp.abs(jnp.linspace(math.log(HY_TARGET) / HY_SLOW_PCT, math.log(HY_TARGET) / HY_FAST_PCT, HY_WIDTH, dtype=F32))
    decay = jnp.exp(-t * jnp.tile(deltas, 2)[None, :])
    return emb, decay


def _hy_filter_kernel(emb_ref, dec_ref, w1_ref, b1_ref, w2_ref, b2_ref, w3_ref, fr_ref, sk_ref, o_ref):
    c = HY_WIDTH
    hp = lax.Precision.HIGHEST
    fr = fr_ref[...]
    hid = jnp.sin(fr * (jnp.dot(emb_ref[...], w1_ref[...], precision=hp, preferred_element_type=F32) + b1_ref[...]))
    hid = jnp.sin(fr * (jnp.dot(hid, w2_ref[...], precision=hp, preferred_element_type=F32) + b2_ref[...]))
    tl = emb_ref.shape[0]
    row = lax.broadcasted_iota(jnp.int32, (tl, 1), 0) + pl.program_id(0) * tl
    first = row == 0
    cw = 512
    for c0 in range(0, 2 * c, cw):
        filt = jnp.dot(hid, w3_ref[:, c0:c0 + cw], precision=hp, preferred_element_type=F32) * dec_ref[:, c0:c0 + cw]
        if c0 < c:
            filt = jnp.where(first, filt + sk_ref[:, c0:c0 + cw], filt)
            o_ref[0, :, c0:c0 + cw] = filt.astype(o_ref.dtype)
        else:
            filt = jnp.where(first, 0.0, filt)
            o_ref[1, :, c0 - c:c0 - c + cw] = filt.astype(o_ref.dtype)


def _hy_filters(l, w1, b1, w2, b2, w3, freq, skip, tl=256):
    c = HY_WIDTH
    hdim = HY_FILT_HIDDEN
    emb, decay = _hy_filter_tables(l)
    w1p = jnp.pad(w1.astype(F32), ((0, LANES - HY_EMB), (0, 0)))
    return pl.pallas_call(
        _hy_filter_kernel,
        grid=(l // tl,),
        in_specs=[
            pl.BlockSpec((tl, LANES), lambda t: (t, 0)),
            pl.BlockSpec((tl, 2 * c), lambda t: (t, 0)),
            pl.BlockSpec((LANES, hdim), lambda t: (0, 0)),
            pl.BlockSpec((1, hdim), lambda t: (0, 0)),
            pl.BlockSpec((hdim, hdim), lambda t: (0, 0)),
            pl.BlockSpec((1, hdim), lambda t: (0, 0)),
            pl.BlockSpec((hdim, 2 * c), lambda t: (0, 0)),
            pl.BlockSpec((1, hdim), lambda t: (0, 0)),
            pl.BlockSpec((1, c), lambda t: (0, 0)),
        ],
        out_specs=pl.BlockSpec((2, tl, c), lambda t: (0, t, 0)),
        out_shape=jax.ShapeDtypeStruct((2, l, c), BF16),
        compiler_params=_cparams("arbitrary"),
        name="hy_filter",
    )(emb, decay, w1p, b1.reshape(1, hdim).astype(F32), w2.astype(F32), b2.reshape(1, hdim).astype(F32),
      w3.astype(F32), freq.reshape(1, hdim).astype(F32), skip.reshape(1, c).astype(F32))


def _fft_consts(l):
    n = 2 * l
    n2 = FFT_N2
    n1 = n // n2
    n1h = n1 // 2
    w0 = 2.0 * math.pi / n
    k1 = jnp.arange(n1, dtype=jnp.int32)
    m1 = jnp.arange(n1h, dtype=jnp.int32)
    m2 = jnp.arange(n2, dtype=jnp.int32)
    ea = (n2 * m1[None, None, :] * k1[None, :, None] + m2[:, None, None] * k1[None, :, None]) % n
    ang = ea.astype(F32) * w0
    fa = jnp.concatenate([jnp.cos(ang), -jnp.sin(ang)], axis=1).astype(BF16)
    eb = (n1 * m2[:, None] * m2[None, :]) % n
    angb = eb.astype(F32) * w0
    fr, fi = jnp.cos(angb), -jnp.sin(angb)
    fb = jnp.concatenate([jnp.concatenate([fr, -fi], axis=1), jnp.concatenate([fi, fr], axis=1)], axis=0).astype(BF16)
    eg = (n1 * m2[None, :, None] * m2[None, None, :] + m2[None, :, None] * k1[:, None, None]) % n
    angg = eg.astype(F32) * w0
    gr, gi = jnp.cos(angg), jnp.sin(angg)
    gb = jnp.concatenate([jnp.concatenate([gr, -gi], axis=2), jnp.concatenate([gi, gr], axis=2)], axis=1).astype(BF16)
    ec = (n2 * m1[:, None] * k1[None, :]) % n
    angc = ec.astype(F32) * w0
    fc = (jnp.concatenate([jnp.cos(angc), -jnp.sin(angc)], axis=1) * (1.0 / n)).astype(BF16)
    return fa, fb, gb, fc


def _fft_a_kernel(u_ref, f_ref, o_ref):
    tn2 = f_ref.shape[0]
    n1 = f_ref.shape[1] // 2
    c = u_ref.shape[2] // tn2
    for j in range(tn2):
        a = _dot(f_ref[j], u_ref[0, :, j * c:(j + 1) * c])
        o_ref[0, 0, :, j * c:(j + 1) * c] = a[:n1].astype(o_ref.dtype)
        o_ref[0, 1, :, j * c:(j + 1) * c] = a[n1:].astype(o_ref.dtype)


def _fft_a(u, fa, tn2=8):
    b, l, c = u.shape
    n2, n1x2, n1h = fa.shape
    n1 = n1x2 // 2
    uv = u.reshape(b, n1h, n2 * c)
    return pl.pallas_call(
        _fft_a_kernel,
        grid=(b, n2 // tn2),
        in_specs=[
            pl.BlockSpec((1, n1h, tn2 * c), lambda i, j: (i, 0, j)),
            pl.BlockSpec((tn2, n1x2, n1h), lambda i, j: (j, 0, 0)),
        ],
        out_specs=pl.BlockSpec((1, 2, n1, tn2 * c), lambda i, j: (i, 0, 0, j)),
        out_shape=jax.ShapeDtypeStruct((b, 2, n1, n2 * c), BF16),
        compiler_params=_cparams("parallel", "arbitrary"),
        name="fft_a",
    )(uv, fa)


def _fft_bf_kernel(a_ref, w_ref, o_ref):
    n2 = a_ref.shape[2]
    ct = a_ref.shape[3]
    cw = 256
    for c0 in range(0, ct, cw):
        xf = _dot(w_ref[...], a_ref[0, :, :, c0:c0 + cw].reshape(2 * n2, cw))
        xb = _dot(w_ref[...], a_ref[1, :, :, c0:c0 + cw].reshape(2 * n2, cw))
        o_ref[0, :, c0:c0 + cw] = xf[:n2] + xb[:n2]
        o_ref[1, :, c0:c0 + cw] = xf[n2:] - xb[n2:]


def _fft_bf(a, fb):
    _, _, n, c = a.shape
    n2 = FFT_N2
    return pl.pallas_call(
        _fft_bf_kernel,
        grid=(n // n2,),
        in_specs=[
            pl.BlockSpec((2, 2, n2, c), lambda k: (0, 0, k, 0)),
            pl.BlockSpec((2 * n2, 2 * n2), lambda k: (0, 0)),
        ],
        out_specs=pl.BlockSpec((2, n2, c), lambda k: (0, k, 0)),
        out_shape=jax.ShapeDtypeStruct((2, n, c), F32),
        compiler_params=_cparams("arbitrary"),
        name="fft_bf",
    )(a, fb)


SUBLANES = 8
CONV_SLABS = 2


def _conv_dims(n1):
    kh = n1 // 2 + 1
    kp = -(-kh // SUBLANES) * SUBLANES
    return kh, kp, 2 * kp + SUBLANES


def _hy_conv_kernel(u_ref, h_ref, la_ref, w_ref, g_ref, fc_ref, o_ref, s_ref, *, n1, unroll_b):
    n2 = FFT_N2
    n1h = n1 // 2
    kh, kp, pitch = _conv_dims(n1)
    grp = SUBLANES
    slabs = range(CONV_SLABS)

    def lanes(x, sl):
        return x[:, sl * LANES:(sl + 1) * LANES]

    def stage_a(jj, carry):
        rows = [u_ref[0, pl.ds(pl.multiple_of(m * n2 + jj * grp, grp), grp), :] for m in range(n1h)]
        a = _dot(la_ref[jj], jnp.concatenate(rows, axis=0).astype(BF16))
        for j0 in range(grp):
            dst = pl.multiple_of((jj * grp + j0) * pitch, grp)
            for sl in slabs:
                s_ref[sl, pl.ds(dst, 2 * kp), :] = lanes(a[j0 * 2 * kp:(j0 + 1) * 2 * kp], sl)
        return carry

    lax.fori_loop(0, n2 // grp, stage_a, 0)

    def stage_b(kk, carry):
        k1s = [jnp.minimum(kk * unroll_b + i, kh - 1) for i in range(unroll_b)]
        ar = [jnp.concatenate([s_ref[sl, pl.ds(k, n2, stride=pitch), :] for sl in slabs], axis=1) for k in k1s]
        ai = [jnp.concatenate([s_ref[sl, pl.ds(kp + k, n2, stride=pitch), :] for sl in slabs], axis=1) for k in k1s]
        x = [_dot(w_ref[...], jnp.concatenate([ar[i], ai[i]], axis=0).astype(BF16)) for i in range(unroll_b)]
        y = []
        for i, k in enumerate(k1s):
            off = pl.multiple_of(k * n2, n2)
            hr, hi = h_ref[0, pl.ds(off, n2), :], h_ref[1, pl.ds(off, n2), :]
            xr, xi = x[i][:n2], x[i][n2:]
            y.append(jnp.concatenate([xr * hr - xi * hi, xr * hi + xi * hr], axis=0).astype(BF16))
        z = [_dot(g_ref[k1s[i]], y[i]) for i in range(unroll_b)]
        for i, k in enumerate(k1s):
            for sl in slabs:
                s_ref[sl, pl.ds(k, n2, stride=pitch), :] = lanes(z[i][:n2], sl)
                s_ref[sl, pl.ds(kp + k, n2, stride=pitch), :] = lanes(z[i][n2:], sl)
        return carry

    lax.fori_loop(0, -(-kh // unroll_b), stage_b, 0)

    def stage_c(jj, carry):
        zz = [jnp.concatenate([s_ref[sl, pl.ds(pl.multiple_of((jj * grp + j0) * pitch, grp), 2 * kp), :]
                               for sl in slabs], axis=1) for j0 in range(grp)]
        y = _dot(fc_ref[...], jnp.concatenate(zz, axis=0).astype(BF16))
        for m in range(n1h):
            o_ref[0, pl.ds(pl.multiple_of(m * n2 + jj * grp, grp), grp), :] = y[m * grp:(m + 1) * grp]
        return carry

    lax.fori_loop(0, n2 // grp, stage_c, 0)


def _hy_conv(u, h, fa, fb, gb, fc):
    b, l, c = u.shape
    n2, n1x2, n1h = fa.shape
    n1 = n1x2 // 2
    kh, kp, pitch = _conv_dims(n1)
    grp = SUBLANES
    ct = CONV_SLABS * LANES

    def half(t, axis):
        re, im = jnp.split(t, 2, axis=axis)
        pad = [(0, 0)] * t.ndim
        pad[axis] = (0, kp - kh)
        cut = lambda x: jnp.pad(lax.slice_in_dim(x, 0, kh, axis=axis), pad)
        return jnp.concatenate([cut(re), cut(im)], axis=axis)

    fa_h = half(fa, 1)
    la = (fa_h.reshape(n2 // grp, grp, 2 * kp, n1h, 1) * jnp.eye(grp, dtype=fa.dtype).reshape(1, grp, 1, 1, grp))
    la = la.reshape(n2 // grp, grp * 2 * kp, n1h * grp)
    wgt = jnp.ones((kh,), F32).at[1:kh - 1].set(2.0)
    wgt = jnp.pad(wgt, (0, kp - kh))
    fc_h = (half(fc, 1).astype(F32) * jnp.tile(wgt, 2)[None, :]).astype(BF16)
    lc = fc_h.reshape(n1h, 1, 1, 2 * kp) * jnp.eye(grp, dtype=fc_h.dtype).reshape(1, grp, grp, 1)
    fc_h = lc.reshape(n1h * grp, grp * 2 * kp)
    gb_h = gb[:kh]
    unroll_b = 3
    once = pl.Buffered(1)
    return pl.pallas_call(
        functools.partial(_hy_conv_kernel, n1=n1, unroll_b=unroll_b),
        grid=(c // ct, b),
        in_specs=[
            pl.BlockSpec((1, l, ct), lambda j, i: (i, 0, j)),
            pl.BlockSpec((2, kh * n2, ct), lambda j, i: (0, 0, j), pipeline_mode=once),
            pl.BlockSpec(la.shape, lambda j, i: (0, 0, 0), pipeline_mode=once),
            pl.BlockSpec(fb.shape, lambda j, i: (0, 0), pipeline_mode=once),
            pl.BlockSpec(gb_h.shape, lambda j, i: (0, 0, 0), pipeline_mode=once),
            pl.BlockSpec(fc_h.shape, lambda j, i: (0, 0), pipeline_mode=once),
        ],
        out_specs=pl.BlockSpec((1, l, ct), lambda j, i: (i, 0, j)),
        out_shape=jax.ShapeDtypeStruct((b, l, c), F32),
        scratch_shapes=[pltpu.VMEM((CONV_SLABS, n2 * pitch, LANES), F32)],
        compiler_params=_cparams("arbitrary", "arbitrary"),
        name="hy_conv",
    )(u, h, la, fb, gb_h, fc_h)


def _hyena_branch(z, l, conv_w, conv_b, w1, b1, w2, b2, w3, freq, skip):
    c = HY_WIDTH
    n = 2 * l
    fa, fb, gb, fc = _fft_consts(l)
    filt = _hy_filters(l, w1, b1, w2, b2, w3, freq, skip)
    fa_spec = _fft_a(filt, fa).reshape(2, 2, n, c)
    h = _fft_bf(fa_spec, fb)
    u, e = _hy_prep(z, conv_w, conv_b)
    return _hy_conv(u, h, fa, fb, gb, fc), e


def _gdn_prep_kernel(zc_ref, zp_ref, zn_ref, cw_ref, s_ref, q_ref, k_ref, v_ref, buf_ref):
    t = pl.program_id(1)
    nt = pl.num_programs(1)
    tl = zc_ref.shape[1]
    dh = GDN_HEAD_DIM
    tr = TAP_ROWS
    _fill_halo_buf(buf_ref, zc_ref, zp_ref, zn_ref, t, nt)
    outs = (q_ref, k_ref, v_ref)
    cw = 2 * dh
    for r0 in range(0, tl, tr):
        for p in range(3):
            for c0 in range(0, GDN_WIDTH, cw):
                lo = p * GDN_WIDTH + c0
                xs = _taps(buf_ref, s_ref, r0, slice(lo, lo + cw))
                acc = xs[0:tr] * cw_ref[0:1, lo:lo + cw]
                for j in range(1, 5):
                    acc = acc + xs[j * tr:(j + 1) * tr] * cw_ref[j:j + 1, lo:lo + cw]
                a = _silu(acc)
                for hh in range(cw // dh):
                    ah = a[:, hh * dh:(hh + 1) * dh]
                    if p < 2:
                        ah = ah * lax.rsqrt(jnp.sum(ah * ah, axis=-1, keepdims=True) + EPS)
                    if p == 0:
                        ah = ah * (dh ** -0.5)
                    outs[p][0, r0:r0 + tr, c0 + hh * dh:c0 + (hh + 1) * dh] = ah.astype(BF16)


def _gdn_prep(z, conv_w, tl=512):
    b, l, _ = z.shape
    w = 3 * GDN_WIDTH
    cur, prev, nxt = _halo_specs(tl, w, l, E_QKV // w)
    o_spec = pl.BlockSpec((1, tl, GDN_WIDTH), lambda i, t: (i, t, 0))
    o_shape = jax.ShapeDtypeStruct((b, l, GDN_WIDTH), BF16)
    return pl.pallas_call(
        _gdn_prep_kernel,
        grid=(b, l // tl),
        in_specs=[cur, prev, nxt, pl.BlockSpec((5, w), lambda i, t: (0, 0)),
                  pl.BlockSpec((5 * TAP_ROWS, TAP_WINDOW), lambda i, t: (0, 0))],
        out_specs=[o_spec, o_spec, o_spec],
        out_shape=[o_shape, o_shape, o_shape],
        scratch_shapes=[pltpu.VMEM((_halo_buf_rows(tl), w), BF16)],
        compiler_params=_cparams("parallel", "arbitrary"),
        name="gdn_prep",
    )(z, z, z, conv_w.astype(F32), _tap_shift_matrix(5))


def _dot_nt(a, b):
    return lax.dot_general(a, b, (((1,), (1,)), ((), ())), preferred_element_type=F32)


def _dot_tn(a, b):
    return lax.dot_general(a, b, (((0,), (0,)), ((), ())), preferred_element_type=F32)


def _split3(x):
    a = x.astype(BF16)
    r = x - a.astype(F32)
    b = r.astype(BF16)
    c = (r - b.astype(F32)).astype(BF16)
    return a, b, c


GDN_CHUNKS_PER_STEP = 8


def _gdn_scan_kernel(*refs, direction, final):
    if final:
        q_ref, k_ref, v_ref, zs_ref, al_ref, dt_ref, of_ref, gg_ref, ng_ref, o_ref, s_ref = refs
    else:
        q_ref, k_ref, v_ref, zs_ref, al_ref, dt_ref, o_ref, s_ref = refs
    c = GDN_CHUNK
    dh = GDN_HEAD_DIM
    nsub = q_ref.shape[1] // c

    @pl.when(pl.program_id(1) == 0)
    def _():
        s_ref[...] = jnp.zeros_like(s_ref)

    row = lax.broadcasted_iota(jnp.int32, (c, LANES), 0)
    col = lax.broadcasted_iota(jnp.int32, (c, LANES), 1)
    if direction == 0:
        incl, strict, last = col <= row, col < row, c - 1
    else:
        incl, strict, last = (col >= row) & (col < c), (col > row) & (col < c), 0
    tri = incl[:, :c].astype(BF16)
    eye = (row == col).astype(F32)
    zc = jnp.zeros((c, LANES), BF16)
    zd = jnp.zeros((dh, dh), BF16)

    def lanes2(xa, xb):
        return jnp.concatenate([xa, xb], axis=1)

    def blockdiag_slots(pa, pb):
        return jnp.concatenate([lanes2(pa, zc), lanes2(zc, zc), lanes2(zc, pb), lanes2(zc, zc)], axis=0)

    heads = range(GDN_HEADS)
    hsl = [slice(h * dh, (h + 1) * dh) for h in heads]
    lns = [2 * GDN_HEADS + direction * GDN_HEADS + h for h in heads]

    rows = [slice(i * c, (i + 1) * c) for i in range(nsub)]
    gcum, gcum_t, beta_all = [], [], []
    for rs in rows:
        zs = zs_ref[0, rs, :]
        beta_all.append(jax.nn.sigmoid(zs))
        gl = -jnp.exp(al_ref[...]) * jax.nn.softplus(zs + dt_ref[...])
        g1, g2, g3 = _split3(gl)
        gsum = _dot(tri, g1) + _dot(tri, g2) + _dot(tri, g3)
        gcum.append(gsum)
        gcum_t.append(jnp.concatenate([gsum, jnp.zeros_like(gsum)], axis=0).T)
    items = [(i, h) for i in range(nsub) for h in heads]
    it = range(len(items))
    beta = [beta_all[i][:, direction * GDN_HEADS + h:direction * GDN_HEADS + h + 1] for i, h in items]
    gc = [gcum[i][:, lns[h]:lns[h] + 1] for i, h in items]
    gct = [gcum_t[i][lns[h]:lns[h] + 1, :] for i, h in items]
    glast = [gcum_t[i][lns[h]:lns[h] + 1, last:last + 1] for i, h in items]
    qh = [q_ref[0, rows[i], hsl[h]] for i, h in items]
    kh = [k_ref[0, rows[i], hsl[h]] for i, h in items]
    vh = [v_ref[0, rows[i], hsl[h]] for i, h in items]
    pr = range(len(items) // 2)
    kq2 = [_dot_nt(lanes2(jnp.concatenate([kh[2 * y], qh[2 * y]], axis=0),
                          jnp.concatenate([kh[2 * y + 1], qh[2 * y + 1]], axis=0)),
                   jnp.concatenate([lanes2(kh[2 * y], zd[:c]), lanes2(zc, zc),
                                    lanes2(zd[:c], kh[2 * y + 1]), lanes2(zc, zc)], axis=0)) for y in pr]
    kq = [kq2[x // 2][:, (x % 2) * LANES:(x % 2 + 1) * LANES] for x in it]
    dmat = [jnp.exp(jnp.where(incl, gc[x] - gct[x], -jnp.inf)) for x in it]
    a = [jnp.where(strict, (beta[x] * kq[x][:c]) * dmat[x], 0.0) for x in it]
    tinv2 = [lanes2(eye - a[2 * y], eye - a[2 * y + 1]) for y in pr]
    ab = [a[x].astype(BF16) for x in it]
    p2 = [_dot(lanes2(ab[2 * y], ab[2 * y + 1]), blockdiag_slots(ab[2 * y], ab[2 * y + 1])) for y in pr]
    for j in range(5):
        pb2 = [p2[y].astype(BF16) for y in pr]
        bd = [blockdiag_slots(pb2[y][:, :LANES], pb2[y][:, LANES:]) for y in pr]
        if j < 4:
            tp = [_dot(jnp.concatenate([tinv2[y].astype(BF16), pb2[y]], axis=0), bd[y]) for y in pr]
            tinv2 = [tinv2[y] + tp[y][:c] for y in pr]
            p2 = [tp[y][c:] for y in pr]
        else:
            tinv2 = [tinv2[y] + _dot(tinv2[y].astype(BF16), bd[y]) for y in pr]
    tinv = [tinv2[x // 2][:, (x % 2) * LANES:(x % 2 + 1) * LANES].astype(BF16) for x in it]
    eg = [jnp.exp(gc[x]) for x in it]
    kf = [kh[x].astype(F32) for x in it]
    rhs = [jnp.concatenate([vh[x].astype(F32) * beta[x], kf[x] * (beta[x] * eg[x])], axis=1).astype(BF16) for x in it]
    zr = jnp.zeros((LANES - c, 2 * dh), BF16)
    uw = [_dot(tinv[x], jnp.concatenate([rhs[x], zr], axis=0)) for x in it]
    wq = [jnp.concatenate([uw[x][:, dh:], qh[x].astype(F32) * eg[x]], axis=0).astype(BF16) for x in it]
    qkd = [(kq[x][c:] * dmat[x]).astype(BF16) for x in it]
    kd = [(kf[x] * jnp.exp(glast[x] - gc[x])).astype(BF16) for x in it]

    order = list(range(nsub)) if direction == 0 else list(range(nsub - 1, -1, -1))
    state = [s_ref[h] for h in heads]
    hp = range(GDN_HEADS // 2)
    for i in order:
        rs = rows[i]
        xs = [i * GDN_HEADS + h for h in heads]
        sb = [state[h].astype(BF16) for h in heads]
        ws2 = [_dot(lanes2(wq[xs[2 * y]], wq[xs[2 * y + 1]]),
                    jnp.concatenate([lanes2(sb[2 * y], zd), lanes2(zd, sb[2 * y + 1])], axis=0)) for y in hp]
        ws = [ws2[h // 2][:, (h % 2) * dh:(h % 2 + 1) * dh] for h in heads]
        vnb = [(uw[xs[h]][:, :dh] - ws[h][:c]).astype(BF16) for h in heads]
        o2 = [_dot(lanes2(qkd[xs[2 * y]], qkd[xs[2 * y + 1]]),
                   jnp.concatenate([lanes2(vnb[2 * y], zd[:c]), lanes2(zc, zc),
                                    lanes2(zd[:c], vnb[2 * y + 1]), lanes2(zc, zc)], axis=0)) for y in hp]
        o = [ws[h][c:] + o2[h // 2][:, (h % 2) * dh:(h % 2 + 1) * dh] for h in heads]
        kv2 = [_dot_tn(jnp.concatenate([kd[xs[2 * y]], kd[xs[2 * y + 1]]], axis=0),
                       jnp.concatenate([lanes2(vnb[2 * y], zd[:c]), lanes2(zd[:c], vnb[2 * y + 1])], axis=0)) for y in hp]
        state = [state[h] * jnp.exp(glast[xs[h]]) + kv2[h // 2][:, (h % 2) * dh:(h % 2 + 1) * dh] for h in heads]
        for h in heads:
            if final:
                tot = of_ref[0, rs, hsl[h]] + o[h]
                y = tot * lax.rsqrt(jnp.mean(tot * tot, axis=-1, keepdims=True) + EPS) * ng_ref[...]
                o_ref[0, rs, hsl[h]] = (y * _silu(gg_ref[0, rs, hsl[h]].astype(F32))).astype(o_ref.dtype)
            else:
                o_ref[0, rs, hsl[h]] = o[h].astype(o_ref.dtype)
    for h in heads:
        s_ref[h] = state[h]


def _gdn_scan(q, k, v, zs, a_log, dt_bias, direction, o_fwd=None, z=None, gate_blk=None, norm_g=None):
    b, l, w = q.shape
    c = GDN_CHUNK * GDN_CHUNKS_PER_STEP
    n = l // c
    final = o_fwd is not None
    if direction == 0:
        cmap = lambda i, t: (i, t, 0)
    else:
        cmap = lambda i, t: (i, n - 1 - t, 0)
    blk = pl.BlockSpec((1, c, w), cmap)
    vec = pl.BlockSpec((1, LANES), lambda i, t: (0, 0))
    pad = lambda x: jnp.pad(x.reshape(1, -1).astype(F32), ((0, 0), (2 * GDN_HEADS, LANES - 4 * GDN_HEADS)))
    in_specs = [blk, blk, blk, pl.BlockSpec((1, c, LANES), cmap), vec, vec]
    args = [q, k, v, zs, pad(a_log), pad(dt_bias)]
    if final:
        if direction == 0:
            gmap = lambda i, t: (i, t, gate_blk)
        else:
            gmap = lambda i, t: (i, n - 1 - t, gate_blk)
        in_specs += [blk, pl.BlockSpec((1, c, w), gmap), pl.BlockSpec((1, GDN_HEAD_DIM), lambda i, t: (0, 0))]
        args += [o_fwd, z, norm_g.reshape(1, GDN_HEAD_DIM).astype(F32)]
    return pl.pallas_call(
        functools.partial(_gdn_scan_kernel, direction=direction, final=final),
        grid=(b, n),
        in_specs=in_specs,
        out_specs=blk,
        out_shape=jax.ShapeDtypeStruct((b, l, w), BF16 if final else F32),
        scratch_shapes=[pltpu.VMEM((GDN_HEADS, GDN_HEAD_DIM, GDN_HEAD_DIM), F32)],
        compiler_params=_cparams("parallel", "arbitrary"),
        name="gdn_scan_bwd" if direction else "gdn_scan_fwd",
    )(*args)


def _gdn_branch(z, zs, conv_w, a_log, dt_bias, norm_g, gate_blk):
    q, k, v = _gdn_prep(z, conv_w)
    o_f = _gdn_scan(q, k, v, zs, a_log, dt_bias, 0)
    return _gdn_scan(q, k, v, zs, a_log, dt_bias, 1, o_fwd=o_f, z=z, gate_blk=gate_blk, norm_g=norm_g)


def _softmax_t(s, extra=None):
    m = jnp.max(s, axis=0, keepdims=True)
    if extra is not None:
        m = jnp.maximum(m, extra)
    p = jnp.exp2(s - m)
    den = jnp.sum(p, axis=0, keepdims=True)
    if extra is not None:
        den = den + jnp.exp2(extra - m)
    return p.astype(BF16), den, m


def _t_bf16(x):
    return x.astype(F32).T.astype(BF16)


def _xattn_kernel(q_ref, g_ref, kv_ref, o_ref):
    dh = X_HEAD_DIM
    for h in range(X_HEADS):
        hs = slice(h * dh, (h + 1) * dh)
        qh = q_ref[0, :, hs].astype(BF16)
        kh = kv_ref[0, :, hs]
        vh = kv_ref[0, :, X_WIDTH + h * dh:X_WIDTH + (h + 1) * dh]
        s = _dot_nt(qh, kh) * (dh ** -0.5)
        p = jnp.exp(s - jnp.max(s, axis=-1, keepdims=True))
        den = jnp.sum(p, axis=-1, keepdims=True)
        o = _dot(p.astype(BF16), vh) / den
        o_ref[0, :, hs] = (o * _silu(g_ref[0, :, hs].astype(F32))).astype(o_ref.dtype)


def _xattn(z, kv, q_blk, g_blk, tq=512):
    b, l, _ = z.shape
    m = kv.shape[1]
    w = X_WIDTH
    return pl.pallas_call(
        _xattn_kernel,
        grid=(b, l // tq),
        in_specs=[
            pl.BlockSpec((1, tq, w), lambda i, t: (i, t, q_blk)),
            pl.BlockSpec((1, tq, w), lambda i, t: (i, t, g_blk)),
            pl.BlockSpec((1, m, 2 * w), lambda i, t: (i, 0, 0)),
        ],
        out_specs=pl.BlockSpec((1, tq, w), lambda i, t: (i, t, 0)),
        out_shape=jax.ShapeDtypeStruct((b, l, w), BF16),
        compiler_params=_cparams("parallel", "arbitrary"),
        name="xattn",
    )(z, z, kv)


def _out_proj_kernel(*refs, widths, gated):
    nparts = len(widths)
    y_refs = refs[:nparts]
    if gated:
        e_ref, w_ref, g_ref, x_ref, o_ref = refs[nparts:]
    else:
        w_ref, g_ref, x_ref, o_ref = refs[nparts:]
    d = o_ref.shape[1]
    cw = 256
    ys = [y_ref[...] for y_ref in y_refs]
    if gated:
        ys[0] = ys[0] * e_ref[...]
    ys = [y.astype(BF16) for y in ys]
    ssq = jnp.zeros((o_ref.shape[0], 1), F32)
    for c0 in range(0, d, cw):
        acc = None
        off = 0
        for y, wd in zip(ys, widths):
            part = _dot(y, w_ref[off:off + wd, c0:c0 + cw])
            acc = part if acc is None else acc + part
            off += wd
        ssq = ssq + jnp.sum(acc * acc, axis=-1, keepdims=True)
        o_ref[:, c0:c0 + cw] = acc
    r = lax.rsqrt(ssq * (1.0 / d) + EPS)
    for c0 in range(0, d, cw):
        o_ref[:, c0:c0 + cw] = x_ref[:, c0:c0 + cw] + o_ref[:, c0:c0 + cw] * r * g_ref[:, c0:c0 + cw]


def _out_proj(parts, w_out, post_g, x2d, gate=None, tm=512):
    t, d = x2d.shape
    widths = tuple(int(p.shape[1]) for p in parts)
    kdim = sum(widths)
    gates = [] if gate is None else [gate]
    return pl.pallas_call(
        functools.partial(_out_proj_kernel, widths=widths, gated=gate is not None),
        grid=(t // tm,),
        in_specs=[pl.BlockSpec((tm, wd), lambda i: (i, 0)) for wd in widths]
        + [pl.BlockSpec((tm, widths[0]), lambda i: (i, 0)) for _ in gates] + [
            pl.BlockSpec((kdim, d), lambda i: (0, 0)),
            pl.BlockSpec((1, d), lambda i: (0, 0)),
            pl.BlockSpec((tm, d), lambda i: (i, 0)),
        ],
        out_specs=pl.BlockSpec((tm, d), lambda i: (i, 0)),
        out_shape=jax.ShapeDtypeStruct((t, d), F32),
        compiler_params=_cparams("parallel"),
        name="out_proj",
    )(*parts, *gates, w_out.astype(BF16), post_g.reshape(1, d).astype(F32), x2d)


def _mem_kv(mem, mem_g, w_mem_kv):
    b, m, d = mem.shape
    kv = _norm_matmul(mem.reshape(b * m, d), mem_g, w_mem_kv, 1024, out_dtype=BF16, name="mem_kv")
    return kv.reshape(b, m, 2 * X_WIDTH)


E_QKV, E_HY, E_GHY, E_GGDN, E_XQ, E_GX, E_BIG = 0, 3072, 6144, 7168, 8192, 8704, 9216


def _even_layer(x, mem, pre_g, post_g, w_in, w_out, hy_conv_w, hy_conv_b, hy_w1, hy_b1, hy_w2, hy_b2, hy_w3, hy_freq,
                hy_skip, gdn_conv_w, gdn_a_log, gdn_dt_bias, gdn_norm_g, mem_g, w_mem_kv):
    b, l, d = x.shape
    x2d = x.reshape(b * l, d)
    w_big = jnp.concatenate([w_in[:, 4096:7168], w_in[:, 0:3072], w_in[:, 3072:4096], w_in[:, 7168:8192],
                             w_in[:, 8224:8736], w_in[:, 8736:9248]], axis=1)
    w_small = jnp.pad(w_in[:, 8192:8224], ((0, 0), (0, LANES - 4 * GDN_HEADS)))
    z, zs = _norm_matmul(x2d, pre_g, w_big, E_BIG // 3, out_dtype=BF16, tm=1024, name="even_in", w_side=w_small)
    z = z.reshape(b, l, E_BIG)
    zs = zs.reshape(b, l, LANES)
    conv, e = _hyena_branch(z, l, hy_conv_w, hy_conv_b, hy_w1, hy_b1, hy_w2, hy_b2, hy_w3, hy_freq, hy_skip)
    y_b = _gdn_branch(z, zs, gdn_conv_w, gdn_a_log, gdn_dt_bias, gdn_norm_g, E_GGDN // GDN_WIDTH)
    kv = _mem_kv(mem, mem_g, w_mem_kv)
    y_x = _xattn(z, kv, E_XQ // X_WIDTH, E_GX // X_WIDTH)
    t = b * l
    out = _out_proj([conv.reshape(t, -1), y_b.reshape(t, -1), y_x.reshape(t, -1)], w_out, post_g, x2d,
                    gate=e.reshape(t, -1))
    return out.reshape(b, l, d)


DIL_PATTERNS = ((128, 1), (512, 4), (2048, 16))
N_DIL = len(DIL_PATTERNS)
DIL_HEADS = 4
DIL_HEAD_DIM = 128
DIL_WIDTH = DIL_HEADS * DIL_HEAD_DIM
SWA_Q_HEADS = 16
SWA_KV_HEADS = 2
SWA_HEAD_DIM = 64
SWA_WIDTH = SWA_Q_HEADS * SWA_HEAD_DIM
SWA_HALF_WINDOW = 128
O_CQKV, O_GC, O_DQ, O_GD, O_XQ, O_GX, O_DKV, O_ALL = 0, 4608, 5120, 6144, 7168, 7680, 8192, 8448


def _rope_tables(l, dh):
    half = dh // 8
    inv = ROPE_THETA ** (-jnp.arange(half, dtype=F32) / half)
    ang = jnp.arange(l, dtype=F32)[:, None] * inv[None, :]
    cos, sin = jnp.cos(ang), jnp.sin(ang)
    one = jnp.ones((l, dh - 2 * half), F32)
    zero_h = jnp.zeros((l, half), F32)
    zero_r = jnp.zeros((l, dh - 2 * half), F32)
    c = jnp.concatenate([cos, cos, one], axis=1)
    sa = jnp.concatenate([-sin, zero_h, zero_r], axis=1)
    sb = jnp.concatenate([zero_h, sin, zero_r], axis=1)
    rep = LANES // dh
    return tuple(jnp.tile(t, (1, rep)) for t in (c, sa, sb))


def _odd_prep_kernel(c_ref, dq_ref, dkv_ref, c1_ref, a1_ref, b1_ref, c2_ref, a2_ref, b2_ref, *refs):
    dil_refs = refs[:3 * N_DIL]
    dqe_ref, dk_ref, dv_ref, xs_ref = refs[3 * N_DIL:]
    tl = c_ref.shape[1]
    lane = lax.broadcasted_iota(jnp.int32, (tl, LANES), 1)
    low = lane < SWA_HEAD_DIM
    c1, a1, b1 = c1_ref[...], a1_ref[...], b1_ref[...]
    c2, a2, b2 = c2_ref[...], a2_ref[...], b2_ref[...]
    h1 = DIL_HEAD_DIM // 8
    h2 = SWA_HEAD_DIM // 8

    def rope1(x):
        return x * c1 + pltpu.roll(x, LANES - h1, 1) * a1 + pltpu.roll(x, h1, 1) * b1

    def rope2(x):
        return x * c2 + pltpu.roll(x, LANES - h2, 1) * a2 + pltpu.roll(x, h2, 1) * b2

    nblk = N_DIL * DIL_HEADS
    slot = 0
    for s in range(3):
        for gi, (_, d) in enumerate(DIL_PATTERNS):
            dst = dil_refs[3 * gi + s]
            for h in range(DIL_HEADS):
                j = s * nblk + gi * DIL_HEADS + h
                x = c_ref[0, :, j * LANES:(j + 1) * LANES]
                if s < 2:
                    x = rope1(x.astype(F32))
                hs = slice(h * LANES, (h + 1) * LANES)
                if d == 1:
                    dst[0, 0, :, hs] = x.astype(BF16)
                else:
                    xs_ref[slot] = x.astype(F32)
                    for r in range(d):
                        dst[0, r, :, hs] = xs_ref[slot, pl.ds(r, tl // d, stride=d), :].astype(BF16)
                    slot += 1
    for j in range(SWA_Q_HEADS // 2):
        xr = rope2(dq_ref[0, :, j * LANES:(j + 1) * LANES].astype(F32)) * (SWA_HEAD_DIM ** -0.5 * LOG2E)
        dqe_ref[0, :, (2 * j) * LANES:(2 * j + 1) * LANES] = jnp.where(low, xr, 0.0).astype(BF16)
        dqe_ref[0, :, (2 * j + 1) * LANES:(2 * j + 2) * LANES] = jnp.where(low, pltpu.roll(xr, SWA_HEAD_DIM, 1), 0.0).astype(BF16)
    kr = rope2(dkv_ref[0, :, 0:LANES].astype(F32))
    vv = dkv_ref[0, :, LANES:2 * LANES].astype(F32)
    kr_sw = pltpu.roll(kr, SWA_HEAD_DIM, 1)
    vv_sw = pltpu.roll(vv, SWA_HEAD_DIM, 1)
    dk_ref[0, :, 0:LANES] = jnp.where(low, kr, 0.0).astype(BF16)
    dk_ref[0, :, LANES:2 * LANES] = jnp.where(low, kr_sw, 0.0).astype(BF16)
    dv_ref[0, :, 0:LANES] = jnp.where(low, vv, vv_sw).astype(BF16)
    dv_ref[0, :, LANES:2 * LANES] = jnp.where(low, vv_sw, vv).astype(BF16)


def _odd_prep(z, tl=256):
    b, l, _ = z.shape
    wc = 3 * N_DIL * DIL_WIDTH
    t1 = _rope_tables(l, DIL_HEAD_DIM)
    t2 = _rope_tables(l, SWA_HEAD_DIM)
    tab = pl.BlockSpec((tl, LANES), lambda i, t: (t, 0))

    def spec(w):
        return pl.BlockSpec((1, tl, w), lambda i, t: (i, t, 0))

    def shape(w):
        return jax.ShapeDtypeStruct((b, l, w), BF16)

    dil_specs, dil_shapes = [], []
    for _, d in DIL_PATTERNS:
        for _ in range(3):
            dil_specs.append(pl.BlockSpec((1, d, tl // d, DIL_WIDTH), lambda i, t: (i, 0, t, 0)))
            dil_shapes.append(jax.ShapeDtypeStruct((b, d, l // d, DIL_WIDTH), BF16))
    n_staged = 3 * DIL_HEADS * sum(1 for _, d in DIL_PATTERNS if d > 1)
    outs = pl.pallas_call(
        _odd_prep_kernel,
        grid=(b, l // tl),
        in_specs=[
            pl.BlockSpec((1, tl, wc), lambda i, t: (i, t, 0)),
            pl.BlockSpec((1, tl, SWA_WIDTH), lambda i, t: (i, t, O_DQ // SWA_WIDTH)),
            pl.BlockSpec((1, tl, 2 * LANES), lambda i, t: (i, t, O_DKV // (2 * LANES))),
            tab, tab, tab, tab, tab, tab,
        ],
        out_specs=dil_specs + [spec(2 * SWA_WIDTH), spec(2 * LANES), spec(2 * LANES)],
        out_shape=dil_shapes + [shape(2 * SWA_WIDTH), shape(2 * LANES), shape(2 * LANES)],
        scratch_shapes=[pltpu.VMEM((n_staged, tl, LANES), F32)],
        compiler_params=_cparams("parallel", "arbitrary"),
        name="odd_prep",
    )(z, z, z, *t1, *t2)
    dil = [tuple(outs[3 * gi:3 * gi + 3]) for gi in range(N_DIL)]
    return dil, outs[3 * N_DIL], outs[3 * N_DIL + 1], outs[3 * N_DIL + 2]


def _band_bias_t(tq, half, ls):
    p0 = pl.program_id(1) * tq
    j = lax.broadcasted_iota(jnp.int32, (tq + 2 * half, tq), 0)
    i = lax.broadcasted_iota(jnp.int32, (tq + 2 * half, tq), 1)
    kpos = p0 - half + j
    valid = (j >= i) & (j - i <= 2 * half) & (kpos >= 0) & (kpos < ls)
    return jnp.where(valid, 0.0, -jnp.inf)


def _band_mask(tq, half, ls):
    p0 = pl.program_id(1) * tq
    i = lax.broadcasted_iota(jnp.int32, (tq, tq + 2 * half), 0)
    j = lax.broadcasted_iota(jnp.int32, (tq, tq + 2 * half), 1)
    kpos = p0 - half + j
    return (j >= i) & (j - i <= 2 * half) & (kpos >= 0) & (kpos < ls)


def _band_attn_kernel(q_ref, kp_ref, kc_ref, kn_ref, vp_ref, vc_ref, vn_ref, o_ref, lse_ref, *, half, ls):
    tq = q_ref.shape[1]
    dh = DIL_HEAD_DIM
    valid = _band_mask(tq, half, ls)
    lane = lax.broadcasted_iota(jnp.int32, (tq, LANES), 1)
    lse_all = jnp.zeros((tq, LANES), F32)
    heads = range(DIL_HEADS)
    hsl = [slice(h * dh, (h + 1) * dh) for h in heads]
    kall = [jnp.concatenate([kp_ref[0, :, s], kc_ref[0, :, s], kn_ref[0, :, s]], axis=0) for s in hsl]
    vall = [jnp.concatenate([vp_ref[0, :, s], vc_ref[0, :, s], vn_ref[0, :, s]], axis=0) for s in hsl]
    sc = [jnp.where(valid, _dot_nt(q_ref[0, :, hsl[h]], kall[h]) * (dh ** -0.5), -jnp.inf) for h in heads]
    m = [jnp.max(sc[h], axis=-1, keepdims=True) for h in heads]
    p = [jnp.exp(sc[h] - m[h]) for h in heads]
    den = [jnp.sum(p[h], axis=-1, keepdims=True) for h in heads]
    o = [_dot(p[h].astype(BF16), vall[h]) / den[h] for h in heads]
    for h in heads:
        o_ref[0, :, hsl[h]] = o[h].astype(o_ref.dtype)
        lse_all = jnp.where(lane == h, m[h] + jnp.log(den[h]), lse_all)
    lse_ref[0] = lse_all


def _band_specs(tq, half, ls, w, col):
    r = tq // half
    nb = ls // half
    cur = pl.BlockSpec((1, tq, w), lambda i, t: (i, t, col))
    prev = pl.BlockSpec((1, half, w), lambda i, t: (i, jnp.maximum(t * r - 1, 0), col))
    nxt = pl.BlockSpec((1, half, w), lambda i, t: (i, jnp.minimum((t + 1) * r, nb - 1), col))
    return cur, prev, nxt


def _band_attn(q, k, v, half):
    n, ls, w = q.shape
    tq = min(2 * LANES, ls)
    cur, prev, nxt = _band_specs(tq, half, ls, w, 0)
    return pl.pallas_call(
        functools.partial(_band_attn_kernel, half=half, ls=ls),
        grid=(n, ls // tq),
        in_specs=[cur, prev, cur, nxt, prev, cur, nxt],
        out_specs=[cur, pl.BlockSpec((1, tq, LANES), lambda i, t: (i, t, 0))],
        out_shape=[jax.ShapeDtypeStruct((n, ls, w), BF16), jax.ShapeDtypeStruct((n, ls, LANES), F32)],
        compiler_params=_cparams("parallel", "arbitrary"),
        name="band_attn",
    )(q, k, k, k, v, v, v)


def _dil_merge_kernel(o0_ref, o1_ref, o2_ref, l0_ref, l1_ref, l2_ref, g_ref, y_ref, os_ref, ls_ref):
    dh = DIL_HEAD_DIM
    tl = y_ref.shape[1]
    o_refs = (o0_ref, o1_ref, o2_ref)
    l_refs = (l0_ref, l1_ref, l2_ref)
    slot = 0
    lse, outs = [], []
    for gi, (_, d) in enumerate(DIL_PATTERNS):
        if d == 1:
            lse.append(l_refs[gi][0, 0])
            outs.append([o_refs[gi][0, 0, :, h * dh:(h + 1) * dh].astype(F32) for h in range(DIL_HEADS)])
            continue
        for r in range(d):
            ls_ref[gi, pl.ds(r, tl // d, stride=d), :] = l_refs[gi][0, r]
        lse.append(ls_ref[gi])
        per_head = []
        for h in range(DIL_HEADS):
            for r in range(d):
                os_ref[slot, pl.ds(r, tl // d, stride=d), :] = o_refs[gi][0, r, :, h * dh:(h + 1) * dh].astype(F32)
            per_head.append(os_ref[slot])
            slot += 1
        outs.append(per_head)
    for h in range(DIL_HEADS):
        hs = slice(h * dh, (h + 1) * dh)
        ls = [x[:, h:h + 1] for x in lse]
        m = jnp.maximum(jnp.maximum(ls[0], ls[1]), ls[2])
        ws = [jnp.exp(x - m) for x in ls]
        den = ws[0] + ws[1] + ws[2]
        y = (ws[0] / den) * outs[0][h] + (ws[1] / den) * outs[1][h] + (ws[2] / den) * outs[2][h]
        y_ref[0, :, hs] = (y * _silu(g_ref[0, :, hs].astype(F32))).astype(y_ref.dtype)


def _dil_merge(outs, lses, z, tl=512):
    b, _, l, w = outs[0].shape
    o_specs = [pl.BlockSpec((1, d, tl // d, w), lambda i, t: (i, 0, t, 0)) for _, d in DIL_PATTERNS]
    l_specs = [pl.BlockSpec((1, d, tl // d, LANES), lambda i, t: (i, 0, t, 0)) for _, d in DIL_PATTERNS]
    n_staged = DIL_HEADS * sum(1 for _, d in DIL_PATTERNS if d > 1)
    return pl.pallas_call(
        _dil_merge_kernel,
        grid=(b, l // tl),
        in_specs=o_specs + l_specs + [pl.BlockSpec((1, tl, w), lambda i, t: (i, t, O_GC // DIL_WIDTH))],
        out_specs=pl.BlockSpec((1, tl, w), lambda i, t: (i, t, 0)),
        out_shape=jax.ShapeDtypeStruct((b, l, w), BF16),
        scratch_shapes=[pltpu.VMEM((n_staged, tl, LANES), F32), pltpu.VMEM((N_DIL, tl, LANES), F32)],
        compiler_params=_cparams("parallel", "arbitrary"),
        name="dil_merge",
    )(*outs, *lses, z)


def _dilated_branch(dil, z):
    outs, lses = [], []
    for (window, d), (q, k, v) in zip(DIL_PATTERNS, dil):
        b, _, ls, w = q.shape
        half = window // (2 * d)
        o, lse = _band_attn(q.reshape(b * d, ls, w), k.reshape(b * d, ls, w), v.reshape(b * d, ls, w), half)
        outs.append(o.reshape(b, d, ls, w))
        lses.append(lse.reshape(b, d, ls, LANES))
    return _dil_merge(outs, lses, z)


def _swa_kernel(q_ref, kp_ref, kc_ref, kn_ref, vp_ref, vc_ref, vn_ref, g_ref, sink_ref, o_ref, *, half, ls):
    tq = q_ref.shape[1]
    dh = SWA_HEAD_DIM
    bias1 = _band_bias_t(tq, half, ls)
    bias = jnp.concatenate([bias1, bias1], axis=1)
    b_prev, b_next = bias[:half], bias[half + tq:]
    low = lax.broadcasted_iota(jnp.int32, (LANES, tq), 0) < dh
    first = lax.broadcasted_iota(jnp.int32, (1, 2 * tq), 1) < tq
    grp = SWA_Q_HEADS // SWA_KV_HEADS
    for g in range(SWA_KV_HEADS):
        gs = slice(g * LANES, (g + 1) * LANES)
        kall = jnp.concatenate([kp_ref[0, :, gs], kc_ref[0, :, gs], kn_ref[0, :, gs]], axis=0)
        vt = _t_bf16(jnp.concatenate([vp_ref[0, :, gs], vc_ref[0, :, gs], vn_ref[0, :, gs]], axis=0))
        pairs = range(grp // 2)
        h0 = [g * grp + 2 * jp for jp in pairs]
        q2 = [q_ref[0, :, h * LANES:(h + 2) * LANES] for h in h0]
        q2 = [jnp.concatenate([x[:, :LANES], x[:, LANES:]], axis=0) for x in q2]
        sc = [_dot_nt(kall, q2[jp]) for jp in pairs]
        sc = [jnp.concatenate([s[:half] + b_prev, s[half:half + tq], s[half + tq:] + b_next], axis=0) for s in sc]
        sk = [jnp.where(first, sink_ref[:, h:h + 1], sink_ref[:, h + 1:h + 2]) * LOG2E for h in h0]
        sm = [_softmax_t(sc[jp], sk[jp]) for jp in pairs]
        ot = [_dot(vt, sm[jp][0]) / sm[jp][1] for jp in pairs]
        for jp in pairs:
            blk = (g * grp) // 2 + jp
            bs = slice(blk * LANES, (blk + 1) * LANES)
            y = jnp.where(low, ot[jp][:, :tq], ot[jp][:, tq:]).T * _silu(g_ref[0, :, bs].astype(F32))
            o_ref[0, :, bs] = y.astype(o_ref.dtype)


def _swa_branch(dqe, dk, dv, z, sink, tq=128):
    b, l, _ = dqe.shape
    half = SWA_HALF_WINDOW
    assert tq == half == LANES
    _, prev, nxt = _band_specs(tq, half, l, 2 * LANES, 0)
    cur = pl.BlockSpec((1, tq, 2 * LANES), lambda i, t: (i, t, 0))
    sink_p = jnp.pad(sink.reshape(1, SWA_Q_HEADS).astype(F32), ((0, 0), (0, LANES - SWA_Q_HEADS)))
    return pl.pallas_call(
        functools.partial(_swa_kernel, half=half, ls=l),
        grid=(b, l // tq),
        in_specs=[
            pl.BlockSpec((1, tq, 2 * SWA_WIDTH), lambda i, t: (i, t, 0)),
            prev, cur, nxt, prev, cur, nxt,
            pl.BlockSpec((1, tq, SWA_WIDTH), lambda i, t: (i, t, O_GD // SWA_WIDTH)),
            pl.BlockSpec((1, LANES), lambda i, t: (0, 0)),
        ],
        out_specs=pl.BlockSpec((1, tq, SWA_WIDTH), lambda i, t: (i, t, 0)),
        out_shape=jax.ShapeDtypeStruct((b, l, SWA_WIDTH), BF16),
        compiler_params=_cparams("parallel", "arbitrary"),
        name="swa",
    )(dqe, dk, dk, dk, dv, dv, dv, z, sink_p)


def _odd_layer(x, mem, pre_g, post_g, w_in, w_out, swa_sink, mem_g, w_mem_kv):
    b, l, d = x.shape
    x2d = x.reshape(b * l, d)
    w_re = jnp.concatenate([w_in[:, 0:6144], w_in[:, 6400:8448], w_in[:, 6144:6400]], axis=1)
    z = _norm_matmul(x2d, pre_g, w_re, O_ALL // 3, out_dtype=BF16, tm=1024, name="odd_in").reshape(b, l, O_ALL)
    dil, dqe, dk, dv = _odd_prep(z)
    y_c = _dilated_branch(dil, z)
    y_d = _swa_branch(dqe, dk, dv, z, swa_sink)
    kv = _mem_kv(mem, mem_g, w_mem_kv)
    y_x = _xattn(z, kv, O_XQ // X_WIDTH, O_GX // X_WIDTH)
    t = b * l
    out = _out_proj([y_c.reshape(t, -1), y_d.reshape(t, -1), y_x.reshape(t, -1)], w_out, post_g, x2d)
    return out.reshape(b, l, d)


def _trunk(x, mem, even_params, odd_params):
    x = _even_layer(x, mem, *[p[0] for p in even_params])
    return _odd_layer(x, mem, *[p[0] for p in odd_params])


def kernel(x_prompt, x_sample, mem_prompt, mem_sample, e_pre_g, e_post_g, e_w_in, e_w_out, hy_conv_w, hy_conv_b,
           hy_filt_w1, hy_filt_b1, hy_filt_w2, hy_filt_b2, hy_filt_w3, hy_freq, hy_skip, gdn_conv_w, gdn_A_log,
           gdn_dt_bias, gdn_norm_g, e_mem_g, e_w_mem_kv, o_pre_g, o_post_g, o_w_in, o_w_out, swa_sink, o_mem_g,
           o_w_mem_kv):
    even_params = (e_pre_g, e_post_g, e_w_in, e_w_out, hy_conv_w, hy_conv_b, hy_filt_w1, hy_filt_b1, hy_filt_w2,
                   hy_filt_b2, hy_filt_w3, hy_freq, hy_skip, gdn_conv_w, gdn_A_log, gdn_dt_bias, gdn_norm_g,
                   e_mem_g, e_w_mem_kv)
    odd_params = (o_pre_g, o_post_g, o_w_in, o_w_out, swa_sink, o_mem_g, o_w_mem_kv)
    y_prompt = _trunk(x_prompt, mem_prompt, even_params, odd_params)
    y_sample = _trunk(x_sample, mem_sample, even_params, odd_params)
    return (y_prompt, y_sample)
```

```python
import functools
import math

import jax
import jax.numpy as jnp
import numpy as np
from jax import lax
from jax.experimental import pallas as pl
from jax.experimental.pallas import tpu as pltpu

D_MODEL = 1024
EPS = 1e-6
ROPE_THETA = 500000.0
HY_WIDTH = 1024
HY_EMB = 33
HY_BANDS = (HY_EMB - 1) // 2
HY_FILT_HIDDEN = 64
HY_TARGET = 1e-2
HY_FAST_PCT = 0.3
HY_SLOW_PCT = 1.5
GDN_HEADS = 8
GDN_HEAD_DIM = 128
GDN_WIDTH = GDN_HEADS * GDN_HEAD_DIM
GDN_CHUNK = 64
X_HEADS = 4
X_HEAD_DIM = 128
X_WIDTH = X_HEADS * X_HEAD_DIM

LOG2E = math.log2(math.e)
LANES = 128
FFT_N2 = 128
VMEM_LIMIT_BYTES = 48 * 1024 * 1024

BF16 = jnp.bfloat16
F32 = jnp.float32


def _cparams(*sem):
    return pltpu.CompilerParams(dimension_semantics=sem, vmem_limit_bytes=VMEM_LIMIT_BYTES)


def _dot(a, b):
    return jnp.dot(a, b, preferred_element_type=F32)


def _silu(x):
    hx = 0.5 * x
    return hx + hx * jnp.tanh(hx)


def _norm_matmul_kernel(*refs, side):
    if side:
        x_ref, g_ref, w_ref, ws_ref, o_ref, os_ref, xn_ref = refs
    else:
        x_ref, g_ref, w_ref, o_ref, xn_ref = refs

    @pl.when(pl.program_id(1) == 0)
    def _():
        x = x_ref[...]
        ms = jnp.mean(x * x, axis=-1, keepdims=True)
        xn_ref[...] = (x * lax.rsqrt(ms + EPS) * g_ref[...]).astype(BF16)
        if side:
            os_ref[...] = _dot(xn_ref[...], ws_ref[...])

    o_ref[...] = _dot(xn_ref[...], w_ref[...]).astype(o_ref.dtype)


def _norm_matmul(x2d, g, w, tn, out_dtype=F32, tm=512, name="norm_matmul", w_side=None):
    t, d = x2d.shape
    n = w.shape[1]
    assert t % tm == 0 and n % tn == 0
    side = w_side is not None
    in_specs = [
        pl.BlockSpec((tm, d), lambda i, j: (i, 0)),
        pl.BlockSpec((1, d), lambda i, j: (0, 0)),
        pl.BlockSpec((d, tn), lambda i, j: (0, j)),
    ]
    out_specs = [pl.BlockSpec((tm, tn), lambda i, j: (i, j))]
    out_shape = [jax.ShapeDtypeStruct((t, n), out_dtype)]
    args = [x2d, g.reshape(1, d).astype(F32), w.astype(BF16)]
    if side:
        in_specs.append(pl.BlockSpec((d, LANES), lambda i, j: (0, 0)))
        out_specs.append(pl.BlockSpec((tm, LANES), lambda i, j: (i, 0)))
        out_shape.append(jax.ShapeDtypeStruct((t, LANES), F32))
        args.append(w_side.astype(BF16))
    outs = pl.pallas_call(
        functools.partial(_norm_matmul_kernel, side=side),
        grid=(t // tm, n // tn),
        in_specs=in_specs,
        out_specs=out_specs,
        out_shape=out_shape,
        scratch_shapes=[pltpu.VMEM((tm, d), BF16)],
        compiler_params=_cparams("parallel", "arbitrary"),
        name=name,
    )(*args)
    return tuple(outs) if side else outs[0]


HALO_ROWS = 16


TAP_ROWS = 128
TAP_WINDOW = 256


def _fill_halo_buf(buf_ref, zc_ref, zp_ref, zn_ref, t, nt):
    tl = zc_ref.shape[1]
    hr = HALO_ROWS
    buf_ref[0:hr, :] = jnp.where(t > 0, zp_ref[0].astype(F32), 0.0).astype(BF16)
    buf_ref[hr:hr + tl, :] = zc_ref[0].astype(BF16)
    buf_ref[hr + tl:2 * hr + tl, :] = jnp.where(t < nt - 1, zn_ref[0].astype(F32), 0.0).astype(BF16)
    buf_ref[2 * hr + tl:, :] = jnp.zeros((buf_ref.shape[0] - 2 * hr - tl, buf_ref.shape[1]), BF16)


def _halo_buf_rows(tl):
    return tl - TAP_ROWS + TAP_WINDOW


def _tap_shift_matrix(ntaps):
    n = lax.broadcasted_iota(jnp.int32, (ntaps, TAP_ROWS, TAP_WINDOW), 1)
    j = lax.broadcasted_iota(jnp.int32, (ntaps, TAP_ROWS, TAP_WINDOW), 0)
    m = lax.broadcasted_iota(jnp.int32, (ntaps, TAP_ROWS, TAP_WINDOW), 2)
    return (m == HALO_ROWS + n + j - ntaps // 2).astype(BF16).reshape(ntaps * TAP_ROWS, TAP_WINDOW)


def _taps(buf_ref, s_ref, r0, lanes):
    return _dot(s_ref[...], buf_ref[r0:r0 + TAP_WINDOW, lanes])


def _halo_specs(tl, w, l, col):
    r = tl // HALO_ROWS
    nb = l // HALO_ROWS
    cur = pl.BlockSpec((1, tl, w), lambda i, t: (i, t, col))
    prev = pl.BlockSpec((1, HALO_ROWS, w), lambda i, t: (i, jnp.maximum(t * r - 1, 0), col))
    nxt = pl.BlockSpec((1, HALO_ROWS, w), lambda i, t: (i, jnp.minimum((t + 1) * r, nb - 1), col))
    return cur, prev, nxt


def _hy_prep_kernel(zc_ref, zp_ref, zn_ref, g_ref, cw_ref, cb_ref, s_ref, u_ref, e_ref, buf_ref):
    t = pl.program_id(1)
    nt = pl.num_programs(1)
    tl = zc_ref.shape[1]
    c = HY_WIDTH
    tr = TAP_ROWS
    _fill_halo_buf(buf_ref, zc_ref, zp_ref, zn_ref, t, nt)
    cw = 256
    for r0 in range(0, tl, tr):
        rs = slice(r0, r0 + tr)
        for c0 in range(0, c, cw):
            parts = []
            for p in range(3):
                lo = p * c + c0
                xs = _taps(buf_ref, s_ref, r0, slice(lo, lo + cw))
                acc = cb_ref[:, lo:lo + cw]
                for j in range(3):
                    acc = acc + xs[j * tr:(j + 1) * tr] * cw_ref[j:j + 1, lo:lo + cw]
                parts.append(acc)
            x0, x1, v = parts
            u_ref[0, rs, c0:c0 + cw] = (v * x1).astype(u_ref.dtype)
            e_ref[0, rs, c0:c0 + cw] = (x0 * _silu(g_ref[0, rs, c0:c0 + cw].astype(F32))).astype(e_ref.dtype)


def _hy_prep(z, conv_w, conv_b, tl=512):
    b, l, _ = z.shape
    c = HY_WIDTH
    assert l % tl == 0
    cur, prev, nxt = _halo_specs(tl, 3 * c, l, E_HY // (3 * c))
    return pl.pallas_call(
        _hy_prep_kernel,
        grid=(b, l // tl),
        in_specs=[
            cur, prev, nxt,
            pl.BlockSpec((1, tl, c), lambda i, t: (i, t, E_GHY // c)),
            pl.BlockSpec((3, 3 * c), lambda i, t: (0, 0)),
            pl.BlockSpec((1, 3 * c), lambda i, t: (0, 0)),
            pl.BlockSpec((3 * TAP_ROWS, TAP_WINDOW), lambda i, t: (0, 0)),
        ],
        out_specs=[
            pl.BlockSpec((1, tl, c), lambda i, t: (i, t, 0)),
            pl.BlockSpec((1, tl, c), lambda i, t: (i, t, 0)),
        ],
        out_shape=[jax.ShapeDtypeStruct((b, l, c), F32), jax.ShapeDtypeStruct((b, l, c), BF16)],
        scratch_shapes=[pltpu.VMEM((_halo_buf_rows(tl), 3 * c), BF16)],
        compiler_params=_cparams("parallel", "arbitrary"),
        name="hy_prep",
    )(z, z, z, z, conv_w.astype(F32), conv_b.reshape(1, 3 * c).astype(F32), _tap_shift_matrix(3))


def _hy_filter_tables(l):
    t = jnp.linspace(0.0, 1.0, l, dtype=F32)[:, None]
    w = (2.0 * math.pi / l) * jnp.arange(l, dtype=F32)[:, None]
    f = jnp.linspace(1e-4, HY_BANDS - 1, HY_BANDS, dtype=F32)[None, :]
    emb = jnp.concatenate([t, jnp.cos(f * w), -jnp.sin(f * w)], axis=-1)
    emb = jnp.pad(emb, ((0, 0), (0, LANES - HY_EMB)))
    deltas = jnp.abs(jnp.linspace(math.log(HY_TARGET) / HY_SLOW_PCT, math.log(HY_TARGET) / HY_FAST_PCT, HY_WIDTH, dtype=F32))
    decay = jnp.exp(-t * jnp.tile(deltas, 2)[None, :])
    return emb, decay


def _hy_filter_kernel(emb_ref, dec_ref, w1_ref, b1_ref, w2_ref, b2_ref, w3_ref, fr_ref, sk_ref, o_ref):
    c = HY_WIDTH
    hp = lax.Precision.HIGHEST
    fr = fr_ref[...]
    hid = jnp.sin(fr * (jnp.dot(emb_ref[...], w1_ref[...], precision=hp, preferred_element_type=F32) + b1_ref[...]))
    hid = jnp.sin(fr * (jnp.dot(hid, w2_ref[...], precision=hp, preferred_element_type=F32) + b2_ref[...]))
    tl = emb_ref.shape[0]
    row = lax.broadcasted_iota(jnp.int32, (tl, 1), 0) + pl.program_id(0) * tl
    first = row == 0
    cw = 512
    for c0 in range(0, 2 * c, cw):
        filt = jnp.dot(hid, w3_ref[:, c0:c0 + cw], precision=hp, preferred_element_type=F32) * dec_ref[:, c0:c0 + cw]
        if c0 < c:
            filt = jnp.where(first, filt + sk_ref[:, c0:c0 + cw], filt)
            o_ref[0, :, c0:c0 + cw] = filt.astype(o_ref.dtype)
        else:
            filt = jnp.where(first, 0.0, filt)
            o_ref[1, :, c0 - c:c0 - c + cw] = filt.astype(o_ref.dtype)


def _hy_filters(l, w1, b1, w2, b2, w3, freq, skip, tl=256):
    c = HY_WIDTH
    hdim = HY_FILT_HIDDEN
    emb, decay = _hy_filter_tables(l)
    w1p = jnp.pad(w1.astype(F32), ((0, LANES - HY_EMB), (0, 0)))
    return pl.pallas_call(
        _hy_filter_kernel,
        grid=(l // tl,),
        in_specs=[
            pl.BlockSpec((tl, LANES), lambda t: (t, 0)),
            pl.BlockSpec((tl, 2 * c), lambda t: (t, 0)),
            pl.BlockSpec((LANES, hdim), lambda t: (0, 0)),
            pl.BlockSpec((1, hdim), lambda t: (0, 0)),
            pl.BlockSpec((hdim, hdim), lambda t: (0, 0)),
            pl.BlockSpec((1, hdim), lambda t: (0, 0)),
            pl.BlockSpec((hdim, 2 * c), lambda t: (0, 0)),
            pl.BlockSpec((1, hdim), lambda t: (0, 0)),
            pl.BlockSpec((1, c), lambda t: (0, 0)),
        ],
        out_specs=pl.BlockSpec((2, tl, c), lambda t: (0, t, 0)),
        out_shape=jax.ShapeDtypeStruct((2, l, c), BF16),
        compiler_params=_cparams("arbitrary"),
        name="hy_filter",
    )(emb, decay, w1p, b1.reshape(1, hdim).astype(F32), w2.astype(F32), b2.reshape(1, hdim).astype(F32),
      w3.astype(F32), freq.reshape(1, hdim).astype(F32), skip.reshape(1, c).astype(F32))


def _fft_consts(l):
    n = 2 * l
    n2 = FFT_N2
    n1 = n // n2
    n1h = n1 // 2
    w0 = 2.0 * math.pi / n
    k1 = jnp.arange(n1, dtype=jnp.int32)
    m1 = jnp.arange(n1h, dtype=jnp.int32)
    m2 = jnp.arange(n2, dtype=jnp.int32)
    ea = (n2 * m1[None, None, :] * k1[None, :, None] + m2[:, None, None] * k1[None, :, None]) % n
    ang = ea.astype(F32) * w0
    fa = jnp.concatenate([jnp.cos(ang), -jnp.sin(ang)], axis=1).astype(BF16)
    eb = (n1 * m2[:, None] * m2[None, :]) % n
    angb = eb.astype(F32) * w0
    fr, fi = jnp.cos(angb), -jnp.sin(angb)
    fb = jnp.concatenate([jnp.concatenate([fr, -fi], axis=1), jnp.concatenate([fi, fr], axis=1)], axis=0).astype(BF16)
    eg = (n1 * m2[None, :, None] * m2[None, None, :] + m2[None, :, None] * k1[:, None, None]) % n
    angg = eg.astype(F32) * w0
    gr, gi = jnp.cos(angg), jnp.sin(angg)
    gb = jnp.concatenate([jnp.concatenate([gr, -gi], axis=2), jnp.concatenate([gi, gr], axis=2)], axis=1).astype(BF16)
    ec = (n2 * m1[:, None] * k1[None, :]) % n
    angc = ec.astype(F32) * w0
    fc = (jnp.concatenate([jnp.cos(angc), -jnp.sin(angc)], axis=1) * (1.0 / n)).astype(BF16)
    return fa, fb, gb, fc


def _fft_a_kernel(u_ref, f_ref, o_ref):
    tn2 = f_ref.shape[0]
    n1 = f_ref.shape[1] // 2
    c = u_ref.shape[2] // tn2
    for j in range(tn2):
        a = _dot(f_ref[j], u_ref[0, :, j * c:(j + 1) * c])
        o_ref[0, 0, :, j * c:(j + 1) * c] = a[:n1].astype(o_ref.dtype)
        o_ref[0, 1, :, j * c:(j + 1) * c] = a[n1:].astype(o_ref.dtype)


def _fft_a(u, fa, tn2=8):
    b, l, c = u.shape
    n2, n1x2, n1h = fa.shape
    n1 = n1x2 // 2
    uv = u.reshape(b, n1h, n2 * c)
    return pl.pallas_call(
        _fft_a_kernel,
        grid=(b, n2 // tn2),
        in_specs=[
            pl.BlockSpec((1, n1h, tn2 * c), lambda i, j: (i, 0, j)),
            pl.BlockSpec((tn2, n1x2, n1h), lambda i, j: (j, 0, 0)),
        ],
        out_specs=pl.BlockSpec((1, 2, n1, tn2 * c), lambda i, j: (i, 0, 0, j)),
        out_shape=jax.ShapeDtypeStruct((b, 2, n1, n2 * c), BF16),
        compiler_params=_cparams("parallel", "arbitrary"),
        name="fft_a",
    )(uv, fa)


def _fft_bf_kernel(a_ref, w_ref, o_ref):
    n2 = a_ref.shape[2]
    ct = a_ref.shape[3]
    cw = 256
    for c0 in range(0, ct, cw):
        xf = _dot(w_ref[...], a_ref[0, :, :, c0:c0 + cw].reshape(2 * n2, cw))
        xb = _dot(w_ref[...], a_ref[1, :, :, c0:c0 + cw].reshape(2 * n2, cw))
        o_ref[0, :, c0:c0 + cw] = xf[:n2] + xb[:n2]
        o_ref[1, :, c0:c0 + cw] = xf[n2:] - xb[n2:]


def _fft_bf(a, fb):
    _, _, n, c = a.shape
    n2 = FFT_N2
    return pl.pallas_call(
        _fft_bf_kernel,
        grid=(n // n2,),
        in_specs=[
            pl.BlockSpec((2, 2, n2, c), lambda k: (0, 0, k, 0)),
            pl.BlockSpec((2 * n2, 2 * n2), lambda k: (0, 0)),
        ],
        out_specs=pl.BlockSpec((2, n2, c), lambda k: (0, k, 0)),
        out_shape=jax.ShapeDtypeStruct((2, n, c), F32),
        compiler_params=_cparams("arbitrary"),
        name="fft_bf",
    )(a, fb)


SUBLANES = 8
CONV_SLABS = 2


def _conv_dims(n1):
    kh = n1 // 2 + 1
    kp = -(-kh // SUBLANES) * SUBLANES
    return kh, kp, 2 * kp + SUBLANES


def _hy_conv_kernel(u_ref, h_ref, la_ref, w_ref, g_ref, fc_ref, o_ref, s_ref, *, n1, unroll_b):
    n2 = FFT_N2
    n1h = n1 // 2
    kh, kp, pitch = _conv_dims(n1)
    grp = SUBLANES
    slabs = range(CONV_SLABS)

    def lanes(x, sl):
        return x[:, sl * LANES:(sl + 1) * LANES]

    def stage_a(jj, carry):
        rows = [u_ref[0, pl.ds(pl.multiple_of(m * n2 + jj * grp, grp), grp), :] for m in range(n1h)]
        a = _dot(la_ref[jj], jnp.concatenate(rows, axis=0).astype(BF16))
        for j0 in range(grp):
            dst = pl.multiple_of((jj * grp + j0) * pitch, grp)
            for sl in slabs:
                s_ref[sl, pl.ds(dst, 2 * kp), :] = lanes(a[j0 * 2 * kp:(j0 + 1) * 2 * kp], sl)
        return carry

    lax.fori_loop(0, n2 // grp, stage_a, 0)

    def stage_b(kk, carry):
        k1s = [jnp.minimum(kk * unroll_b + i, kh - 1) for i in range(unroll_b)]
        ar = [jnp.concatenate([s_ref[sl, pl.ds(k, n2, stride=pitch), :] for sl in slabs], axis=1) for k in k1s]
        ai = [jnp.concatenate([s_ref[sl, pl.ds(kp + k, n2, stride=pitch), :] for sl in slabs], axis=1) for k in k1s]
        x = [_dot(w_ref[...], jnp.concatenate([ar[i], ai[i]], axis=0).astype(BF16)) for i in range(unroll_b)]
        y = []
        for i, k in enumerate(k1s):
            off = pl.multiple_of(k * n2, n2)
            hr, hi = h_ref[0, pl.ds(off, n2), :], h_ref[1, pl.ds(off, n2), :]
            xr, xi = x[i][:n2], x[i][n2:]
            y.append(jnp.concatenate([xr * hr - xi * hi, xr * hi + xi * hr], axis=0).astype(BF16))
        z = [_dot(g_ref[k1s[i]], y[i]) for i in range(unroll_b)]
        for i, k in enumerate(k1s):
            for sl in slabs:
                s_ref[sl, pl.ds(k, n2, stride=pitch), :] = lanes(z[i][:n2], sl)
                s_ref[sl, pl.ds(kp + k, n2, stride=pitch), :] = lanes(z[i][n2:], sl)
        return carry

    lax.fori_loop(0, -(-kh // unroll_b), stage_b, 0)

    def stage_c(jj, carry):
        zz = [jnp.concatenate([s_ref[sl, pl.ds(pl.multiple_of((jj * grp + j0) * pitch, grp), 2 * kp), :]
                               for sl in slabs], axis=1) for j0 in range(grp)]
        y = _dot(fc_ref[...], jnp.concatenate(zz, axis=0).astype(BF16))
        for m in range(n1h):
            o_ref[0, pl.ds(pl.multiple_of(m * n2 + jj * grp, grp), grp), :] = y[m * grp:(m + 1) * grp]
        return carry

    lax.fori_loop(0, n2 // grp, stage_c, 0)


def _hy_conv(u, h, fa, fb, gb, fc):
    b, l, c = u.shape
    n2, n1x2, n1h = fa.shape
    n1 = n1x2 // 2
    kh, kp, pitch = _conv_dims(n1)
    grp = SUBLANES
    ct = CONV_SLABS * LANES

    def half(t, axis):
        re, im = jnp.split(t, 2, axis=axis)
        pad = [(0, 0)] * t.ndim
        pad[axis] = (0, kp - kh)
        cut = lambda x: jnp.pad(lax.slice_in_dim(x, 0, kh, axis=axis), pad)
        return jnp.concatenate([cut(re), cut(im)], axis=axis)

    fa_h = half(fa, 1)
    la = (fa_h.reshape(n2 // grp, grp, 2 * kp, n1h, 1) * jnp.eye(grp, dtype=fa.dtype).reshape(1, grp, 1, 1, grp))
    la = la.reshape(n2 // grp, grp * 2 * kp, n1h * grp)
    wgt = jnp.ones((kh,), F32).at[1:kh - 1].set(2.0)
    wgt = jnp.pad(wgt, (0, kp - kh))
    fc_h = (half(fc, 1).astype(F32) * jnp.tile(wgt, 2)[None, :]).astype(BF16)
    lc = fc_h.reshape(n1h, 1, 1, 2 * kp) * jnp.eye(grp, dtype=fc_h.dtype).reshape(1, grp, grp, 1)
    fc_h = lc.reshape(n1h * grp, grp * 2 * kp)
    gb_h = gb[:kh]
    unroll_b = 3
    once = pl.Buffered(1)
    return pl.pallas_call(
        functools.partial(_hy_conv_kernel, n1=n1, unroll_b=unroll_b),
        grid=(c // ct, b),
        in_specs=[
            pl.BlockSpec((1, l, ct), lambda j, i: (i, 0, j)),
            pl.BlockSpec((2, kh * n2, ct), lambda j, i: (0, 0, j), pipeline_mode=once),
            pl.BlockSpec(la.shape, lambda j, i: (0, 0, 0), pipeline_mode=once),
            pl.BlockSpec(fb.shape, lambda j, i: (0, 0), pipeline_mode=once),
            pl.BlockSpec(gb_h.shape, lambda j, i: (0, 0, 0), pipeline_mode=once),
            pl.BlockSpec(fc_h.shape, lambda j, i: (0, 0), pipeline_mode=once),
        ],
        out_specs=pl.BlockSpec((1, l, ct), lambda j, i: (i, 0, j)),
        out_shape=jax.ShapeDtypeStruct((b, l, c), F32),
        scratch_shapes=[pltpu.VMEM((CONV_SLABS, n2 * pitch, LANES), F32)],
        compiler_params=_cparams("arbitrary", "arbitrary"),
        name="hy_conv",
    )(u, h, la, fb, gb_h, fc_h)


def _hyena_branch(z, l, conv_w, conv_b, w1, b1, w2, b2, w3, freq, skip):
    c = HY_WIDTH
    n = 2 * l
    fa, fb, gb, fc = _fft_consts(l)
    filt = _hy_filters(l, w1, b1, w2, b2, w3, freq, skip)
    fa_spec = _fft_a(filt, fa).reshape(2, 2, n, c)
    h = _fft_bf(fa_spec, fb)
    u, e = _hy_prep(z, conv_w, conv_b)
    return _hy_conv(u, h, fa, fb, gb, fc), e


def _gdn_prep_kernel(zc_ref, zp_ref, zn_ref, cw_ref, s_ref, q_ref, k_ref, v_ref, buf_ref):
    t = pl.program_id(1)
    nt = pl.num_programs(1)
    tl = zc_ref.shape[1]
    dh = GDN_HEAD_DIM
    tr = TAP_ROWS
    _fill_halo_buf(buf_ref, zc_ref, zp_ref, zn_ref, t, nt)
    outs = (q_ref, k_ref, v_ref)
    cw = 2 * dh
    for r0 in range(0, tl, tr):
        for p in range(3):
            for c0 in range(0, GDN_WIDTH, cw):
                lo = p * GDN_WIDTH + c0
                xs = _taps(buf_ref, s_ref, r0, slice(lo, lo + cw))
                acc = xs[0:tr] * cw_ref[0:1, lo:lo + cw]
                for j in range(1, 5):
                    acc = acc + xs[j * tr:(j + 1) * tr] * cw_ref[j:j + 1, lo:lo + cw]
                a = _silu(acc)
                for hh in range(cw // dh):
                    ah = a[:, hh * dh:(hh + 1) * dh]
                    if p < 2:
                        ah = ah * lax.rsqrt(jnp.sum(ah * ah, axis=-1, keepdims=True) + EPS)
                    if p == 0:
                        ah = ah * (dh ** -0.5)
                    outs[p][0, r0:r0 + tr, c0 + hh * dh:c0 + (hh + 1) * dh] = ah.astype(BF16)


def _gdn_prep(z, conv_w, tl=512):
    b, l, _ = z.shape
    w = 3 * GDN_WIDTH
    cur, prev, nxt = _halo_specs(tl, w, l, E_QKV // w)
    o_spec = pl.BlockSpec((1, tl, GDN_WIDTH), lambda i, t: (i, t, 0))
    o_shape = jax.ShapeDtypeStruct((b, l, GDN_WIDTH), BF16)
    return pl.pallas_call(
        _gdn_prep_kernel,
        grid=(b, l // tl),
        in_specs=[cur, prev, nxt, pl.BlockSpec((5, w), lambda i, t: (0, 0)),
                  pl.BlockSpec((5 * TAP_ROWS, TAP_WINDOW), lambda i, t: (0, 0))],
        out_specs=[o_spec, o_spec, o_spec],
        out_shape=[o_shape, o_shape, o_shape],
        scratch_shapes=[pltpu.VMEM((_halo_buf_rows(tl), w), BF16)],
        compiler_params=_cparams("parallel", "arbitrary"),
        name="gdn_prep",
    )(z, z, z, conv_w.astype(F32), _tap_shift_matrix(5))


def _dot_nt(a, b):
    return lax.dot_general(a, b, (((1,), (1,)), ((), ())), preferred_element_type=F32)


def _dot_tn(a, b):
    return lax.dot_general(a, b, (((0,), (0,)), ((), ())), preferred_element_type=F32)


def _split3(x):
    a = x.astype(BF16)
    r = x - a.astype(F32)
    b = r.astype(BF16)
    c = (r - b.astype(F32)).astype(BF16)
    return a, b, c


GDN_CHUNKS_PER_STEP = 8


def _gdn_scan_kernel(*refs, direction, final):
    if final:
        q_ref, k_ref, v_ref, zs_ref, al_ref, dt_ref, of_ref, gg_ref, ng_ref, o_ref, s_ref = refs
    else:
        q_ref, k_ref, v_ref, zs_ref, al_ref, dt_ref, o_ref, s_ref = refs
    c = GDN_CHUNK
    dh = GDN_HEAD_DIM
    nsub = q_ref.shape[1] // c

    @pl.when(pl.program_id(1) == 0)
    def _():
        s_ref[...] = jnp.zeros_like(s_ref)

    row = lax.broadcasted_iota(jnp.int32, (c, LANES), 0)
    col = lax.broadcasted_iota(jnp.int32, (c, LANES), 1)
    if direction == 0:
        incl, strict, last = col <= row, col < row, c - 1
    else:
        incl, strict, last = (col >= row) & (col < c), (col > row) & (col < c), 0
    tri = incl[:, :c].astype(BF16)
    eye = (row == col).astype(F32)
    zc = jnp.zeros((c, LANES), BF16)
    zd = jnp.zeros((dh, dh), BF16)

    def lanes2(xa, xb):
        return jnp.concatenate([xa, xb], axis=1)

    def blockdiag_slots(pa, pb):
        return jnp.concatenate([lanes2(pa, zc), lanes2(zc, zc), lanes2(zc, pb), lanes2(zc, zc)], axis=0)

    heads = range(GDN_HEADS)
    hsl = [slice(h * dh, (h + 1) * dh) for h in heads]
    lns = [2 * GDN_HEADS + direction * GDN_HEADS + h for h in heads]

    rows = [slice(i * c, (i + 1) * c) for i in range(nsub)]
    gcum, gcum_t, beta_all = [], [], []
    for rs in rows:
        zs = zs_ref[0, rs, :]
        beta_all.append(jax.nn.sigmoid(zs))
        gl = -jnp.exp(al_ref[...]) * jax.nn.softplus(zs + dt_ref[...])
        g1, g2, g3 = _split3(gl)
        gsum = _dot(tri, g1) + _dot(tri, g2) + _dot(tri, g3)
        gcum.append(gsum)
        gcum_t.append(jnp.concatenate([gsum, jnp.zeros_like(gsum)], axis=0).T)
    items = [(i, h) for i in range(nsub) for h in heads]
    it = range(len(items))
    beta = [beta_all[i][:, direction * GDN_HEADS + h:direction * GDN_HEADS + h + 1] for i, h in items]
    gc = [gcum[i][:, lns[h]:lns[h] + 1] for i, h in items]
    gct = [gcum_t[i][lns[h]:lns[h] + 1, :] for i, h in items]
    glast = [gcum_t[i][lns[h]:lns[h] + 1, last:last + 1] for i, h in items]
    qh = [q_ref[0, rows[i], hsl[h]] for i, h in items]
    kh = [k_ref[0, rows[i], hsl[h]] for i, h in items]
    vh = [v_ref[0, rows[i], hsl[h]] for i, h in items]
    pr = range(len(items) // 2)
    kq2 = [_dot_nt(lanes2(jnp.concatenate([kh[2 * y], qh[2 * y]], axis=0),
                          jnp.concatenate([kh[2 * y + 1], qh[2 * y + 1]], axis=0)),
                   jnp.concatenate([lanes2(kh[2 * y], zd[:c]), lanes2(zc, zc),
                                    lanes2(zd[:c], kh[2 * y + 1]), lanes2(zc, zc)], axis=0)) for y in pr]
    kq = [kq2[x // 2][:, (x % 2) * LANES:(x % 2 + 1) * LANES] for x in it]
    dmat = [jnp.exp(jnp.where(incl, gc[x] - gct[x], -jnp.inf)) for x in it]
    a = [jnp.where(strict, (beta[x] * kq[x][:c]) * dmat[x], 0.0) for x in it]
    tinv2 = [lanes2(eye - a[2 * y], eye - a[2 * y + 1]) for y in pr]
    ab = [a[x].astype(BF16) for x in it]
    p2 = [_dot(lanes2(ab[2 * y], ab[2 * y + 1]), blockdiag_slots(ab[2 * y], ab[2 * y + 1])) for y in pr]
    for j in range(5):
        pb2 = [p2[y].astype(BF16) for y in pr]
        bd = [blockdiag_slots(pb2[y][:, :LANES], pb2[y][:, LANES:]) for y in pr]
        if j < 4:
            tp = [_dot(jnp.concatenate([tinv2[y].astype(BF16), pb2[y]], axis=0), bd[y]) for y in pr]
            tinv2 = [tinv2[y] + tp[y][:c] for y in pr]
            p2 = [tp[y][c:] for y in pr]
        else:
            tinv2 = [tinv2[y] + _dot(tinv2[y].astype(BF16), bd[y]) for y in pr]
    tinv = [tinv2[x // 2][:, (x % 2) * LANES:(x % 2 + 1) * LANES].astype(BF16) for x in it]
    eg = [jnp.exp(gc[x]) for x in it]
    kf = [kh[x].astype(F32) for x in it]
    rhs = [jnp.concatenate([vh[x].astype(F32) * beta[x], kf[x] * (beta[x] * eg[x])], axis=1).astype(BF16) for x in it]
    zr = jnp.zeros((LANES - c, 2 * dh), BF16)
    uw = [_dot(tinv[x], jnp.concatenate([rhs[x], zr], axis=0)) for x in it]
    wq = [jnp.concatenate([uw[x][:, dh:], qh[x].astype(F32) * eg[x]], axis=0).astype(BF16) for x in it]
    qkd = [(kq[x][c:] * dmat[x]).astype(BF16) for x in it]
    kd = [(kf[x] * jnp.exp(glast[x] - gc[x])).astype(BF16) for x in it]

    order = list(range(nsub)) if direction == 0 else list(range(nsub - 1, -1, -1))
    state = [s_ref[h] for h in heads]
    hp = range(GDN_HEADS // 2)
    for i in order:
        rs = rows[i]
        xs = [i * GDN_HEADS + h for h in heads]
        sb = [state[h].astype(BF16) for h in heads]
        ws2 = [_dot(lanes2(wq[xs[2 * y]], wq[xs[2 * y + 1]]),
                    jnp.concatenate([lanes2(sb[2 * y], zd), lanes2(zd, sb[2 * y + 1])], axis=0)) for y in hp]
        ws = [ws2[h // 2][:, (h % 2) * dh:(h % 2 + 1) * dh] for h in heads]
        vnb = [(uw[xs[h]][:, :dh] - ws[h][:c]).astype(BF16) for h in heads]
        o2 = [_dot(lanes2(qkd[xs[2 * y]], qkd[xs[2 * y + 1]]),
                   jnp.concatenate([lanes2(vnb[2 * y], zd[:c]), lanes2(zc, zc),
                                    lanes2(zd[:c], vnb[2 * y + 1]), lanes2(zc, zc)], axis=0)) for y in hp]
        o = [ws[h][c:] + o2[h // 2][:, (h % 2) * dh:(h % 2 + 1) * dh] for h in heads]
        kv2 = [_dot_tn(jnp.concatenate([kd[xs[2 * y]], kd[xs[2 * y + 1]]], axis=0),
                       jnp.concatenate([lanes2(vnb[2 * y], zd[:c]), lanes2(zd[:c], vnb[2 * y + 1])], axis=0)) for y in hp]
        state = [state[h] * jnp.exp(glast[xs[h]]) + kv2[h // 2][:, (h % 2) * dh:(h % 2 + 1) * dh] for h in heads]
        for h in heads:
            if final:
                tot = of_ref[0, rs, hsl[h]] + o[h]
                y = tot * lax.rsqrt(jnp.mean(tot * tot, axis=-1, keepdims=True) + EPS) * ng_ref[...]
                o_ref[0, rs, hsl[h]] = (y * _silu(gg_ref[0, rs, hsl[h]].astype(F32))).astype(o_ref.dtype)
            else:
                o_ref[0, rs, hsl[h]] = o[h].astype(o_ref.dtype)
    for h in heads:
        s_ref[h] = state[h]


def _gdn_scan(q, k, v, zs, a_log, dt_bias, direction, o_fwd=None, z=None, gate_blk=None, norm_g=None):
    b, l, w = q.shape
    c = GDN_CHUNK * GDN_CHUNKS_PER_STEP
    n = l // c
    final = o_fwd is not None
    if direction == 0:
        cmap = lambda i, t: (i, t, 0)
    else:
        cmap = lambda i, t: (i, n - 1 - t, 0)
    blk = pl.BlockSpec((1, c, w), cmap)
    vec = pl.BlockSpec((1, LANES), lambda i, t: (0, 0))
    pad = lambda x: jnp.pad(x.reshape(1, -1).astype(F32), ((0, 0), (2 * GDN_HEADS, LANES - 4 * GDN_HEADS)))
    in_specs = [blk, blk, blk, pl.BlockSpec((1, c, LANES), cmap), vec, vec]
    args = [q, k, v, zs, pad(a_log), pad(dt_bias)]
    if final:
        if direction == 0:
            gmap = lambda i, t: (i, t, gate_blk)
        else:
            gmap = lambda i, t: (i, n - 1 - t, gate_blk)
        in_specs += [blk, pl.BlockSpec((1, c, w), gmap), pl.BlockSpec((1, GDN_HEAD_DIM), lambda i, t: (0, 0))]
        args += [o_fwd, z, norm_g.reshape(1, GDN_HEAD_DIM).astype(F32)]
    return pl.pallas_call(
        functools.partial(_gdn_scan_kernel, direction=direction, final=final),
        grid=(b, n),
        in_specs=in_specs,
        out_specs=blk,
        out_shape=jax.ShapeDtypeStruct((b, l, w), BF16 if final else F32),
        scratch_shapes=[pltpu.VMEM((GDN_HEADS, GDN_HEAD_DIM, GDN_HEAD_DIM), F32)],
        compiler_params=_cparams("parallel", "arbitrary"),
        name="gdn_scan_bwd" if direction else "gdn_scan_fwd",
    )(*args)


def _gdn_branch(z, zs, conv_w, a_log, dt_bias, norm_g, gate_blk):
    q, k, v = _gdn_prep(z, conv_w)
    o_f = _gdn_scan(q, k, v, zs, a_log, dt_bias, 0)
    return _gdn_scan(q, k, v, zs, a_log, dt_bias, 1, o_fwd=o_f, z=z, gate_blk=gate_blk, norm_g=norm_g)


def _softmax_t(s, extra=None):
    m = jnp.max(s, axis=0, keepdims=True)
    if extra is not None:
        m = jnp.maximum(m, extra)
    p = jnp.exp2(s - m)
    den = jnp.sum(p, axis=0, keepdims=True)
    if extra is not None:
        den = den + jnp.exp2(extra - m)
    return p.astype(BF16), den, m


def _t_bf16(x):
    return x.astype(F32).T.astype(BF16)


def _xattn_kernel(q_ref, g_ref, kv_ref, o_ref):
    dh = X_HEAD_DIM
    for h in range(X_HEADS):
        hs = slice(h * dh, (h + 1) * dh)
        qh = q_ref[0, :, hs].astype(BF16)
        kh = kv_ref[0, :, hs]
        vh = kv_ref[0, :, X_WIDTH + h * dh:X_WIDTH + (h + 1) * dh]
        s = _dot_nt(qh, kh) * (dh ** -0.5)
        p = jnp.exp(s - jnp.max(s, axis=-1, keepdims=True))
        den = jnp.sum(p, axis=-1, keepdims=True)
        o = _dot(p.astype(BF16), vh) / den
        o_ref[0, :, hs] = (o * _silu(g_ref[0, :, hs].astype(F32))).astype(o_ref.dtype)


def _xattn(z, kv, q_blk, g_blk, tq=512):
    b, l, _ = z.shape
    m = kv.shape[1]
    w = X_WIDTH
    return pl.pallas_call(
        _xattn_kernel,
        grid=(b, l // tq),
        in_specs=[
            pl.BlockSpec((1, tq, w), lambda i, t: (i, t, q_blk)),
            pl.BlockSpec((1, tq, w), lambda i, t: (i, t, g_blk)),
            pl.BlockSpec((1, m, 2 * w), lambda i, t: (i, 0, 0)),
        ],
        out_specs=pl.BlockSpec((1, tq, w), lambda i, t: (i, t, 0)),
        out_shape=jax.ShapeDtypeStruct((b, l, w), BF16),
        compiler_params=_cparams("parallel", "arbitrary"),
        name="xattn",
    )(z, z, kv)


def _out_proj_kernel(*refs, widths, gated):
    nparts = len(widths)
    y_refs = refs[:nparts]
    if gated:
        e_ref, w_ref, g_ref, x_ref, o_ref = refs[nparts:]
    else:
        w_ref, g_ref, x_ref, o_ref = refs[nparts:]
    d = o_ref.shape[1]
    cw = 256
    ys = [y_ref[...] for y_ref in y_refs]
    if gated:
        ys[0] = ys[0] * e_ref[...]
    ys = [y.astype(BF16) for y in ys]
    ssq = jnp.zeros((o_ref.shape[0], 1), F32)
    for c0 in range(0, d, cw):
        acc = None
        off = 0
        for y, wd in zip(ys, widths):
            part = _dot(y, w_ref[off:off + wd, c0:c0 + cw])
            acc = part if acc is None else acc + part
            off += wd
        ssq = ssq + jnp.sum(acc * acc, axis=-1, keepdims=True)
        o_ref[:, c0:c0 + cw] = acc
    r = lax.rsqrt(ssq * (1.0 / d) + EPS)
    for c0 in range(0, d, cw):
        o_ref[:, c0:c0 + cw] = x_ref[:, c0:c0 + cw] + o_ref[:, c0:c0 + cw] * r * g_ref[:, c0:c0 + cw]


def _out_proj(parts, w_out, post_g, x2d, gate=None, tm=512):
    t, d = x2d.shape
    widths = tuple(int(p.shape[1]) for p in parts)
    kdim = sum(widths)
    gates = [] if gate is None else [gate]
    return pl.pallas_call(
        functools.partial(_out_proj_kernel, widths=widths, gated=gate is not None),
        grid=(t // tm,),
        in_specs=[pl.BlockSpec((tm, wd), lambda i: (i, 0)) for wd in widths]
        + [pl.BlockSpec((tm, widths[0]), lambda i: (i, 0)) for _ in gates] + [
            pl.BlockSpec((kdim, d), lambda i: (0, 0)),
            pl.BlockSpec((1, d), lambda i: (0, 0)),
            pl.BlockSpec((tm, d), lambda i: (i, 0)),
        ],
        out_specs=pl.BlockSpec((tm, d), lambda i: (i, 0)),
        out_shape=jax.ShapeDtypeStruct((t, d), F32),
        compiler_params=_cparams("parallel"),
        name="out_proj",
    )(*parts, *gates, w_out.astype(BF16), post_g.reshape(1, d).astype(F32), x2d)


def _mem_kv(mem, mem_g, w_mem_kv):
    b, m, d = mem.shape
    kv = _norm_matmul(mem.reshape(b * m, d), mem_g, w_mem_kv, 1024, out_dtype=BF16, name="mem_kv")
    return kv.reshape(b, m, 2 * X_WIDTH)


E_QKV, E_HY, E_GHY, E_GGDN, E_XQ, E_GX, E_BIG = 0, 3072, 6144, 7168, 8192, 8704, 9216


def _even_layer(x, mem, pre_g, post_g, w_in, w_out, hy_conv_w, hy_conv_b, hy_w1, hy_b1, hy_w2, hy_b2, hy_w3, hy_freq,
                hy_skip, gdn_conv_w, gdn_a_log, gdn_dt_bias, gdn_norm_g, mem_g, w_mem_kv):
    b, l, d = x.shape
    x2d = x.reshape(b * l, d)
    w_big = jnp.concatenate([w_in[:, 4096:7168], w_in[:, 0:3072], w_in[:, 3072:4096], w_in[:, 7168:8192],
                             w_in[:, 8224:8736], w_in[:, 8736:9248]], axis=1)
    w_small = jnp.pad(w_in[:, 8192:8224], ((0, 0), (0, LANES - 4 * GDN_HEADS)))
    z, zs = _norm_matmul(x2d, pre_g, w_big, E_BIG // 3, out_dtype=BF16, tm=1024, name="even_in", w_side=w_small)
    z = z.reshape(b, l, E_BIG)
    zs = zs.reshape(b, l, LANES)
    conv, e = _hyena_branch(z, l, hy_conv_w, hy_conv_b, hy_w1, hy_b1, hy_w2, hy_b2, hy_w3, hy_freq, hy_skip)
    y_b = _gdn_branch(z, zs, gdn_conv_w, gdn_a_log, gdn_dt_bias, gdn_norm_g, E_GGDN // GDN_WIDTH)
    kv = _mem_kv(mem, mem_g, w_mem_kv)
    y_x = _xattn(z, kv, E_XQ // X_WIDTH, E_GX // X_WIDTH)
    t = b * l
    out = _out_proj([conv.reshape(t, -1), y_b.reshape(t, -1), y_x.reshape(t, -1)], w_out, post_g, x2d,
                    gate=e.reshape(t, -1))
    return out.reshape(b, l, d)


DIL_PATTERNS = ((128, 1), (512, 4), (2048, 16))
N_DIL = len(DIL_PATTERNS)
DIL_HEADS = 4
DIL_HEAD_DIM = 128
DIL_WIDTH = DIL_HEADS * DIL_HEAD_DIM
SWA_Q_HEADS = 16
SWA_KV_HEADS = 2
SWA_HEAD_DIM = 64
SWA_WIDTH = SWA_Q_HEADS * SWA_HEAD_DIM
SWA_HALF_WINDOW = 128
O_CQKV, O_GC, O_DQ, O_GD, O_XQ, O_GX, O_DKV, O_ALL = 0, 4608, 5120, 6144, 7168, 7680, 8192, 8448


def _rope_tables(l, dh):
    half = dh // 8
    inv = ROPE_THETA ** (-jnp.arange(half, dtype=F32) / half)
    ang = jnp.arange(l, dtype=F32)[:, None] * inv[None, :]
    cos, sin = jnp.cos(ang), jnp.sin(ang)
    one = jnp.ones((l, dh - 2 * half), F32)
    zero_h = jnp.zeros((l, half), F32)
    zero_r = jnp.zeros((l, dh - 2 * half), F32)
    c = jnp.concatenate([cos, cos, one], axis=1)
    sa = jnp.concatenate([-sin, zero_h, zero_r], axis=1)
    sb = jnp.concatenate([zero_h, sin, zero_r], axis=1)
    rep = LANES // dh
    return tuple(jnp.tile(t, (1, rep)) for t in (c, sa, sb))


def _odd_prep_kernel(c_ref, dq_ref, dkv_ref, c1_ref, a1_ref, b1_ref, c2_ref, a2_ref, b2_ref, *refs):
    dil_refs = refs[:3 * N_DIL]
    dqe_ref, dk_ref, dv_ref, xs_ref = refs[3 * N_DIL:]
    tl = c_ref.shape[1]
    lane = lax.broadcasted_iota(jnp.int32, (tl, LANES), 1)
    low = lane < SWA_HEAD_DIM
    c1, a1, b1 = c1_ref[...], a1_ref[...], b1_ref[...]
    c2, a2, b2 = c2_ref[...], a2_ref[...], b2_ref[...]
    h1 = DIL_HEAD_DIM // 8
    h2 = SWA_HEAD_DIM // 8

    def rope1(x):
        return x * c1 + pltpu.roll(x, LANES - h1, 1) * a1 + pltpu.roll(x, h1, 1) * b1

    def rope2(x):
        return x * c2 + pltpu.roll(x, LANES - h2, 1) * a2 + pltpu.roll(x, h2, 1) * b2

    nblk = N_DIL * DIL_HEADS
    slot = 0
    for s in range(3):
        for gi, (_, d) in enumerate(DIL_PATTERNS):
            dst = dil_refs[3 * gi + s]
            for h in range(DIL_HEADS):
                j = s * nblk + gi * DIL_HEADS + h
                x = c_ref[0, :, j * LANES:(j + 1) * LANES]
                if s < 2:
                    x = rope1(x.astype(F32))
                if s == 0:
                    x = x * (DIL_HEAD_DIM ** -0.5 * LOG2E)
                hs = slice(h * LANES, (h + 1) * LANES)
                if d == 1:
                    dst[0, 0, :, hs] = x.astype(BF16)
                else:
                    xs_ref[slot] = x.astype(F32)
                    for r in range(d):
                        dst[0, r, :, hs] = xs_ref[slot, pl.ds(r, tl // d, stride=d), :].astype(BF16)
                    slot += 1
    for j in range(SWA_Q_HEADS // 2):
        xr = rope2(dq_ref[0, :, j * LANES:(j + 1) * LANES].astype(F32)) * (SWA_HEAD_DIM ** -0.5 * LOG2E)
        dqe_ref[0, :, (2 * j) * LANES:(2 * j + 1) * LANES] = jnp.where(low, xr, 0.0).astype(BF16)
        dqe_ref[0, :, (2 * j + 1) * LANES:(2 * j + 2) * LANES] = jnp.where(low, pltpu.roll(xr, SWA_HEAD_DIM, 1), 0.0).astype(BF16)
    kr = rope2(dkv_ref[0, :, 0:LANES].astype(F32))
    vv = dkv_ref[0, :, LANES:2 * LANES].astype(F32)
    kr_sw = pltpu.roll(kr, SWA_HEAD_DIM, 1)
    vv_sw = pltpu.roll(vv, SWA_HEAD_DIM, 1)
    dk_ref[0, :, 0:LANES] = jnp.where(low, kr, 0.0).astype(BF16)
    dk_ref[0, :, LANES:2 * LANES] = jnp.where(low, kr_sw, 0.0).astype(BF16)
    dv_ref[0, :, 0:LANES] = jnp.where(low, vv, vv_sw).astype(BF16)
    dv_ref[0, :, LANES:2 * LANES] = jnp.where(low, vv_sw, vv).astype(BF16)


def _odd_prep(z, tl=256):
    b, l, _ = z.shape
    wc = 3 * N_DIL * DIL_WIDTH
    t1 = _rope_tables(l, DIL_HEAD_DIM)
    t2 = _rope_tables(l, SWA_HEAD_DIM)
    tab = pl.BlockSpec((tl, LANES), lambda i, t: (t, 0))

    def spec(w):
        return pl.BlockSpec((1, tl, w), lambda i, t: (i, t, 0))

    def shape(w):
        return jax.ShapeDtypeStruct((b, l, w), BF16)

    dil_specs, dil_shapes = [], []
    for _, d in DIL_PATTERNS:
        for _ in range(3):
            dil_specs.append(pl.BlockSpec((1, d, tl // d, DIL_WIDTH), lambda i, t: (i, 0, t, 0)))
            dil_shapes.append(jax.ShapeDtypeStruct((b, d, l // d, DIL_WIDTH), BF16))
    n_staged = 3 * DIL_HEADS * sum(1 for _, d in DIL_PATTERNS if d > 1)
    outs = pl.pallas_call(
        _odd_prep_kernel,
        grid=(b, l // tl),
        in_specs=[
            pl.BlockSpec((1, tl, wc), lambda i, t: (i, t, 0)),
            pl.BlockSpec((1, tl, SWA_WIDTH), lambda i, t: (i, t, O_DQ // SWA_WIDTH)),
            pl.BlockSpec((1, tl, 2 * LANES), lambda i, t: (i, t, O_DKV // (2 * LANES))),
            tab, tab, tab, tab, tab, tab,
        ],
        out_specs=dil_specs + [spec(2 * SWA_WIDTH), spec(2 * LANES), spec(2 * LANES)],
        out_shape=dil_shapes + [shape(2 * SWA_WIDTH), shape(2 * LANES), shape(2 * LANES)],
        scratch_shapes=[pltpu.VMEM((n_staged, tl, LANES), F32)],
        compiler_params=_cparams("parallel", "arbitrary"),
        name="odd_prep",
    )(z, z, z, *t1, *t2)
    dil = [tuple(outs[3 * gi:3 * gi + 3]) for gi in range(N_DIL)]
    return dil, outs[3 * N_DIL], outs[3 * N_DIL + 1], outs[3 * N_DIL + 2]


def _band_bias_t(tq, half, ls):
    p0 = pl.program_id(1) * tq
    j = lax.broadcasted_iota(jnp.int32, (tq + 2 * half, tq), 0)
    i = lax.broadcasted_iota(jnp.int32, (tq + 2 * half, tq), 1)
    kpos = p0 - half + j
    valid = (j >= i) & (j - i <= 2 * half) & (kpos >= 0) & (kpos < ls)
    return jnp.where(valid, 0.0, -jnp.inf)


def _band_mask(tq, half, ls):
    p0 = pl.program_id(1) * tq
    i = lax.broadcasted_iota(jnp.int32, (tq, tq + 2 * half), 0)
    j = lax.broadcasted_iota(jnp.int32, (tq, tq + 2 * half), 1)
    kpos = p0 - half + j
    return (j >= i) & (j - i <= 2 * half) & (kpos >= 0) & (kpos < ls)


def _band_attn_kernel(q_ref, kp_ref, kc_ref, kn_ref, vp_ref, vc_ref, vn_ref, o_ref, lse_ref, *, half, ls):
    tq = q_ref.shape[1]
    dh = DIL_HEAD_DIM
    valid = _band_mask(tq, half, ls)
    lane = lax.broadcasted_iota(jnp.int32, (tq, LANES), 1)
    lse_all = jnp.zeros((tq, LANES), F32)
    heads = range(DIL_HEADS)
    hsl = [slice(h * dh, (h + 1) * dh) for h in heads]
    kall = [jnp.concatenate([kp_ref[0, :, s], kc_ref[0, :, s], kn_ref[0, :, s]], axis=0) for s in hsl]
    ones = jnp.ones((tq + 2 * half, dh), BF16)
    vall = [jnp.concatenate([jnp.concatenate([vp_ref[0, :, s], vc_ref[0, :, s], vn_ref[0, :, s]], axis=0), ones], axis=1)
            for s in hsl]
    sc = [jnp.where(valid, _dot_nt(q_ref[0, :, hsl[h]], kall[h]), -jnp.inf) for h in heads]
    m = [jnp.max(sc[h], axis=-1, keepdims=True) for h in heads]
    p = [jnp.exp2(sc[h] - m[h]).astype(BF16) for h in heads]
    od = [_dot(p[h], vall[h]) for h in heads]
    for h in heads:
        den = od[h][:, dh:]
        o_ref[0, :, hsl[h]] = (od[h][:, :dh] / den).astype(o_ref.dtype)
        lse_all = jnp.where(lane == h, m[h] + jnp.log2(den), lse_all)
    lse_ref[0] = lse_all


def _band_specs(tq, half, ls, w, col):
    r = tq // half
    nb = ls // half
    cur = pl.BlockSpec((1, tq, w), lambda i, t: (i, t, col))
    prev = pl.BlockSpec((1, half, w), lambda i, t: (i, jnp.maximum(t * r - 1, 0), col))
    nxt = pl.BlockSpec((1, half, w), lambda i, t: (i, jnp.minimum((t + 1) * r, nb - 1), col))
    return cur, prev, nxt


def _band_attn(q, k, v, half):
    n, ls, w = q.shape
    tq = min(2 * LANES, ls)
    cur, prev, nxt = _band_specs(tq, half, ls, w, 0)
    return pl.pallas_call(
        functools.partial(_band_attn_kernel, half=half, ls=ls),
        grid=(n, ls // tq),
        in_specs=[cur, prev, cur, nxt, prev, cur, nxt],
        out_specs=[cur, pl.BlockSpec((1, tq, LANES), lambda i, t: (i, t, 0))],
        out_shape=[jax.ShapeDtypeStruct((n, ls, w), BF16), jax.ShapeDtypeStruct((n, ls, LANES), F32)],
        compiler_params=_cparams("parallel", "arbitrary"),
        name="band_attn",
    )(q, k, k, k, v, v, v)


def _dil_merge_kernel(o0_ref, o1_ref, o2_ref, l0_ref, l1_ref, l2_ref, g_ref, y_ref, os_ref, ls_ref):
    dh = DIL_HEAD_DIM
    tl = y_ref.shape[1]
    o_refs = (o0_ref, o1_ref, o2_ref)
    l_refs = (l0_ref, l1_ref, l2_ref)
    slot = 0
    lse, outs = [], []
    for gi, (_, d) in enumerate(DIL_PATTERNS):
        if d == 1:
            lse.append(l_refs[gi][0, 0])
            outs.append([o_refs[gi][0, 0, :, h * dh:(h + 1) * dh].astype(F32) for h in range(DIL_HEADS)])
            continue
        for r in range(d):
            ls_ref[gi, pl.ds(r, tl // d, stride=d), :] = l_refs[gi][0, r]
        lse.append(ls_ref[gi])
        per_head = []
        for h in range(DIL_HEADS):
            for r in range(d):
                os_ref[slot, pl.ds(r, tl // d, stride=d), :] = o_refs[gi][0, r, :, h * dh:(h + 1) * dh].astype(F32)
            per_head.append(os_ref[slot])
            slot += 1
        outs.append(per_head)
    for h in range(DIL_HEADS):
        hs = slice(h * dh, (h + 1) * dh)
        ls = [x[:, h:h + 1] for x in lse]
        m = jnp.maximum(jnp.maximum(ls[0], ls[1]), ls[2])
        ws = [jnp.exp2(x - m) for x in ls]
        den = ws[0] + ws[1] + ws[2]
        y = (ws[0] / den) * outs[0][h] + (ws[1] / den) * outs[1][h] + (ws[2] / den) * outs[2][h]
        y_ref[0, :, hs] = (y * _silu(g_ref[0, :, hs].astype(F32))).astype(y_ref.dtype)


def _dil_merge(outs, lses, z, tl=512):
    b, _, l, w = outs[0].shape
    o_specs = [pl.BlockSpec((1, d, tl // d, w), lambda i, t: (i, 0, t, 0)) for _, d in DIL_PATTERNS]
    l_specs = [pl.BlockSpec((1, d, tl // d, LANES), lambda i, t: (i, 0, t, 0)) for _, d in DIL_PATTERNS]
    n_staged = DIL_HEADS * sum(1 for _, d in DIL_PATTERNS if d > 1)
    return pl.pallas_call(
        _dil_merge_kernel,
        grid=(b, l // tl),
        in_specs=o_specs + l_specs + [pl.BlockSpec((1, tl, w), lambda i, t: (i, t, O_GC // DIL_WIDTH))],
        out_specs=pl.BlockSpec((1, tl, w), lambda i, t: (i, t, 0)),
        out_shape=jax.ShapeDtypeStruct((b, l, w), BF16),
        scratch_shapes=[pltpu.VMEM((n_staged, tl, LANES), F32), pltpu.VMEM((N_DIL, tl, LANES), F32)],
        compiler_params=_cparams("parallel", "arbitrary"),
        name="dil_merge",
    )(*outs, *lses, z)


def _dilated_branch(dil, z):
    outs, lses = [], []
    for (window, d), (q, k, v) in zip(DIL_PATTERNS, dil):
        b, _, ls, w = q.shape
        half = window // (2 * d)
        o, lse = _band_attn(q.reshape(b * d, ls, w), k.reshape(b * d, ls, w), v.reshape(b * d, ls, w), half)
        outs.append(o.reshape(b, d, ls, w))
        lses.append(lse.reshape(b, d, ls, LANES))
    return _dil_merge(outs, lses, z)


def _swa_kernel(q_ref, kp_ref, kc_ref, kn_ref, vp_ref, vc_ref, vn_ref, g_ref, sink_ref, o_ref, *, half, ls):
    tq = q_ref.shape[1]
    dh = SWA_HEAD_DIM
    bias1 = _band_bias_t(tq, half, ls)
    bias = jnp.concatenate([bias1, bias1], axis=1)
    b_prev, b_next = bias[:half], bias[half + tq:]
    low = lax.broadcasted_iota(jnp.int32, (LANES, tq), 0) < dh
    first = lax.broadcasted_iota(jnp.int32, (1, 2 * tq), 1) < tq
    grp = SWA_Q_HEADS // SWA_KV_HEADS
    for g in range(SWA_KV_HEADS):
        gs = slice(g * LANES, (g + 1) * LANES)
        kall = jnp.concatenate([kp_ref[0, :, gs], kc_ref[0, :, gs], kn_ref[0, :, gs]], axis=0)
        vt = _t_bf16(jnp.concatenate([vp_ref[0, :, gs], vc_ref[0, :, gs], vn_ref[0, :, gs]], axis=0))
        pairs = range(grp // 2)
        h0 = [g * grp + 2 * jp for jp in pairs]
        q2 = [q_ref[0, :, h * LANES:(h + 2) * LANES] for h in h0]
        q2 = [jnp.concatenate([x[:, :LANES], x[:, LANES:]], axis=0) for x in q2]
        sc = [_dot_nt(kall, q2[jp]) for jp in pairs]
        sc = [jnp.concatenate([s[:half] + b_prev, s[half:half + tq], s[half + tq:] + b_next], axis=0) for s in sc]
        sk = [jnp.where(first, sink_ref[:, h:h + 1], sink_ref[:, h + 1:h + 2]) * LOG2E for h in h0]
        sm = [_softmax_t(sc[jp], sk[jp]) for jp in pairs]
        ot = [_dot(vt, sm[jp][0]) / sm[jp][1] for jp in pairs]
        for jp in pairs:
            blk = (g * grp) // 2 + jp
            bs = slice(blk * LANES, (blk + 1) * LANES)
            y = jnp.where(low, ot[jp][:, :tq], ot[jp][:, tq:]).T * _silu(g_ref[0, :, bs].astype(F32))
            o_ref[0, :, bs] = y.astype(o_ref.dtype)


def _swa_branch(dqe, dk, dv, z, sink, tq=128):
    b, l, _ = dqe.shape
    half = SWA_HALF_WINDOW
    assert tq == half == LANES
    _, prev, nxt = _band_specs(tq, half, l, 2 * LANES, 0)
    cur = pl.BlockSpec((1, tq, 2 * LANES), lambda i, t: (i, t, 0))
    sink_p = jnp.pad(sink.reshape(1, SWA_Q_HEADS).astype(F32), ((0, 0), (0, LANES - SWA_Q_HEADS)))
    return pl.pallas_call(
        functools.partial(_swa_kernel, half=half, ls=l),
        grid=(b, l // tq),
        in_specs=[
            pl.BlockSpec((1, tq, 2 * SWA_WIDTH), lambda i, t: (i, t, 0)),
            prev, cur, nxt, prev, cur, nxt,
            pl.BlockSpec((1, tq, SWA_WIDTH), lambda i, t: (i, t, O_GD // SWA_WIDTH)),
            pl.BlockSpec((1, LANES), lambda i, t: (0, 0)),
        ],
        out_specs=pl.BlockSpec((1, tq, SWA_WIDTH), lambda i, t: (i, t, 0)),
        out_shape=jax.ShapeDtypeStruct((b, l, SWA_WIDTH), BF16),
        compiler_params=_cparams("parallel", "arbitrary"),
        name="swa",
    )(dqe, dk, dk, dk, dv, dv, dv, z, sink_p)


def _odd_layer(x, mem, pre_g, post_g, w_in, w_out, swa_sink, mem_g, w_mem_kv):
    b, l, d = x.shape
    x2d = x.reshape(b * l, d)
    w_re = jnp.concatenate([w_in[:, 0:6144], w_in[:, 6400:8448], w_in[:, 6144:6400]], axis=1)
    z = _norm_matmul(x2d, pre_g, w_re, O_ALL // 3, out_dtype=BF16, tm=1024, name="odd_in").reshape(b, l, O_ALL)
    dil, dqe, dk, dv = _odd_prep(z)
    y_c = _dilated_branch(dil, z)
    y_d = _swa_branch(dqe, dk, dv, z, swa_sink)
    kv = _mem_kv(mem, mem_g, w_mem_kv)
    y_x = _xattn(z, kv, O_XQ // X_WIDTH, O_GX // X_WIDTH)
    t = b * l
    out = _out_proj([y_c.reshape(t, -1), y_d.reshape(t, -1), y_x.reshape(t, -1)], w_out, post_g, x2d)
    return out.reshape(b, l, d)


def _trunk(x, mem, even_params, odd_params):
    x = _even_layer(x, mem, *[p[0] for p in even_params])
    return _odd_layer(x, mem, *[p[0] for p in odd_params])


def kernel(x_prompt, x_sample, mem_prompt, mem_sample, e_pre_g, e_post_g, e_w_in, e_w_out, hy_conv_w, hy_conv_b,
           hy_filt_w1, hy_filt_b1, hy_filt_w2, hy_filt_b2, hy_filt_w3, hy_freq, hy_skip, gdn_conv_w, gdn_A_log,
           gdn_dt_bias, gdn_norm_g, e_mem_g, e_w_mem_kv, o_pre_g, o_post_g, o_w_in, o_w_out, swa_sink, o_mem_g,
           o_w_mem_kv):
    even_params = (e_pre_g, e_post_g, e_w_in, e_w_out, hy_conv_w, hy_conv_b, hy_filt_w1, hy_filt_b1, hy_filt_w2,
                   hy_filt_b2, hy_filt_w3, hy_freq, hy_skip, gdn_conv_w, gdn_A_log, gdn_dt_bias, gdn_norm_g,
                   e_mem_g, e_w_mem_kv)
    odd_params = (o_pre_g, o_post_g, o_w_in, o_w_out, swa_sink, o_mem_g, o_w_mem_kv)
    y_prompt = _trunk(x_prompt, mem_prompt, even_params, odd_params)
    y_sample = _trunk(x_sample, mem_sample, even_params, odd_params)
    return (y_prompt, y_sample)
```

```python
import functools
import math

import jax
import jax.numpy as jnp
import numpy as np
from jax import lax
from jax.experimental import pallas as pl
from jax.experimental.pallas import tpu as pltpu

D_MODEL = 1024
EPS = 1e-6
ROPE_THETA = 500000.0
HY_WIDTH = 1024
HY_EMB = 33
HY_BANDS = (HY_EMB - 1) // 2
HY_FILT_HIDDEN = 64
HY_TARGET = 1e-2
HY_FAST_PCT = 0.3
HY_SLOW_PCT = 1.5
GDN_HEADS = 8
GDN_HEAD_DIM = 128
GDN_WIDTH = GDN_HEADS * GDN_HEAD_DIM
GDN_CHUNK = 64
X_HEADS = 4
X_HEAD_DIM = 128
X_WIDTH = X_HEADS * X_HEAD_DIM

LOG2E = math.log2(math.e)
LANES = 128
FFT_N2 = 128
VMEM_LIMIT_BYTES = 48 * 1024 * 1024

BF16 = jnp.bfloat16
F32 = jnp.float32


def _cparams(*sem):
    return pltpu.CompilerParams(dimension_semantics=sem, vmem_limit_bytes=VMEM_LIMIT_BYTES)


def _dot(a, b):
    return jnp.dot(a, b, preferred_element_type=F32)


def _silu(x):
    hx = 0.5 * x
    return hx + hx * jnp.tanh(hx)


def _norm_matmul_kernel(*refs, side):
    if side:
        x_ref, g_ref, w_ref, ws_ref, o_ref, os_ref, xn_ref = refs
    else:
        x_ref, g_ref, w_ref, o_ref, xn_ref = refs

    @pl.when(pl.program_id(1) == 0)
    def _():
        x = x_ref[...]
        ms = jnp.mean(x * x, axis=-1, keepdims=True)
        xn_ref[...] = (x * lax.rsqrt(ms + EPS) * g_ref[...]).astype(BF16)
        if side:
            os_ref[...] = _dot(xn_ref[...], ws_ref[...])

    o_ref[...] = _dot(xn_ref[...], w_ref[...]).astype(o_ref.dtype)


def _norm_matmul(x2d, g, w, tn, out_dtype=F32, tm=512, name="norm_matmul", w_side=None):
    t, d = x2d.shape
    n = w.shape[1]
    assert t % tm == 0 and n % tn == 0
    side = w_side is not None
    in_specs = [
        pl.BlockSpec((tm, d), lambda i, j: (i, 0)),
        pl.BlockSpec((1, d), lambda i, j: (0, 0)),
        pl.BlockSpec((d, tn), lambda i, j: (0, j)),
    ]
    out_specs = [pl.BlockSpec((tm, tn), lambda i, j: (i, j))]
    out_shape = [jax.ShapeDtypeStruct((t, n), out_dtype)]
    args = [x2d, g.reshape(1, d).astype(F32), w.astype(BF16)]
    if side:
        in_specs.append(pl.BlockSpec((d, LANES), lambda i, j: (0, 0)))
        out_specs.append(pl.BlockSpec((tm, LANES), lambda i, j: (i, 0)))
        out_shape.append(jax.ShapeDtypeStruct((t, LANES), F32))
        args.append(w_side.astype(BF16))
    outs = pl.pallas_call(
        functools.partial(_norm_matmul_kernel, side=side),
        grid=(t // tm, n // tn),
        in_specs=in_specs,
        out_specs=out_specs,
        out_shape=out_shape,
        scratch_shapes=[pltpu.VMEM((tm, d), BF16)],
        compiler_params=_cparams("parallel", "arbitrary"),
        name=name,
    )(*args)
    return tuple(outs) if side else outs[0]


HALO_ROWS = 16


TAP_ROWS = 128
TAP_WINDOW = 256


def _fill_halo_buf(buf_ref, zc_ref, zp_ref, zn_ref, t, nt):
    tl = zc_ref.shape[1]
    hr = HALO_ROWS
    buf_ref[0:hr, :] = jnp.where(t > 0, zp_ref[0].astype(F32), 0.0).astype(BF16)
    buf_ref[hr:hr + tl, :] = zc_ref[0].astype(BF16)
    buf_ref[hr + tl:2 * hr + tl, :] = jnp.where(t < nt - 1, zn_ref[0].astype(F32), 0.0).astype(BF16)
    buf_ref[2 * hr + tl:, :] = jnp.zeros((buf_ref.shape[0] - 2 * hr - tl, buf_ref.shape[1]), BF16)


def _halo_buf_rows(tl):
    return tl - TAP_ROWS + TAP_WINDOW


def _tap_shift_matrix(ntaps):
    n = lax.broadcasted_iota(jnp.int32, (ntaps, TAP_ROWS, TAP_WINDOW), 1)
    j = lax.broadcasted_iota(jnp.int32, (ntaps, TAP_ROWS, TAP_WINDOW), 0)
    m = lax.broadcasted_iota(jnp.int32, (ntaps, TAP_ROWS, TAP_WINDOW), 2)
    return (m == HALO_ROWS + n + j - ntaps // 2).astype(BF16).reshape(ntaps * TAP_ROWS, TAP_WINDOW)


def _taps(buf_ref, s_ref, r0, lanes):
    return _dot(s_ref[...], buf_ref[r0:r0 + TAP_WINDOW, lanes])


def _halo_specs(tl, w, l, col):
    r = tl // HALO_ROWS
    nb = l // HALO_ROWS
    cur = pl.BlockSpec((1, tl, w), lambda i, t: (i, t, col))
    prev = pl.BlockSpec((1, HALO_ROWS, w), lambda i, t: (i, jnp.maximum(t * r - 1, 0), col))
    nxt = pl.BlockSpec((1, HALO_ROWS, w), lambda i, t: (i, jnp.minimum((t + 1) * r, nb - 1), col))
    return cur, prev, nxt


def _hy_prep_kernel(zc_ref, zp_ref, zn_ref, g_ref, cw_ref, cb_ref, s_ref, u_ref, e_ref, buf_ref):
    t = pl.program_id(1)
    nt = pl.num_programs(1)
    tl = zc_ref.shape[1]
    c = HY_WIDTH
    tr = TAP_ROWS
    _fill_halo_buf(buf_ref, zc_ref, zp_ref, zn_ref, t, nt)
    cw = 256
    for r0 in range(0, tl, tr):
        rs = slice(r0, r0 + tr)
        for c0 in range(0, c, cw):
            parts = []
            for p in range(3):
                lo = p * c + c0
                xs = _taps(buf_ref, s_ref, r0, slice(lo, lo + cw))
                acc = cb_ref[:, lo:lo + cw]
                for j in range(3):
                    acc = acc + xs[j * tr:(j + 1) * tr] * cw_ref[j:j + 1, lo:lo + cw]
                parts.append(acc)
            x0, x1, v = parts
            u_ref[0, rs, c0:c0 + cw] = (v * x1).astype(u_ref.dtype)
            e_ref[0, rs, c0:c0 + cw] = (x0 * _silu(g_ref[0, rs, c0:c0 + cw].astype(F32))).astype(e_ref.dtype)


def _hy_prep(z, conv_w, conv_b, tl=512):
    b, l, _ = z.shape
    c = HY_WIDTH
    assert l % tl == 0
    cur, prev, nxt = _halo_specs(tl, 3 * c, l, E_HY // (3 * c))
    return pl.pallas_call(
        _hy_prep_kernel,
        grid=(b, l // tl),
        in_specs=[
            cur, prev, nxt,
            pl.BlockSpec((1, tl, c), lambda i, t: (i, t, E_GHY // c)),
            pl.BlockSpec((3, 3 * c), lambda i, t: (0, 0)),
            pl.BlockSpec((1, 3 * c), lambda i, t: (0, 0)),
            pl.BlockSpec((3 * TAP_ROWS, TAP_WINDOW), lambda i, t: (0, 0)),
        ],
        out_specs=[
            pl.BlockSpec((1, tl, c), lambda i, t: (i, t, 0)),
            pl.BlockSpec((1, tl, c), lambda i, t: (i, t, 0)),
        ],
        out_shape=[jax.ShapeDtypeStruct((b, l, c), F32), jax.ShapeDtypeStruct((b, l, c), BF16)],
        scratch_shapes=[pltpu.VMEM((_halo_buf_rows(tl), 3 * c), BF16)],
        compiler_params=_cparams("parallel", "arbitrary"),
        name="hy_prep",
    )(z, z, z, z, conv_w.astype(F32), conv_b.reshape(1, 3 * c).astype(F32), _tap_shift_matrix(3))


def _hy_filter_tables(l):
    t = jnp.linspace(0.0, 1.0, l, dtype=F32)[:, None]
    w = (2.0 * math.pi / l) * jnp.arange(l, dtype=F32)[:, None]
    f = jnp.linspace(1e-4, HY_BANDS - 1, HY_BANDS, dtype=F32)[None, :]
    emb = jnp.concatenate([t, jnp.cos(f * w), -jnp.sin(f * w)], axis=-1)
    emb = jnp.pad(emb, ((0, 0), (0, LANES - HY_EMB)))
    deltas = jnp.abs(jnp.linspace(math.log(HY_TARGET) / HY_SLOW_PCT, math.log(HY_TARGET) / HY_FAST_PCT, HY_WIDTH, dtype=F32))
    decay = jnp.exp(-t * jnp.tile(deltas, 2)[None, :])
    return emb, decay


def _hy_filter_kernel(emb_ref, dec_ref, w1_ref, b1_ref, w2_ref, b2_ref, w3_ref, fr_ref, sk_ref, o_ref):
    c = HY_WIDTH
    hp = lax.Precision.HIGHEST
    fr = fr_ref[...]
    hid = jnp.sin(fr * (jnp.dot(emb_ref[...], w1_ref[...], precision=hp, preferred_element_type=F32) + b1_ref[...]))
    hid = jnp.sin(fr * (jnp.dot(hid, w2_ref[...], precision=hp, preferred_element_type=F32) + b2_ref[...]))
    tl = emb_ref.shape[0]
    row = lax.broadcasted_iota(jnp.int32, (tl, 1), 0) + pl.program_id(0) * tl
    first = row == 0
    cw = 512
    for c0 in range(0, 2 * c, cw):
        filt = jnp.dot(hid, w3_ref[:, c0:c0 + cw], precision=hp, preferred_element_type=F32) * dec_ref[:, c0:c0 + cw]
        if c0 < c:
            filt = jnp.where(first, filt + sk_ref[:, c0:c0 + cw], filt)
            o_ref[0, :, c0:c0 + cw] = filt.astype(o_ref.dtype)
        else:
            filt = jnp.where(first, 0.0, filt)
            o_ref[1, :, c0 - c:c0 - c + cw] = filt.astype(o_ref.dtype)


def _hy_filters(l, w1, b1, w2, b2, w3, freq, skip, tl=256):
    c = HY_WIDTH
    hdim = HY_FILT_HIDDEN
    emb, decay = _hy_filter_tables(l)
    w1p = jnp.pad(w1.astype(F32), ((0, LANES - HY_EMB), (0, 0)))
    return pl.pallas_call(
        _hy_filter_kernel,
        grid=(l // tl,),
        in_specs=[
            pl.BlockSpec((tl, LANES), lambda t: (t, 0)),
            pl.BlockSpec((tl, 2 * c), lambda t: (t, 0)),
            pl.BlockSpec((LANES, hdim), lambda t: (0, 0)),
            pl.BlockSpec((1, hdim), lambda t: (0, 0)),
            pl.BlockSpec((hdim, hdim), lambda t: (0, 0)),
            pl.BlockSpec((1, hdim), lambda t: (0, 0)),
            pl.BlockSpec((hdim, 2 * c), lambda t: (0, 0)),
            pl.BlockSpec((1, hdim), lambda t: (0, 0)),
            pl.BlockSpec((1, c), lambda t: (0, 0)),
        ],
        out_specs=pl.BlockSpec((2, tl, c), lambda t: (0, t, 0)),
        out_shape=jax.ShapeDtypeStruct((2, l, c), BF16),
        compiler_params=_cparams("arbitrary"),
        name="hy_filter",
    )(emb, decay, w1p, b1.reshape(1, hdim).astype(F32), w2.astype(F32), b2.reshape(1, hdim).astype(F32),
      w3.astype(F32), freq.reshape(1, hdim).astype(F32), skip.reshape(1, c).astype(F32))


def _fft_consts(l):
    n = 2 * l
    n2 = FFT_N2
    n1 = n // n2
    n1h = n1 // 2
    w0 = 2.0 * math.pi / n
    k1 = jnp.arange(n1, dtype=jnp.int32)
    m1 = jnp.arange(n1h, dtype=jnp.int32)
    m2 = jnp.arange(n2, dtype=jnp.int32)
    ea = (n2 * m1[None, None, :] * k1[None, :, None] + m2[:, None, None] * k1[None, :, None]) % n
    ang = ea.astype(F32) * w0
    fa = jnp.concatenate([jnp.cos(ang), -jnp.sin(ang)], axis=1).astype(BF16)
    eb = (n1 * m2[:, None] * m2[None, :]) % n
    angb = eb.astype(F32) * w0
    fr, fi = jnp.cos(angb), -jnp.sin(angb)
    fb = jnp.concatenate([jnp.concatenate([fr, -fi], axis=1), jnp.concatenate([fi, fr], axis=1)], axis=0).astype(BF16)
    eg = (n1 * m2[None, :, None] * m2[None, None, :] + m2[None, :, None] * k1[:, None, None]) % n
    angg = eg.astype(F32) * w0
    gr, gi = jnp.cos(angg), jnp.sin(angg)
    gb = jnp.concatenate([jnp.concatenate([gr, -gi], axis=2), jnp.concatenate([gi, gr], axis=2)], axis=1).astype(BF16)
    ec = (n2 * m1[:, None] * k1[None, :]) % n
    angc = ec.astype(F32) * w0
    fc = (jnp.concatenate([jnp.cos(angc), -jnp.sin(angc)], axis=1) * (1.0 / n)).astype(BF16)
    return fa, fb, gb, fc


def _fft_a_kernel(u_ref, f_ref, o_ref):
    tn2 = f_ref.shape[0]
    n1 = f_ref.shape[1] // 2
    c = u_ref.shape[2] // tn2
    for j in range(tn2):
        a = _dot(f_ref[j], u_ref[0, :, j * c:(j + 1) * c])
        o_ref[0, 0, :, j * c:(j + 1) * c] = a[:n1].astype(o_ref.dtype)
        o_ref[0, 1, :, j * c:(j + 1) * c] = a[n1:].astype(o_ref.dtype)


def _fft_a(u, fa, tn2=8):
    b, l, c = u.shape
    n2, n1x2, n1h = fa.shape
    n1 = n1x2 // 2
    uv = u.reshape(b, n1h, n2 * c)
    return pl.pallas_call(
        _fft_a_kernel,
        grid=(b, n2 // tn2),
        in_specs=[
            pl.BlockSpec((1, n1h, tn2 * c), lambda i, j: (i, 0, j)),
            pl.BlockSpec((tn2, n1x2, n1h), lambda i, j: (j, 0, 0)),
        ],
        out_specs=pl.BlockSpec((1, 2, n1, tn2 * c), lambda i, j: (i, 0, 0, j)),
        out_shape=jax.ShapeDtypeStruct((b, 2, n1, n2 * c), BF16),
        compiler_params=_cparams("parallel", "arbitrary"),
        name="fft_a",
    )(uv, fa)


def _fft_bf_kernel(a_ref, w_ref, o_ref):
    n2 = a_ref.shape[2]
    ct = a_ref.shape[3]
    cw = 256
    for c0 in range(0, ct, cw):
        xf = _dot(w_ref[...], a_ref[0, :, :, c0:c0 + cw].reshape(2 * n2, cw))
        xb = _dot(w_ref[...], a_ref[1, :, :, c0:c0 + cw].reshape(2 * n2, cw))
        o_ref[0, :, c0:c0 + cw] = xf[:n2] + xb[:n2]
        o_ref[1, :, c0:c0 + cw] = xf[n2:] - xb[n2:]


def _fft_bf(a, fb):
    _, _, n, c = a.shape
    n2 = FFT_N2
    return pl.pallas_call(
        _fft_bf_kernel,
        grid=(n // n2,),
        in_specs=[
            pl.BlockSpec((2, 2, n2, c), lambda k: (0, 0, k, 0)),
            pl.BlockSpec((2 * n2, 2 * n2), lambda k: (0, 0)),
        ],
        out_specs=pl.BlockSpec((2, n2, c), lambda k: (0, k, 0)),
        out_shape=jax.ShapeDtypeStruct((2, n, c), F32),
        compiler_params=_cparams("arbitrary"),
        name="fft_bf",
    )(a, fb)


SUBLANES = 8
CONV_SLABS = 2


def _conv_dims(n1):
    kh = n1 // 2 + 1
    kp = -(-kh // SUBLANES) * SUBLANES
    return kh, kp, 2 * kp + SUBLANES


def _hy_conv_kernel(u_ref, h_ref, la_ref, w_ref, g_ref, fc_ref, o_ref, s_ref, *, n1, unroll_b):
    n2 = FFT_N2
    n1h = n1 // 2
    kh, kp, pitch = _conv_dims(n1)
    grp = SUBLANES
    slabs = range(CONV_SLABS)

    def lanes(x, sl):
        return x[:, sl * LANES:(sl + 1) * LANES]

    def stage_a(jj, carry):
        rows = [u_ref[0, pl.ds(pl.multiple_of(m * n2 + jj * grp, grp), grp), :] for m in range(n1h)]
        a = _dot(la_ref[jj], jnp.concatenate(rows, axis=0).astype(BF16))
        for j0 in range(grp):
            dst = pl.multiple_of((jj * grp + j0) * pitch, grp)
            for sl in slabs:
                s_ref[sl, pl.ds(dst, 2 * kp), :] = lanes(a[j0 * 2 * kp:(j0 + 1) * 2 * kp], sl)
        return carry

    lax.fori_loop(0, n2 // grp, stage_a, 0, unroll=2)

    def stage_b(kk, carry):
        k1s = [jnp.minimum(kk * unroll_b + i, kh - 1) for i in range(unroll_b)]
        ar = [jnp.concatenate([s_ref[sl, pl.ds(k, n2, stride=pitch), :] for sl in slabs], axis=1) for k in k1s]
        ai = [jnp.concatenate([s_ref[sl, pl.ds(kp + k, n2, stride=pitch), :] for sl in slabs], axis=1) for k in k1s]
        x = [_dot(w_ref[...], jnp.concatenate([ar[i], ai[i]], axis=0).astype(BF16)) for i in range(unroll_b)]
        y = []
        for i, k in enumerate(k1s):
            off = pl.multiple_of(k * n2, n2)
            hr, hi = h_ref[0, pl.ds(off, n2), :], h_ref[1, pl.ds(off, n2), :]
            xr, xi = x[i][:n2], x[i][n2:]
            y.append(jnp.concatenate([xr * hr - xi * hi, xr * hi + xi * hr], axis=0).astype(BF16))
        z = [_dot(g_ref[k1s[i]], y[i]) for i in range(unroll_b)]
        for i, k in enumerate(k1s):
            for sl in slabs:
                s_ref[sl, pl.ds(k, n2, stride=pitch), :] = lanes(z[i][:n2], sl)
                s_ref[sl, pl.ds(kp + k, n2, stride=pitch), :] = lanes(z[i][n2:], sl)
        return carry

    lax.fori_loop(0, -(-kh // unroll_b), stage_b, 0)

    def stage_c(jj, carry):
        zz = [jnp.concatenate([s_ref[sl, pl.ds(pl.multiple_of((jj * grp + j0) * pitch, grp), 2 * kp), :]
                               for sl in slabs], axis=1) for j0 in range(grp)]
        y = _dot(fc_ref[...], jnp.concatenate(zz, axis=0).astype(BF16))
        for m in range(n1h):
            o_ref[0, pl.ds(pl.multiple_of(m * n2 + jj * grp, grp), grp), :] = y[m * grp:(m + 1) * grp]
        return carry

    lax.fori_loop(0, n2 // grp, stage_c, 0, unroll=2)


def _hy_conv(u, h, fa, fb, gb, fc):
    b, l, c = u.shape
    n2, n1x2, n1h = fa.shape
    n1 = n1x2 // 2
    kh, kp, pitch = _conv_dims(n1)
    grp = SUBLANES
    ct = CONV_SLABS * LANES

    def half(t, axis):
        re, im = jnp.split(t, 2, axis=axis)
        pad = [(0, 0)] * t.ndim
        pad[axis] = (0, kp - kh)
        cut = lambda x: jnp.pad(lax.slice_in_dim(x, 0, kh, axis=axis), pad)
        return jnp.concatenate([cut(re), cut(im)], axis=axis)

    fa_h = half(fa, 1)
    la = (fa_h.reshape(n2 // grp, grp, 2 * kp, n1h, 1) * jnp.eye(grp, dtype=fa.dtype).reshape(1, grp, 1, 1, grp))
    la = la.reshape(n2 // grp, grp * 2 * kp, n1h * grp)
    wgt = jnp.ones((kh,), F32).at[1:kh - 1].set(2.0)
    wgt = jnp.pad(wgt, (0, kp - kh))
    fc_h = (half(fc, 1).astype(F32) * jnp.tile(wgt, 2)[None, :]).astype(BF16)
    lc = fc_h.reshape(n1h, 1, 1, 2 * kp) * jnp.eye(grp, dtype=fc_h.dtype).reshape(1, grp, grp, 1)
    fc_h = lc.reshape(n1h * grp, grp * 2 * kp)
    gb_h = gb[:kh]
    unroll_b = 3
    once = pl.Buffered(1)
    return pl.pallas_call(
        functools.partial(_hy_conv_kernel, n1=n1, unroll_b=unroll_b),
        grid=(c // ct, b),
        in_specs=[
            pl.BlockSpec((1, l, ct), lambda j, i: (i, 0, j)),
            pl.BlockSpec((2, kh * n2, ct), lambda j, i: (0, 0, j), pipeline_mode=once),
            pl.BlockSpec(la.shape, lambda j, i: (0, 0, 0), pipeline_mode=once),
            pl.BlockSpec(fb.shape, lambda j, i: (0, 0), pipeline_mode=once),
            pl.BlockSpec(gb_h.shape, lambda j, i: (0, 0, 0), pipeline_mode=once),
            pl.BlockSpec(fc_h.shape, lambda j, i: (0, 0), pipeline_mode=once),
        ],
        out_specs=pl.BlockSpec((1, l, ct), lambda j, i: (i, 0, j)),
        out_shape=jax.ShapeDtypeStruct((b, l, c), F32),
        scratch_shapes=[pltpu.VMEM((CONV_SLABS, n2 * pitch, LANES), F32)],
        compiler_params=_cparams("arbitrary", "arbitrary"),
        name="hy_conv",
    )(u, h, la, fb, gb_h, fc_h)


def _hyena_branch(z, l, conv_w, conv_b, w1, b1, w2, b2, w3, freq, skip):
    c = HY_WIDTH
    n = 2 * l
    fa, fb, gb, fc = _fft_consts(l)
    filt = _hy_filters(l, w1, b1, w2, b2, w3, freq, skip)
    fa_spec = _fft_a(filt, fa).reshape(2, 2, n, c)
    h = _fft_bf(fa_spec, fb)
    u, e = _hy_prep(z, conv_w, conv_b)
    return _hy_conv(u, h, fa, fb, gb, fc), e


def _gdn_prep_kernel(zc_ref, zp_ref, zn_ref, cw_ref, s_ref, q_ref, k_ref, v_ref, buf_ref):
    t = pl.program_id(1)
    nt = pl.num_programs(1)
    tl = zc_ref.shape[1]
    dh = GDN_HEAD_DIM
    tr = TAP_ROWS
    _fill_halo_buf(buf_ref, zc_ref, zp_ref, zn_ref, t, nt)
    outs = (q_ref, k_ref, v_ref)
    cw = 2 * dh
    for r0 in range(0, tl, tr):
        for p in range(3):
            for c0 in range(0, GDN_WIDTH, cw):
                lo = p * GDN_WIDTH + c0
                xs = _taps(buf_ref, s_ref, r0, slice(lo, lo + cw))
                acc = xs[0:tr] * cw_ref[0:1, lo:lo + cw]
                for j in range(1, 5):
                    acc = acc + xs[j * tr:(j + 1) * tr] * cw_ref[j:j + 1, lo:lo + cw]
                a = _silu(acc)
                for hh in range(cw // dh):
                    ah = a[:, hh * dh:(hh + 1) * dh]
                    if p < 2:
                        ah = ah * lax.rsqrt(jnp.sum(ah * ah, axis=-1, keepdims=True) + EPS)
                    if p == 0:
                        ah = ah * (dh ** -0.5)
                    outs[p][0, r0:r0 + tr, c0 + hh * dh:c0 + (hh + 1) * dh] = ah.astype(BF16)


def _gdn_prep(z, conv_w, tl=512):
    b, l, _ = z.shape
    w = 3 * GDN_WIDTH
    cur, prev, nxt = _halo_specs(tl, w, l, E_QKV // w)
    o_spec = pl.BlockSpec((1, tl, GDN_WIDTH), lambda i, t: (i, t, 0))
    o_shape = jax.ShapeDtypeStruct((b, l, GDN_WIDTH), BF16)
    return pl.pallas_call(
        _gdn_prep_kernel,
        grid=(b, l // tl),
        in_specs=[cur, prev, nxt, pl.BlockSpec((5, w), lambda i, t: (0, 0)),
                  pl.BlockSpec((5 * TAP_ROWS, TAP_WINDOW), lambda i, t: (0, 0))],
        out_specs=[o_spec, o_spec, o_spec],
        out_shape=[o_shape, o_shape, o_shape],
        scratch_shapes=[pltpu.VMEM((_halo_buf_rows(tl), w), BF16)],
        compiler_params=_cparams("parallel", "arbitrary"),
        name="gdn_prep",
    )(z, z, z, conv_w.astype(F32), _tap_shift_matrix(5))


def _dot_nt(a, b):
    return lax.dot_general(a, b, (((1,), (1,)), ((), ())), preferred_element_type=F32)


def _dot_tn(a, b):
    return lax.dot_general(a, b, (((0,), (0,)), ((), ())), preferred_element_type=F32)


def _split3(x):
    a = x.astype(BF16)
    r = x - a.astype(F32)
    b = r.astype(BF16)
    c = (r - b.astype(F32)).astype(BF16)
    return a, b, c


GDN_CHUNKS_PER_STEP = 8


def _gdn_scan_kernel(*refs, direction, final):
    if final:
        q_ref, k_ref, v_ref, zs_ref, al_ref, dt_ref, of_ref, gg_ref, ng_ref, o_ref, s_ref = refs
    else:
        q_ref, k_ref, v_ref, zs_ref, al_ref, dt_ref, o_ref, s_ref = refs
    c = GDN_CHUNK
    dh = GDN_HEAD_DIM
    nsub = q_ref.shape[1] // c

    @pl.when(pl.program_id(1) == 0)
    def _():
        s_ref[...] = jnp.zeros_like(s_ref)

    row = lax.broadcasted_iota(jnp.int32, (c, LANES), 0)
    col = lax.broadcasted_iota(jnp.int32, (c, LANES), 1)
    if direction == 0:
        incl, strict, last = col <= row, col < row, c - 1
    else:
        incl, strict, last = (col >= row) & (col < c), (col > row) & (col < c), 0
    tri = incl[:, :c].astype(BF16)
    eye = (row == col).astype(F32)
    zc = jnp.zeros((c, LANES), BF16)
    zd = jnp.zeros((dh, dh), BF16)

    def lanes2(xa, xb):
        return jnp.concatenate([xa, xb], axis=1)

    def blockdiag_slots(pa, pb):
        return jnp.concatenate([lanes2(pa, zc), lanes2(zc, zc), lanes2(zc, pb), lanes2(zc, zc)], axis=0)

    heads = range(GDN_HEADS)
    hsl = [slice(h * dh, (h + 1) * dh) for h in heads]
    lns = [2 * GDN_HEADS + direction * GDN_HEADS + h for h in heads]

    rows = [slice(i * c, (i + 1) * c) for i in range(nsub)]
    gcum, gcum_t, beta_all = [], [], []
    for rs in rows:
        zs = zs_ref[0, rs, :]
        beta_all.append(jax.nn.sigmoid(zs))
        gl = -jnp.exp(al_ref[...]) * jax.nn.softplus(zs + dt_ref[...])
        g1, g2, g3 = _split3(gl)
        gsum = _dot(tri, g1) + _dot(tri, g2) + _dot(tri, g3)
        gcum.append(gsum)
        gcum_t.append(jnp.concatenate([gsum, jnp.zeros_like(gsum)], axis=0).T)
    items = [(i, h) for i in range(nsub) for h in heads]
    it = range(len(items))
    beta = [beta_all[i][:, direction * GDN_HEADS + h:direction * GDN_HEADS + h + 1] for i, h in items]
    gc = [gcum[i][:, lns[h]:lns[h] + 1] for i, h in items]
    gct = [gcum_t[i][lns[h]:lns[h] + 1, :] for i, h in items]
    glast = [gcum_t[i][lns[h]:lns[h] + 1, last:last + 1] for i, h in items]
    qh = [q_ref[0, rows[i], hsl[h]] for i, h in items]
    kh = [k_ref[0, rows[i], hsl[h]] for i, h in items]
    vh = [v_ref[0, rows[i], hsl[h]] for i, h in items]
    pr = range(len(items) // 2)
    kq2 = [_dot_nt(lanes2(jnp.concatenate([kh[2 * y], qh[2 * y]], axis=0),
                          jnp.concatenate([kh[2 * y + 1], qh[2 * y + 1]], axis=0)),
                   jnp.concatenate([lanes2(kh[2 * y], zd[:c]), lanes2(zc, zc),
                                    lanes2(zd[:c], kh[2 * y + 1]), lanes2(zc, zc)], axis=0)) for y in pr]
    kq = [kq2[x // 2][:, (x % 2) * LANES:(x % 2 + 1) * LANES] for x in it]
    dmat = [jnp.exp(jnp.where(incl, gc[x] - gct[x], -jnp.inf)) for x in it]
    a = [jnp.where(strict, (beta[x] * kq[x][:c]) * dmat[x], 0.0) for x in it]
    tinv2 = [lanes2(eye - a[2 * y], eye - a[2 * y + 1]) for y in pr]
    ab = [a[x].astype(BF16) for x in it]
    p2 = [_dot(lanes2(ab[2 * y], ab[2 * y + 1]), blockdiag_slots(ab[2 * y], ab[2 * y + 1])) for y in pr]
    for j in range(5):
        pb2 = [p2[y].astype(BF16) for y in pr]
        bd = [blockdiag_slots(pb2[y][:, :LANES], pb2[y][:, LANES:]) for y in pr]
        if j < 4:
            tp = [_dot(jnp.concatenate([tinv2[y].astype(BF16), pb2[y]], axis=0), bd[y]) for y in pr]
            tinv2 = [tinv2[y] + tp[y][:c] for y in pr]
            p2 = [tp[y][c:] for y in pr]
        else:
            tinv2 = [tinv2[y] + _dot(tinv2[y].astype(BF16), bd[y]) for y in pr]
    tinv = [tinv2[x // 2][:, (x % 2) * LANES:(x % 2 + 1) * LANES].astype(BF16) for x in it]
    eg = [jnp.exp(gc[x]) for x in it]
    kf = [kh[x].astype(F32) for x in it]
    rhs = [jnp.concatenate([vh[x].astype(F32) * beta[x], kf[x] * (beta[x] * eg[x])], axis=1).astype(BF16) for x in it]
    zr = jnp.zeros((LANES - c, 2 * dh), BF16)
    uw = [_dot(tinv[x], jnp.concatenate([rhs[x], zr], axis=0)) for x in it]
    wq = [jnp.concatenate([uw[x][:, dh:], qh[x].astype(F32) * eg[x]], axis=0).astype(BF16) for x in it]
    qkd = [(kq[x][c:] * dmat[x]).astype(BF16) for x in it]
    kd = [(kf[x] * jnp.exp(glast[x] - gc[x])).astype(BF16) for x in it]

    order = list(range(nsub)) if direction == 0 else list(range(nsub - 1, -1, -1))
    state = [s_ref[h] for h in heads]
    hp = range(GDN_HEADS // 2)
    for i in order:
        rs = rows[i]
        xs = [i * GDN_HEADS + h for h in heads]
        sb = [state[h].astype(BF16) for h in heads]
        ws2 = [_dot(lanes2(wq[xs[2 * y]], wq[xs[2 * y + 1]]),
                    jnp.concatenate([lanes2(sb[2 * y], zd), lanes2(zd, sb[2 * y + 1])], axis=0)) for y in hp]
        ws = [ws2[h // 2][:, (h % 2) * dh:(h % 2 + 1) * dh] for h in heads]
        vnb = [(uw[xs[h]][:, :dh] - ws[h][:c]).astype(BF16) for h in heads]
        o2 = [_dot(lanes2(qkd[xs[2 * y]], qkd[xs[2 * y + 1]]),
                   jnp.concatenate([lanes2(vnb[2 * y], zd[:c]), lanes2(zc, zc),
                                    lanes2(zd[:c], vnb[2 * y + 1]), lanes2(zc, zc)], axis=0)) for y in hp]
        o = [ws[h][c:] + o2[h // 2][:, (h % 2) * dh:(h % 2 + 1) * dh] for h in heads]
        kv2 = [_dot_tn(jnp.concatenate([kd[xs[2 * y]], kd[xs[2 * y + 1]]], axis=0),
                       jnp.concatenate([lanes2(vnb[2 * y], zd[:c]), lanes2(zd[:c], vnb[2 * y + 1])], axis=0)) for y in hp]
        state = [state[h] * jnp.exp(glast[xs[h]]) + kv2[h // 2][:, (h % 2) * dh:(h % 2 + 1) * dh] for h in heads]
        for h in heads:
            if final:
                tot = of_ref[0, rs, hsl[h]] + o[h]
                y = tot * lax.rsqrt(jnp.mean(tot * tot, axis=-1, keepdims=True) + EPS) * ng_ref[...]
                o_ref[0, rs, hsl[h]] = (y * _silu(gg_ref[0, rs, hsl[h]].astype(F32))).astype(o_ref.dtype)
            else:
                o_ref[0, rs, hsl[h]] = o[h].astype(o_ref.dtype)
    for h in heads:
        s_ref[h] = state[h]


def _gdn_scan(q, k, v, zs, a_log, dt_bias, direction, o_fwd=None, z=None, gate_blk=None, norm_g=None):
    b, l, w = q.shape
    c = GDN_CHUNK * GDN_CHUNKS_PER_STEP
    n = l // c
    final = o_fwd is not None
    if direction == 0:
        cmap = lambda i, t: (i, t, 0)
    else:
        cmap = lambda i, t: (i, n - 1 - t, 0)
    blk = pl.BlockSpec((1, c, w), cmap)
    vec = pl.BlockSpec((1, LANES), lambda i, t: (0, 0))
    pad = lambda x: jnp.pad(x.reshape(1, -1).astype(F32), ((0, 0), (2 * GDN_HEADS, LANES - 4 * GDN_HEADS)))
    in_specs = [blk, blk, blk, pl.BlockSpec((1, c, LANES), cmap), vec, vec]
    args = [q, k, v, zs, pad(a_log), pad(dt_bias)]
    if final:
        if direction == 0:
            gmap = lambda i, t: (i, t, gate_blk)
        else:
            gmap = lambda i, t: (i, n - 1 - t, gate_blk)
        in_specs += [blk, pl.BlockSpec((1, c, w), gmap), pl.BlockSpec((1, GDN_HEAD_DIM), lambda i, t: (0, 0))]
        args += [o_fwd, z, norm_g.reshape(1, GDN_HEAD_DIM).astype(F32)]
    return pl.pallas_call(
        functools.partial(_gdn_scan_kernel, direction=direction, final=final),
        grid=(b, n),
        in_specs=in_specs,
        out_specs=blk,
        out_shape=jax.ShapeDtypeStruct((b, l, w), BF16 if final else F32),
        scratch_shapes=[pltpu.VMEM((GDN_HEADS, GDN_HEAD_DIM, GDN_HEAD_DIM), F32)],
        compiler_params=_cparams("parallel", "arbitrary"),
        name="gdn_scan_bwd" if direction else "gdn_scan_fwd",
    )(*args)


def _gdn_branch(z, zs, conv_w, a_log, dt_bias, norm_g, gate_blk):
    q, k, v = _gdn_prep(z, conv_w)
    o_f = _gdn_scan(q, k, v, zs, a_log, dt_bias, 0)
    return _gdn_scan(q, k, v, zs, a_log, dt_bias, 1, o_fwd=o_f, z=z, gate_blk=gate_blk, norm_g=norm_g)


def _softmax_t(s, extra=None):
    m = jnp.max(s, axis=0, keepdims=True)
    if extra is not None:
        m = jnp.maximum(m, extra)
    p = jnp.exp2(s - m)
    den = jnp.sum(p, axis=0, keepdims=True)
    if extra is not None:
        den = den + jnp.exp2(extra - m)
    return p.astype(BF16), den, m


def _t_bf16(x):
    return x.astype(F32).T.astype(BF16)


def _xattn_kernel(q_ref, g_ref, kv_ref, o_ref):
    dh = X_HEAD_DIM
    for h in range(X_HEADS):
        hs = slice(h * dh, (h + 1) * dh)
        qh = q_ref[0, :, hs].astype(BF16)
        kh = kv_ref[0, :, hs]
        vh = kv_ref[0, :, X_WIDTH + h * dh:X_WIDTH + (h + 1) * dh]
        s = _dot_nt(qh, kh) * (dh ** -0.5)
        p = jnp.exp(s - jnp.max(s, axis=-1, keepdims=True))
        den = jnp.sum(p, axis=-1, keepdims=True)
        o = _dot(p.astype(BF16), vh) / den
        o_ref[0, :, hs] = (o * _silu(g_ref[0, :, hs].astype(F32))).astype(o_ref.dtype)


def _xattn(z, kv, q_blk, g_blk, tq=512):
    b, l, _ = z.shape
    m = kv.shape[1]
    w = X_WIDTH
    return pl.pallas_call(
        _xattn_kernel,
        grid=(b, l // tq),
        in_specs=[
            pl.BlockSpec((1, tq, w), lambda i, t: (i, t, q_blk)),
            pl.BlockSpec((1, tq, w), lambda i, t: (i, t, g_blk)),
            pl.BlockSpec((1, m, 2 * w), lambda i, t: (i, 0, 0)),
        ],
        out_specs=pl.BlockSpec((1, tq, w), lambda i, t: (i, t, 0)),
        out_shape=jax.ShapeDtypeStruct((b, l, w), BF16),
        compiler_params=_cparams("parallel", "arbitrary"),
        name="xattn",
    )(z, z, kv)


def _out_proj_kernel(*refs, widths, gated):
    nparts = len(widths)
    y_refs = refs[:nparts]
    if gated:
        e_ref, w_ref, g_ref, x_ref, o_ref = refs[nparts:]
    else:
        w_ref, g_ref, x_ref, o_ref = refs[nparts:]
    d = o_ref.shape[1]
    cw = 256
    ys = [y_ref[...] for y_ref in y_refs]
    if gated:
        ys[0] = ys[0] * e_ref[...]
    ys = [y.astype(BF16) for y in ys]
    ssq = jnp.zeros((o_ref.shape[0], 1), F32)
    for c0 in range(0, d, cw):
        acc = None
        off = 0
        for y, wd in zip(ys, widths):
            part = _dot(y, w_ref[off:off + wd, c0:c0 + cw])
            acc = part if acc is None else acc + part
            off += wd
        ssq = ssq + jnp.sum(acc * acc, axis=-1, keepdims=True)
        o_ref[:, c0:c0 + cw] = acc
    r = lax.rsqrt(ssq * (1.0 / d) + EPS)
    for c0 in range(0, d, cw):
        o_ref[:, c0:c0 + cw] = x_ref[:, c0:c0 + cw] + o_ref[:, c0:c0 + cw] * r * g_ref[:, c0:c0 + cw]


def _out_proj(parts, w_out, post_g, x2d, gate=None, tm=512):
    t, d = x2d.shape
    widths = tuple(int(p.shape[1]) for p in parts)
    kdim = sum(widths)
    gates = [] if gate is None else [gate]
    return pl.pallas_call(
        functools.partial(_out_proj_kernel, widths=widths, gated=gate is not None),
        grid=(t // tm,),
        in_specs=[pl.BlockSpec((tm, wd), lambda i: (i, 0)) for wd in widths]
        + [pl.BlockSpec((tm, widths[0]), lambda i: (i, 0)) for _ in gates] + [
            pl.BlockSpec((kdim, d), lambda i: (0, 0)),
            pl.BlockSpec((1, d), lambda i: (0, 0)),
            pl.BlockSpec((tm, d), lambda i: (i, 0)),
        ],
        out_specs=pl.BlockSpec((tm, d), lambda i: (i, 0)),
        out_shape=jax.ShapeDtypeStruct((t, d), F32),
        compiler_params=_cparams("parallel"),
        name="out_proj",
    )(*parts, *gates, w_out.astype(BF16), post_g.reshape(1, d).astype(F32), x2d)


def _mem_kv(mem, mem_g, w_mem_kv):
    b, m, d = mem.shape
    kv = _norm_matmul(mem.reshape(b * m, d), mem_g, w_mem_kv, 1024, out_dtype=BF16, name="mem_kv")
    return kv.reshape(b, m, 2 * X_WIDTH)


E_QKV, E_HY, E_GHY, E_GGDN, E_XQ, E_GX, E_BIG = 0, 3072, 6144, 7168, 8192, 8704, 9216


def _even_layer(x, mem, pre_g, post_g, w_in, w_out, hy_conv_w, hy_conv_b, hy_w1, hy_b1, hy_w2, hy_b2, hy_w3, hy_freq,
                hy_skip, gdn_conv_w, gdn_a_log, gdn_dt_bias, gdn_norm_g, mem_g, w_mem_kv):
    b, l, d = x.shape
    x2d = x.reshape(b * l, d)
    w_big = jnp.concatenate([w_in[:, 4096:7168], w_in[:, 0:3072], w_in[:, 3072:4096], w_in[:, 7168:8192],
                             w_in[:, 8224:8736], w_in[:, 8736:9248]], axis=1)
    w_small = jnp.pad(w_in[:, 8192:8224], ((0, 0), (0, LANES - 4 * GDN_HEADS)))
    z, zs = _norm_matmul(x2d, pre_g, w_big, E_BIG // 3, out_dtype=BF16, tm=1024, name="even_in", w_side=w_small)
    z = z.reshape(b, l, E_BIG)
    zs = zs.reshape(b, l, LANES)
    conv, e = _hyena_branch(z, l, hy_conv_w, hy_conv_b, hy_w1, hy_b1, hy_w2, hy_b2, hy_w3, hy_freq, hy_skip)
    y_b = _gdn_branch(z, zs, gdn_conv_w, gdn_a_log, gdn_dt_bias, gdn_norm_g, E_GGDN // GDN_WIDTH)
    kv = _mem_kv(mem, mem_g, w_mem_kv)
    y_x = _xattn(z, kv, E_XQ // X_WIDTH, E_GX // X_WIDTH)
    t = b * l
    out = _out_proj([conv.reshape(t, -1), y_b.reshape(t, -1), y_x.reshape(t, -1)], w_out, post_g, x2d,
                    gate=e.reshape(t, -1))
    return out.reshape(b, l, d)


DIL_PATTERNS = ((128, 1), (512, 4), (2048, 16))
N_DIL = len(DIL_PATTERNS)
DIL_HEADS = 4
DIL_HEAD_DIM = 128
DIL_WIDTH = DIL_HEADS * DIL_HEAD_DIM
SWA_Q_HEADS = 16
SWA_KV_HEADS = 2
SWA_HEAD_DIM = 64
SWA_WIDTH = SWA_Q_HEADS * SWA_HEAD_DIM
SWA_HALF_WINDOW = 128
O_CQKV, O_GC, O_DQ, O_GD, O_XQ, O_GX, O_DKV, O_ALL = 0, 4608, 5120, 6144, 7168, 7680, 8192, 8448


def _rope_tables(l, dh):
    half = dh // 8
    inv = ROPE_THETA ** (-jnp.arange(half, dtype=F32) / half)
    ang = jnp.arange(l, dtype=F32)[:, None] * inv[None, :]
    cos, sin = jnp.cos(ang), jnp.sin(ang)
    one = jnp.ones((l, dh - 2 * half), F32)
    zero_h = jnp.zeros((l, half), F32)
    zero_r = jnp.zeros((l, dh - 2 * half), F32)
    c = jnp.concatenate([cos, cos, one], axis=1)
    sa = jnp.concatenate([-sin, zero_h, zero_r], axis=1)
    sb = jnp.concatenate([zero_h, sin, zero_r], axis=1)
    rep = LANES // dh
    return tuple(jnp.tile(t, (1, rep)) for t in (c, sa, sb))


def _odd_prep_kernel(c_ref, dq_ref, dkv_ref, c1_ref, a1_ref, b1_ref, c2_ref, a2_ref, b2_ref, *refs):
    dil_refs = refs[:3 * N_DIL]
    dqe_ref, dk_ref, dv_ref, xs_ref = refs[3 * N_DIL:]
    tl = c_ref.shape[1]
    lane = lax.broadcasted_iota(jnp.int32, (tl, LANES), 1)
    low = lane < SWA_HEAD_DIM
    c1, a1, b1 = c1_ref[...], a1_ref[...], b1_ref[...]
    c2, a2, b2 = c2_ref[...], a2_ref[...], b2_ref[...]
    h1 = DIL_HEAD_DIM // 8
    h2 = SWA_HEAD_DIM // 8

    def rope1(x):
        return x * c1 + pltpu.roll(x, LANES - h1, 1) * a1 + pltpu.roll(x, h1, 1) * b1

    def rope2(x):
        return x * c2 + pltpu.roll(x, LANES - h2, 1) * a2 + pltpu.roll(x, h2, 1) * b2

    nblk = N_DIL * DIL_HEADS
    slot = 0
    for s in range(3):
        for gi, (_, d) in enumerate(DIL_PATTERNS):
            dst = dil_refs[3 * gi + s]
            for h in range(DIL_HEADS):
                j = s * nblk + gi * DIL_HEADS + h
                x = c_ref[0, :, j * LANES:(j + 1) * LANES]
                if s < 2:
                    x = rope1(x.astype(F32))
                if s == 0:
                    x = x * (DIL_HEAD_DIM ** -0.5 * LOG2E)
                hs = slice(h * LANES, (h + 1) * LANES)
                if d == 1:
                    dst[0, 0, :, hs] = x.astype(BF16)
                else:
                    xs_ref[slot] = x.astype(F32)
                    for r in range(d):
                        dst[0, r, :, hs] = xs_ref[slot, pl.ds(r, tl // d, stride=d), :].astype(BF16)
                    slot += 1
    for j in range(SWA_Q_HEADS // 2):
        xr = rope2(dq_ref[0, :, j * LANES:(j + 1) * LANES].astype(F32)) * (SWA_HEAD_DIM ** -0.5 * LOG2E)
        dqe_ref[0, :, (2 * j) * LANES:(2 * j + 1) * LANES] = jnp.where(low, xr, 0.0).astype(BF16)
        dqe_ref[0, :, (2 * j + 1) * LANES:(2 * j + 2) * LANES] = jnp.where(low, pltpu.roll(xr, SWA_HEAD_DIM, 1), 0.0).astype(BF16)
    kr = rope2(dkv_ref[0, :, 0:LANES].astype(F32))
    vv = dkv_ref[0, :, LANES:2 * LANES].astype(F32)
    kr_sw = pltpu.roll(kr, SWA_HEAD_DIM, 1)
    vv_sw = pltpu.roll(vv, SWA_HEAD_DIM, 1)
    dk_ref[0, :, 0:LANES] = jnp.where(low, kr, 0.0).astype(BF16)
    dk_ref[0, :, LANES:2 * LANES] = jnp.where(low, kr_sw, 0.0).astype(BF16)
    dv_ref[0, :, 0:LANES] = jnp.where(low, vv, vv_sw).astype(BF16)
    dv_ref[0, :, LANES:2 * LANES] = jnp.where(low, vv_sw, vv).astype(BF16)


def _odd_prep(z, tl=512):
    b, l, _ = z.shape
    wc = 3 * N_DIL * DIL_WIDTH
    t1 = _rope_tables(l, DIL_HEAD_DIM)
    t2 = _rope_tables(l, SWA_HEAD_DIM)
    tab = pl.BlockSpec((tl, LANES), lambda i, t: (t, 0))

    def spec(w):
        return pl.BlockSpec((1, tl, w), lambda i, t: (i, t, 0))

    def shape(w):
        return jax.ShapeDtypeStruct((b, l, w), BF16)

    dil_specs, dil_shapes = [], []
    for _, d in DIL_PATTERNS:
        for _ in range(3):
            dil_specs.append(pl.BlockSpec((1, d, tl // d, DIL_WIDTH), lambda i, t: (i, 0, t, 0)))
            dil_shapes.append(jax.ShapeDtypeStruct((b, d, l // d, DIL_WIDTH), BF16))
    n_staged = 3 * DIL_HEADS * sum(1 for _, d in DIL_PATTERNS if d > 1)
    outs = pl.pallas_call(
        _odd_prep_kernel,
        grid=(b, l // tl),
        in_specs=[
            pl.BlockSpec((1, tl, wc), lambda i, t: (i, t, 0)),
            pl.BlockSpec((1, tl, SWA_WIDTH), lambda i, t: (i, t, O_DQ // SWA_WIDTH)),
            pl.BlockSpec((1, tl, 2 * LANES), lambda i, t: (i, t, O_DKV // (2 * LANES))),
            tab, tab, tab, tab, tab, tab,
        ],
        out_specs=dil_specs + [spec(2 * SWA_WIDTH), spec(2 * LANES), spec(2 * LANES)],
        out_shape=dil_shapes + [shape(2 * SWA_WIDTH), shape(2 * LANES), shape(2 * LANES)],
        scratch_shapes=[pltpu.VMEM((n_staged, tl, LANES), F32)],
        compiler_params=_cparams("parallel", "arbitrary"),
        name="odd_prep",
    )(z, z, z, *t1, *t2)
    dil = [tuple(outs[3 * gi:3 * gi + 3]) for gi in range(N_DIL)]
    return dil, outs[3 * N_DIL], outs[3 * N_DIL + 1], outs[3 * N_DIL + 2]


def _band_bias_t(tq, half, ls):
    p0 = pl.program_id(1) * tq
    j = lax.broadcasted_iota(jnp.int32, (tq + 2 * half, tq), 0)
    i = lax.broadcasted_iota(jnp.int32, (tq + 2 * half, tq), 1)
    kpos = p0 - half + j
    valid = (j >= i) & (j - i <= 2 * half) & (kpos >= 0) & (kpos < ls)
    return jnp.where(valid, 0.0, -jnp.inf)


def _band_mask(tq, half, ls):
    p0 = pl.program_id(1) * tq
    i = lax.broadcasted_iota(jnp.int32, (tq, tq + 2 * half), 0)
    j = lax.broadcasted_iota(jnp.int32, (tq, tq + 2 * half), 1)
    kpos = p0 - half + j
    return (j >= i) & (j - i <= 2 * half) & (kpos >= 0) & (kpos < ls)


def _band_attn_kernel(q_ref, kp_ref, kc_ref, kn_ref, vp_ref, vc_ref, vn_ref, o_ref, lse_ref, *, half, ls):
    tq = q_ref.shape[1]
    dh = DIL_HEAD_DIM
    valid = _band_mask(tq, half, ls)
    lane = lax.broadcasted_iota(jnp.int32, (tq, LANES), 1)
    lse_all = jnp.zeros((tq, LANES), F32)
    heads = range(DIL_HEADS)
    hsl = [slice(h * dh, (h + 1) * dh) for h in heads]
    kall = [jnp.concatenate([kp_ref[0, :, s], kc_ref[0, :, s], kn_ref[0, :, s]], axis=0) for s in hsl]
    ones = jnp.ones((tq + 2 * half, dh), BF16)
    vall = [jnp.concatenate([jnp.concatenate([vp_ref[0, :, s], vc_ref[0, :, s], vn_ref[0, :, s]], axis=0), ones], axis=1)
            for s in hsl]
    sc = [jnp.where(valid, _dot_nt(q_ref[0, :, hsl[h]], kall[h]), -jnp.inf) for h in heads]
    m = [jnp.max(sc[h], axis=-1, keepdims=True) for h in heads]
    p = [jnp.exp2(sc[h] - m[h]).astype(BF16) for h in heads]
    od = [_dot(p[h], vall[h]) for h in heads]
    for h in heads:
        den = od[h][:, dh:]
        o_ref[0, :, hsl[h]] = (od[h][:, :dh] / den).astype(o_ref.dtype)
        lse_all = jnp.where(lane == h, m[h] + jnp.log2(den), lse_all)
    lse_ref[0] = lse_all


def _band_specs(tq, half, ls, w, col):
    r = tq // half
    nb = ls // half
    cur = pl.BlockSpec((1, tq, w), lambda i, t: (i, t, col))
    prev = pl.BlockSpec((1, half, w), lambda i, t: (i, jnp.maximum(t * r - 1, 0), col))
    nxt = pl.BlockSpec((1, half, w), lambda i, t: (i, jnp.minimum((t + 1) * r, nb - 1), col))
    return cur, prev, nxt


def _band_attn(q, k, v, half):
    n, ls, w = q.shape
    tq = min(2 * LANES, ls)
    cur, prev, nxt = _band_specs(tq, half, ls, w, 0)
    return pl.pallas_call(
        functools.partial(_band_attn_kernel, half=half, ls=ls),
        grid=(n, ls // tq),
        in_specs=[cur, prev, cur, nxt, prev, cur, nxt],
        out_specs=[cur, pl.BlockSpec((1, tq, LANES), lambda i, t: (i, t, 0))],
        out_shape=[jax.ShapeDtypeStruct((n, ls, w), BF16), jax.ShapeDtypeStruct((n, ls, LANES), F32)],
        compiler_params=_cparams("parallel", "arbitrary"),
        name="band_attn",
    )(q, k, k, k, v, v, v)


def _dil_merge_kernel(o0_ref, o1_ref, o2_ref, l0_ref, l1_ref, l2_ref, g_ref, y_ref, os_ref, ls_ref):
    dh = DIL_HEAD_DIM
    tl = y_ref.shape[1]
    o_refs = (o0_ref, o1_ref, o2_ref)
    l_refs = (l0_ref, l1_ref, l2_ref)
    slot = 0
    lse, outs = [], []
    for gi, (_, d) in enumerate(DIL_PATTERNS):
        if d == 1:
            lse.append(l_refs[gi][0, 0])
            outs.append([o_refs[gi][0, 0, :, h * dh:(h + 1) * dh].astype(F32) for h in range(DIL_HEADS)])
            continue
        for r in range(d):
            ls_ref[gi, pl.ds(r, tl // d, stride=d), :] = l_refs[gi][0, r]
        lse.append(ls_ref[gi])
        per_head = []
        for h in range(DIL_HEADS):
            for r in range(d):
                os_ref[slot, pl.ds(r, tl // d, stride=d), :] = o_refs[gi][0, r, :, h * dh:(h + 1) * dh].astype(F32)
            per_head.append(os_ref[slot])
            slot += 1
        outs.append(per_head)
    for h in range(DIL_HEADS):
        hs = slice(h * dh, (h + 1) * dh)
        ls = [x[:, h:h + 1] for x in lse]
        m = jnp.maximum(jnp.maximum(ls[0], ls[1]), ls[2])
        ws = [jnp.exp2(x - m) for x in ls]
        den = ws[0] + ws[1] + ws[2]
        y = (ws[0] / den) * outs[0][h] + (ws[1] / den) * outs[1][h] + (ws[2] / den) * outs[2][h]
        y_ref[0, :, hs] = (y * _silu(g_ref[0, :, hs].astype(F32))).astype(y_ref.dtype)


def _dil_merge(outs, lses, z, tl=512):
    b, _, l, w = outs[0].shape
    o_specs = [pl.BlockSpec((1, d, tl // d, w), lambda i, t: (i, 0, t, 0)) for _, d in DIL_PATTERNS]
    l_specs = [pl.BlockSpec((1, d, tl // d, LANES), lambda i, t: (i, 0, t, 0)) for _, d in DIL_PATTERNS]
    n_staged = DIL_HEADS * sum(1 for _, d in DIL_PATTERNS if d > 1)
    return pl.pallas_call(
        _dil_merge_kernel,
        grid=(b, l // tl),
        in_specs=o_specs + l_specs + [pl.BlockSpec((1, tl, w), lambda i, t: (i, t, O_GC // DIL_WIDTH))],
        out_specs=pl.BlockSpec((1, tl, w), lambda i, t: (i, t, 0)),
        out_shape=jax.ShapeDtypeStruct((b, l, w), BF16),
        scratch_shapes=[pltpu.VMEM((n_staged, tl, LANES), F32), pltpu.VMEM((N_DIL, tl, LANES), F32)],
        compiler_params=_cparams("parallel", "arbitrary"),
        name="dil_merge",
    )(*outs, *lses, z)


def _dilated_branch(dil, z):
    outs, lses = [], []
    for (window, d), (q, k, v) in zip(DIL_PATTERNS, dil):
        b, _, ls, w = q.shape
        half = window // (2 * d)
        o, lse = _band_attn(q.reshape(b * d, ls, w), k.reshape(b * d, ls, w), v.reshape(b * d, ls, w), half)
        outs.append(o.reshape(b, d, ls, w))
        lses.append(lse.reshape(b, d, ls, LANES))
    return _dil_merge(outs, lses, z)


def _swa_kernel(q_ref, kp_ref, kc_ref, kn_ref, vp_ref, vc_ref, vn_ref, g_ref, sink_ref, o_ref, *, half, ls):
    tq = q_ref.shape[1]
    dh = SWA_HEAD_DIM
    bias1 = _band_bias_t(tq, half, ls)
    bias = jnp.concatenate([bias1, bias1], axis=1)
    b_prev, b_next = bias[:half], bias[half + tq:]
    low = lax.broadcasted_iota(jnp.int32, (LANES, tq), 0) < dh
    first = lax.broadcasted_iota(jnp.int32, (1, 2 * tq), 1) < tq
    grp = SWA_Q_HEADS // SWA_KV_HEADS
    for g in range(SWA_KV_HEADS):
        gs = slice(g * LANES, (g + 1) * LANES)
        kall = jnp.concatenate([kp_ref[0, :, gs], kc_ref[0, :, gs], kn_ref[0, :, gs]], axis=0)
        vt = _t_bf16(jnp.concatenate([vp_ref[0, :, gs], vc_ref[0, :, gs], vn_ref[0, :, gs]], axis=0))
        pairs = range(grp // 2)
        h0 = [g * grp + 2 * jp for jp in pairs]
        q2 = [q_ref[0, :, h * LANES:(h + 2) * LANES] for h in h0]
        q2 = [jnp.concatenate([x[:, :LANES], x[:, LANES:]], axis=0) for x in q2]
        sc = [_dot_nt(kall, q2[jp]) for jp in pairs]
        sc = [jnp.concatenate([s[:half] + b_prev, s[half:half + tq], s[half + tq:] + b_next], axis=0) for s in sc]
        sk = [jnp.where(first, sink_ref[:, h:h + 1], sink_ref[:, h + 1:h + 2]) * LOG2E for h in h0]
        sm = [_softmax_t(sc[jp], sk[jp]) for jp in pairs]
        ot = [_dot(vt, sm[jp][0]) / sm[jp][1] for jp in pairs]
        for jp in pairs:
            blk = (g * grp) // 2 + jp
            bs = slice(blk * LANES, (blk + 1) * LANES)
            y = jnp.where(low, ot[jp][:, :tq], ot[jp][:, tq:]).T * _silu(g_ref[0, :, bs].astype(F32))
            o_ref[0, :, bs] = y.astype(o_ref.dtype)


def _swa_branch(dqe, dk, dv, z, sink, tq=128):
    b, l, _ = dqe.shape
    half = SWA_HALF_WINDOW
    assert tq == half == LANES
    _, prev, nxt = _band_specs(tq, half, l, 2 * LANES, 0)
    cur = pl.BlockSpec((1, tq, 2 * LANES), lambda i, t: (i, t, 0))
    sink_p = jnp.pad(sink.reshape(1, SWA_Q_HEADS).astype(F32), ((0, 0), (0, LANES - SWA_Q_HEADS)))
    return pl.pallas_call(
        functools.partial(_swa_kernel, half=half, ls=l),
        grid=(b, l // tq),
        in_specs=[
            pl.BlockSpec((1, tq, 2 * SWA_WIDTH), lambda i, t: (i, t, 0)),
            prev, cur, nxt, prev, cur, nxt,
            pl.BlockSpec((1, tq, SWA_WIDTH), lambda i, t: (i, t, O_GD // SWA_WIDTH)),
            pl.BlockSpec((1, LANES), lambda i, t: (0, 0)),
        ],
        out_specs=pl.BlockSpec((1, tq, SWA_WIDTH), lambda i, t: (i, t, 0)),
        out_shape=jax.ShapeDtypeStruct((b, l, SWA_WIDTH), BF16),
        compiler_params=_cparams("parallel", "arbitrary"),
        name="swa",
    )(dqe, dk, dk, dk, dv, dv, dv, z, sink_p)


def _odd_layer(x, mem, pre_g, post_g, w_in, w_out, swa_sink, mem_g, w_mem_kv):
    b, l, d = x.shape
    x2d = x.reshape(b * l, d)
    w_re = jnp.concatenate([w_in[:, 0:6144], w_in[:, 6400:8448], w_in[:, 6144:6400]], axis=1)
    z = _norm_matmul(x2d, pre_g, w_re, O_ALL // 3, out_dtype=BF16, tm=1024, name="odd_in").reshape(b, l, O_ALL)
    dil, dqe, dk, dv = _odd_prep(z)
    y_c = _dilated_branch(dil, z)
    y_d = _swa_branch(dqe, dk, dv, z, swa_sink)
    kv = _mem_kv(mem, mem_g, w_mem_kv)
    y_x = _xattn(z, kv, O_XQ // X_WIDTH, O_GX // X_WIDTH)
    t = b * l
    out = _out_proj([y_c.reshape(t, -1), y_d.reshape(t, -1), y_x.reshape(t, -1)], w_out, post_g, x2d)
    return out.reshape(b, l, d)


def _trunk(x, mem, even_params, odd_params):
    x = _even_layer(x, mem, *[p[0] for p in even_params])
    return _odd_layer(x, mem, *[p[0] for p in odd_params])


def kernel(x_prompt, x_sample, mem_prompt, mem_sample, e_pre_g, e_post_g, e_w_in, e_w_out, hy_conv_w, hy_conv_b,
           hy_filt_w1, hy_filt_b1, hy_filt_w2, hy_filt_b2, hy_filt_w3, hy_freq, hy_skip, gdn_conv_w, gdn_A_log,
           gdn_dt_bias, gdn_norm_g, e_mem_g, e_w_mem_kv, o_pre_g, o_post_g, o_w_in, o_w_out, swa_sink, o_mem_g,
           o_w_mem_kv):
    even_params = (e_pre_g, e_post_g, e_w_in, e_w_out, hy_conv_w, hy_conv_b, hy_filt_w1, hy_filt_b1, hy_filt_w2,
                   hy_filt_b2, hy_filt_w3, hy_freq, hy_skip, gdn_conv_w, gdn_A_log, gdn_dt_bias, gdn_norm_g,
                   e_mem_g, e_w_mem_kv)
    odd_params = (o_pre_g, o_post_g, o_w_in, o_w_out, swa_sink, o_mem_g, o_w_mem_kv)
    y_prompt = _trunk(x_prompt, mem_prompt, even_params, odd_params)
    y_sample = _trunk(x_sample, mem_sample, even_params, odd_params)
    return (y_prompt, y_sample)
```

```python
import functools
import math

import jax
import jax.numpy as jnp
import numpy as np
from jax import lax
from jax.experimental import pallas as pl
from jax.experimental.pallas import tpu as pltpu

D_MODEL = 1024
EPS = 1e-6
ROPE_THETA = 500000.0
HY_WIDTH = 1024
HY_EMB = 33
HY_BANDS = (HY_EMB - 1) // 2
HY_FILT_HIDDEN = 64
HY_TARGET = 1e-2
HY_FAST_PCT = 0.3
HY_SLOW_PCT = 1.5
GDN_HEADS = 8
GDN_HEAD_DIM = 128
GDN_WIDTH = GDN_HEADS * GDN_HEAD_DIM
GDN_CHUNK = 64
X_HEADS = 4
X_HEAD_DIM = 128
X_WIDTH = X_HEADS * X_HEAD_DIM

LOG2E = math.log2(math.e)
LANES = 128
FFT_N2 = 128
VMEM_LIMIT_BYTES = 48 * 1024 * 1024

BF16 = jnp.bfloat16
F32 = jnp.float32


def _cparams(*sem):
    return pltpu.CompilerParams(dimension_semantics=sem, vmem_limit_bytes=VMEM_LIMIT_BYTES)


def _dot(a, b):
    return jnp.dot(a, b, preferred_element_type=F32)


def _silu(x):
    hx = 0.5 * x
    return hx + hx * jnp.tanh(hx)


def _norm_matmul_kernel(*refs, side):
    if side:
        x_ref, g_ref, w_ref, ws_ref, o_ref, os_ref, xn_ref = refs
    else:
        x_ref, g_ref, w_ref, o_ref, xn_ref = refs

    @pl.when(pl.program_id(1) == 0)
    def _():
        x = x_ref[...]
        ms = jnp.mean(x * x, axis=-1, keepdims=True)
        xn_ref[...] = (x * lax.rsqrt(ms + EPS) * g_ref[...]).astype(BF16)
        if side:
            os_ref[...] = _dot(xn_ref[...], ws_ref[...])

    o_ref[...] = _dot(xn_ref[...], w_ref[...]).astype(o_ref.dtype)


def _norm_matmul(x2d, g, w, tn, out_dtype=F32, tm=512, name="norm_matmul", w_side=None):
    t, d = x2d.shape
    n = w.shape[1]
    assert t % tm == 0 and n % tn == 0
    side = w_side is not None
    in_specs = [
        pl.BlockSpec((tm, d), lambda i, j: (i, 0)),
        pl.BlockSpec((1, d), lambda i, j: (0, 0)),
        pl.BlockSpec((d, tn), lambda i, j: (0, j)),
    ]
    out_specs = [pl.BlockSpec((tm, tn), lambda i, j: (i, j))]
    out_shape = [jax.ShapeDtypeStruct((t, n), out_dtype)]
    args = [x2d, g.reshape(1, d).astype(F32), w.astype(BF16)]
    if side:
        in_specs.append(pl.BlockSpec((d, LANES), lambda i, j: (0, 0)))
        out_specs.append(pl.BlockSpec((tm, LANES), lambda i, j: (i, 0)))
        out_shape.append(jax.ShapeDtypeStruct((t, LANES), F32))
        args.append(w_side.astype(BF16))
    outs = pl.pallas_call(
        functools.partial(_norm_matmul_kernel, side=side),
        grid=(t // tm, n // tn),
        in_specs=in_specs,
        out_specs=out_specs,
        out_shape=out_shape,
        scratch_shapes=[pltpu.VMEM((tm, d), BF16)],
        compiler_params=_cparams("parallel", "arbitrary"),
        name=name,
    )(*args)
    return tuple(outs) if side else outs[0]


HALO_ROWS = 16


TAP_ROWS = 128
TAP_WINDOW = 256


def _fill_halo_buf(buf_ref, zc_ref, zp_ref, zn_ref, t, nt):
    tl = zc_ref.shape[1]
    hr = HALO_ROWS
    buf_ref[0:hr, :] = jnp.where(t > 0, zp_ref[0].astype(F32), 0.0).astype(BF16)
    buf_ref[hr:hr + tl, :] = zc_ref[0].astype(BF16)
    buf_ref[hr + tl:2 * hr + tl, :] = jnp.where(t < nt - 1, zn_ref[0].astype(F32), 0.0).astype(BF16)
    buf_ref[2 * hr + tl:, :] = jnp.zeros((buf_ref.shape[0] - 2 * hr - tl, buf_ref.shape[1]), BF16)


def _halo_buf_rows(tl):
    return tl - TAP_ROWS + TAP_WINDOW


def _tap_shift_matrix(ntaps):
    n = lax.broadcasted_iota(jnp.int32, (ntaps, TAP_ROWS, TAP_WINDOW), 1)
    j = lax.broadcasted_iota(jnp.int32, (ntaps, TAP_ROWS, TAP_WINDOW), 0)
    m = lax.broadcasted_iota(jnp.int32, (ntaps, TAP_ROWS, TAP_WINDOW), 2)
    return (m == HALO_ROWS + n + j - ntaps // 2).astype(BF16).reshape(ntaps * TAP_ROWS, TAP_WINDOW)


def _taps(buf_ref, s_ref, r0, lanes):
    return _dot(s_ref[...], buf_ref[r0:r0 + TAP_WINDOW, lanes])


def _halo_specs(tl, w, l, col):
    r = tl // HALO_ROWS
    nb = l // HALO_ROWS
    cur = pl.BlockSpec((1, tl, w), lambda i, t: (i, t, col))
    prev = pl.BlockSpec((1, HALO_ROWS, w), lambda i, t: (i, jnp.maximum(t * r - 1, 0), col))
    nxt = pl.BlockSpec((1, HALO_ROWS, w), lambda i, t: (i, jnp.minimum((t + 1) * r, nb - 1), col))
    return cur, prev, nxt


def _hy_prep_kernel(zc_ref, zp_ref, zn_ref, g_ref, cw_ref, cb_ref, s_ref, u_ref, e_ref, buf_ref):
    t = pl.program_id(1)
    nt = pl.num_programs(1)
    tl = zc_ref.shape[1]
    c = HY_WIDTH
    tr = TAP_ROWS
    _fill_halo_buf(buf_ref, zc_ref, zp_ref, zn_ref, t, nt)
    cw = 256
    for r0 in range(0, tl, tr):
        rs = slice(r0, r0 + tr)
        for c0 in range(0, c, cw):
            parts = []
            for p in range(3):
                lo = p * c + c0
                xs = _taps(buf_ref, s_ref, r0, slice(lo, lo + cw))
                acc = cb_ref[:, lo:lo + cw]
                for j in range(3):
                    acc = acc + xs[j * tr:(j + 1) * tr] * cw_ref[j:j + 1, lo:lo + cw]
                parts.append(acc)
            x0, x1, v = parts
            u_ref[0, rs, c0:c0 + cw] = (v * x1).astype(u_ref.dtype)
            e_ref[0, rs, c0:c0 + cw] = (x0 * _silu(g_ref[0, rs, c0:c0 + cw].astype(F32))).astype(e_ref.dtype)


def _hy_prep(z, conv_w, conv_b, tl=512):
    b, l, _ = z.shape
    c = HY_WIDTH
    assert l % tl == 0
    cur, prev, nxt = _halo_specs(tl, 3 * c, l, E_HY // (3 * c))
    return pl.pallas_call(
        _hy_prep_kernel,
        grid=(b, l // tl),
        in_specs=[
            cur, prev, nxt,
            pl.BlockSpec((1, tl, c), lambda i, t: (i, t, E_GHY // c)),
            pl.BlockSpec((3, 3 * c), lambda i, t: (0, 0)),
            pl.BlockSpec((1, 3 * c), lambda i, t: (0, 0)),
            pl.BlockSpec((3 * TAP_ROWS, TAP_WINDOW), lambda i, t: (0, 0)),
        ],
        out_specs=[
            pl.BlockSpec((1, tl, c), lambda i, t: (i, t, 0)),
            pl.BlockSpec((1, tl, c), lambda i, t: (i, t, 0)),
        ],
        out_shape=[jax.ShapeDtypeStruct((b, l, c), F32), jax.ShapeDtypeStruct((b, l, c), BF16)],
        scratch_shapes=[pltpu.VMEM((_halo_buf_rows(tl), 3 * c), BF16)],
        compiler_params=_cparams("parallel", "arbitrary"),
        name="hy_prep",
    )(z, z, z, z, conv_w.astype(F32), conv_b.reshape(1, 3 * c).astype(F32), _tap_shift_matrix(3))


def _hy_filter_tables(l):
    t = jnp.linspace(0.0, 1.0, l, dtype=F32)[:, None]
    w = (2.0 * math.pi / l) * jnp.arange(l, dtype=F32)[:, None]
    f = jnp.linspace(1e-4, HY_BANDS - 1, HY_BANDS, dtype=F32)[None, :]
    emb = jnp.concatenate([t, jnp.cos(f * w), -jnp.sin(f * w)], axis=-1)
    emb = jnp.pad(emb, ((0, 0), (0, LANES - HY_EMB)))
    deltas = jnp.abs(jnp.linspace(math.log(HY_TARGET) / HY_SLOW_PCT, math.log(HY_TARGET) / HY_FAST_PCT, HY_WIDTH, dtype=F32))
    decay = jnp.exp(-t * jnp.tile(deltas, 2)[None, :])
    return emb, decay


def _hy_filter_kernel(emb_ref, dec_ref, w1_ref, b1_ref, w2_ref, b2_ref, w3_ref, fr_ref, sk_ref, o_ref):
    c = HY_WIDTH
    hp = lax.Precision.HIGHEST
    fr = fr_ref[...]
    hid = jnp.sin(fr * (jnp.dot(emb_ref[...], w1_ref[...], precision=hp, preferred_element_type=F32) + b1_ref[...]))
    hid = jnp.sin(fr * (jnp.dot(hid, w2_ref[...], precision=hp, preferred_element_type=F32) + b2_ref[...]))
    tl = emb_ref.shape[0]
    row = lax.broadcasted_iota(jnp.int32, (tl, 1), 0) + pl.program_id(0) * tl
    first = row == 0
    cw = 512
    for c0 in range(0, 2 * c, cw):
        filt = jnp.dot(hid, w3_ref[:, c0:c0 + cw], precision=hp, preferred_element_type=F32) * dec_ref[:, c0:c0 + cw]
        if c0 < c:
            filt = jnp.where(first, filt + sk_ref[:, c0:c0 + cw], filt)
            o_ref[0, :, c0:c0 + cw] = filt.astype(o_ref.dtype)
        else:
            filt = jnp.where(first, 0.0, filt)
            o_ref[1, :, c0 - c:c0 - c + cw] = filt.astype(o_ref.dtype)


def _hy_filters(l, w1, b1, w2, b2, w3, freq, skip, tl=256):
    c = HY_WIDTH
    hdim = HY_FILT_HIDDEN
    emb, decay = _hy_filter_tables(l)
    w1p = jnp.pad(w1.astype(F32), ((0, LANES - HY_EMB), (0, 0)))
    return pl.pallas_call(
        _hy_filter_kernel,
        grid=(l // tl,),
        in_specs=[
            pl.BlockSpec((tl, LANES), lambda t: (t, 0)),
            pl.BlockSpec((tl, 2 * c), lambda t: (t, 0)),
            pl.BlockSpec((LANES, hdim), lambda t: (0, 0)),
            pl.BlockSpec((1, hdim), lambda t: (0, 0)),
            pl.BlockSpec((hdim, hdim), lambda t: (0, 0)),
            pl.BlockSpec((1, hdim), lambda t: (0, 0)),
            pl.BlockSpec((hdim, 2 * c), lambda t: (0, 0)),
            pl.BlockSpec((1, hdim), lambda t: (0, 0)),
            pl.BlockSpec((1, c), lambda t: (0, 0)),
        ],
        out_specs=pl.BlockSpec((2, tl, c), lambda t: (0, t, 0)),
        out_shape=jax.ShapeDtypeStruct((2, l, c), BF16),
        compiler_params=_cparams("arbitrary"),
        name="hy_filter",
    )(emb, decay, w1p, b1.reshape(1, hdim).astype(F32), w2.astype(F32), b2.reshape(1, hdim).astype(F32),
      w3.astype(F32), freq.reshape(1, hdim).astype(F32), skip.reshape(1, c).astype(F32))


def _fft_consts(l):
    n = 2 * l
    n2 = FFT_N2
    n1 = n // n2
    n1h = n1 // 2
    w0 = 2.0 * math.pi / n
    k1 = jnp.arange(n1, dtype=jnp.int32)
    m1 = jnp.arange(n1h, dtype=jnp.int32)
    m2 = jnp.arange(n2, dtype=jnp.int32)
    ea = (n2 * m1[None, None, :] * k1[None, :, None] + m2[:, None, None] * k1[None, :, None]) % n
    ang = ea.astype(F32) * w0
    fa = jnp.concatenate([jnp.cos(ang), -jnp.sin(ang)], axis=1).astype(BF16)
    eb = (n1 * m2[:, None] * m2[None, :]) % n
    angb = eb.astype(F32) * w0
    fr, fi = jnp.cos(angb), -jnp.sin(angb)
    fb = jnp.concatenate([jnp.concatenate([fr, -fi], axis=1), jnp.concatenate([fi, fr], axis=1)], axis=0).astype(BF16)
    eg = (n1 * m2[None, :, None] * m2[None, None, :] + m2[None, :, None] * k1[:, None, None]) % n
    angg = eg.astype(F32) * w0
    gr, gi = jnp.cos(angg), jnp.sin(angg)
    gb = jnp.concatenate([jnp.concatenate([gr, -gi], axis=2), jnp.concatenate([gi, gr], axis=2)], axis=1).astype(BF16)
    ec = (n2 * m1[:, None] * k1[None, :]) % n
    angc = ec.astype(F32) * w0
    fc = (jnp.concatenate([jnp.cos(angc), -jnp.sin(angc)], axis=1) * (1.0 / n)).astype(BF16)
    return fa, fb, gb, fc


def _fft_a_kernel(u_ref, f_ref, o_ref):
    tn2 = f_ref.shape[0]
    n1 = f_ref.shape[1] // 2
    c = u_ref.shape[2] // tn2
    for j in range(tn2):
        a = _dot(f_ref[j], u_ref[0, :, j * c:(j + 1) * c])
        o_ref[0, 0, :, j * c:(j + 1) * c] = a[:n1].astype(o_ref.dtype)
        o_ref[0, 1, :, j * c:(j + 1) * c] = a[n1:].astype(o_ref.dtype)


def _fft_a(u, fa, tn2=8):
    b, l, c = u.shape
    n2, n1x2, n1h = fa.shape
    n1 = n1x2 // 2
    uv = u.reshape(b, n1h, n2 * c)
    return pl.pallas_call(
        _fft_a_kernel,
        grid=(b, n2 // tn2),
        in_specs=[
            pl.BlockSpec((1, n1h, tn2 * c), lambda i, j: (i, 0, j)),
            pl.BlockSpec((tn2, n1x2, n1h), lambda i, j: (j, 0, 0)),
        ],
        out_specs=pl.BlockSpec((1, 2, n1, tn2 * c), lambda i, j: (i, 0, 0, j)),
        out_shape=jax.ShapeDtypeStruct((b, 2, n1, n2 * c), BF16),
        compiler_params=_cparams("parallel", "arbitrary"),
        name="fft_a",
    )(uv, fa)


def _fft_bf_kernel(a_ref, w_ref, o_ref):
    n2 = a_ref.shape[2]
    ct = a_ref.shape[3]
    cw = 256
    for c0 in range(0, ct, cw):
        xf = _dot(w_ref[...], a_ref[0, :, :, c0:c0 + cw].reshape(2 * n2, cw))
        xb = _dot(w_ref[...], a_ref[1, :, :, c0:c0 + cw].reshape(2 * n2, cw))
        o_ref[0, :, c0:c0 + cw] = xf[:n2] + xb[:n2]
        o_ref[1, :, c0:c0 + cw] = xf[n2:] - xb[n2:]


def _fft_bf(a, fb):
    _, _, n, c = a.shape
    n2 = FFT_N2
    return pl.pallas_call(
        _fft_bf_kernel,
        grid=(n // n2,),
        in_specs=[
            pl.BlockSpec((2, 2, n2, c), lambda k: (0, 0, k, 0)),
            pl.BlockSpec((2 * n2, 2 * n2), lambda k: (0, 0)),
        ],
        out_specs=pl.BlockSpec((2, n2, c), lambda k: (0, k, 0)),
        out_shape=jax.ShapeDtypeStruct((2, n, c), F32),
        compiler_params=_cparams("arbitrary"),
        name="fft_bf",
    )(a, fb)


SUBLANES = 8
CONV_SLABS = 2


def _conv_dims(n1):
    kh = n1 // 2 + 1
    kp = -(-kh // SUBLANES) * SUBLANES
    return kh, kp, 2 * kp + SUBLANES


def _hy_conv_kernel(u_ref, h_ref, la_ref, w_ref, g_ref, fc_ref, o_ref, s_ref, *, n1, unroll_b):
    n2 = FFT_N2
    n1h = n1 // 2
    kh, kp, pitch = _conv_dims(n1)
    grp = SUBLANES
    slabs = range(CONV_SLABS)

    def lanes(x, sl):
        return x[:, sl * LANES:(sl + 1) * LANES]

    def stage_a(jj, carry):
        rows = [u_ref[0, pl.ds(pl.multiple_of(m * n2 + jj * grp, grp), grp), :] for m in range(n1h)]
        a = _dot(la_ref[jj], jnp.concatenate(rows, axis=0).astype(BF16))
        for j0 in range(grp):
            dst = pl.multiple_of((jj * grp + j0) * pitch, grp)
            for sl in slabs:
                s_ref[sl, pl.ds(dst, 2 * kp), :] = lanes(a[j0 * 2 * kp:(j0 + 1) * 2 * kp], sl)
        return carry

    lax.fori_loop(0, n2 // grp, stage_a, 0, unroll=4)

    def stage_b(kk, carry):
        k1s = [jnp.minimum(kk * unroll_b + i, kh - 1) for i in range(unroll_b)]
        ar = [jnp.concatenate([s_ref[sl, pl.ds(k, n2, stride=pitch), :] for sl in slabs], axis=1) for k in k1s]
        ai = [jnp.concatenate([s_ref[sl, pl.ds(kp + k, n2, stride=pitch), :] for sl in slabs], axis=1) for k in k1s]
        x = [_dot(w_ref[...], jnp.concatenate([ar[i], ai[i]], axis=0).astype(BF16)) for i in range(unroll_b)]
        y = []
        for i, k in enumerate(k1s):
            off = pl.multiple_of(k * n2, n2)
            hr, hi = h_ref[0, pl.ds(off, n2), :], h_ref[1, pl.ds(off, n2), :]
            xr, xi = x[i][:n2], x[i][n2:]
            y.append(jnp.concatenate([xr * hr - xi * hi, xr * hi + xi * hr], axis=0).astype(BF16))
        z = [_dot(g_ref[k1s[i]], y[i]) for i in range(unroll_b)]
        for i, k in enumerate(k1s):
            for sl in slabs:
                s_ref[sl, pl.ds(k, n2, stride=pitch), :] = lanes(z[i][:n2], sl)
                s_ref[sl, pl.ds(kp + k, n2, stride=pitch), :] = lanes(z[i][n2:], sl)
        return carry

    lax.fori_loop(0, -(-kh // unroll_b), stage_b, 0)

    def stage_c(jj, carry):
        zz = [jnp.concatenate([s_ref[sl, pl.ds(pl.multiple_of((jj * grp + j0) * pitch, grp), 2 * kp), :]
                               for sl in slabs], axis=1) for j0 in range(grp)]
        y = _dot(fc_ref[...], jnp.concatenate(zz, axis=0).astype(BF16))
        for m in range(n1h):
            o_ref[0, pl.ds(pl.multiple_of(m * n2 + jj * grp, grp), grp), :] = y[m * grp:(m + 1) * grp]
        return carry

    lax.fori_loop(0, n2 // grp, stage_c, 0, unroll=4)


def _hy_conv(u, h, fa, fb, gb, fc):
    b, l, c = u.shape
    n2, n1x2, n1h = fa.shape
    n1 = n1x2 // 2
    kh, kp, pitch = _conv_dims(n1)
    grp = SUBLANES
    ct = CONV_SLABS * LANES

    def half(t, axis):
        re, im = jnp.split(t, 2, axis=axis)
        pad = [(0, 0)] * t.ndim
        pad[axis] = (0, kp - kh)
        cut = lambda x: jnp.pad(lax.slice_in_dim(x, 0, kh, axis=axis), pad)
        return jnp.concatenate([cut(re), cut(im)], axis=axis)

    fa_h = half(fa, 1)
    la = (fa_h.reshape(n2 // grp, grp, 2 * kp, n1h, 1) * jnp.eye(grp, dtype=fa.dtype).reshape(1, grp, 1, 1, grp))
    la = la.reshape(n2 // grp, grp * 2 * kp, n1h * grp)
    wgt = jnp.ones((kh,), F32).at[1:kh - 1].set(2.0)
    wgt = jnp.pad(wgt, (0, kp - kh))
    fc_h = (half(fc, 1).astype(F32) * jnp.tile(wgt, 2)[None, :]).astype(BF16)
    lc = fc_h.reshape(n1h, 1, 1, 2 * kp) * jnp.eye(grp, dtype=fc_h.dtype).reshape(1, grp, grp, 1)
    fc_h = lc.reshape(n1h * grp, grp * 2 * kp)
    gb_h = gb[:kh]
    unroll_b = 4
    once = pl.Buffered(1)
    return pl.pallas_call(
        functools.partial(_hy_conv_kernel, n1=n1, unroll_b=unroll_b),
        grid=(c // ct, b),
        in_specs=[
            pl.BlockSpec((1, l, ct), lambda j, i: (i, 0, j)),
            pl.BlockSpec((2, kh * n2, ct), lambda j, i: (0, 0, j), pipeline_mode=once),
            pl.BlockSpec(la.shape, lambda j, i: (0, 0, 0), pipeline_mode=once),
            pl.BlockSpec(fb.shape, lambda j, i: (0, 0), pipeline_mode=once),
            pl.BlockSpec(gb_h.shape, lambda j, i: (0, 0, 0), pipeline_mode=once),
            pl.BlockSpec(fc_h.shape, lambda j, i: (0, 0), pipeline_mode=once),
        ],
        out_specs=pl.BlockSpec((1, l, ct), lambda j, i: (i, 0, j)),
        out_shape=jax.ShapeDtypeStruct((b, l, c), F32),
        scratch_shapes=[pltpu.VMEM((CONV_SLABS, n2 * pitch, LANES), F32)],
        compiler_params=_cparams("arbitrary", "arbitrary"),
        name="hy_conv",
    )(u, h, la, fb, gb_h, fc_h)


def _hyena_branch(z, l, conv_w, conv_b, w1, b1, w2, b2, w3, freq, skip):
    c = HY_WIDTH
    n = 2 * l
    fa, fb, gb, fc = _fft_consts(l)
    filt = _hy_filters(l, w1, b1, w2, b2, w3, freq, skip)
    fa_spec = _fft_a(filt, fa).reshape(2, 2, n, c)
    h = _fft_bf(fa_spec, fb)
    u, e = _hy_prep(z, conv_w, conv_b)
    return _hy_conv(u, h, fa, fb, gb, fc), e


def _gdn_prep_kernel(zc_ref, zp_ref, zn_ref, cw_ref, s_ref, q_ref, k_ref, v_ref, buf_ref):
    t = pl.program_id(1)
    nt = pl.num_programs(1)
    tl = zc_ref.shape[1]
    dh = GDN_HEAD_DIM
    tr = TAP_ROWS
    _fill_halo_buf(buf_ref, zc_ref, zp_ref, zn_ref, t, nt)
    outs = (q_ref, k_ref, v_ref)
    cw = 2 * dh
    for r0 in range(0, tl, tr):
        for p in range(3):
            for c0 in range(0, GDN_WIDTH, cw):
                lo = p * GDN_WIDTH + c0
                xs = _taps(buf_ref, s_ref, r0, slice(lo, lo + cw))
                acc = xs[0:tr] * cw_ref[0:1, lo:lo + cw]
                for j in range(1, 5):
                    acc = acc + xs[j * tr:(j + 1) * tr] * cw_ref[j:j + 1, lo:lo + cw]
                a = _silu(acc)
                for hh in range(cw // dh):
                    ah = a[:, hh * dh:(hh + 1) * dh]
                    if p < 2:
                        ah = ah * lax.rsqrt(jnp.sum(ah * ah, axis=-1, keepdims=True) + EPS)
                    if p == 0:
                        ah = ah * (dh ** -0.5)
                    outs[p][0, r0:r0 + tr, c0 + hh * dh:c0 + (hh + 1) * dh] = ah.astype(BF16)


def _gdn_prep(z, conv_w, tl=512):
    b, l, _ = z.shape
    w = 3 * GDN_WIDTH
    cur, prev, nxt = _halo_specs(tl, w, l, E_QKV // w)
    o_spec = pl.BlockSpec((1, tl, GDN_WIDTH), lambda i, t: (i, t, 0))
    o_shape = jax.ShapeDtypeStruct((b, l, GDN_WIDTH), BF16)
    return pl.pallas_call(
        _gdn_prep_kernel,
        grid=(b, l // tl),
        in_specs=[cur, prev, nxt, pl.BlockSpec((5, w), lambda i, t: (0, 0)),
                  pl.BlockSpec((5 * TAP_ROWS, TAP_WINDOW), lambda i, t: (0, 0))],
        out_specs=[o_spec, o_spec, o_spec],
        out_shape=[o_shape, o_shape, o_shape],
        scratch_shapes=[pltpu.VMEM((_halo_buf_rows(tl), w), BF16)],
        compiler_params=_cparams("parallel", "arbitrary"),
        name="gdn_prep",
    )(z, z, z, conv_w.astype(F32), _tap_shift_matrix(5))


def _dot_nt(a, b):
    return lax.dot_general(a, b, (((1,), (1,)), ((), ())), preferred_element_type=F32)


def _dot_tn(a, b):
    return lax.dot_general(a, b, (((0,), (0,)), ((), ())), preferred_element_type=F32)


def _split3(x):
    a = x.astype(BF16)
    r = x - a.astype(F32)
    b = r.astype(BF16)
    c = (r - b.astype(F32)).astype(BF16)
    return a, b, c


GDN_CHUNKS_PER_STEP = 8


def _gdn_scan_kernel(*refs, direction, final):
    if final:
        q_ref, k_ref, v_ref, zs_ref, al_ref, dt_ref, of_ref, gg_ref, ng_ref, o_ref, s_ref = refs
    else:
        q_ref, k_ref, v_ref, zs_ref, al_ref, dt_ref, o_ref, s_ref = refs
    c = GDN_CHUNK
    dh = GDN_HEAD_DIM
    nsub = q_ref.shape[1] // c

    @pl.when(pl.program_id(1) == 0)
    def _():
        s_ref[...] = jnp.zeros_like(s_ref)

    row = lax.broadcasted_iota(jnp.int32, (c, LANES), 0)
    col = lax.broadcasted_iota(jnp.int32, (c, LANES), 1)
    if direction == 0:
        incl, strict, last = col <= row, col < row, c - 1
    else:
        incl, strict, last = (col >= row) & (col < c), (col > row) & (col < c), 0
    tri = incl[:, :c].astype(BF16)
    eye = (row == col).astype(F32)
    zc = jnp.zeros((c, LANES), BF16)
    zd = jnp.zeros((dh, dh), BF16)

    def lanes2(xa, xb):
        return jnp.concatenate([xa, xb], axis=1)

    def blockdiag_slots(pa, pb):
        return jnp.concatenate([lanes2(pa, zc), lanes2(zc, zc), lanes2(zc, pb), lanes2(zc, zc)], axis=0)

    heads = range(GDN_HEADS)
    hsl = [slice(h * dh, (h + 1) * dh) for h in heads]
    lns = [2 * GDN_HEADS + direction * GDN_HEADS + h for h in heads]

    rows = [slice(i * c, (i + 1) * c) for i in range(nsub)]
    gcum, gcum_t, beta_all = [], [], []
    for rs in rows:
        zs = zs_ref[0, rs, :]
        beta_all.append(jax.nn.sigmoid(zs))
        gl = -jnp.exp(al_ref[...]) * jax.nn.softplus(zs + dt_ref[...])
        g1, g2, g3 = _split3(gl)
        gsum = _dot(tri, g1) + _dot(tri, g2) + _dot(tri, g3)
        gcum.append(gsum)
        gcum_t.append(jnp.concatenate([gsum, jnp.zeros_like(gsum)], axis=0).T)
    items = [(i, h) for i in range(nsub) for h in heads]
    it = range(len(items))
    beta = [beta_all[i][:, direction * GDN_HEADS + h:direction * GDN_HEADS + h + 1] for i, h in items]
    gc = [gcum[i][:, lns[h]:lns[h] + 1] for i, h in items]
    gct = [gcum_t[i][lns[h]:lns[h] + 1, :] for i, h in items]
    glast = [gcum_t[i][lns[h]:lns[h] + 1, last:last + 1] for i, h in items]
    qh = [q_ref[0, rows[i], hsl[h]] for i, h in items]
    kh = [k_ref[0, rows[i], hsl[h]] for i, h in items]
    vh = [v_ref[0, rows[i], hsl[h]] for i, h in items]
    pr = range(len(items) // 2)
    kq2 = [_dot_nt(lanes2(jnp.concatenate([kh[2 * y], qh[2 * y]], axis=0),
                          jnp.concatenate([kh[2 * y + 1], qh[2 * y + 1]], axis=0)),
                   jnp.concatenate([lanes2(kh[2 * y], zd[:c]), lanes2(zc, zc),
                                    lanes2(zd[:c], kh[2 * y + 1]), lanes2(zc, zc)], axis=0)) for y in pr]
    kq = [kq2[x // 2][:, (x % 2) * LANES:(x % 2 + 1) * LANES] for x in it]
    dmat = [jnp.exp(jnp.where(incl, gc[x] - gct[x], -jnp.inf)) for x in it]
    a = [jnp.where(strict, (beta[x] * kq[x][:c]) * dmat[x], 0.0) for x in it]
    tinv2 = [lanes2(eye - a[2 * y], eye - a[2 * y + 1]) for y in pr]
    ab = [a[x].astype(BF16) for x in it]
    p2 = [_dot(lanes2(ab[2 * y], ab[2 * y + 1]), blockdiag_slots(ab[2 * y], ab[2 * y + 1])) for y in pr]
    for j in range(5):
        pb2 = [p2[y].astype(BF16) for y in pr]
        bd = [blockdiag_slots(pb2[y][:, :LANES], pb2[y][:, LANES:]) for y in pr]
        if j < 4:
            tp = [_dot(jnp.concatenate([tinv2[y].astype(BF16), pb2[y]], axis=0), bd[y]) for y in pr]
            tinv2 = [tinv2[y] + tp[y][:c] for y in pr]
            p2 = [tp[y][c:] for y in pr]
        else:
            tinv2 = [tinv2[y] + _dot(tinv2[y].astype(BF16), bd[y]) for y in pr]
    tinv = [tinv2[x // 2][:, (x % 2) * LANES:(x % 2 + 1) * LANES].astype(BF16) for x in it]
    eg = [jnp.exp(gc[x]) for x in it]
    kf = [kh[x].astype(F32) for x in it]
    rhs = [jnp.concatenate([vh[x].astype(F32) * beta[x], kf[x] * (beta[x] * eg[x])], axis=1).astype(BF16) for x in it]
    zr = jnp.zeros((LANES - c, 2 * dh), BF16)
    uw = [_dot(tinv[x], jnp.concatenate([rhs[x], zr], axis=0)) for x in it]
    wq = [jnp.concatenate([uw[x][:, dh:], qh[x].astype(F32) * eg[x]], axis=0).astype(BF16) for x in it]
    qkd = [(kq[x][c:] * dmat[x]).astype(BF16) for x in it]
    kd = [(kf[x] * jnp.exp(glast[x] - gc[x])).astype(BF16) for x in it]

    order = list(range(nsub)) if direction == 0 else list(range(nsub - 1, -1, -1))
    state = [s_ref[h] for h in heads]
    hp = range(GDN_HEADS // 2)
    for i in order:
        rs = rows[i]
        xs = [i * GDN_HEADS + h for h in heads]
        sb = [state[h].astype(BF16) for h in heads]
        ws2 = [_dot(lanes2(wq[xs[2 * y]], wq[xs[2 * y + 1]]),
                    jnp.concatenate([lanes2(sb[2 * y], zd), lanes2(zd, sb[2 * y + 1])], axis=0)) for y in hp]
        ws = [ws2[h // 2][:, (h % 2) * dh:(h % 2 + 1) * dh] for h in heads]
        vnb = [(uw[xs[h]][:, :dh] - ws[h][:c]).astype(BF16) for h in heads]
        o2 = [_dot(lanes2(qkd[xs[2 * y]], qkd[xs[2 * y + 1]]),
                   jnp.concatenate([lanes2(vnb[2 * y], zd[:c]), lanes2(zc, zc),
                                    lanes2(zd[:c], vnb[2 * y + 1]), lanes2(zc, zc)], axis=0)) for y in hp]
        o = [ws[h][c:] + o2[h // 2][:, (h % 2) * dh:(h % 2 + 1) * dh] for h in heads]
        kv2 = [_dot_tn(jnp.concatenate([kd[xs[2 * y]], kd[xs[2 * y + 1]]], axis=0),
                       jnp.concatenate([lanes2(vnb[2 * y], zd[:c]), lanes2(zd[:c], vnb[2 * y + 1])], axis=0)) for y in hp]
        state = [state[h] * jnp.exp(glast[xs[h]]) + kv2[h // 2][:, (h % 2) * dh:(h % 2 + 1) * dh] for h in heads]
        for h in heads:
            if final:
                tot = of_ref[0, rs, hsl[h]] + o[h]
                y = tot * lax.rsqrt(jnp.mean(tot * tot, axis=-1, keepdims=True) + EPS) * ng_ref[...]
                o_ref[0, rs, hsl[h]] = (y * _silu(gg_ref[0, rs, hsl[h]].astype(F32))).astype(o_ref.dtype)
            else:
                o_ref[0, rs, hsl[h]] = o[h].astype(o_ref.dtype)
    for h in heads:
        s_ref[h] = state[h]


def _gdn_scan(q, k, v, zs, a_log, dt_bias, direction, o_fwd=None, z=None, gate_blk=None, norm_g=None):
    b, l, w = q.shape
    c = GDN_CHUNK * GDN_CHUNKS_PER_STEP
    n = l // c
    final = o_fwd is not None
    if direction == 0:
        cmap = lambda i, t: (i, t, 0)
    else:
        cmap = lambda i, t: (i, n - 1 - t, 0)
    blk = pl.BlockSpec((1, c, w), cmap)
    vec = pl.BlockSpec((1, LANES), lambda i, t: (0, 0))
    pad = lambda x: jnp.pad(x.reshape(1, -1).astype(F32), ((0, 0), (2 * GDN_HEADS, LANES - 4 * GDN_HEADS)))
    in_specs = [blk, blk, blk, pl.BlockSpec((1, c, LANES), cmap), vec, vec]
    args = [q, k, v, zs, pad(a_log), pad(dt_bias)]
    if final:
        if direction == 0:
            gmap = lambda i, t: (i, t, gate_blk)
        else:
            gmap = lambda i, t: (i, n - 1 - t, gate_blk)
        in_specs += [blk, pl.BlockSpec((1, c, w), gmap), pl.BlockSpec((1, GDN_HEAD_DIM), lambda i, t: (0, 0))]
        args += [o_fwd, z, norm_g.reshape(1, GDN_HEAD_DIM).astype(F32)]
    return pl.pallas_call(
        functools.partial(_gdn_scan_kernel, direction=direction, final=final),
        grid=(b, n),
        in_specs=in_specs,
        out_specs=blk,
        out_shape=jax.ShapeDtypeStruct((b, l, w), BF16 if final else F32),
        scratch_shapes=[pltpu.VMEM((GDN_HEADS, GDN_HEAD_DIM, GDN_HEAD_DIM), F32)],
        compiler_params=_cparams("parallel", "arbitrary"),
        name="gdn_scan_bwd" if direction else "gdn_scan_fwd",
    )(*args)


def _gdn_branch(z, zs, conv_w, a_log, dt_bias, norm_g, gate_blk):
    q, k, v = _gdn_prep(z, conv_w)
    o_f = _gdn_scan(q, k, v, zs, a_log, dt_bias, 0)
    return _gdn_scan(q, k, v, zs, a_log, dt_bias, 1, o_fwd=o_f, z=z, gate_blk=gate_blk, norm_g=norm_g)


def _softmax_t(s, extra=None):
    m = jnp.max(s, axis=0, keepdims=True)
    if extra is not None:
        m = jnp.maximum(m, extra)
    p = jnp.exp2(s - m)
    den = jnp.sum(p, axis=0, keepdims=True)
    if extra is not None:
        den = den + jnp.exp2(extra - m)
    return p.astype(BF16), den, m


def _t_bf16(x):
    return x.astype(F32).T.astype(BF16)


def _xattn_kernel(q_ref, g_ref, kv_ref, o_ref):
    dh = X_HEAD_DIM
    for h in range(X_HEADS):
        hs = slice(h * dh, (h + 1) * dh)
        qh = q_ref[0, :, hs].astype(BF16)
        kh = kv_ref[0, :, hs]
        vh = kv_ref[0, :, X_WIDTH + h * dh:X_WIDTH + (h + 1) * dh]
        s = _dot_nt(qh, kh) * (dh ** -0.5)
        p = jnp.exp(s - jnp.max(s, axis=-1, keepdims=True))
        den = jnp.sum(p, axis=-1, keepdims=True)
        o = _dot(p.astype(BF16), vh) / den
        o_ref[0, :, hs] = (o * _silu(g_ref[0, :, hs].astype(F32))).astype(o_ref.dtype)


def _xattn(z, kv, q_blk, g_blk, tq=512):
    b, l, _ = z.shape
    m = kv.shape[1]
    w = X_WIDTH
    return pl.pallas_call(
        _xattn_kernel,
        grid=(b, l // tq),
        in_specs=[
            pl.BlockSpec((1, tq, w), lambda i, t: (i, t, q_blk)),
            pl.BlockSpec((1, tq, w), lambda i, t: (i, t, g_blk)),
            pl.BlockSpec((1, m, 2 * w), lambda i, t: (i, 0, 0)),
        ],
        out_specs=pl.BlockSpec((1, tq, w), lambda i, t: (i, t, 0)),
        out_shape=jax.ShapeDtypeStruct((b, l, w), BF16),
        compiler_params=_cparams("parallel", "arbitrary"),
        name="xattn",
    )(z, z, kv)


def _out_proj_kernel(*refs, widths, gated):
    nparts = len(widths)
    y_refs = refs[:nparts]
    if gated:
        e_ref, w_ref, g_ref, x_ref, o_ref = refs[nparts:]
    else:
        w_ref, g_ref, x_ref, o_ref = refs[nparts:]
    d = o_ref.shape[1]
    cw = 256
    ys = [y_ref[...] for y_ref in y_refs]
    if gated:
        ys[0] = ys[0] * e_ref[...]
    ys = [y.astype(BF16) for y in ys]
    ssq = jnp.zeros((o_ref.shape[0], 1), F32)
    for c0 in range(0, d, cw):
        acc = None
        off = 0
        for y, wd in zip(ys, widths):
            part = _dot(y, w_ref[off:off + wd, c0:c0 + cw])
            acc = part if acc is None else acc + part
            off += wd
        ssq = ssq + jnp.sum(acc * acc, axis=-1, keepdims=True)
        o_ref[:, c0:c0 + cw] = acc
    r = lax.rsqrt(ssq * (1.0 / d) + EPS)
    for c0 in range(0, d, cw):
        o_ref[:, c0:c0 + cw] = x_ref[:, c0:c0 + cw] + o_ref[:, c0:c0 + cw] * r * g_ref[:, c0:c0 + cw]


def _out_proj(parts, w_out, post_g, x2d, gate=None, tm=512):
    t, d = x2d.shape
    widths = tuple(int(p.shape[1]) for p in parts)
    kdim = sum(widths)
    gates = [] if gate is None else [gate]
    return pl.pallas_call(
        functools.partial(_out_proj_kernel, widths=widths, gated=gate is not None),
        grid=(t // tm,),
        in_specs=[pl.BlockSpec((tm, wd), lambda i: (i, 0)) for wd in widths]
        + [pl.BlockSpec((tm, widths[0]), lambda i: (i, 0)) for _ in gates] + [
            pl.BlockSpec((kdim, d), lambda i: (0, 0)),
            pl.BlockSpec((1, d), lambda i: (0, 0)),
            pl.BlockSpec((tm, d), lambda i: (i, 0)),
        ],
        out_specs=pl.BlockSpec((tm, d), lambda i: (i, 0)),
        out_shape=jax.ShapeDtypeStruct((t, d), F32),
        compiler_params=_cparams("parallel"),
        name="out_proj",
    )(*parts, *gates, w_out.astype(BF16), post_g.reshape(1, d).astype(F32), x2d)


def _mem_kv(mem, mem_g, w_mem_kv):
    b, m, d = mem.shape
    kv = _norm_matmul(mem.reshape(b * m, d), mem_g, w_mem_kv, 1024, out_dtype=BF16, name="mem_kv")
    return kv.reshape(b, m, 2 * X_WIDTH)


E_QKV, E_HY, E_GHY, E_GGDN, E_XQ, E_GX, E_BIG = 0, 3072, 6144, 7168, 8192, 8704, 9216


def _even_layer(x, mem, pre_g, post_g, w_in, w_out, hy_conv_w, hy_conv_b, hy_w1, hy_b1, hy_w2, hy_b2, hy_w3, hy_freq,
                hy_skip, gdn_conv_w, gdn_a_log, gdn_dt_bias, gdn_norm_g, mem_g, w_mem_kv):
    b, l, d = x.shape
    x2d = x.reshape(b * l, d)
    w_big = jnp.concatenate([w_in[:, 4096:7168], w_in[:, 0:3072], w_in[:, 3072:4096], w_in[:, 7168:8192],
                             w_in[:, 8224:8736], w_in[:, 8736:9248]], axis=1)
    w_small = jnp.pad(w_in[:, 8192:8224], ((0, 0), (0, LANES - 4 * GDN_HEADS)))
    z, zs = _norm_matmul(x2d, pre_g, w_big, E_BIG // 3, out_dtype=BF16, tm=1024, name="even_in", w_side=w_small)
    z = z.reshape(b, l, E_BIG)
    zs = zs.reshape(b, l, LANES)
    conv, e = _hyena_branch(z, l, hy_conv_w, hy_conv_b, hy_w1, hy_b1, hy_w2, hy_b2, hy_w3, hy_freq, hy_skip)
    y_b = _gdn_branch(z, zs, gdn_conv_w, gdn_a_log, gdn_dt_bias, gdn_norm_g, E_GGDN // GDN_WIDTH)
    kv = _mem_kv(mem, mem_g, w_mem_kv)
    y_x = _xattn(z, kv, E_XQ // X_WIDTH, E_GX // X_WIDTH)
    t = b * l
    out = _out_proj([conv.reshape(t, -1), y_b.reshape(t, -1), y_x.reshape(t, -1)], w_out, post_g, x2d,
                    gate=e.reshape(t, -1))
    return out.reshape(b, l, d)


DIL_PATTERNS = ((128, 1), (512, 4), (2048, 16))
N_DIL = len(DIL_PATTERNS)
DIL_HEADS = 4
DIL_HEAD_DIM = 128
DIL_WIDTH = DIL_HEADS * DIL_HEAD_DIM
SWA_Q_HEADS = 16
SWA_KV_HEADS = 2
SWA_HEAD_DIM = 64
SWA_WIDTH = SWA_Q_HEADS * SWA_HEAD_DIM
SWA_HALF_WINDOW = 128
O_CQKV, O_GC, O_DQ, O_GD, O_XQ, O_GX, O_DKV, O_ALL = 0, 4608, 5120, 6144, 7168, 7680, 8192, 8448


def _rope_tables(l, dh):
    half = dh // 8
    inv = ROPE_THETA ** (-jnp.arange(half, dtype=F32) / half)
    ang = jnp.arange(l, dtype=F32)[:, None] * inv[None, :]
    cos, sin = jnp.cos(ang), jnp.sin(ang)
    one = jnp.ones((l, dh - 2 * half), F32)
    zero_h = jnp.zeros((l, half), F32)
    zero_r = jnp.zeros((l, dh - 2 * half), F32)
    c = jnp.concatenate([cos, cos, one], axis=1)
    sa = jnp.concatenate([-sin, zero_h, zero_r], axis=1)
    sb = jnp.concatenate([zero_h, sin, zero_r], axis=1)
    rep = LANES // dh
    return tuple(jnp.tile(t, (1, rep)) for t in (c, sa, sb))


def _odd_prep_kernel(c_ref, dq_ref, dkv_ref, c1_ref, a1_ref, b1_ref, c2_ref, a2_ref, b2_ref, *refs):
    dil_refs = refs[:3 * N_DIL]
    dqe_ref, dk_ref, dv_ref, xs_ref = refs[3 * N_DIL:]
    tl = c_ref.shape[1]
    lane = lax.broadcasted_iota(jnp.int32, (tl, LANES), 1)
    low = lane < SWA_HEAD_DIM
    c1, a1, b1 = c1_ref[...], a1_ref[...], b1_ref[...]
    c2, a2, b2 = c2_ref[...], a2_ref[...], b2_ref[...]
    h1 = DIL_HEAD_DIM // 8
    h2 = SWA_HEAD_DIM // 8

    def rope1(x):
        return x * c1 + pltpu.roll(x, LANES - h1, 1) * a1 + pltpu.roll(x, h1, 1) * b1

    def rope2(x):
        return x * c2 + pltpu.roll(x, LANES - h2, 1) * a2 + pltpu.roll(x, h2, 1) * b2

    nblk = N_DIL * DIL_HEADS
    slot = 0
    for s in range(3):
        for gi, (_, d) in enumerate(DIL_PATTERNS):
            dst = dil_refs[3 * gi + s]
            for h in range(DIL_HEADS):
                j = s * nblk + gi * DIL_HEADS + h
                x = c_ref[0, :, j * LANES:(j + 1) * LANES]
                if s < 2:
                    x = rope1(x.astype(F32))
                if s == 0:
                    x = x * (DIL_HEAD_DIM ** -0.5 * LOG2E)
                hs = slice(h * LANES, (h + 1) * LANES)
                if d == 1:
                    dst[0, 0, :, hs] = x.astype(BF16)
                else:
                    xs_ref[slot] = x.astype(F32)
                    for r in range(d):
                        dst[0, r, :, hs] = xs_ref[slot, pl.ds(r, tl // d, stride=d), :].astype(BF16)
                    slot += 1
    for j in range(SWA_Q_HEADS // 2):
        xr = rope2(dq_ref[0, :, j * LANES:(j + 1) * LANES].astype(F32)) * (SWA_HEAD_DIM ** -0.5 * LOG2E)
        dqe_ref[0, :, (2 * j) * LANES:(2 * j + 1) * LANES] = jnp.where(low, xr, 0.0).astype(BF16)
        dqe_ref[0, :, (2 * j + 1) * LANES:(2 * j + 2) * LANES] = jnp.where(low, pltpu.roll(xr, SWA_HEAD_DIM, 1), 0.0).astype(BF16)
    kr = rope2(dkv_ref[0, :, 0:LANES].astype(F32))
    vv = dkv_ref[0, :, LANES:2 * LANES].astype(F32)
    kr_sw = pltpu.roll(kr, SWA_HEAD_DIM, 1)
    vv_sw = pltpu.roll(vv, SWA_HEAD_DIM, 1)
    dk_ref[0, :, 0:LANES] = jnp.where(low, kr, 0.0).astype(BF16)
    dk_ref[0, :, LANES:2 * LANES] = jnp.where(low, kr_sw, 0.0).astype(BF16)
    dv_ref[0, :, 0:LANES] = jnp.where(low, vv, vv_sw).astype(BF16)
    dv_ref[0, :, LANES:2 * LANES] = jnp.where(low, vv_sw, vv).astype(BF16)


def _odd_prep(z, tl=512):
    b, l, _ = z.shape
    wc = 3 * N_DIL * DIL_WIDTH
    t1 = _rope_tables(l, DIL_HEAD_DIM)
    t2 = _rope_tables(l, SWA_HEAD_DIM)
    tab = pl.BlockSpec((tl, LANES), lambda i, t: (t, 0))

    def spec(w):
        return pl.BlockSpec((1, tl, w), lambda i, t: (i, t, 0))

    def shape(w):
        return jax.ShapeDtypeStruct((b, l, w), BF16)

    dil_specs, dil_shapes = [], []
    for _, d in DIL_PATTERNS:
        for _ in range(3):
            dil_specs.append(pl.BlockSpec((1, d, tl // d, DIL_WIDTH), lambda i, t: (i, 0, t, 0)))
            dil_shapes.append(jax.ShapeDtypeStruct((b, d, l // d, DIL_WIDTH), BF16))
    n_staged = 3 * DIL_HEADS * sum(1 for _, d in DIL_PATTERNS if d > 1)
    outs = pl.pallas_call(
        _odd_prep_kernel,
        grid=(b, l // tl),
        in_specs=[
            pl.BlockSpec((1, tl, wc), lambda i, t: (i, t, 0)),
            pl.BlockSpec((1, tl, SWA_WIDTH), lambda i, t: (i, t, O_DQ // SWA_WIDTH)),
            pl.BlockSpec((1, tl, 2 * LANES), lambda i, t: (i, t, O_DKV // (2 * LANES))),
            tab, tab, tab, tab, tab, tab,
        ],
        out_specs=dil_specs + [spec(2 * SWA_WIDTH), spec(2 * LANES), spec(2 * LANES)],
        out_shape=dil_shapes + [shape(2 * SWA_WIDTH), shape(2 * LANES), shape(2 * LANES)],
        scratch_shapes=[pltpu.VMEM((n_staged, tl, LANES), F32)],
        compiler_params=_cparams("parallel", "arbitrary"),
        name="odd_prep",
    )(z, z, z, *t1, *t2)
    dil = [tuple(outs[3 * gi:3 * gi + 3]) for gi in range(N_DIL)]
    return dil, outs[3 * N_DIL], outs[3 * N_DIL + 1], outs[3 * N_DIL + 2]


def _band_bias_t(tq, half, ls):
    p0 = pl.program_id(1) * tq
    j = lax.broadcasted_iota(jnp.int32, (tq + 2 * half, tq), 0)
    i = lax.broadcasted_iota(jnp.int32, (tq + 2 * half, tq), 1)
    kpos = p0 - half + j
    valid = (j >= i) & (j - i <= 2 * half) & (kpos >= 0) & (kpos < ls)
    return jnp.where(valid, 0.0, -jnp.inf)


def _band_mask(tq, half, ls):
    p0 = pl.program_id(1) * tq
    i = lax.broadcasted_iota(jnp.int32, (tq, tq + 2 * half), 0)
    j = lax.broadcasted_iota(jnp.int32, (tq, tq + 2 * half), 1)
    kpos = p0 - half + j
    return (j >= i) & (j - i <= 2 * half) & (kpos >= 0) & (kpos < ls)


def _band_attn_kernel(q_ref, kp_ref, kc_ref, kn_ref, vp_ref, vc_ref, vn_ref, o_ref, lse_ref, *, half, ls):
    tq = q_ref.shape[1]
    dh = DIL_HEAD_DIM
    valid = _band_mask(tq, half, ls)
    lane = lax.broadcasted_iota(jnp.int32, (tq, LANES), 1)
    lse_all = jnp.zeros((tq, LANES), F32)
    heads = range(DIL_HEADS)
    hsl = [slice(h * dh, (h + 1) * dh) for h in heads]
    kall = [jnp.concatenate([kp_ref[0, :, s], kc_ref[0, :, s], kn_ref[0, :, s]], axis=0) for s in hsl]
    ones = jnp.ones((tq + 2 * half, dh), BF16)
    vall = [jnp.concatenate([jnp.concatenate([vp_ref[0, :, s], vc_ref[0, :, s], vn_ref[0, :, s]], axis=0), ones], axis=1)
            for s in hsl]
    sc = [jnp.where(valid, _dot_nt(q_ref[0, :, hsl[h]], kall[h]), -jnp.inf) for h in heads]
    m = [jnp.max(sc[h], axis=-1, keepdims=True) for h in heads]
    p = [jnp.exp2(sc[h] - m[h]).astype(BF16) for h in heads]
    od = [_dot(p[h], vall[h]) for h in heads]
    for h in heads:
        den = od[h][:, dh:]
        o_ref[0, :, hsl[h]] = (od[h][:, :dh] / den).astype(o_ref.dtype)
        lse_all = jnp.where(lane == h, m[h] + jnp.log2(den), lse_all)
    lse_ref[0] = lse_all


def _band_specs(tq, half, ls, w, col):
    r = tq // half
    nb = ls // half
    cur = pl.BlockSpec((1, tq, w), lambda i, t: (i, t, col))
    prev = pl.BlockSpec((1, half, w), lambda i, t: (i, jnp.maximum(t * r - 1, 0), col))
    nxt = pl.BlockSpec((1, half, w), lambda i, t: (i, jnp.minimum((t + 1) * r, nb - 1), col))
    return cur, prev, nxt


def _band_attn(q, k, v, half):
    n, ls, w = q.shape
    tq = min(2 * LANES, ls)
    cur, prev, nxt = _band_specs(tq, half, ls, w, 0)
    return pl.pallas_call(
        functools.partial(_band_attn_kernel, half=half, ls=ls),
        grid=(n, ls // tq),
        in_specs=[cur, prev, cur, nxt, prev, cur, nxt],
        out_specs=[cur, pl.BlockSpec((1, tq, LANES), lambda i, t: (i, t, 0))],
        out_shape=[jax.ShapeDtypeStruct((n, ls, w), BF16), jax.ShapeDtypeStruct((n, ls, LANES), F32)],
        compiler_params=_cparams("parallel", "arbitrary"),
        name="band_attn",
    )(q, k, k, k, v, v, v)


def _dil_merge_kernel(o0_ref, o1_ref, o2_ref, l0_ref, l1_ref, l2_ref, g_ref, y_ref, os_ref, ls_ref):
    dh = DIL_HEAD_DIM
    tl = y_ref.shape[1]
    o_refs = (o0_ref, o1_ref, o2_ref)
    l_refs = (l0_ref, l1_ref, l2_ref)
    slot = 0
    lse, outs = [], []
    for gi, (_, d) in enumerate(DIL_PATTERNS):
        if d == 1:
            lse.append(l_refs[gi][0, 0])
            outs.append([o_refs[gi][0, 0, :, h * dh:(h + 1) * dh].astype(F32) for h in range(DIL_HEADS)])
            continue
        for r in range(d):
            ls_ref[gi, pl.ds(r, tl // d, stride=d), :] = l_refs[gi][0, r]
        lse.append(ls_ref[gi])
        per_head = []
        for h in range(DIL_HEADS):
            for r in range(d):
                os_ref[slot, pl.ds(r, tl // d, stride=d), :] = o_refs[gi][0, r, :, h * dh:(h + 1) * dh].astype(F32)
            per_head.append(os_ref[slot])
            slot += 1
        outs.append(per_head)
    for h in range(DIL_HEADS):
        hs = slice(h * dh, (h + 1) * dh)
        ls = [x[:, h:h + 1] for x in lse]
        m = jnp.maximum(jnp.maximum(ls[0], ls[1]), ls[2])
        ws = [jnp.exp2(x - m) for x in ls]
        den = ws[0] + ws[1] + ws[2]
        y = (ws[0] / den) * outs[0][h] + (ws[1] / den) * outs[1][h] + (ws[2] / den) * outs[2][h]
        y_ref[0, :, hs] = (y * _silu(g_ref[0, :, hs].astype(F32))).astype(y_ref.dtype)


def _dil_merge(outs, lses, z, tl=512):
    b, _, l, w = outs[0].shape
    o_specs = [pl.BlockSpec((1, d, tl // d, w), lambda i, t: (i, 0, t, 0)) for _, d in DIL_PATTERNS]
    l_specs = [pl.BlockSpec((1, d, tl // d, LANES), lambda i, t: (i, 0, t, 0)) for _, d in DIL_PATTERNS]
    n_staged = DIL_HEADS * sum(1 for _, d in DIL_PATTERNS if d > 1)
    return pl.pallas_call(
        _dil_merge_kernel,
        grid=(b, l // tl),
        in_specs=o_specs + l_specs + [pl.BlockSpec((1, tl, w), lambda i, t: (i, t, O_GC // DIL_WIDTH))],
        out_specs=pl.BlockSpec((1, tl, w), lambda i, t: (i, t, 0)),
        out_shape=jax.ShapeDtypeStruct((b, l, w), BF16),
        scratch_shapes=[pltpu.VMEM((n_staged, tl, LANES), F32), pltpu.VMEM((N_DIL, tl, LANES), F32)],
        compiler_params=_cparams("parallel", "arbitrary"),
        name="dil_merge",
    )(*outs, *lses, z)


def _dilated_branch(dil, z):
    outs, lses = [], []
    for (window, d), (q, k, v) in zip(DIL_PATTERNS, dil):
        b, _, ls, w = q.shape
        half = window // (2 * d)
        o, lse = _band_attn(q.reshape(b * d, ls, w), k.reshape(b * d, ls, w), v.reshape(b * d, ls, w), half)
        outs.append(o.reshape(b, d, ls, w))
        lses.append(lse.reshape(b, d, ls, LANES))
    return _dil_merge(outs, lses, z)


def _swa_kernel(q_ref, kp_ref, kc_ref, kn_ref, vp_ref, vc_ref, vn_ref, g_ref, sink_ref, o_ref, *, half, ls):
    tq = q_ref.shape[1]
    dh = SWA_HEAD_DIM
    bias1 = _band_bias_t(tq, half, ls)
    bias = jnp.concatenate([bias1, bias1], axis=1)
    b_prev, b_next = bias[:half], bias[half + tq:]
    low = lax.broadcasted_iota(jnp.int32, (LANES, tq), 0) < dh
    first = lax.broadcasted_iota(jnp.int32, (1, 2 * tq), 1) < tq
    grp = SWA_Q_HEADS // SWA_KV_HEADS
    for g in range(SWA_KV_HEADS):
        gs = slice(g * LANES, (g + 1) * LANES)
        kall = jnp.concatenate([kp_ref[0, :, gs], kc_ref[0, :, gs], kn_ref[0, :, gs]], axis=0)
        vt = _t_bf16(jnp.concatenate([vp_ref[0, :, gs], vc_ref[0, :, gs], vn_ref[0, :, gs]], axis=0))
        pairs = range(grp // 2)
        h0 = [g * grp + 2 * jp for jp in pairs]
        q2 = [q_ref[0, :, h * LANES:(h + 2) * LANES] for h in h0]
        q2 = [jnp.concatenate([x[:, :LANES], x[:, LANES:]], axis=0) for x in q2]
        sc = [_dot_nt(kall, q2[jp]) for jp in pairs]
        sc = [jnp.concatenate([s[:half] + b_prev, s[half:half + tq], s[half + tq:] + b_next], axis=0) for s in sc]
        sk = [jnp.where(first, sink_ref[:, h:h + 1], sink_ref[:, h + 1:h + 2]) * LOG2E for h in h0]
        sm = [_softmax_t(sc[jp], sk[jp]) for jp in pairs]
        ot = [_dot(vt, sm[jp][0]) / sm[jp][1] for jp in pairs]
        for jp in pairs:
            blk = (g * grp) // 2 + jp
            bs = slice(blk * LANES, (blk + 1) * LANES)
            y = jnp.where(low, ot[jp][:, :tq], ot[jp][:, tq:]).T * _silu(g_ref[0, :, bs].astype(F32))
            o_ref[0, :, bs] = y.astype(o_ref.dtype)


def _swa_branch(dqe, dk, dv, z, sink, tq=128):
    b, l, _ = dqe.shape
    half = SWA_HALF_WINDOW
    assert tq == half == LANES
    _, prev, nxt = _band_specs(tq, half, l, 2 * LANES, 0)
    cur = pl.BlockSpec((1, tq, 2 * LANES), lambda i, t: (i, t, 0))
    sink_p = jnp.pad(sink.reshape(1, SWA_Q_HEADS).astype(F32), ((0, 0), (0, LANES - SWA_Q_HEADS)))
    return pl.pallas_call(
        functools.partial(_swa_kernel, half=half, ls=l),
        grid=(b, l // tq),
        in_specs=[
            pl.BlockSpec((1, tq, 2 * SWA_WIDTH), lambda i, t: (i, t, 0)),
            prev, cur, nxt, prev, cur, nxt,
            pl.BlockSpec((1, tq, SWA_WIDTH), lambda i, t: (i, t, O_GD // SWA_WIDTH)),
            pl.BlockSpec((1, LANES), lambda i, t: (0, 0)),
        ],
        out_specs=pl.BlockSpec((1, tq, SWA_WIDTH), lambda i, t: (i, t, 0)),
        out_shape=jax.ShapeDtypeStruct((b, l, SWA_WIDTH), BF16),
        compiler_params=_cparams("parallel", "arbitrary"),
        name="swa",
    )(dqe, dk, dk, dk, dv, dv, dv, z, sink_p)


def _odd_layer(x, mem, pre_g, post_g, w_in, w_out, swa_sink, mem_g, w_mem_kv):
    b, l, d = x.shape
    x2d = x.reshape(b * l, d)
    w_re = jnp.concatenate([w_in[:, 0:6144], w_in[:, 6400:8448], w_in[:, 6144:6400]], axis=1)
    z = _norm_matmul(x2d, pre_g, w_re, O_ALL // 3, out_dtype=BF16, tm=1024, name="odd_in").reshape(b, l, O_ALL)
    dil, dqe, dk, dv = _odd_prep(z)
    y_c = _dilated_branch(dil, z)
    y_d = _swa_branch(dqe, dk, dv, z, swa_sink)
    kv = _mem_kv(mem, mem_g, w_mem_kv)
    y_x = _xattn(z, kv, O_XQ // X_WIDTH, O_GX // X_WIDTH)
    t = b * l
    out = _out_proj([y_c.reshape(t, -1), y_d.reshape(t, -1), y_x.reshape(t, -1)], w_out, post_g, x2d)
    return out.reshape(b, l, d)


def _trunk(x, mem, even_params, odd_params):
    x = _even_layer(x, mem, *[p[0] for p in even_params])
    return _odd_layer(x, mem, *[p[0] for p in odd_params])


def kernel(x_prompt, x_sample, mem_prompt, mem_sample, e_pre_g, e_post_g, e_w_in, e_w_out, hy_conv_w, hy_conv_b,
           hy_filt_w1, hy_filt_b1, hy_filt_w2, hy_filt_b2, hy_filt_w3, hy_freq, hy_skip, gdn_conv_w, gdn_A_log,
           gdn_dt_bias, gdn_norm_g, e_mem_g, e_w_mem_kv, o_pre_g, o_post_g, o_w_in, o_w_out, swa_sink, o_mem_g,
           o_w_mem_kv):
    even_params = (e_pre_g, e_post_g, e_w_in, e_w_out, hy_conv_w, hy_conv_b, hy_filt_w1, hy_filt_b1, hy_filt_w2,
                   hy_filt_b2, hy_filt_w3, hy_freq, hy_skip, gdn_conv_w, gdn_A_log, gdn_dt_bias, gdn_norm_g,
                   e_mem_g, e_w_mem_kv)
    odd_params = (o_pre_g, o_post_g, o_w_in, o_w_out, swa_sink, o_mem_g, o_w_mem_kv)
    y_prompt = _trunk(x_prompt, mem_prompt, even_params, odd_params)
    y_sample = _trunk(x_sample, mem_sample, even_params, odd_params)
    return (y_prompt, y_sample)
```
